```python
import math
import jax, jax.numpy as jnp
from jax import lax
import numpy as np

D_MODEL = 1024
BATCH = 8
SEQ = 2048
DEPTH = 1

SSM_WIDTH = D_MODEL // 2
SSM_GROUP = 16
SSM_GROUPS = SSM_WIDTH // SSM_GROUP
SSM_STATE = 64
ATTN_HEAD_DIM = 64
ATTN_HEADS = D_MODEL // (2 * ATTN_HEAD_DIM)
ATTN_WIDTH = ATTN_HEADS * 2 * ATTN_HEAD_DIM
Q_BLOCK = 128
IN_WIDTH = SSM_WIDTH + 3 * ATTN_WIDTH + 2 * D_MODEL
N_EXPERTS = 32
TOP_K = 4
D_FF = D_MODEL
SWIGLU_ALPHA = 1.702
SWIGLU_LIMIT = 7.0
MOE_BLOCK = 256
NORM_EPS = 1e-5

kernel_name = "hybrid_s5_diffattn_moe_block"


def rms_norm(x, gain, eps=NORM_EPS):
    xf = x.astype(jnp.float32)
    y = xf * lax.rsqrt(jnp.mean(xf * xf, axis=-1, keepdims=True) + eps)
    return (y * gain.astype(jnp.float32)).astype(x.dtype)


def _complex_linear_combine(e1, e2):
    a1r, a1i, b1r, b1i = e1
    a2r, a2i, b2r, b2i = e2
    ar = a1r * a2r - a1i * a2i
    ai = a1r * a2i + a1i * a2r
    br = a2r * b1r - a2i * b1i + b2r
    bi = a2r * b1i + a2i * b1r + b2i
    return (ar, ai, br, bi)


def s5_ssm(u, lambda_re, lambda_im, log_dt, b_re, b_im, c_re, c_im, d_skip):
    bsz, seq, _ = u.shape
    uf = u.astype(jnp.float32).reshape(bsz, seq, SSM_GROUPS, SSM_GROUP)
    dt = jnp.exp(log_dt.astype(jnp.float32))[:, None]
    lr = jnp.minimum(lambda_re.astype(jnp.float32), -1e-4)
    li = lambda_im.astype(jnp.float32)
    mag = jnp.exp(lr * dt)
    abar_re = mag * jnp.cos(li * dt)
    abar_im = mag * jnp.sin(li * dt)
    den = lr * lr + li * li
    nr = abar_re - 1.0
    coef_re = (nr * lr + abar_im * li) / den
    coef_im = (abar_im * lr - nr * li) / den
    br = b_re.astype(jnp.float32)
    bi = b_im.astype(jnp.float32)
    bbar_re = coef_re[..., None] * br - coef_im[..., None] * bi
    bbar_im = coef_re[..., None] * bi + coef_im[..., None] * br
    bu_re = jnp.einsum('bsgh,gph->bsgp', uf, bbar_re)
    bu_im = jnp.einsum('bsgh,gph->bsgp', uf, bbar_im)
    a_re = jnp.broadcast_to(abar_re, bu_re.shape)
    a_im = jnp.broadcast_to(abar_im, bu_im.shape)
    _, _, x_re, x_im = lax.associative_scan(
        _complex_linear_combine, (a_re, a_im, bu_re, bu_im), axis=1)
    y = (jnp.einsum('bsgp,ghp->bsgh', x_re, c_re.astype(jnp.float32))
         - jnp.einsum('bsgp,ghp->bsgh', x_im, c_im.astype(jnp.float32))
         + d_skip.astype(jnp.float32) * uf)
    return y.reshape(bsz, seq, SSM_WIDTH)


def diff_attention(q, k, v, q_gain, k_gain, lq1, lk1, lq2, lk2, subln_gain, lambda_init):
    bsz, seq, _ = q.shape
    q = rms_norm(q.reshape(bsz, seq, ATTN_HEADS, 2, ATTN_HEAD_DIM), q_gain)
    k = rms_norm(k.reshape(bsz, seq, ATTN_HEADS, 2, ATTN_HEAD_DIM), k_gain)
    q = q.transpose(0, 3, 2, 1, 4)
    k = k.transpose(0, 3, 2, 1, 4)
    v = v.reshape(bsz, seq, ATTN_HEADS, 2 * ATTN_HEAD_DIM).transpose(0, 2, 1, 3)
    lam = (jnp.exp(jnp.sum(lq1.astype(jnp.float32) * lk1.astype(jnp.float32)))
           - jnp.exp(jnp.sum(lq2.astype(jnp.float32) * lk2.astype(jnp.float32)))
           + lambda_init)
    scale = 1.0 / math.sqrt(ATTN_HEAD_DIM)
    outs = []
    for i in range(seq // Q_BLOCK):
        q0 = i * Q_BLOCK
        kend = q0 + Q_BLOCK
        s = jnp.einsum('bmhqd,bmhkd->bmhqk', q[:, :, :, q0:kend], k[:, :, :, :kend]).astype(jnp.float32) * scale
        mask = jnp.arange(kend)[None, :] <= (q0 + jnp.arange(Q_BLOCK))[:, None]
        p = jax.nn.softmax(jnp.where(mask, s, -jnp.inf), axis=-1)
        a = p[:, 0] - lam * p[:, 1]
        outs.append(jnp.einsum('bhqk,bhkv->bhqv', a.astype(v.dtype), v[:, :, :kend]))
    o = jnp.concatenate(outs, axis=2)
    o = rms_norm(o, subln_gain) * (1.0 - lambda_init)
    return o.transpose(0, 2, 1, 3).reshape(bsz, seq, ATTN_WIDTH)


def clamped_swiglu_expert(xb, w1, b1, w2, b2):
    h = xb @ w1 + b1
    gate = jnp.minimum(h[..., ::2], SWIGLU_LIMIT)
    up = jnp.clip(h[..., 1::2], -SWIGLU_LIMIT, SWIGLU_LIMIT)
    glu = gate * jax.nn.sigmoid(SWIGLU_ALPHA * gate)
    return ((up + 1.0) * glu) @ w2 + b2


def moe_ffn(h, w_router, b_router, w1, b1, w2, b2):
    bsz, seq, d = h.shape
    n_tok = bsz * seq
    xt = h.reshape(n_tok, d)
    logits = (xt @ w_router + b_router).astype(jnp.float32)
    top_vals, top_idx = lax.top_k(logits, TOP_K)
    gates = jax.nn.softmax(top_vals, axis=-1)
    n_pairs = n_tok * TOP_K
    e_flat = top_idx.reshape(-1).astype(jnp.int32)
    tok_flat = jnp.arange(n_pairs, dtype=jnp.int32) // TOP_K
    g_flat = gates.reshape(-1)
    order = jnp.argsort(e_flat)
    e_sorted = e_flat[order]
    counts = jnp.bincount(e_flat, length=N_EXPERTS)
    padded = (counts + MOE_BLOCK - 1) // MOE_BLOCK * MOE_BLOCK
    pad_end = jnp.cumsum(padded)
    pad_start = pad_end - padded
    grp_start = jnp.cumsum(counts) - counts
    dest = pad_start[e_sorted] + (jnp.arange(n_pairs, dtype=jnp.int32) - grp_start[e_sorted])
    n_slots = n_pairs + N_EXPERTS * MOE_BLOCK
    n_blocks = n_slots // MOE_BLOCK
    slot_tok = jnp.full((n_slots,), n_tok, jnp.int32).at[dest].set(tok_flat[order])
    slot_gate = jnp.zeros((n_slots,), jnp.float32).at[dest].set(g_flat[order])
    block_exp = jnp.minimum(
        jnp.searchsorted(pad_end, jnp.arange(n_blocks) * MOE_BLOCK, side='right'), N_EXPERTS - 1)
    x_pad = jnp.concatenate([xt, jnp.zeros((1, d), xt.dtype)], axis=0)
    xb = x_pad[slot_tok].reshape(n_blocks, MOE_BLOCK, d)

    def expert_block(args):
        xblk, e = args
        return clamped_swiglu_expert(xblk, w1[e], b1[e], w2[e], b2[e])

    yb = lax.map(expert_block, (xb, block_exp))
    y = yb.reshape(n_slots, d).astype(jnp.float32) * slot_gate[:, None]
    out = jax.ops.segment_sum(y, slot_tok, num_segments=n_tok + 1)[:n_tok]
    return out.reshape(bsz, seq, d).astype(h.dtype)


def setup_inputs(seed: int = 0) -> dict:
    key = jax.random.key(seed)
    ks = jax.random.split(key, 32)
    L, D, G, H, P = DEPTH, D_MODEL, SSM_GROUPS, SSM_GROUP, SSM_STATE
    nrm = lambda k, shape, s: jax.random.normal(k, shape, jnp.float32) * s
    lam_im0 = jnp.pi * jnp.arange(P, dtype=jnp.float32)
    return {
        "x": nrm(ks[0], (BATCH, SEQ, D), 1.0),
        "norm1_gain": 1.0 + nrm(ks[1], (L, D), 0.01),
        "w_in": nrm(ks[2], (L, D, IN_WIDTH), D ** -0.5),
        "lambda_re": -0.5 + nrm(ks[3], (L, G, P), 0.01),
        "lambda_im": lam_im0 + nrm(ks[4], (L, G, P), 0.01),
        "log_dt": jax.random.uniform(ks[5], (L, G), jnp.float32, math.log(1e-3), math.log(1e-1)),
        "ssm_b_re": nrm(ks[6], (L, G, P, H), (2 * H) ** -0.5),
        "ssm_b_im": nrm(ks[7], (L, G, P, H), (2 * H) ** -0.5),
        "ssm_c_re": nrm(ks[8], (L, G, H, P), P ** -0.5),
        "ssm_c_im": nrm(ks[9], (L, G, H, P), P ** -0.5),
        "ssm_d": nrm(ks[10], (L, G, H), 1.0),
        "w_glu": nrm(ks[11], (L, SSM_WIDTH, SSM_WIDTH), SSM_WIDTH ** -0.5),
        "b_glu": nrm(ks[12], (L, SSM_WIDTH), 0.01),
        "q_norm_gain": 1.0 + nrm(ks[13], (L, ATTN_HEAD_DIM), 0.01),
        "k_norm_gain": 1.0 + nrm(ks[14], (L, ATTN_HEAD_DIM), 0.01),
        "lambda_q1": nrm(ks[15], (L, ATTN_HEAD_DIM), 0.1),
        "lambda_k1": nrm(ks[16], (L, ATTN_HEAD_DIM), 0.1),
        "lambda_q2": nrm(ks[17], (L, ATTN_HEAD_DIM), 0.1),
        "lambda_k2": nrm(ks[18], (L, ATTN_HEAD_DIM), 0.1),
        "subln_gain": 1.0 + nrm(ks[19], (L, 2 * ATTN_HEAD_DIM), 0.01),
        "w_proj_ssm": nrm(ks[20], (L, SSM_WIDTH, D), SSM_WIDTH ** -0.5),
        "w_proj_attn": nrm(ks[21], (L, ATTN_WIDTH, D), ATTN_WIDTH ** -0.5),
        "w_out": nrm(ks[22], (L, D, D), D ** -0.5),
        "norm2_gain": 1.0 + nrm(ks[23], (L, D), 0.01),
        "w_router": nrm(ks[24], (L, D, N_EXPERTS), D ** -0.5),
        "b_router": nrm(ks[25], (L, N_EXPERTS), 0.01),
        "w_exp1": nrm(ks[26], (L, N_EXPERTS, D, 2 * D_FF), D ** -0.5),
        "b_exp1": nrm(ks[27], (L, N_EXPERTS, 2 * D_FF), 0.01),
        "w_exp2": nrm(ks[28], (L, N_EXPERTS, D_FF, D), D_FF ** -0.5),
        "b_exp2": nrm(ks[29], (L, N_EXPERTS, D), 0.01),
    }


def reference(x, norm1_gain, w_in, lambda_re, lambda_im, log_dt, ssm_b_re, ssm_b_im, ssm_c_re,
              ssm_c_im, ssm_d, w_glu, b_glu, q_norm_gain, k_norm_gain, lambda_q1, lambda_k1,
              lambda_q2, lambda_k2, subln_gain, w_proj_ssm, w_proj_attn, w_out, norm2_gain,
              w_router, b_router, w_exp1, b_exp1, w_exp2, b_exp2):
    splits = np.cumsum([SSM_WIDTH, ATTN_WIDTH, ATTN_WIDTH, ATTN_WIDTH, D_MODEL]).tolist()
    for l in range(DEPTH):
        lambda_init = 0.8 - 0.6 * math.exp(-0.3 * l)
        h = rms_norm(x, norm1_gain[l])
        proj = h @ w_in[l]
        u_ssm, q, k, v, g_ssm, g_attn = jnp.split(proj, splits, axis=-1)
        y = s5_ssm(u_ssm, lambda_re[l], lambda_im[l], log_dt[l], ssm_b_re[l], ssm_b_im[l],
                   ssm_c_re[l], ssm_c_im[l], ssm_d[l])
        z = jax.nn.gelu(y)
        ssm_out = (z * jax.nn.sigmoid(z @ w_glu[l].astype(jnp.float32) + b_glu[l])).astype(x.dtype)
        attn_out = diff_attention(q, k, v, q_norm_gain[l], k_norm_gain[l], lambda_q1[l], lambda_k1[l],
                                  lambda_q2[l], lambda_k2[l], subln_gain[l], lambda_init)
        merged = (jax.nn.sigmoid(g_ssm) * (ssm_out @ w_proj_ssm[l])
                  + jax.nn.sigmoid(g_attn) * (attn_out @ w_proj_attn[l]))
        x = x + merged @ w_out[l]
        h2 = rms_norm(x, norm2_gain[l])
        x = x + moe_ffn(h2, w_router[l], b_router[l], w_exp1[l], b_exp1[l], w_exp2[l], b_exp2[l])
    return x
```

```python
import functools
import math

import jax
import jax.numpy as jnp
from jax import lax
from jax.experimental import pallas as pl
from jax.experimental.pallas import tpu as pltpu
from jax.experimental.pallas import tpu_sc as plsc

F32 = jnp.float32
BF16 = jnp.bfloat16

D_MODEL = 1024
NORM_EPS = 1e-5
SSM_WIDTH = 512
SSM_GROUP = 16
SSM_GROUPS = 32
SSM_STATE = 64
N_STATE = SSM_GROUPS * SSM_STATE
HEADS = 8
HEAD_DIM = 64
N_EXPERTS = 32
TOP_K = 4
D_FF = 1024
SWIGLU_ALPHA = 1.702
SWIGLU_LIMIT = 7.0

LANES = 128
SUBLANES = 8
MXU_DIM = 256
VMEM_LIMIT = 56 * 1024 * 1024

ROW_TILE = 512
SSM_CHUNK = 128
SCAN_LANES = 512
ATTN_BLOCK = 256
EXPERT_BLOCK = 256
SC_WINDOW = 128
PACK_W = D_MODEL // 2
PACK_HALF = PACK_W // 2

_NEG = -1e30


def _cparams(sem):
    return pltpu.CompilerParams(dimension_semantics=sem, vmem_limit_bytes=VMEM_LIMIT)


def _full(shape):
    nd = len(shape)
    return pl.BlockSpec(shape, lambda *_: (0,) * nd)


def _inproj_kernel(x_ref, g1_ref, w_ref, qg_ref, kg_ref, seg_ref,
                   u_ref, q_ref, k_ref, v_ref, gs_ref, ga_ref):
    x = x_ref[...]
    ms = jnp.mean(x * x, axis=-1, keepdims=True)
    h = (x * lax.rsqrt(ms + NORM_EPS) * g1_ref[...]).astype(BF16)

    def proj(c0, width):
        return jnp.dot(h, w_ref[:, c0:c0 + width], preferred_element_type=F32)

    u_ref[...] = proj(0, SSM_WIDTH)
    seg = seg_ref[...]
    q0 = SSM_WIDTH
    k0 = q0 + D_MODEL
    v0 = k0 + D_MODEL
    gs0 = v0 + D_MODEL
    ga0 = gs0 + D_MODEL
    for base, gain_ref, out_ref, scale in ((q0, qg_ref, q_ref, 1.0 / math.sqrt(HEAD_DIM)),
                                           (k0, kg_ref, k_ref, 1.0)):
        for c in range(D_MODEL // MXU_DIM):
            lo = c * MXU_DIM
            y = proj(base + lo, MXU_DIM)
            ss = jnp.dot((y * y).astype(BF16), seg, preferred_element_type=F32)
            yn = y * lax.rsqrt(ss * (1.0 / HEAD_DIM) + NORM_EPS) * gain_ref[:, lo:lo + MXU_DIM]
            out_ref[:, lo:lo + MXU_DIM] = (yn * scale).astype(BF16)
    half = D_MODEL // 2
    for c in range(2):
        lo = c * half
        v_ref[:, lo:lo + half] = proj(v0 + lo, half).astype(BF16)
        gs_ref[:, lo:lo + half] = jax.nn.sigmoid(proj(gs0 + lo, half)).astype(BF16)
        ga_ref[:, lo:lo + half] = jax.nn.sigmoid(proj(ga0 + lo, half)).astype(BF16)


def _in_proj(x2d, gain1, w_in_bf, q_gain, k_gain):
    n_tok = x2d.shape[0]
    tm = min(ROW_TILE, n_tok)
    in_width = w_in_bf.shape[1]
    seg = (jnp.arange(MXU_DIM)[:, None] // HEAD_DIM == jnp.arange(MXU_DIM)[None, :] // HEAD_DIM).astype(BF16)
    reps = D_MODEL // HEAD_DIM
    qg = jnp.tile(q_gain.astype(F32), reps)[None, :]
    kg = jnp.tile(k_gain.astype(F32), reps)[None, :]
    row = lambda w: pl.BlockSpec((tm, w), lambda i: (i, 0))
    out_shapes = ([jax.ShapeDtypeStruct((n_tok, SSM_WIDTH), F32)]
                  + [jax.ShapeDtypeStruct((n_tok, D_MODEL), BF16)] * 5)
    return pl.pallas_call(
        _inproj_kernel,
        grid=(n_tok // tm,),
        in_specs=[row(D_MODEL), _full((1, D_MODEL)), _full((D_MODEL, in_width)),
                  _full((1, D_MODEL)), _full((1, D_MODEL)), _full((MXU_DIM, MXU_DIM))],
        out_specs=[row(SSM_WIDTH)] + [row(D_MODEL)] * 5,
        out_shape=out_shapes,
        compiler_params=_cparams(("parallel",)),
        name="in_proj",
    )(x2d, gain1, w_in_bf, qg, kg, seg)


def _ssm_kernel(u_ref, bblk_ref, a_ref, cblk_ref, d_ref, wglu_ref, bglu_ref, o_ref, bu_ref, st_ref):
    chunk = u_ref.shape[0] // SUBLANES

    @pl.when(pl.program_id(0) == 0)
    def _():
        st_ref[...] = jnp.zeros_like(st_ref)

    u = u_ref[...]
    bu_ref[...] = jnp.dot(u.astype(BF16), bblk_ref[...], preferred_element_type=F32)

    for j in range(N_STATE // SCAN_LANES):
        re = slice(j * SCAN_LANES, (j + 1) * SCAN_LANES)
        im = slice(N_STATE + j * SCAN_LANES, N_STATE + (j + 1) * SCAN_LANES)
        ar = a_ref[:, re]
        ai = a_ref[:, im]

        def step(t, carry, re=re, im=im, ar=ar, ai=ai):
            xr, xi = carry
            rows = pl.ds(pl.multiple_of(t * SUBLANES, SUBLANES), SUBLANES)
            nr = ar * xr - ai * xi + bu_ref[rows, re]
            ni = ar * xi + ai * xr + bu_ref[rows, im]
            bu_ref[rows, re] = nr
            bu_ref[rows, im] = ni
            return nr, ni

        xr, xi = lax.fori_loop(0, chunk, step, (st_ref[:, re], st_ref[:, im]), unroll=4)
        st_ref[:, re] = xr
        st_ref[:, im] = xi

    y = jnp.dot(bu_ref[...].astype(BF16), cblk_ref[...], preferred_element_type=F32) + d_ref[...] * u
    z = jax.nn.gelu(y)
    gate = jax.nn.sigmoid(jnp.dot(z.astype(BF16), wglu_ref[...], preferred_element_type=F32) + bglu_ref[...])
    o_ref[...] = (z * gate).astype(BF16)


def _ssm_params(lambda_re, lambda_im, log_dt, b_re, b_im, c_re, c_im):
    dt = jnp.exp(log_dt.astype(F32))[:, None]
    lr = jnp.minimum(lambda_re.astype(F32), -1e-4)
    li = lambda_im.astype(F32)
    mag = jnp.exp(lr * dt)
    abar_re = mag * jnp.cos(li * dt)
    abar_im = mag * jnp.sin(li * dt)
    den = lr * lr + li * li
    nr = abar_re - 1.0
    coef_re = (nr * lr + abar_im * li) / den
    coef_im = (abar_im * lr - nr * li) / den
    br = b_re.astype(F32)
    bi = b_im.astype(F32)
    bbar_re = coef_re[..., None] * br - coef_im[..., None] * bi
    bbar_im = coef_re[..., None] * bi + coef_im[..., None] * br
    eye = jnp.eye(SSM_GROUPS, dtype=F32)

    def expand_b(b):
        return jnp.einsum('gph,gk->ghkp', b, eye).reshape(SSM_WIDTH, N_STATE)

    def expand_c(c):
        return jnp.einsum('ghp,gk->gpkh', c, eye).reshape(N_STATE, SSM_WIDTH)

    bblk = jnp.concatenate([expand_b(bbar_re), expand_b(bbar_im)], axis=1).astype(BF16)
    cblk = jnp.concatenate([expand_c(c_re.astype(F32)), -expand_c(c_im.astype(F32))], axis=0).astype(BF16)
    a_row = jnp.concatenate([abar_re.reshape(-1), abar_im.reshape(-1)])[None, :]
    return bblk, cblk, jnp.broadcast_to(a_row, (SUBLANES, 2 * N_STATE))


def _ssm(u_tb, seq, bblk, cblk, a_tile, d_skip, w_glu_bf, b_glu):
    chunk = min(SSM_CHUNK, seq)
    rows = chunk * SUBLANES
    return pl.pallas_call(
        _ssm_kernel,
        grid=(seq // chunk,),
        in_specs=[pl.BlockSpec((rows, SSM_WIDTH), lambda c: (c, 0)),
                  _full(bblk.shape), _full(a_tile.shape), _full(cblk.shape),
                  _full((1, SSM_WIDTH)), _full(w_glu_bf.shape), _full((1, SSM_WIDTH))],
        out_specs=pl.BlockSpec((rows, SSM_WIDTH), lambda c: (c, 0)),
        out_shape=jax.ShapeDtypeStruct(u_tb.shape, BF16),
        scratch_shapes=[pltpu.VMEM((rows, 2 * N_STATE), F32), pltpu.VMEM((SUBLANES, 2 * N_STATE), F32)],
        compiler_params=_cparams(("arbitrary",)),
        name="ssm",
    )(u_tb, bblk, a_tile, cblk, d_skip, w_glu_bf, b_glu)


def _attn_kernel(lam_ref, sg_ref, q_ref, k_ref, v_ref, o_ref, *, out_scale):
    qi = pl.program_id(2)
    tq = q_ref.shape[1]
    q = q_ref[0]
    lane = lax.broadcasted_iota(jnp.int32, q.shape, 1)
    zero = jnp.zeros_like(q)
    q1 = jnp.where(lane < HEAD_DIM, q, zero)
    q2 = jnp.where(lane >= HEAD_DIM, q, zero)
    row = lax.broadcasted_iota(jnp.int32, (tq, tq), 0)
    col = lax.broadcasted_iota(jnp.int32, (tq, tq), 1)
    keep = col <= row
    contract_last = (((1,), (1,)), ((), ()))

    def update(s, v, m, l, acc):
        m_new = jnp.maximum(m, jnp.max(s, axis=-1, keepdims=True))
        p = jnp.exp(s - m_new)
        alpha = jnp.exp(m - m_new)
        l = alpha * l + jnp.sum(p, axis=-1, keepdims=True)
        acc = alpha * acc + jnp.dot(p.astype(BF16), v, preferred_element_type=F32)
        return m_new, l, acc

    def block(kb, carry, masked):
        rows = pl.ds(pl.multiple_of(kb * tq, tq), tq)
        k = k_ref[0, rows, :]
        v = v_ref[0, rows, :]
        s1 = lax.dot_general(q1, k, contract_last, preferred_element_type=F32)
        s2 = lax.dot_general(q2, k, contract_last, preferred_element_type=F32)
        if masked:
            s1 = jnp.where(keep, s1, _NEG)
            s2 = jnp.where(keep, s2, _NEG)
        m1, l1, a1, m2, l2, a2 = carry
        m1, l1, a1 = update(s1, v, m1, l1, a1)
        m2, l2, a2 = update(s2, v, m2, l2, a2)
        return m1, l1, a1, m2, l2, a2

    col0 = jnp.full((tq, 1), _NEG, F32)
    zcol = jnp.zeros((tq, 1), F32)
    zacc = jnp.zeros((tq, 2 * HEAD_DIM), F32)
    carry = (col0, zcol, zacc, col0, zcol, zacc)
    carry = lax.fori_loop(0, qi, lambda kb, c: block(kb, c, False), carry)
    _, l1, a1, _, l2, a2 = block(qi, carry, True)
    o = a1 / l1 - lam_ref[0] * (a2 / l2)
    o = o * lax.rsqrt(jnp.mean(o * o, axis=-1, keepdims=True) + NORM_EPS) * sg_ref[...] * out_scale
    o_ref[0] = o.astype(BF16)


def _attention(q, k, v, lam, subln_gain, lambda_init, bsz, seq):
    tq = min(ATTN_BLOCK, seq)
    hw = 2 * HEAD_DIM
    q3, k3, v3 = (a.reshape(bsz, seq, D_MODEL) for a in (q, k, v))
    qspec = pl.BlockSpec((1, tq, hw), lambda b, h, i: (b, i, h))
    kvspec = pl.BlockSpec((1, seq, hw), lambda b, h, i: (b, 0, h))
    out = pl.pallas_call(
        functools.partial(_attn_kernel, out_scale=1.0 - lambda_init),
        grid=(bsz, HEADS, seq // tq),
        in_specs=[pl.BlockSpec(memory_space=pltpu.SMEM), _full((1, hw)), qspec, kvspec, kvspec],
        out_specs=qspec,
        out_shape=jax.ShapeDtypeStruct((bsz, seq, D_MODEL), BF16),
        compiler_params=_cparams(("parallel", "parallel", "arbitrary")),
        name="diff_attn",
    )(lam, subln_gain.astype(F32)[None, :], q3, k3, v3)
    return out.reshape(bsz * seq, D_MODEL)


def _pack_rows(y):
    bits = lax.bitcast_convert_type(y.astype(BF16).astype(F32), jnp.uint32)
    return (bits[:, :PACK_W] >> 16) | (bits[:, PACK_W:] & jnp.uint32(0xFFFF0000))


def _unpack_rows(w):
    lo = lax.bitcast_convert_type(w << 16, F32)
    hi = lax.bitcast_convert_type(w & jnp.uint32(0xFFFF0000), F32)
    return jnp.concatenate([lo, hi], axis=1)


def _merge_kernel(so_ref, ao_ref, gs_ref, ga_ref, x_ref, wps_ref, wpa_ref, wo_ref, g2_ref,
                  wrh_ref, wrl_ref, br_ref, tri_ref,
                  x1_ref, hlo_ref, hhi_ref, route_ref, gate_ref, cnt_ref, run_ref, *, region):
    @pl.when(pl.program_id(0) == 0)
    def _():
        run_ref[...] = jnp.zeros_like(run_ref)

    ps = jnp.dot(so_ref[...], wps_ref[...], preferred_element_type=F32)
    pa = jnp.dot(ao_ref[...], wpa_ref[...], preferred_element_type=F32)
    merged = gs_ref[...].astype(F32) * ps + ga_ref[...].astype(F32) * pa
    x1 = x_ref[...] + jnp.dot(merged.astype(BF16), wo_ref[...], preferred_element_type=F32)
    x1_ref[...] = x1
    h2 = x1 * lax.rsqrt(jnp.mean(x1 * x1, axis=-1, keepdims=True) + NORM_EPS) * g2_ref[...]
    words = _pack_rows(h2)
    hlo_ref[...] = words[:, :PACK_HALF]
    hhi_ref[...] = words[:, PACK_HALF:]

    h_hi = h2.astype(BF16)
    h_lo = (h2 - h_hi.astype(F32)).astype(BF16)
    logits = (jnp.dot(h_hi, wrh_ref[...], preferred_element_type=F32)
              + jnp.dot(h_lo, wrh_ref[...], preferred_element_type=F32)
              + jnp.dot(h_hi, wrl_ref[...], preferred_element_type=F32)) + br_ref[...]

    tm = logits.shape[0]
    lane = lax.broadcasted_iota(jnp.int32, (tm, LANES), 1)
    lane_f = lane.astype(F32)
    work = logits
    onehots, vals, ids = [], [], []
    for _ in range(TOP_K):
        m = jnp.max(work, axis=-1, keepdims=True)
        idx = jnp.min(jnp.where(work == m, lane_f, float(LANES)), axis=-1, keepdims=True)
        oh = lane_f == idx
        onehots.append(oh)
        vals.append(m)
        ids.append(idx.astype(jnp.int32))
        work = jnp.where(oh, -jnp.inf, work)
    exps = [jnp.exp(v - vals[0]) for v in vals]
    den = exps[0] + exps[1] + exps[2] + exps[3]

    multi = (onehots[0] | onehots[1] | onehots[2] | onehots[3]).astype(F32)
    before = jnp.dot(tri_ref[...], multi.astype(BF16), preferred_element_type=F32) + run_ref[...]
    route = jnp.zeros((tm, LANES), jnp.int32)
    gates = jnp.zeros((tm, LANES), F32)
    for kk in range(TOP_K):
        rank = jnp.sum(jnp.where(onehots[kk], before, 0.0), axis=-1, keepdims=True).astype(jnp.int32)
        route = jnp.where(lane == kk, ids[kk] * region + rank, route)
        gates = jnp.where(lane == kk, exps[kk] / den, gates)
    route_ref[...] = route
    gate_ref[...] = gates
    run = run_ref[...] + jnp.sum(multi, axis=0, keepdims=True)
    run_ref[...] = run
    cnt_ref[...] = run.astype(jnp.int32)


def _merge_route(so, ao, gs, ga, x2d, wps, wpa, wo, gain2, w_router, b_router, region):
    n_tok = x2d.shape[0]
    tm = min(ROW_TILE, n_tok)
    wr = jnp.zeros((D_MODEL, LANES), F32).at[:, :N_EXPERTS].set(w_router.astype(F32))
    wr_hi = wr.astype(BF16)
    wr_lo = (wr - wr_hi.astype(F32)).astype(BF16)
    br = jnp.full((1, LANES), -jnp.inf, F32).at[0, :N_EXPERTS].set(b_router.astype(F32))
    tri = (jnp.arange(tm)[:, None] > jnp.arange(tm)[None, :]).astype(BF16)
    row = lambda w: pl.BlockSpec((tm, w), lambda i: (i, 0))
    out_shapes = [jax.ShapeDtypeStruct((n_tok, D_MODEL), F32),
                  jax.ShapeDtypeStruct((n_tok, PACK_HALF), jnp.uint32),
                  jax.ShapeDtypeStruct((n_tok, PACK_HALF), jnp.uint32),
                  jax.ShapeDtypeStruct((n_tok, LANES), jnp.int32),
                  jax.ShapeDtypeStruct((n_tok, LANES), F32),
                  jax.ShapeDtypeStruct((1, LANES), jnp.int32)]
    return pl.pallas_call(
        functools.partial(_merge_kernel, region=region),
        grid=(n_tok // tm,),
        in_specs=[row(SSM_WIDTH), row(D_MODEL), row(D_MODEL), row(D_MODEL), row(D_MODEL),
                  _full(wps.shape), _full(wpa.shape), _full(wo.shape), _full((1, D_MODEL)),
                  _full(wr_hi.shape), _full(wr_lo.shape), _full((1, LANES)), _full((tm, tm))],
        out_specs=[row(D_MODEL), row(PACK_HALF), row(PACK_HALF), row(LANES), row(LANES), _full((1, LANES))],
        out_shape=out_shapes,
        scratch_shapes=[pltpu.VMEM((1, LANES), F32)],
        compiler_params=_cparams(("arbitrary",)),
        name="merge_route",
    )(so, ao, gs, ga, x2d, wps, wpa, wo, gain2, wr_hi, wr_lo, br, tri)


def _sc_scatter_rows(rows, dest, n_slots):
    n_tok, width = rows.shape
    mesh = plsc.VectorSubcoreMesh(core_axis_name="core", subcore_axis_name="subcore")

    @pl.kernel(out_type=jax.ShapeDtypeStruct((n_slots, width), rows.dtype), mesh=mesh, scratch_types=[])
    def scatter(rows_hbm, dest_hbm, out_hbm):
        def body(rows_vmem, dest_vmem):
            pltpu.sync_copy(rows_vmem, out_hbm.at[dest_vmem.at[0]])

        pltpu.emit_pipeline(
            body,
            grid=(TOP_K, n_tok // SC_WINDOW),
            in_specs=[pl.BlockSpec((SC_WINDOW, width), lambda k, i: (i, 0)),
                      pl.BlockSpec((1, SC_WINDOW), lambda k, i: (k, i))],
            out_specs=[],
            core_axis_name=("core", "subcore"),
            dimension_semantics=(pltpu.PARALLEL, pltpu.PARALLEL),
        )(rows_hbm, dest_hbm)

    return scatter(rows, dest)


def _sc_gather_rows(table, idx):
    n = idx.shape[1]
    width = table.shape[1]
    mesh = plsc.VectorSubcoreMesh(core_axis_name="core", subcore_axis_name="subcore")

    @pl.kernel(out_type=jax.ShapeDtypeStruct((n, width), table.dtype), mesh=mesh, scratch_types=[])
    def gather(table_hbm, idx_hbm, out_hbm):
        def body(idx_vmem, out_vmem):
            pltpu.sync_copy(table_hbm.at[idx_vmem.at[0]], out_vmem)

        pltpu.emit_pipeline(
            body,
            grid=(n // SC_WINDOW,),
            in_specs=[pl.BlockSpec((1, SC_WINDOW), lambda i: (0, i))],
            out_specs=[pl.BlockSpec((SC_WINDOW, width), lambda i: (i, 0))],
            core_axis_name=("core", "subcore"),
            dimension_semantics=(pltpu.PARALLEL,),
        )(idx_hbm, out_hbm)

    return gather(table, idx)


def _expert_kernel(blk_e_ref, blk_row_ref, nvalid_ref, xlo_ref, xhi_ref, w1_ref, b1_ref, w2_ref, b2_ref,
                   perm_ref, ylo_ref, yhi_ref, w1p_ref, w2b_ref, act_ref):
    i = pl.program_id(0)
    e = blk_e_ref[i]
    e_prev = blk_e_ref[jnp.maximum(i - 1, 0)]
    n_chunks = (2 * D_FF) // MXU_DIM

    @pl.when((i == 0) | (e != e_prev))
    def _():
        perm = perm_ref[...]
        for c in range(n_chunks):
            cols = slice(c * MXU_DIM, (c + 1) * MXU_DIM)
            w1p_ref[:, cols] = jnp.dot(w1_ref[0, :, cols].astype(BF16), perm,
                                       preferred_element_type=F32).astype(BF16)
        w2b_ref[...] = w2_ref[0].astype(BF16)

    @pl.when(i < nvalid_ref[0])
    def _():
        words = jnp.concatenate([xlo_ref[...], xhi_ref[...]], axis=1)
        x = _unpack_rows(words).astype(BF16)
        for c in range(n_chunks):
            cols = slice(c * MXU_DIM, (c + 1) * MXU_DIM)
            h = jnp.dot(x, w1p_ref[:, cols], preferred_element_type=F32) + b1_ref[0, :, cols]
            gate = jnp.minimum(h[:, :LANES], SWIGLU_LIMIT)
            up = jnp.clip(h[:, LANES:], -SWIGLU_LIMIT, SWIGLU_LIMIT)
            glu = gate * jax.nn.sigmoid(SWIGLU_ALPHA * gate)
            act_ref[:, c * LANES:(c + 1) * LANES] = ((up + 1.0) * glu).astype(BF16)
        y = jnp.dot(act_ref[...], w2b_ref[...], preferred_element_type=F32) + b2_ref[0]
        words = _pack_rows(y)
        ylo_ref[...] = words[:, :PACK_HALF]
        yhi_ref[...] = words[:, PACK_HALF:]


def _gate_up_order():
    j = jnp.arange(MXU_DIM)
    within = jnp.where(j < LANES, 2 * j, 2 * (j - LANES) + 1)
    return within


def _experts(xs_lo, xs_hi, w1, b1, w2, b2, blk_e, blk_row, nvalid, n_blocks):
    within = _gate_up_order()
    perm = (jnp.arange(MXU_DIM)[:, None] == within[None, :]).astype(BF16)
    order = (jnp.arange(0, 2 * D_FF, MXU_DIM)[:, None] + within[None, :]).reshape(-1)
    b1p = b1.astype(F32)[:, order][:, None, :]
    b2r = b2.astype(F32)[:, None, :]
    tb = EXPERT_BLOCK
    xspec = pl.BlockSpec((tb, PACK_HALF), lambda i, be, br, nv: (br[i], 0))
    wspec = lambda shape: pl.BlockSpec((1,) + shape, lambda i, be, br, nv: (be[i], 0, 0))
    grid_spec = pltpu.PrefetchScalarGridSpec(
        num_scalar_prefetch=3,
        grid=(n_blocks,),
        in_specs=[xspec, xspec, wspec((D_MODEL, 2 * D_FF)), wspec((1, 2 * D_FF)),
                  wspec((D_FF, D_MODEL)), wspec((1, D_MODEL)),
                  pl.BlockSpec((MXU_DIM, MXU_DIM), lambda i, be, br, nv: (0, 0))],
        out_specs=[xspec, xspec],
        scratch_shapes=[pltpu.VMEM((D_MODEL, 2 * D_FF), BF16), pltpu.VMEM((D_FF, D_MODEL), BF16),
                        pltpu.VMEM((tb, D_FF), BF16)],
    )
    return pl.pallas_call(
        _expert_kernel,
        grid_spec=grid_spec,
        out_shape=[jax.ShapeDtypeStruct(xs_lo.shape, jnp.uint32)] * 2,
        compiler_params=_cparams(("arbitrary",)),
        name="experts",
    )(blk_e, blk_row, nvalid, xs_lo, xs_hi, w1, b1p, w2, b2r, perm)


def _combine_kernel(x1_ref, gate_ref, *refs):
    lo_refs, hi_refs, o_ref = refs[:TOP_K], refs[TOP_K:2 * TOP_K], refs[2 * TOP_K]
    acc = x1_ref[...]
    gates = gate_ref[...]
    for kk in range(TOP_K):
        words = jnp.concatenate([lo_refs[kk][...], hi_refs[kk][...]], axis=1)
        acc = acc + gates[:, kk:kk + 1] * _unpack_rows(words)
    o_ref[...] = acc


def _combine(x1, gates, yg_lo, yg_hi):
    n_tok = x1.shape[0]
    tm = min(ROW_TILE, n_tok)
    nblk = n_tok // tm
    row = lambda w: pl.BlockSpec((tm, w), lambda i: (i, 0))
    plane = lambda kk: pl.BlockSpec((tm, PACK_HALF), lambda i, kk=kk: (kk * nblk + i, 0))
    planes = [plane(kk) for kk in range(TOP_K)]
    return pl.pallas_call(
        _combine_kernel,
        grid=(nblk,),
        in_specs=[row(D_MODEL), row(LANES)] + planes + planes,
        out_specs=row(D_MODEL),
        out_shape=jax.ShapeDtypeStruct((n_tok, D_MODEL), F32),
        compiler_params=_cparams(("parallel",)),
        name="combine",
    )(x1, gates, *([yg_lo] * TOP_K), *([yg_hi] * TOP_K))


def _expert_blocks(counts, region, n_blocks):
    nb_e = (counts + EXPERT_BLOCK - 1) // EXPERT_BLOCK
    cum = jnp.cumsum(nb_e)
    total = cum[-1]
    i = jnp.arange(n_blocks, dtype=jnp.int32)
    i_eff = jnp.minimum(i, total - 1)
    e_i = jnp.minimum(jnp.searchsorted(cum, i_eff, side='right'), N_EXPERTS - 1).astype(jnp.int32)
    j_i = i_eff - (cum[e_i] - nb_e[e_i])
    blk_row = e_i * (region // EXPERT_BLOCK) + j_i
    return e_i, blk_row.astype(jnp.int32), total.reshape(1).astype(jnp.int32)


def kernel(x, norm1_gain, w_in, lambda_re, lambda_im, log_dt, ssm_b_re, ssm_b_im, ssm_c_re, ssm_c_im, ssm_d, w_glu, b_glu, q_norm_gain, k_norm_gain, lambda_q1, lambda_k1, lambda_q2, lambda_k2, subln_gain, w_proj_ssm, w_proj_attn, w_out, norm2_gain, w_router, b_router, w_exp1, b_exp1, w_exp2, b_exp2):
    bsz, seq, d = x.shape
    n_tok = bsz * seq
    depth = norm1_gain.shape[0]
    row1 = lambda a: a.astype(F32).reshape(1, -1)
    for l in range(depth):
        lambda_init = 0.8 - 0.6 * math.exp(-0.3 * l)
        x2d = x.reshape(n_tok, d)

        u, q, k, v, gs, ga = _in_proj(x2d, row1(norm1_gain[l]), w_in[l].astype(BF16),
                                      q_norm_gain[l], k_norm_gain[l])

        bblk, cblk, a_tile = _ssm_params(lambda_re[l], lambda_im[l], log_dt[l], ssm_b_re[l], ssm_b_im[l],
                                         ssm_c_re[l], ssm_c_im[l])
        u_tb = u.reshape(bsz, seq, SSM_WIDTH).transpose(1, 0, 2).reshape(n_tok, SSM_WIDTH)
        so_tb = _ssm(u_tb, seq, bblk, cblk, a_tile, row1(ssm_d[l]), w_glu[l].astype(BF16), row1(b_glu[l]))
        so = so_tb.reshape(seq, bsz, SSM_WIDTH).transpose(1, 0, 2).reshape(n_tok, SSM_WIDTH)

        lam = (jnp.exp(jnp.sum(lambda_q1[l].astype(F32) * lambda_k1[l].astype(F32)))
               - jnp.exp(jnp.sum(lambda_q2[l].astype(F32) * lambda_k2[l].astype(F32)))
               + lambda_init).reshape(1)
        ao = _attention(q, k, v, lam, subln_gain[l], lambda_init, bsz, seq)

        region = n_tok
        x1, h_lo, h_hi, route, gates, counts = _merge_route(
            so, ao, gs, ga, x2d, w_proj_ssm[l].astype(BF16), w_proj_attn[l].astype(BF16),
            w_out[l].astype(BF16), row1(norm2_gain[l]), w_router[l], b_router[l], region)

        dest = route[:, :TOP_K].T
        n_slots = N_EXPERTS * region
        xs_lo = _sc_scatter_rows(h_lo, dest, n_slots)
        xs_hi = _sc_scatter_rows(h_hi, dest, n_slots)

        n_blocks = n_tok * TOP_K // EXPERT_BLOCK + N_EXPERTS
        blk_e, blk_row, nvalid = _expert_blocks(counts[0, :N_EXPERTS], region, n_blocks)
        ys_lo, ys_hi = _experts(xs_lo, xs_hi, w_exp1[l], b_exp1[l], w_exp2[l], b_exp2[l],
                                blk_e, blk_row, nvalid, n_blocks)

        flat = dest.reshape(1, TOP_K * n_tok)
        yg_lo = _sc_gather_rows(ys_lo, flat)
        yg_hi = _sc_gather_rows(ys_hi, flat)
        x = _combine(x1, gates, yg_lo, yg_hi).reshape(bsz, seq, d)
    return x
```

```python
import functools
import math

import jax
import jax.numpy as jnp
from jax import lax
from jax.experimental import pallas as pl
from jax.experimental.pallas import tpu as pltpu
from jax.experimental.pallas import tpu_sc as plsc

F32 = jnp.float32
BF16 = jnp.bfloat16

D_MODEL = 1024
NORM_EPS = 1e-5
SSM_WIDTH = 512
SSM_GROUP = 16
SSM_GROUPS = 32
SSM_STATE = 64
N_STATE = SSM_GROUPS * SSM_STATE
HEADS = 8
HEAD_DIM = 64
N_EXPERTS = 32
TOP_K = 4
D_FF = 1024
SWIGLU_ALPHA = 1.702
SWIGLU_LIMIT = 7.0

LANES = 128
SUBLANES = 8
MXU_DIM = 256
VMEM_LIMIT = 56 * 1024 * 1024

ROW_TILE = 512
SSM_CHUNK = 128
SCAN_LANES = 512
ATTN_BLOCK = 256
ATTN_HEADS_PER_STEP = 8
EXPERT_BLOCK = 256
SC_WINDOW = 128
PACK_W = D_MODEL // 2
PACK_HALF = PACK_W // 2

_NEG = -1e30
Q_SCALE = math.log2(math.e) / math.sqrt(HEAD_DIM)


def _cparams(sem):
    return pltpu.CompilerParams(dimension_semantics=sem, vmem_limit_bytes=VMEM_LIMIT)


def _full(shape):
    nd = len(shape)
    return pl.BlockSpec(shape, lambda *_: (0,) * nd)


def _inproj_kernel(x_ref, g1_ref, w_ref, wvt_ref, qg_ref, kg_ref, seg_ref,
                   u_ref, q_ref, k_ref, vt_ref, gs_ref, ga_ref):
    x = x_ref[...]
    ms = jnp.mean(x * x, axis=-1, keepdims=True)
    h = (x * lax.rsqrt(ms + NORM_EPS) * g1_ref[...]).astype(BF16)

    def proj(c0, width):
        return jnp.dot(h, w_ref[:, c0:c0 + width], preferred_element_type=F32)

    u_ref[...] = proj(0, SSM_WIDTH)
    seg = seg_ref[...]
    q0 = SSM_WIDTH
    k0 = q0 + D_MODEL
    gs0 = k0 + D_MODEL
    ga0 = gs0 + D_MODEL
    for base, gain_ref, out_ref, scale in ((q0, qg_ref, q_ref, Q_SCALE), (k0, kg_ref, k_ref, 1.0)):
        for c in range(D_MODEL // MXU_DIM):
            lo = c * MXU_DIM
            y = proj(base + lo, MXU_DIM)
            ss = jnp.dot((y * y).astype(BF16), seg, preferred_element_type=F32)
            yn = y * lax.rsqrt(ss * (1.0 / HEAD_DIM) + NORM_EPS) * gain_ref[:, lo:lo + MXU_DIM]
            out_ref[:, lo:lo + MXU_DIM] = (yn * scale).astype(BF16)
    vt_ref[...] = lax.dot_general(wvt_ref[...], h, (((1,), (1,)), ((), ())),
                                  preferred_element_type=F32).astype(BF16)
    half = D_MODEL // 2
    for c in range(2):
        lo = c * half
        gs_ref[:, lo:lo + half] = jax.nn.sigmoid(proj(gs0 + lo, half)).astype(BF16)
        ga_ref[:, lo:lo + half] = jax.nn.sigmoid(proj(ga0 + lo, half)).astype(BF16)


def _in_proj(x2d, gain1, w_in, q_gain, k_gain):
    n_tok = x2d.shape[0]
    tm = min(ROW_TILE, n_tok)
    v0 = SSM_WIDTH + 2 * D_MODEL
    w_bf = w_in.astype(BF16)
    w_main = jnp.concatenate([w_bf[:, :v0], w_bf[:, v0 + D_MODEL:]], axis=1)
    w_vt = w_bf[:, v0:v0 + D_MODEL].T
    in_width = w_main.shape[1]
    seg = (jnp.arange(MXU_DIM)[:, None] // HEAD_DIM == jnp.arange(MXU_DIM)[None, :] // HEAD_DIM).astype(BF16)
    reps = D_MODEL // HEAD_DIM
    qg = jnp.tile(q_gain.astype(F32), reps)[None, :]
    kg = jnp.tile(k_gain.astype(F32), reps)[None, :]
    row = lambda w: pl.BlockSpec((tm, w), lambda i: (i, 0))
    tok = jax.ShapeDtypeStruct((n_tok, D_MODEL), BF16)
    out_shapes = [jax.ShapeDtypeStruct((n_tok, SSM_WIDTH), F32), tok, tok,
                  jax.ShapeDtypeStruct((D_MODEL, n_tok), BF16), tok, tok]
    vt_spec = pl.BlockSpec((D_MODEL, tm), lambda i: (0, i))
    return pl.pallas_call(
        _inproj_kernel,
        grid=(n_tok // tm,),
        in_specs=[row(D_MODEL), _full((1, D_MODEL)), _full((D_MODEL, in_width)), _full((D_MODEL, D_MODEL)),
                  _full((1, D_MODEL)), _full((1, D_MODEL)), _full((MXU_DIM, MXU_DIM))],
        out_specs=[row(SSM_WIDTH), row(D_MODEL), row(D_MODEL), vt_spec, row(D_MODEL), row(D_MODEL)],
        out_shape=out_shapes,
        compiler_params=_cparams(("parallel",)),
        name="in_proj",
    )(x2d, gain1, w_main, w_vt, qg, kg, seg)


def _ssm_kernel(u_ref, bblk_ref, a_ref, cblk_ref, d_ref, wglu_ref, bglu_ref, o_ref, bu_ref, st_ref):
    chunk = u_ref.shape[0] // SUBLANES

    @pl.when(pl.program_id(0) == 0)
    def _():
        st_ref[...] = jnp.zeros_like(st_ref)

    u = u_ref[...]
    bu_ref[...] = jnp.dot(u.astype(BF16), bblk_ref[...], preferred_element_type=F32)

    for j in range(N_STATE // SCAN_LANES):
        re = slice(j * SCAN_LANES, (j + 1) * SCAN_LANES)
        im = slice(N_STATE + j * SCAN_LANES, N_STATE + (j + 1) * SCAN_LANES)
        ar = a_ref[:, re]
        ai = a_ref[:, im]

        def step(t, carry, re=re, im=im, ar=ar, ai=ai):
            xr, xi = carry
            rows = pl.ds(pl.multiple_of(t * SUBLANES, SUBLANES), SUBLANES)
            nr = ar * xr - ai * xi + bu_ref[rows, re]
            ni = ar * xi + ai * xr + bu_ref[rows, im]
            bu_ref[rows, re] = nr
            bu_ref[rows, im] = ni
            return nr, ni

        xr, xi = lax.fori_loop(0, chunk, step, (st_ref[:, re], st_ref[:, im]), unroll=4)
        st_ref[:, re] = xr
        st_ref[:, im] = xi

    y = jnp.dot(bu_ref[...].astype(BF16), cblk_ref[...], preferred_element_type=F32) + d_ref[...] * u
    z = jax.nn.gelu(y)
    gate = jax.nn.sigmoid(jnp.dot(z.astype(BF16), wglu_ref[...], preferred_element_type=F32) + bglu_ref[...])
    o_ref[...] = (z * gate).astype(BF16)


def _ssm_params(lambda_re, lambda_im, log_dt, b_re, b_im, c_re, c_im):
    dt = jnp.exp(log_dt.astype(F32))[:, None]
    lr = jnp.minimum(lambda_re.astype(F32), -1e-4)
    li = lambda_im.astype(F32)
    mag = jnp.exp(lr * dt)
    abar_re = mag * jnp.cos(li * dt)
    abar_im = mag * jnp.sin(li * dt)
    den = lr * lr + li * li
    nr = abar_re - 1.0
    coef_re = (nr * lr + abar_im * li) / den
    coef_im = (abar_im * lr - nr * li) / den
    br = b_re.astype(F32)
    bi = b_im.astype(F32)
    bbar_re = coef_re[..., None] * br - coef_im[..., None] * bi
    bbar_im = coef_re[..., None] * bi + coef_im[..., None] * br
    eye = jnp.eye(SSM_GROUPS, dtype=F32)

    def expand_b(b):
        return jnp.einsum('gph,gk->ghkp', b, eye).reshape(SSM_WIDTH, N_STATE)

    def expand_c(c):
        return jnp.einsum('ghp,gk->gpkh', c, eye).reshape(N_STATE, SSM_WIDTH)

    bblk = jnp.concatenate([expand_b(bbar_re), expand_b(bbar_im)], axis=1).astype(BF16)
    cblk = jnp.concatenate([expand_c(c_re.astype(F32)), -expand_c(c_im.astype(F32))], axis=0).astype(BF16)
    a_row = jnp.concatenate([abar_re.reshape(-1), abar_im.reshape(-1)])[None, :]
    return bblk, cblk, jnp.broadcast_to(a_row, (SUBLANES, 2 * N_STATE))


def _ssm(u_tb, seq, bblk, cblk, a_tile, d_skip, w_glu_bf, b_glu):
    chunk = min(SSM_CHUNK, seq)
    rows = chunk * SUBLANES
    return pl.pallas_call(
        _ssm_kernel,
        grid=(seq // chunk,),
        in_specs=[pl.BlockSpec((rows, SSM_WIDTH), lambda c: (c, 0)),
                  _full(bblk.shape), _full(a_tile.shape), _full(cblk.shape),
                  _full((1, SSM_WIDTH)), _full(w_glu_bf.shape), _full((1, SSM_WIDTH))],
        out_specs=pl.BlockSpec((rows, SSM_WIDTH), lambda c: (c, 0)),
        out_shape=jax.ShapeDtypeStruct(u_tb.shape, BF16),
        scratch_shapes=[pltpu.VMEM((rows, 2 * N_STATE), F32), pltpu.VMEM((SUBLANES, 2 * N_STATE), F32)],
        compiler_params=_cparams(("arbitrary",)),
        name="ssm",
    )(u_tb, bblk, a_tile, cblk, d_skip, w_glu_bf, b_glu)


def _attn_kernel(lam_ref, sg_ref, q_ref, k_ref, vt_ref, o_ref, acc_ref, *, out_scale, blk, nh):
    seq = q_ref.shape[1]
    hw = 2 * HEAD_DIM
    lane = lax.broadcasted_iota(jnp.int32, (blk, hw), 1)
    key_i = lax.broadcasted_iota(jnp.int32, (blk, blk), 0)
    qry_i = lax.broadcasted_iota(jnp.int32, (blk, blk), 1)
    keep = key_i <= qry_i
    contract_last = (((1,), (1,)), ((), ()))
    lam = lam_ref[0]
    n_chain = 2 * nh

    def q_block(qi, _):
        qrows = pl.ds(pl.multiple_of(qi * blk, blk), blk)
        qs = []
        for hh in range(nh):
            q = q_ref[0, qrows, hh * hw:(hh + 1) * hw]
            zero = jnp.zeros_like(q)
            qs += [jnp.where(lane < HEAD_DIM, q, zero), jnp.where(lane >= HEAD_DIM, q, zero)]
        acc_ref[...] = jnp.zeros_like(acc_ref)

        def kv_block(kb, carry, masked):
            krows = pl.ds(pl.multiple_of(kb * blk, blk), blk)
            scores = []
            for c in range(n_chain):
                hh = c // 2
                k = k_ref[0, krows, hh * hw:(hh + 1) * hw]
                scores.append(lax.dot_general(k, qs[c], contract_last, preferred_element_type=F32))
            out, probs, alphas = [], [], []
            for c in range(n_chain):
                m, l = carry[2 * c], carry[2 * c + 1]
                s = jnp.where(keep, scores[c], _NEG) if masked else scores[c]
                m_new = jnp.maximum(m, jnp.max(s, axis=0, keepdims=True))
                p = jnp.exp2(s - m_new)
                alpha = jnp.exp2(m - m_new)
                out += [m_new, alpha * l + jnp.sum(p, axis=0, keepdims=True)]
                probs.append(p.astype(BF16))
                alphas.append(alpha)
            for c in range(n_chain):
                hh = c // 2
                vt = vt_ref[hh * hw:(hh + 1) * hw, krows]
                acc_ref[c] = alphas[c] * acc_ref[c] + jnp.dot(vt, probs[c], preferred_element_type=F32)
            return tuple(out)

        carry = (jnp.full((1, blk), _NEG, F32), jnp.zeros((1, blk), F32)) * n_chain
        carry = lax.fori_loop(0, qi, lambda kb, c: kv_block(kb, c, False), carry)
        carry = kv_block(qi, carry, True)
        for hh in range(nh):
            l1, l2 = carry[4 * hh + 1], carry[4 * hh + 3]
            ot = acc_ref[2 * hh] * (1.0 / l1) - acc_ref[2 * hh + 1] * (lam / l2)
            ot = ot * lax.rsqrt(jnp.mean(ot * ot, axis=0, keepdims=True) + NORM_EPS)
            o_ref[0, qrows, hh * hw:(hh + 1) * hw] = (ot.T * sg_ref[...] * out_scale).astype(BF16)
        return 0

    lax.fori_loop(0, seq // blk, q_block, 0)


def _attention(q, k, vt, lam, subln_gain, lambda_init, bsz, seq):
    blk = min(ATTN_BLOCK, seq)
    nh = ATTN_HEADS_PER_STEP
    hw = 2 * HEAD_DIM
    q3, k3 = (a.reshape(bsz, seq, D_MODEL) for a in (q, k))
    tok_spec = pl.BlockSpec((1, seq, nh * hw), lambda b, h: (b, 0, h))
    out = pl.pallas_call(
        functools.partial(_attn_kernel, out_scale=1.0 - lambda_init, blk=blk, nh=nh),
        grid=(bsz, HEADS // nh),
        in_specs=[pl.BlockSpec(memory_space=pltpu.SMEM), _full((1, hw)), tok_spec, tok_spec,
                  pl.BlockSpec((nh * hw, seq), lambda b, h: (h, b))],
        out_specs=tok_spec,
        out_shape=jax.ShapeDtypeStruct((bsz, seq, D_MODEL), BF16),
        scratch_shapes=[pltpu.VMEM((2 * nh, hw, blk), F32)],
        compiler_params=_cparams(("parallel", "parallel")),
        name="diff_attn",
    )(lam, subln_gain.astype(F32)[None, :], q3, k3, vt)
    return out.reshape(bsz * seq, D_MODEL)


def _pack_rows(y):
    bits = lax.bitcast_convert_type(y.astype(BF16).astype(F32), jnp.uint32)
    return (bits[:, :PACK_W] >> 16) | (bits[:, PACK_W:] & jnp.uint32(0xFFFF0000))


def _unpack_rows(w):
    lo = lax.bitcast_convert_type(w << 16, F32)
    hi = lax.bitcast_convert_type(w & jnp.uint32(0xFFFF0000), F32)
    return jnp.concatenate([lo, hi], axis=1)


def _merge_kernel(so_ref, ao_ref, gs_ref, ga_ref, x_ref, wps_ref, wpa_ref, wo_ref, g2_ref,
                  wrh_ref, wrl_ref, br_ref, tri_ref,
                  x1_ref, hlo_ref, hhi_ref, route_ref, gate_ref, cnt_ref, run_ref, *, region):
    @pl.when(pl.program_id(0) == 0)
    def _():
        run_ref[...] = jnp.zeros_like(run_ref)

    ps = jnp.dot(so_ref[...], wps_ref[...], preferred_element_type=F32)
    pa = jnp.dot(ao_ref[...], wpa_ref[...], preferred_element_type=F32)
    merged = gs_ref[...].astype(F32) * ps + ga_ref[...].astype(F32) * pa
    x1 = x_ref[...] + jnp.dot(merged.astype(BF16), wo_ref[...], preferred_element_type=F32)
    x1_ref[...] = x1
    h2 = x1 * lax.rsqrt(jnp.mean(x1 * x1, axis=-1, keepdims=True) + NORM_EPS) * g2_ref[...]
    words = _pack_rows(h2)
    hlo_ref[...] = words[:, :PACK_HALF]
    hhi_ref[...] = words[:, PACK_HALF:]

    h_hi = h2.astype(BF16)
    h_lo = (h2 - h_hi.astype(F32)).astype(BF16)
    logits = (jnp.dot(h_hi, wrh_ref[...], preferred_element_type=F32)
              + jnp.dot(h_lo, wrh_ref[...], preferred_element_type=F32)
              + jnp.dot(h_hi, wrl_ref[...], preferred_element_type=F32)) + br_ref[...]

    tm = logits.shape[0]
    lane = lax.broadcasted_iota(jnp.int32, (tm, LANES), 1)
    lane_f = lane.astype(F32)
    work = logits
    onehots, vals, ids = [], [], []
    for _ in range(TOP_K):
        m = jnp.max(work, axis=-1, keepdims=True)
        idx = jnp.min(jnp.where(work == m, lane_f, float(LANES)), axis=-1, keepdims=True)
        oh = lane_f == idx
        onehots.append(oh)
        vals.append(m)
        ids.append(idx.astype(jnp.int32))
        work = jnp.where(oh, -jnp.inf, work)
    exps = [jnp.exp(v - vals[0]) for v in vals]
    den = exps[0] + exps[1] + exps[2] + exps[3]

    multi = (onehots[0] | onehots[1] | onehots[2] | onehots[3]).astype(F32)
    before = jnp.dot(tri_ref[...], multi.astype(BF16), preferred_element_type=F32) + run_ref[...]
    route = jnp.zeros((tm, LANES), jnp.int32)
    gates = jnp.zeros((tm, LANES), F32)
    for kk in range(TOP_K):
        rank = jnp.sum(jnp.where(onehots[kk], before, 0.0), axis=-1, keepdims=True).astype(jnp.int32)
        route = jnp.where(lane == kk, ids[kk] * region + rank, route)
        gates = jnp.where(lane == kk, exps[kk] / den, gates)
    route_ref[...] = route
    gate_ref[...] = gates
    run = run_ref[...] + jnp.sum(multi, axis=0, keepdims=True)
    run_ref[...] = run
    cnt_ref[...] = run.astype(jnp.int32)


def _merge_route(so, ao, gs, ga, x2d, wps, wpa, wo, gain2, w_router, b_router, region):
    n_tok = x2d.shape[0]
    tm = min(ROW_TILE, n_tok)
    wr = jnp.zeros((D_MODEL, LANES), F32).at[:, :N_EXPERTS].set(w_router.astype(F32))
    wr_hi = wr.astype(BF16)
    wr_lo = (wr - wr_hi.astype(F32)).astype(BF16)
    br = jnp.full((1, LANES), -jnp.inf, F32).at[0, :N_EXPERTS].set(b_router.astype(F32))
    tri = (jnp.arange(tm)[:, None] > jnp.arange(tm)[None, :]).astype(BF16)
    row = lambda w: pl.BlockSpec((tm, w), lambda i: (i, 0))
    out_shapes = [jax.ShapeDtypeStruct((n_tok, D_MODEL), F32),
                  jax.ShapeDtypeStruct((n_tok, PACK_HALF), jnp.uint32),
                  jax.ShapeDtypeStruct((n_tok, PACK_HALF), jnp.uint32),
                  jax.ShapeDtypeStruct((n_tok, LANES), jnp.int32),
                  jax.ShapeDtypeStruct((n_tok, LANES), F32),
                  jax.ShapeDtypeStruct((1, LANES), jnp.int32)]
    return pl.pallas_call(
        functools.partial(_merge_kernel, region=region),
        grid=(n_tok // tm,),
        in_specs=[row(SSM_WIDTH), row(D_MODEL), row(D_MODEL), row(D_MODEL), row(D_MODEL),
                  _full(wps.shape), _full(wpa.shape), _full(wo.shape), _full((1, D_MODEL)),
                  _full(wr_hi.shape), _full(wr_lo.shape), _full((1, LANES)), _full((tm, tm))],
        out_specs=[row(D_MODEL), row(PACK_HALF), row(PACK_HALF), row(LANES), row(LANES), _full((1, LANES))],
        out_shape=out_shapes,
        scratch_shapes=[pltpu.VMEM((1, LANES), F32)],
        compiler_params=_cparams(("arbitrary",)),
        name="merge_route",
    )(so, ao, gs, ga, x2d, wps, wpa, wo, gain2, wr_hi, wr_lo, br, tri)


def _sc_scatter_rows(rows, dest, n_slots):
    n_tok, width = rows.shape
    mesh = plsc.VectorSubcoreMesh(core_axis_name="core", subcore_axis_name="subcore")

    @pl.kernel(out_type=jax.ShapeDtypeStruct((n_slots, width), rows.dtype), mesh=mesh, scratch_types=[])
    def scatter(rows_hbm, dest_hbm, out_hbm):
        def body(rows_vmem, dest_vmem):
            pltpu.sync_copy(rows_vmem, out_hbm.at[dest_vmem.at[0]])

        pltpu.emit_pipeline(
            body,
            grid=(TOP_K, n_tok // SC_WINDOW),
            in_specs=[pl.BlockSpec((SC_WINDOW, width), lambda k, i: (i, 0)),
                      pl.BlockSpec((1, SC_WINDOW), lambda k, i: (k, i))],
            out_specs=[],
            core_axis_name=("core", "subcore"),
            dimension_semantics=(pltpu.PARALLEL, pltpu.PARALLEL),
        )(rows_hbm, dest_hbm)

    return scatter(rows, dest)


def _sc_gather_rows(table, idx):
    n = idx.shape[1]
    width = table.shape[1]
    mesh = plsc.VectorSubcoreMesh(core_axis_name="core", subcore_axis_name="subcore")

    @pl.kernel(out_type=jax.ShapeDtypeStruct((n, width), table.dtype), mesh=mesh, scratch_types=[])
    def gather(table_hbm, idx_hbm, out_hbm):
        def body(idx_vmem, out_vmem):
            pltpu.sync_copy(table_hbm.at[idx_vmem.at[0]], out_vmem)

        pltpu.emit_pipeline(
            body,
            grid=(n // SC_WINDOW,),
            in_specs=[pl.BlockSpec((1, SC_WINDOW), lambda i: (0, i))],
            out_specs=[pl.BlockSpec((SC_WINDOW, width), lambda i: (i, 0))],
            core_axis_name=("core", "subcore"),
            dimension_semantics=(pltpu.PARALLEL,),
        )(idx_hbm, out_hbm)

    return gather(table, idx)


def _expert_kernel(blk_e_ref, blk_row_ref, nvalid_ref, xlo_ref, xhi_ref, w1_ref, b1_ref, w2_ref, b2_ref,
                   perm_ref, ylo_ref, yhi_ref, w1p_ref, w2b_ref, act_ref):
    i = pl.program_id(0)
    e = blk_e_ref[i]
    e_prev = blk_e_ref[jnp.maximum(i - 1, 0)]
    n_chunks = (2 * D_FF) // MXU_DIM

    @pl.when((i == 0) | (e != e_prev))
    def _():
        perm = perm_ref[...]
        for c in range(n_chunks):
            cols = slice(c * MXU_DIM, (c + 1) * MXU_DIM)
            w1p_ref[:, cols] = jnp.dot(w1_ref[0, :, cols].astype(BF16), perm,
                                       preferred_element_type=F32).astype(BF16)
        w2b_ref[...] = w2_ref[0].astype(BF16)

    @pl.when(i < nvalid_ref[0])
    def _():
        words = jnp.concatenate([xlo_ref[...], xhi_ref[...]], axis=1)
        x = _unpack_rows(words).astype(BF16)
        for c in range(n_chunks):
            cols = slice(c * MXU_DIM, (c + 1) * MXU_DIM)
            h = jnp.dot(x, w1p_ref[:, cols], preferred_element_type=F32) + b1_ref[0, :, cols]
            gate = jnp.minimum(h[:, :LANES], SWIGLU_LIMIT)
            up = jnp.clip(h[:, LANES:], -SWIGLU_LIMIT, SWIGLU_LIMIT)
            glu = gate * jax.nn.sigmoid(SWIGLU_ALPHA * gate)
            act_ref[:, c * LANES:(c + 1) * LANES] = ((up + 1.0) * glu).astype(BF16)
        y = jnp.dot(act_ref[...], w2b_ref[...], preferred_element_type=F32) + b2_ref[0]
        words = _pack_rows(y)
        ylo_ref[...] = words[:, :PACK_HALF]
        yhi_ref[...] = words[:, PACK_HALF:]


def _gate_up_order():
    j = jnp.arange(MXU_DIM)
    within = jnp.where(j < LANES, 2 * j, 2 * (j - LANES) + 1)
    return within


def _experts(xs_lo, xs_hi, w1, b1, w2, b2, blk_e, blk_row, nvalid, n_blocks):
    within = _gate_up_order()
    perm = (jnp.arange(MXU_DIM)[:, None] == within[None, :]).astype(BF16)
    order = (jnp.arange(0, 2 * D_FF, MXU_DIM)[:, None] + within[None, :]).reshape(-1)
    b1p = b1.astype(F32)[:, order][:, None, :]
    b2r = b2.astype(F32)[:, None, :]
    tb = EXPERT_BLOCK
    xspec = pl.BlockSpec((tb, PACK_HALF), lambda i, be, br, nv: (br[i], 0))
    wspec = lambda shape: pl.BlockSpec((1,) + shape, lambda i, be, br, nv: (be[i], 0, 0))
    grid_spec = pltpu.PrefetchScalarGridSpec(
        num_scalar_prefetch=3,
        grid=(n_blocks,),
        in_specs=[xspec, xspec, wspec((D_MODEL, 2 * D_FF)), wspec((1, 2 * D_FF)),
                  wspec((D_FF, D_MODEL)), wspec((1, D_MODEL)),
                  pl.BlockSpec((MXU_DIM, MXU_DIM), lambda i, be, br, nv: (0, 0))],
        out_specs=[xspec, xspec],
        scratch_shapes=[pltpu.VMEM((D_MODEL, 2 * D_FF), BF16), pltpu.VMEM((D_FF, D_MODEL), BF16),
                        pltpu.VMEM((tb, D_FF), BF16)],
    )
    return pl.pallas_call(
        _expert_kernel,
        grid_spec=grid_spec,
        out_shape=[jax.ShapeDtypeStruct(xs_lo.shape, jnp.uint32)] * 2,
        compiler_params=_cparams(("arbitrary",)),
        name="experts",
    )(blk_e, blk_row, nvalid, xs_lo, xs_hi, w1, b1p, w2, b2r, perm)


def _combine_kernel(x1_ref, gate_ref, *refs):
    lo_refs, hi_refs, o_ref = refs[:TOP_K], refs[TOP_K:2 * TOP_K], refs[2 * TOP_K]
    acc = x1_ref[...]
    gates = gate_ref[...]
    for kk in range(TOP_K):
        words = jnp.concatenate([lo_refs[kk][...], hi_refs[kk][...]], axis=1)
        acc = acc + gates[:, kk:kk + 1] * _unpack_rows(words)
    o_ref[...] = acc


def _combine(x1, gates, yg_lo, yg_hi):
    n_tok = x1.shape[0]
    tm = min(ROW_TILE, n_tok)
    nblk = n_tok // tm
    row = lambda w: pl.BlockSpec((tm, w), lambda i: (i, 0))
    plane = lambda kk: pl.BlockSpec((tm, PACK_HALF), lambda i, kk=kk: (kk * nblk + i, 0))
    planes = [plane(kk) for kk in range(TOP_K)]
    return pl.pallas_call(
        _combine_kernel,
        grid=(nblk,),
        in_specs=[row(D_MODEL), row(LANES)] + planes + planes,
        out_specs=row(D_MODEL),
        out_shape=jax.ShapeDtypeStruct((n_tok, D_MODEL), F32),
        compiler_params=_cparams(("parallel",)),
        name="combine",
    )(x1, gates, *([yg_lo] * TOP_K), *([yg_hi] * TOP_K))


def _expert_blocks(counts, region, n_blocks):
    nb_e = (counts + EXPERT_BLOCK - 1) // EXPERT_BLOCK
    cum = jnp.cumsum(nb_e)
    total = cum[-1]
    i = jnp.arange(n_blocks, dtype=jnp.int32)
    i_eff = jnp.minimum(i, total - 1)
    e_i = jnp.sum(cum[None, :] <= i_eff[:, None], axis=1).astype(jnp.int32)
    j_i = i_eff - (cum[e_i] - nb_e[e_i])
    blk_row = e_i * (region // EXPERT_BLOCK) + j_i
    return e_i, blk_row.astype(jnp.int32), total.reshape(1).astype(jnp.int32)


def kernel(x, norm1_gain, w_in, lambda_re, lambda_im, log_dt, ssm_b_re, ssm_b_im, ssm_c_re, ssm_c_im, ssm_d, w_glu, b_glu, q_norm_gain, k_norm_gain, lambda_q1, lambda_k1, lambda_q2, lambda_k2, subln_gain, w_proj_ssm, w_proj_attn, w_out, norm2_gain, w_router, b_router, w_exp1, b_exp1, w_exp2, b_exp2):
    bsz, seq, d = x.shape
    n_tok = bsz * seq
    depth = norm1_gain.shape[0]
    row1 = lambda a: a.astype(F32).reshape(1, -1)
    for l in range(depth):
        lambda_init = 0.8 - 0.6 * math.exp(-0.3 * l)
        x2d = x.reshape(n_tok, d)

        u, q, k, vt, gs, ga = _in_proj(x2d, row1(norm1_gain[l]), w_in[l], q_norm_gain[l], k_norm_gain[l])

        bblk, cblk, a_tile = _ssm_params(lambda_re[l], lambda_im[l], log_dt[l], ssm_b_re[l], ssm_b_im[l],
                                         ssm_c_re[l], ssm_c_im[l])
        u_tb = u.reshape(bsz, seq, SSM_WIDTH).transpose(1, 0, 2).reshape(n_tok, SSM_WIDTH)
        so_tb = _ssm(u_tb, seq, bblk, cblk, a_tile, row1(ssm_d[l]), w_glu[l].astype(BF16), row1(b_glu[l]))
        so = so_tb.reshape(seq, bsz, SSM_WIDTH).transpose(1, 0, 2).reshape(n_tok, SSM_WIDTH)

        lam = (jnp.exp(jnp.sum(lambda_q1[l].astype(F32) * lambda_k1[l].astype(F32)))
               - jnp.exp(jnp.sum(lambda_q2[l].astype(F32) * lambda_k2[l].astype(F32)))
               + lambda_init).reshape(1)
        ao = _attention(q, k, vt, lam, subln_gain[l], lambda_init, bsz, seq)

        region = n_tok
        x1, h_lo, h_hi, route, gates, counts = _merge_route(
            so, ao, gs, ga, x2d, w_proj_ssm[l].astype(BF16), w_proj_attn[l].astype(BF16),
            w_out[l].astype(BF16), row1(norm2_gain[l]), w_router[l], b_router[l], region)

        dest = route[:, :TOP_K].T
        n_slots = N_EXPERTS * region
        xs_lo = _sc_scatter_rows(h_lo, dest, n_slots)
        xs_hi = _sc_scatter_rows(h_hi, dest, n_slots)

        n_blocks = n_tok * TOP_K // EXPERT_BLOCK + N_EXPERTS
        blk_e, blk_row, nvalid = _expert_blocks(counts[0, :N_EXPERTS], region, n_blocks)
        ys_lo, ys_hi = _experts(xs_lo, xs_hi, w_exp1[l], b_exp1[l], w_exp2[l], b_exp2[l],
                                blk_e, blk_row, nvalid, n_blocks)

        flat = dest.reshape(1, TOP_K * n_tok)
        yg_lo = _sc_gather_rows(ys_lo, flat)
        yg_hi = _sc_gather_rows(ys_hi, flat)
        x = _combine(x1, gates, yg_lo, yg_hi).reshape(bsz, seq, d)
    return x
```

```python
import functools
import math

import jax
import jax.numpy as jnp
from jax import lax
from jax.experimental import pallas as pl
from jax.experimental.pallas import tpu as pltpu
from jax.experimental.pallas import tpu_sc as plsc

F32 = jnp.float32
BF16 = jnp.bfloat16

D_MODEL = 1024
NORM_EPS = 1e-5
SSM_WIDTH = 512
SSM_GROUP = 16
SSM_GROUPS = 32
SSM_STATE = 64
N_STATE = SSM_GROUPS * SSM_STATE
HEADS = 8
HEAD_DIM = 64
N_EXPERTS = 32
TOP_K = 4
D_FF = 1024
SWIGLU_ALPHA = 1.702
SWIGLU_LIMIT = 7.0

LANES = 128
SUBLANES = 8
MXU_DIM = 256
VMEM_LIMIT = 56 * 1024 * 1024

ROW_TILE = 512
SSM_CHUNK = 128
SCAN_LANES = 512
ATTN_BLOCK = 256
ATTN_HEADS_PER_STEP = 8
EXPERT_BLOCK = 256
SC_WINDOW = 128
PACK_W = D_MODEL // 2
PACK_HALF = PACK_W // 2

_NEG = -1e30
Q_SCALE = math.log2(math.e) / math.sqrt(HEAD_DIM)


def _cparams(sem):
    return pltpu.CompilerParams(dimension_semantics=sem, vmem_limit_bytes=VMEM_LIMIT)


def _full(shape):
    nd = len(shape)
    return pl.BlockSpec(shape, lambda *_: (0,) * nd)


def _run_lookahead(tasks):
    pending = tasks[0][0]()
    for i, (_, epilogue) in enumerate(tasks):
        result = pending
        if i + 1 < len(tasks):
            pending = tasks[i + 1][0]()
        epilogue(result)


def _inproj_kernel(x_ref, g1_ref, w_ref, wvt_ref, qg_ref, kg_ref, seg_ref,
                   u_ref, q_ref, k_ref, vt_ref, gs_ref, ga_ref):
    x = x_ref[...]
    ms = jnp.mean(x * x, axis=-1, keepdims=True)
    h = (x * lax.rsqrt(ms + NORM_EPS) * g1_ref[...]).astype(BF16)

    def proj(c0, width):
        return lambda: jnp.dot(h, w_ref[:, c0:c0 + width], preferred_element_type=F32)

    seg = seg_ref[...]
    q0 = SSM_WIDTH
    k0 = q0 + D_MODEL
    gs0 = k0 + D_MODEL
    ga0 = gs0 + D_MODEL
    half = D_MODEL // 2
    tasks = []

    def store(out_ref, cols, fn):
        def epilogue(r):
            out_ref[:, cols] = fn(r)
        return epilogue

    tasks.append((proj(0, SSM_WIDTH), store(u_ref, slice(None), lambda r: r)))

    def head_norm_tasks(base, gain_ref, out_ref, scale, c):
        cols = slice(c * MXU_DIM, (c + 1) * MXU_DIM)
        kept = {}

        def after_proj(y):
            kept["y"] = y
            kept["sq"] = (y * y).astype(BF16)

        def after_sum(ss):
            yn = kept["y"] * lax.rsqrt(ss * (1.0 / HEAD_DIM) + NORM_EPS) * gain_ref[:, cols]
            out_ref[:, cols] = (yn * scale).astype(BF16)

        return ((proj(base + c * MXU_DIM, MXU_DIM), after_proj),
                (lambda: jnp.dot(kept["sq"], seg, preferred_element_type=F32), after_sum))

    pairs = [head_norm_tasks(base, gain_ref, out_ref, scale, c)
             for base, gain_ref, out_ref, scale in ((q0, qg_ref, q_ref, Q_SCALE), (k0, kg_ref, k_ref, 1.0))
             for c in range(D_MODEL // MXU_DIM)]
    tasks.append(pairs[0][0])
    for prev, cur in zip(pairs, pairs[1:]):
        tasks += [cur[0], prev[1]]
    tasks.append(pairs[-1][1])

    to_gate = lambda r: jax.nn.sigmoid(r).astype(BF16)
    for c in range(2):
        cols = slice(c * half, (c + 1) * half)
        tasks.append((proj(gs0 + c * half, half), store(gs_ref, cols, to_gate)))
        tasks.append((proj(ga0 + c * half, half), store(ga_ref, cols, to_gate)))

    def vt_task(r0):
        def matmul():
            return lax.dot_general(wvt_ref[r0:r0 + MXU_DIM, :], h, (((1,), (1,)), ((), ())),
                                   preferred_element_type=F32)

        def epilogue(r):
            vt_ref[r0:r0 + MXU_DIM, :] = r.astype(BF16)
        return matmul, epilogue

    tasks += [vt_task(r0) for r0 in range(0, D_MODEL, MXU_DIM)]
    _run_lookahead(tasks)


def _in_proj(x2d, gain1, w_in, q_gain, k_gain):
    n_tok = x2d.shape[0]
    tm = min(ROW_TILE, n_tok)
    v0 = SSM_WIDTH + 2 * D_MODEL
    w_bf = w_in.astype(BF16)
    w_main = jnp.concatenate([w_bf[:, :v0], w_bf[:, v0 + D_MODEL:]], axis=1)
    w_vt = w_bf[:, v0:v0 + D_MODEL].T
    in_width = w_main.shape[1]
    seg = (jnp.arange(MXU_DIM)[:, None] // HEAD_DIM == jnp.arange(MXU_DIM)[None, :] // HEAD_DIM).astype(BF16)
    reps = D_MODEL // HEAD_DIM
    qg = jnp.tile(q_gain.astype(F32), reps)[None, :]
    kg = jnp.tile(k_gain.astype(F32), reps)[None, :]
    row = lambda w: pl.BlockSpec((tm, w), lambda i: (i, 0))
    tok = jax.ShapeDtypeStruct((n_tok, D_MODEL), BF16)
    out_shapes = [jax.ShapeDtypeStruct((n_tok, SSM_WIDTH), F32), tok, tok,
                  jax.ShapeDtypeStruct((D_MODEL, n_tok), BF16), tok, tok]
    vt_spec = pl.BlockSpec((D_MODEL, tm), lambda i: (0, i))
    return pl.pallas_call(
        _inproj_kernel,
        grid=(n_tok // tm,),
        in_specs=[row(D_MODEL), _full((1, D_MODEL)), _full((D_MODEL, in_width)), _full((D_MODEL, D_MODEL)),
                  _full((1, D_MODEL)), _full((1, D_MODEL)), _full((MXU_DIM, MXU_DIM))],
        out_specs=[row(SSM_WIDTH), row(D_MODEL), row(D_MODEL), vt_spec, row(D_MODEL), row(D_MODEL)],
        out_shape=out_shapes,
        compiler_params=_cparams(("parallel",)),
        name="in_proj",
    )(x2d, gain1, w_main, w_vt, qg, kg, seg)


def _ssm_kernel(u_ref, bblk_ref, a_ref, cblk_ref, d_ref, wglu_ref, bglu_ref, o_ref, bu_ref, st_ref):
    chunk = u_ref.shape[0] // SUBLANES

    @pl.when(pl.program_id(0) == 0)
    def _():
        st_ref[...] = jnp.zeros_like(st_ref)

    u = u_ref[...]
    u_bf = u.astype(BF16)
    tiles_per_part = N_STATE // MXU_DIM
    ch_per_tile = SSM_WIDTH // tiles_per_part
    for n in range(2 * tiles_per_part):
        ch0 = ((n % tiles_per_part) * ch_per_tile) // LANES * LANES
        lanes = slice(n * MXU_DIM, (n + 1) * MXU_DIM)
        bu_ref[:, lanes] = jnp.dot(u_bf[:, ch0:ch0 + LANES], bblk_ref[ch0:ch0 + LANES, lanes],
                                   preferred_element_type=F32)

    for j in range(N_STATE // SCAN_LANES):
        re = slice(j * SCAN_LANES, (j + 1) * SCAN_LANES)
        im = slice(N_STATE + j * SCAN_LANES, N_STATE + (j + 1) * SCAN_LANES)
        ar = a_ref[:, re]
        ai = a_ref[:, im]

        def step(t, carry, re=re, im=im, ar=ar, ai=ai):
            xr, xi = carry
            rows = pl.ds(pl.multiple_of(t * SUBLANES, SUBLANES), SUBLANES)
            nr = ar * xr - ai * xi + bu_ref[rows, re]
            ni = ar * xi + ai * xr + bu_ref[rows, im]
            bu_ref[rows, re] = nr
            bu_ref[rows, im] = ni
            return nr, ni

        xr, xi = lax.fori_loop(0, chunk, step, (st_ref[:, re], st_ref[:, im]), unroll=4)
        st_ref[:, re] = xr
        st_ref[:, im] = xi

    n_out = SSM_WIDTH // MXU_DIM
    lanes_per_out = N_STATE // n_out
    ys = []
    for j in range(n_out):
        cols = slice(j * MXU_DIM, (j + 1) * MXU_DIM)
        acc = None
        for part in range(2):
            lanes = slice(part * N_STATE + j * lanes_per_out, part * N_STATE + (j + 1) * lanes_per_out)
            term = jnp.dot(bu_ref[:, lanes].astype(BF16), cblk_ref[lanes, cols], preferred_element_type=F32)
            acc = term if acc is None else acc + term
        ys.append(acc)
    y = jnp.concatenate(ys, axis=1) + d_ref[...] * u
    z = jax.nn.gelu(y)
    gate = jax.nn.sigmoid(jnp.dot(z.astype(BF16), wglu_ref[...], preferred_element_type=F32) + bglu_ref[...])
    o_ref[...] = (z * gate).astype(BF16)


def _ssm_params(lambda_re, lambda_im, log_dt, b_re, b_im, c_re, c_im):
    dt = jnp.exp(log_dt.astype(F32))[:, None]
    lr = jnp.minimum(lambda_re.astype(F32), -1e-4)
    li = lambda_im.astype(F32)
    mag = jnp.exp(lr * dt)
    abar_re = mag * jnp.cos(li * dt)
    abar_im = mag * jnp.sin(li * dt)
    den = lr * lr + li * li
    nr = abar_re - 1.0
    coef_re = (nr * lr + abar_im * li) / den
    coef_im = (abar_im * lr - nr * li) / den
    br = b_re.astype(F32)
    bi = b_im.astype(F32)
    bbar_re = coef_re[..., None] * br - coef_im[..., None] * bi
    bbar_im = coef_re[..., None] * bi + coef_im[..., None] * br
    eye = jnp.eye(SSM_GROUPS, dtype=F32)

    def expand_b(b):
        return jnp.einsum('gph,gk->ghkp', b, eye).reshape(SSM_WIDTH, N_STATE)

    def expand_c(c):
        return jnp.einsum('ghp,gk->gpkh', c, eye).reshape(N_STATE, SSM_WIDTH)

    bblk = jnp.concatenate([expand_b(bbar_re), expand_b(bbar_im)], axis=1).astype(BF16)
    cblk = jnp.concatenate([expand_c(c_re.astype(F32)), -expand_c(c_im.astype(F32))], axis=0).astype(BF16)
    a_row = jnp.concatenate([abar_re.reshape(-1), abar_im.reshape(-1)])[None, :]
    return bblk, cblk, jnp.broadcast_to(a_row, (SUBLANES, 2 * N_STATE))


def _ssm(u_tb, seq, bblk, cblk, a_tile, d_skip, w_glu_bf, b_glu):
    chunk = min(SSM_CHUNK, seq)
    rows = chunk * SUBLANES
    return pl.pallas_call(
        _ssm_kernel,
        grid=(seq // chunk,),
        in_specs=[pl.BlockSpec((rows, SSM_WIDTH), lambda c: (c, 0)),
                  _full(bblk.shape), _full(a_tile.shape), _full(cblk.shape),
                  _full((1, SSM_WIDTH)), _full(w_glu_bf.shape), _full((1, SSM_WIDTH))],
        out_specs=pl.BlockSpec((rows, SSM_WIDTH), lambda c: (c, 0)),
        out_shape=jax.ShapeDtypeStruct(u_tb.shape, BF16),
        scratch_shapes=[pltpu.VMEM((rows, 2 * N_STATE), F32), pltpu.VMEM((SUBLANES, 2 * N_STATE), F32)],
        compiler_params=_cparams(("arbitrary",)),
        name="ssm",
    )(u_tb, bblk, a_tile, cblk, d_skip, w_glu_bf, b_glu)


def _attn_kernel(lam_ref, sg_ref, q_ref, k_ref, vt_ref, o_ref, acc_ref, *, out_scale, blk, nh):
    seq = q_ref.shape[1]
    hw = 2 * HEAD_DIM
    lane = lax.broadcasted_iota(jnp.int32, (blk, hw), 1)
    key_i = lax.broadcasted_iota(jnp.int32, (blk, blk), 0)
    qry_i = lax.broadcasted_iota(jnp.int32, (blk, blk), 1)
    keep = key_i <= qry_i
    contract_last = (((1,), (1,)), ((), ()))
    lam = lam_ref[0]
    n_chain = 2 * nh

    def q_block(qi, _):
        qrows = pl.ds(pl.multiple_of(qi * blk, blk), blk)
        qs = []
        for hh in range(nh):
            q = q_ref[0, qrows, hh * hw:(hh + 1) * hw]
            zero = jnp.zeros_like(q)
            qs += [jnp.where(lane < HEAD_DIM, q, zero), jnp.where(lane >= HEAD_DIM, q, zero)]
        acc_ref[...] = jnp.zeros_like(acc_ref)

        def kv_block(kb, carry, masked):
            krows = pl.ds(pl.multiple_of(kb * blk, blk), blk)
            scores = []
            for c in range(n_chain):
                hh = c // 2
                k = k_ref[0, krows, hh * hw:(hh + 1) * hw]
                scores.append(lax.dot_general(k, qs[c], contract_last, preferred_element_type=F32))
            out, probs, alphas = [], [], []
            for c in range(n_chain):
                m, l = carry[2 * c], carry[2 * c + 1]
                s = jnp.where(keep, scores[c], _NEG) if masked else scores[c]
                m_new = jnp.maximum(m, jnp.max(s, axis=0, keepdims=True))
                p = jnp.exp2(s - m_new)
                alpha = jnp.exp2(m - m_new)
                out += [m_new, alpha * l + jnp.sum(p, axis=0, keepdims=True)]
                probs.append(p.astype(BF16))
                alphas.append(alpha)
            for c in range(n_chain):
                hh = c // 2
                vt = vt_ref[hh * hw:(hh + 1) * hw, krows]
                acc_ref[c] = alphas[c] * acc_ref[c] + jnp.dot(vt, probs[c], preferred_element_type=F32)
            return tuple(out)

        carry = (jnp.full((1, blk), _NEG, F32), jnp.zeros((1, blk), F32)) * n_chain
        carry = lax.fori_loop(0, qi, lambda kb, c: kv_block(kb, c, False), carry)
        carry = kv_block(qi, carry, True)
        for hh in range(nh):
            l1, l2 = carry[4 * hh + 1], carry[4 * hh + 3]
            ot = acc_ref[2 * hh] * (1.0 / l1) - acc_ref[2 * hh + 1] * (lam / l2)
            ot = ot * lax.rsqrt(jnp.mean(ot * ot, axis=0, keepdims=True) + NORM_EPS)
            o_ref[0, qrows, hh * hw:(hh + 1) * hw] = (ot.T * sg_ref[...] * out_scale).astype(BF16)
        return 0

    lax.fori_loop(0, seq // blk, q_block, 0)


def _attention(q, k, vt, lam, subln_gain, lambda_init, bsz, seq):
    blk = min(ATTN_BLOCK, seq)
    nh = ATTN_HEADS_PER_STEP
    hw = 2 * HEAD_DIM
    q3, k3 = (a.reshape(bsz, seq, D_MODEL) for a in (q, k))
    tok_spec = pl.BlockSpec((1, seq, nh * hw), lambda b, h: (b, 0, h))
    out = pl.pallas_call(
        functools.partial(_attn_kernel, out_scale=1.0 - lambda_init, blk=blk, nh=nh),
        grid=(bsz, HEADS // nh),
        in_specs=[pl.BlockSpec(memory_space=pltpu.SMEM), _full((1, hw)), tok_spec, tok_spec,
                  pl.BlockSpec((nh * hw, seq), lambda b, h: (h, b))],
        out_specs=tok_spec,
        out_shape=jax.ShapeDtypeStruct((bsz, seq, D_MODEL), BF16),
        scratch_shapes=[pltpu.VMEM((2 * nh, hw, blk), F32)],
        compiler_params=_cparams(("parallel", "parallel")),
        name="diff_attn",
    )(lam, subln_gain.astype(F32)[None, :], q3, k3, vt)
    return out.reshape(bsz * seq, D_MODEL)


def _pack_rows(y):
    bits = lax.bitcast_convert_type(y.astype(BF16).astype(F32), jnp.uint32)
    return (bits[:, :PACK_W] >> 16) | (bits[:, PACK_W:] & jnp.uint32(0xFFFF0000))


def _unpack_rows(w):
    lo = lax.bitcast_convert_type(w << 16, F32)
    hi = lax.bitcast_convert_type(w & jnp.uint32(0xFFFF0000), F32)
    return jnp.concatenate([lo, hi], axis=1)


def _merge_kernel(so_ref, ao_ref, gs_ref, ga_ref, x_ref, wps_ref, wpa_ref, wo_ref, g2_ref,
                  wrh_ref, wrl_ref, br_ref, tri_ref,
                  x1_ref, hlo_ref, hhi_ref, route_ref, gate_ref, cnt_ref, run_ref, *, region):
    @pl.when(pl.program_id(0) == 0)
    def _():
        run_ref[...] = jnp.zeros_like(run_ref)

    tm = x_ref.shape[0]
    n_part = 2 if tm % (2 * MXU_DIM) == 0 else 1
    rows_per = tm // n_part
    lane = lax.broadcasted_iota(jnp.int32, (rows_per, LANES), 1)
    lane_f = lane.astype(F32)
    parts = [dict(rows=slice(p * rows_per, (p + 1) * rows_per)) for p in range(n_part)]

    def stage_proj(st):
        def matmul():
            return (jnp.dot(so_ref[st["rows"], :], wps_ref[...], preferred_element_type=F32),
                    jnp.dot(ao_ref[st["rows"], :], wpa_ref[...], preferred_element_type=F32))

        def epilogue(r):
            ps, pa = r
            merged = gs_ref[st["rows"], :].astype(F32) * ps + ga_ref[st["rows"], :].astype(F32) * pa
            st["merged"] = merged.astype(BF16)
        return matmul, epilogue

    def stage_out(st):
        def matmul():
            return jnp.dot(st["merged"], wo_ref[...], preferred_element_type=F32)

        def epilogue(r):
            x1 = x_ref[st["rows"], :] + r
            x1_ref[st["rows"], :] = x1
            h2 = x1 * lax.rsqrt(jnp.mean(x1 * x1, axis=-1, keepdims=True) + NORM_EPS) * g2_ref[...]
            words = _pack_rows(h2)
            hlo_ref[st["rows"], :] = words[:, :PACK_HALF]
            hhi_ref[st["rows"], :] = words[:, PACK_HALF:]
            st["h_hi"] = h2.astype(BF16)
            st["h_lo"] = (h2 - st["h_hi"].astype(F32)).astype(BF16)
        return matmul, epilogue

    def stage_router(st):
        def matmul():
            return (jnp.dot(st["h_hi"], wrh_ref[...], preferred_element_type=F32)
                    + jnp.dot(st["h_lo"], wrh_ref[...], preferred_element_type=F32)
                    + jnp.dot(st["h_hi"], wrl_ref[...], preferred_element_type=F32))

        def epilogue(r):
            work = r + br_ref[...]
            onehots, vals, ids = [], [], []
            for _ in range(TOP_K):
                m = jnp.max(work, axis=-1, keepdims=True)
                idx = jnp.min(jnp.where(work == m, lane_f, float(LANES)), axis=-1, keepdims=True)
                oh = lane_f == idx
                onehots.append(oh)
                vals.append(m)
                ids.append(idx.astype(jnp.int32))
                work = jnp.where(oh, -jnp.inf, work)
            exps = [jnp.exp(v - vals[0]) for v in vals]
            den = exps[0] + exps[1] + exps[2] + exps[3]
            st.update(onehots=onehots, ids=ids, gates=[e / den for e in exps],
                      multi=(onehots[0] | onehots[1] | onehots[2] | onehots[3]).astype(F32))
        return matmul, epilogue

    _run_lookahead([stage(st) for stage in (stage_proj, stage_out, stage_router) for st in parts])

    multi = jnp.concatenate([st["multi"] for st in parts], axis=0)
    before = jnp.dot(tri_ref[...], multi.astype(BF16), preferred_element_type=F32) + run_ref[...]
    for st in parts:
        route = jnp.zeros((rows_per, LANES), jnp.int32)
        gates = jnp.zeros((rows_per, LANES), F32)
        for kk in range(TOP_K):
            rank = jnp.sum(jnp.where(st["onehots"][kk], before[st["rows"], :], 0.0), axis=-1, keepdims=True)
            route = jnp.where(lane == kk, st["ids"][kk] * region + rank.astype(jnp.int32), route)
            gates = jnp.where(lane == kk, st["gates"][kk], gates)
        route_ref[st["rows"], :] = route
        gate_ref[st["rows"], :] = gates
    run = run_ref[...] + jnp.sum(multi, axis=0, keepdims=True)
    run_ref[...] = run
    cnt_ref[...] = run.astype(jnp.int32)


def _merge_route(so, ao, gs, ga, x2d, wps, wpa, wo, gain2, w_router, b_router, region):
    n_tok = x2d.shape[0]
    tm = min(ROW_TILE, n_tok)
    wr = jnp.zeros((D_MODEL, LANES), F32).at[:, :N_EXPERTS].set(w_router.astype(F32))
    wr_hi = wr.astype(BF16)
    wr_lo = (wr - wr_hi.astype(F32)).astype(BF16)
    br = jnp.full((1, LANES), -jnp.inf, F32).at[0, :N_EXPERTS].set(b_router.astype(F32))
    tri = (jnp.arange(tm)[:, None] > jnp.arange(tm)[None, :]).astype(BF16)
    row = lambda w: pl.BlockSpec((tm, w), lambda i: (i, 0))
    out_shapes = [jax.ShapeDtypeStruct((n_tok, D_MODEL), F32),
                  jax.ShapeDtypeStruct((n_tok, PACK_HALF), jnp.uint32),
                  jax.ShapeDtypeStruct((n_tok, PACK_HALF), jnp.uint32),
                  jax.ShapeDtypeStruct((n_tok, LANES), jnp.int32),
                  jax.ShapeDtypeStruct((n_tok, LANES), F32),
                  jax.ShapeDtypeStruct((1, LANES), jnp.int32)]
    return pl.pallas_call(
        functools.partial(_merge_kernel, region=region),
        grid=(n_tok // tm,),
        in_specs=[row(SSM_WIDTH), row(D_MODEL), row(D_MODEL), row(D_MODEL), row(D_MODEL),
                  _full(wps.shape), _full(wpa.shape), _full(wo.shape), _full((1, D_MODEL)),
                  _full(wr_hi.shape), _full(wr_lo.shape), _full((1, LANES)), _full((tm, tm))],
        out_specs=[row(D_MODEL), row(PACK_HALF), row(PACK_HALF), row(LANES), row(LANES), _full((1, LANES))],
        out_shape=out_shapes,
        scratch_shapes=[pltpu.VMEM((1, LANES), F32)],
        compiler_params=_cparams(("arbitrary",)),
        name="merge_route",
    )(so, ao, gs, ga, x2d, wps, wpa, wo, gain2, wr_hi, wr_lo, br, tri)


def _sc_scatter_rows(rows, dest, n_slots):
    n_tok, width = rows.shape
    mesh = plsc.VectorSubcoreMesh(core_axis_name="core", subcore_axis_name="subcore")

    @pl.kernel(out_type=jax.ShapeDtypeStruct((n_slots, width), rows.dtype), mesh=mesh, scratch_types=[])
    def scatter(rows_hbm, dest_hbm, out_hbm):
        def body(rows_vmem, dest_vmem):
            pltpu.sync_copy(rows_vmem, out_hbm.at[dest_vmem.at[0]])

        pltpu.emit_pipeline(
            body,
            grid=(TOP_K, n_tok // SC_WINDOW),
            in_specs=[pl.BlockSpec((SC_WINDOW, width), lambda k, i: (i, 0)),
                      pl.BlockSpec((1, SC_WINDOW), lambda k, i: (k, i))],
            out_specs=[],
            core_axis_name=("core", "subcore"),
            dimension_semantics=(pltpu.PARALLEL, pltpu.PARALLEL),
        )(rows_hbm, dest_hbm)

    return scatter(rows, dest)


def _sc_gather_rows(table, idx):
    n = idx.shape[1]
    width = table.shape[1]
    mesh = plsc.VectorSubcoreMesh(core_axis_name="core", subcore_axis_name="subcore")

    @pl.kernel(out_type=jax.ShapeDtypeStruct((n, width), table.dtype), mesh=mesh, scratch_types=[])
    def gather(table_hbm, idx_hbm, out_hbm):
        def body(idx_vmem, out_vmem):
            pltpu.sync_copy(table_hbm.at[idx_vmem.at[0]], out_vmem)

        pltpu.emit_pipeline(
            body,
            grid=(n // SC_WINDOW,),
            in_specs=[pl.BlockSpec((1, SC_WINDOW), lambda i: (0, i))],
            out_specs=[pl.BlockSpec((SC_WINDOW, width), lambda i: (i, 0))],
            core_axis_name=("core", "subcore"),
            dimension_semantics=(pltpu.PARALLEL,),
        )(idx_hbm, out_hbm)

    return gather(table, idx)


def _expert_kernel(blk_e_ref, blk_row_ref, nvalid_ref, xlo_ref, xhi_ref, w1_ref, b1_ref, w2_ref, b2_ref,
                   perm_ref, ylo_ref, yhi_ref, w1p_ref, w2b_ref, act_ref):
    i = pl.program_id(0)
    e = blk_e_ref[i]
    e_prev = blk_e_ref[jnp.maximum(i - 1, 0)]
    n_chunks = (2 * D_FF) // MXU_DIM

    @pl.when((i == 0) | (e != e_prev))
    def _():
        perm = perm_ref[...]
        for c in range(n_chunks):
            cols = slice(c * MXU_DIM, (c + 1) * MXU_DIM)
            w1p_ref[:, cols] = jnp.dot(w1_ref[0, :, cols].astype(BF16), perm,
                                       preferred_element_type=F32).astype(BF16)
        w2b_ref[...] = w2_ref[0].astype(BF16)

    @pl.when(i < nvalid_ref[0])
    def _():
        words = jnp.concatenate([xlo_ref[...], xhi_ref[...]], axis=1)
        x = _unpack_rows(words).astype(BF16)
        for c in range(n_chunks):
            cols = slice(c * MXU_DIM, (c + 1) * MXU_DIM)
            h = jnp.dot(x, w1p_ref[:, cols], preferred_element_type=F32) + b1_ref[0, :, cols]
            gate = jnp.minimum(h[:, :LANES], SWIGLU_LIMIT)
            up = jnp.clip(h[:, LANES:], -SWIGLU_LIMIT, SWIGLU_LIMIT)
            glu = gate * jax.nn.sigmoid(SWIGLU_ALPHA * gate)
            act_ref[:, c * LANES:(c + 1) * LANES] = ((up + 1.0) * glu).astype(BF16)
        y = jnp.dot(act_ref[...], w2b_ref[...], preferred_element_type=F32) + b2_ref[0]
        words = _pack_rows(y)
        ylo_ref[...] = words[:, :PACK_HALF]
        yhi_ref[...] = words[:, PACK_HALF:]


def _gate_up_order():
    j = jnp.arange(MXU_DIM)
    within = jnp.where(j < LANES, 2 * j, 2 * (j - LANES) + 1)
    return within


def _experts(xs_lo, xs_hi, w1, b1, w2, b2, blk_e, blk_row, nvalid, n_blocks):
    within = _gate_up_order()
    perm = (jnp.arange(MXU_DIM)[:, None] == within[None, :]).astype(BF16)
    order = (jnp.arange(0, 2 * D_FF, MXU_DIM)[:, None] + within[None, :]).reshape(-1)
    b1p = b1.astype(F32)[:, order][:, None, :]
    b2r = b2.astype(F32)[:, None, :]
    tb = EXPERT_BLOCK
    xspec = pl.BlockSpec((tb, PACK_HALF), lambda i, be, br, nv: (br[i], 0))
    wspec = lambda shape: pl.BlockSpec((1,) + shape, lambda i, be, br, nv: (be[i], 0, 0))
    grid_spec = pltpu.PrefetchScalarGridSpec(
        num_scalar_prefetch=3,
        grid=(n_blocks,),
        in_specs=[xspec, xspec, wspec((D_MODEL, 2 * D_FF)), wspec((1, 2 * D_FF)),
                  wspec((D_FF, D_MODEL)), wspec((1, D_MODEL)),
                  pl.BlockSpec((MXU_DIM, MXU_DIM), lambda i, be, br, nv: (0, 0))],
        out_specs=[xspec, xspec],
        scratch_shapes=[pltpu.VMEM((D_MODEL, 2 * D_FF), BF16), pltpu.VMEM((D_FF, D_MODEL), BF16),
                        pltpu.VMEM((tb, D_FF), BF16)],
    )
    return pl.pallas_call(
        _expert_kernel,
        grid_spec=grid_spec,
        out_shape=[jax.ShapeDtypeStruct(xs_lo.shape, jnp.uint32)] * 2,
        compiler_params=_cparams(("arbitrary",)),
        name="experts",
    )(blk_e, blk_row, nvalid, xs_lo, xs_hi, w1, b1p, w2, b2r, perm)


def _combine_kernel(x1_ref, gate_ref, *refs):
    lo_refs, hi_refs, o_ref = refs[:TOP_K], refs[TOP_K:2 * TOP_K], refs[2 * TOP_K]
    acc = x1_ref[...]
    gates = gate_ref[...]
    for kk in range(TOP_K):
        words = jnp.concatenate([lo_refs[kk][...], hi_refs[kk][...]], axis=1)
        acc = acc + gates[:, kk:kk + 1] * _unpack_rows(words)
    o_ref[...] = acc


def _combine(x1, gates, yg_lo, yg_hi):
    n_tok = x1.shape[0]
    tm = min(ROW_TILE, n_tok)
    nblk = n_tok // tm
    row = lambda w: pl.BlockSpec((tm, w), lambda i: (i, 0))
    plane = lambda kk: pl.BlockSpec((tm, PACK_HALF), lambda i, kk=kk: (kk * nblk + i, 0))
    planes = [plane(kk) for kk in range(TOP_K)]
    return pl.pallas_call(
        _combine_kernel,
        grid=(nblk,),
        in_specs=[row(D_MODEL), row(LANES)] + planes + planes,
        out_specs=row(D_MODEL),
        out_shape=jax.ShapeDtypeStruct((n_tok, D_MODEL), F32),
        compiler_params=_cparams(("parallel",)),
        name="combine",
    )(x1, gates, *([yg_lo] * TOP_K), *([yg_hi] * TOP_K))


def _expert_blocks(counts, region, n_blocks):
    nb_e = (counts + EXPERT_BLOCK - 1) // EXPERT_BLOCK
    cum = jnp.cumsum(nb_e)
    total = cum[-1]
    i = jnp.arange(n_blocks, dtype=jnp.int32)
    i_eff = jnp.minimum(i, total - 1)
    e_i = jnp.sum(cum[None, :] <= i_eff[:, None], axis=1).astype(jnp.int32)
    j_i = i_eff - (cum[e_i] - nb_e[e_i])
    blk_row = e_i * (region // EXPERT_BLOCK) + j_i
    return e_i, blk_row.astype(jnp.int32), total.reshape(1).astype(jnp.int32)


def kernel(x, norm1_gain, w_in, lambda_re, lambda_im, log_dt, ssm_b_re, ssm_b_im, ssm_c_re, ssm_c_im, ssm_d, w_glu, b_glu, q_norm_gain, k_norm_gain, lambda_q1, lambda_k1, lambda_q2, lambda_k2, subln_gain, w_proj_ssm, w_proj_attn, w_out, norm2_gain, w_router, b_router, w_exp1, b_exp1, w_exp2, b_exp2):
    bsz, seq, d = x.shape
    n_tok = bsz * seq
    depth = norm1_gain.shape[0]
    row1 = lambda a: a.astype(F32).reshape(1, -1)
    for l in range(depth):
        lambda_init = 0.8 - 0.6 * math.exp(-0.3 * l)
        x2d = x.reshape(n_tok, d)

        u, q, k, vt, gs, ga = _in_proj(x2d, row1(norm1_gain[l]), w_in[l], q_norm_gain[l], k_norm_gain[l])

        bblk, cblk, a_tile = _ssm_params(lambda_re[l], lambda_im[l], log_dt[l], ssm_b_re[l], ssm_b_im[l],
                                         ssm_c_re[l], ssm_c_im[l])
        u_tb = u.reshape(bsz, seq, SSM_WIDTH).transpose(1, 0, 2).reshape(n_tok, SSM_WIDTH)
        so_tb = _ssm(u_tb, seq, bblk, cblk, a_tile, row1(ssm_d[l]), w_glu[l].astype(BF16), row1(b_glu[l]))
        so = so_tb.reshape(seq, bsz, SSM_WIDTH).transpose(1, 0, 2).reshape(n_tok, SSM_WIDTH)

        lam = (jnp.exp(jnp.sum(lambda_q1[l].astype(F32) * lambda_k1[l].astype(F32)))
               - jnp.exp(jnp.sum(lambda_q2[l].astype(F32) * lambda_k2[l].astype(F32)))
               + lambda_init).reshape(1)
        ao = _attention(q, k, vt, lam, subln_gain[l], lambda_init, bsz, seq)

        region = n_tok
        x1, h_lo, h_hi, route, gates, counts = _merge_route(
            so, ao, gs, ga, x2d, w_proj_ssm[l].astype(BF16), w_proj_attn[l].astype(BF16),
            w_out[l].astype(BF16), row1(norm2_gain[l]), w_router[l], b_router[l], region)

        dest = route[:, :TOP_K].T
        n_slots = N_EXPERTS * region
        xs_lo = _sc_scatter_rows(h_lo, dest, n_slots)
        xs_hi = _sc_scatter_rows(h_hi, dest, n_slots)

        n_blocks = n_tok * TOP_K // EXPERT_BLOCK + N_EXPERTS
        blk_e, blk_row, nvalid = _expert_blocks(counts[0, :N_EXPERTS], region, n_blocks)
        ys_lo, ys_hi = _experts(xs_lo, xs_hi, w_exp1[l], b_exp1[l], w_exp2[l], b_exp2[l],
                                blk_e, blk_row, nvalid, n_blocks)

        flat = dest.reshape(1, TOP_K * n_tok)
        yg_lo = _sc_gather_rows(ys_lo, flat)
        yg_hi = _sc_gather_rows(ys_hi, flat)
        x = _combine(x1, gates, yg_lo, yg_hi).reshape(bsz, seq, d)
    return x
```

```python
import functools
import math

import jax
import jax.numpy as jnp
from jax import lax
from jax.experimental import pallas as pl
from jax.experimental.pallas import tpu as pltpu
from jax.experimental.pallas import tpu_sc as plsc

F32 = jnp.float32
BF16 = jnp.bfloat16

D_MODEL = 1024
NORM_EPS = 1e-5
SSM_WIDTH = 512
SSM_GROUP = 16
SSM_GROUPS = 32
SSM_STATE = 64
N_STATE = SSM_GROUPS * SSM_STATE
HEADS = 8
HEAD_DIM = 64
N_EXPERTS = 32
TOP_K = 4
D_FF = 1024
SWIGLU_ALPHA = 1.702
SWIGLU_LIMIT = 7.0

LANES = 128
SUBLANES = 8
MXU_DIM = 256
VMEM_LIMIT = 56 * 1024 * 1024

ROW_TILE = 512
SSM_CHUNK = 128
SCAN_LANES = 512
ATTN_BLOCK = 256
ATTN_HEADS_PER_STEP = 8
EXPERT_BLOCK = 256
SC_WINDOW = 128
PACK_W = D_MODEL // 2
PACK_HALF = PACK_W // 2

_NEG = -1e30
Q_SCALE = math.log2(math.e) / math.sqrt(HEAD_DIM)


def _cparams(sem):
    return pltpu.CompilerParams(dimension_semantics=sem, vmem_limit_bytes=VMEM_LIMIT)


def _full(shape):
    nd = len(shape)
    return pl.BlockSpec(shape, lambda *_: (0,) * nd)


def _run_lookahead(tasks):
    pending = tasks[0][0]()
    for i, (_, epilogue) in enumerate(tasks):
        result = pending
        if i + 1 < len(tasks):
            pending = tasks[i + 1][0]()
        epilogue(result)


def _inproj_kernel(x_ref, g1_ref, w_ref, wvt_ref, qg_ref, kg_ref, seg_ref,
                   u_ref, q_ref, k_ref, vt_ref, gs_ref, ga_ref):
    x = x_ref[...]
    ms = jnp.mean(x * x, axis=-1, keepdims=True)
    h = (x * lax.rsqrt(ms + NORM_EPS) * g1_ref[...]).astype(BF16)

    def proj(c0, width):
        return lambda: jnp.dot(h, w_ref[:, c0:c0 + width], preferred_element_type=F32)

    seg = seg_ref[...]
    q0 = SSM_WIDTH
    k0 = q0 + D_MODEL
    gs0 = k0 + D_MODEL
    ga0 = gs0 + D_MODEL
    half = D_MODEL // 2
    tasks = []

    def store(out_ref, cols, fn):
        def epilogue(r):
            out_ref[:, cols] = fn(r)
        return epilogue

    tasks.append((proj(0, SSM_WIDTH), store(u_ref, slice(None), lambda r: r)))

    def head_norm_tasks(base, gain_ref, out_ref, scale, c):
        cols = slice(c * MXU_DIM, (c + 1) * MXU_DIM)
        kept = {}

        def after_proj(y):
            kept["y"] = y
            kept["sq"] = (y * y).astype(BF16)

        def after_sum(ss):
            yn = kept["y"] * lax.rsqrt(ss * (1.0 / HEAD_DIM) + NORM_EPS) * gain_ref[:, cols]
            out_ref[:, cols] = (yn * scale).astype(BF16)

        return ((proj(base + c * MXU_DIM, MXU_DIM), after_proj),
                (lambda: jnp.dot(kept["sq"], seg, preferred_element_type=F32), after_sum))

    pairs = [head_norm_tasks(base, gain_ref, out_ref, scale, c)
             for base, gain_ref, out_ref, scale in ((q0, qg_ref, q_ref, Q_SCALE), (k0, kg_ref, k_ref, 1.0))
             for c in range(D_MODEL // MXU_DIM)]
    tasks.append(pairs[0][0])
    for prev, cur in zip(pairs, pairs[1:]):
        tasks += [cur[0], prev[1]]
    tasks.append(pairs[-1][1])

    to_gate = lambda r: jax.nn.sigmoid(r).astype(BF16)
    for c in range(2):
        cols = slice(c * half, (c + 1) * half)
        tasks.append((proj(gs0 + c * half, half), store(gs_ref, cols, to_gate)))
        tasks.append((proj(ga0 + c * half, half), store(ga_ref, cols, to_gate)))

    def vt_task(r0):
        def matmul():
            return lax.dot_general(wvt_ref[r0:r0 + MXU_DIM, :], h, (((1,), (1,)), ((), ())),
                                   preferred_element_type=F32)

        def epilogue(r):
            vt_ref[r0:r0 + MXU_DIM, :] = r.astype(BF16)
        return matmul, epilogue

    tasks += [vt_task(r0) for r0 in range(0, D_MODEL, MXU_DIM)]
    _run_lookahead(tasks)


def _in_proj(x2d, gain1, w_in, q_gain, k_gain):
    n_tok = x2d.shape[0]
    tm = min(ROW_TILE, n_tok)
    v0 = SSM_WIDTH + 2 * D_MODEL
    w_bf = w_in.astype(BF16)
    w_main = jnp.concatenate([w_bf[:, :v0], w_bf[:, v0 + D_MODEL:]], axis=1)
    w_vt = w_bf[:, v0:v0 + D_MODEL].T
    in_width = w_main.shape[1]
    seg = (jnp.arange(MXU_DIM)[:, None] // HEAD_DIM == jnp.arange(MXU_DIM)[None, :] // HEAD_DIM).astype(BF16)
    reps = D_MODEL // HEAD_DIM
    qg = jnp.tile(q_gain.astype(F32), reps)[None, :]
    kg = jnp.tile(k_gain.astype(F32), reps)[None, :]
    row = lambda w: pl.BlockSpec((tm, w), lambda i: (i, 0))
    tok = jax.ShapeDtypeStruct((n_tok, D_MODEL), BF16)
    out_shapes = [jax.ShapeDtypeStruct((n_tok, SSM_WIDTH), F32), tok, tok,
                  jax.ShapeDtypeStruct((D_MODEL, n_tok), BF16), tok, tok]
    vt_spec = pl.BlockSpec((D_MODEL, tm), lambda i: (0, i))
    return pl.pallas_call(
        _inproj_kernel,
        grid=(n_tok // tm,),
        in_specs=[row(D_MODEL), _full((1, D_MODEL)), _full((D_MODEL, in_width)), _full((D_MODEL, D_MODEL)),
                  _full((1, D_MODEL)), _full((1, D_MODEL)), _full((MXU_DIM, MXU_DIM))],
        out_specs=[row(SSM_WIDTH), row(D_MODEL), row(D_MODEL), vt_spec, row(D_MODEL), row(D_MODEL)],
        out_shape=out_shapes,
        compiler_params=_cparams(("parallel",)),
        name="in_proj",
    )(x2d, gain1, w_main, w_vt, qg, kg, seg)


def _ssm_kernel(u_ref, bblk_ref, a_ref, cblk_ref, d_ref, wglu_ref, bglu_ref, o_ref, bu_ref, st_ref):
    chunk = u_ref.shape[0] // SUBLANES

    @pl.when(pl.program_id(0) == 0)
    def _():
        st_ref[...] = jnp.zeros_like(st_ref)

    u = u_ref[...]
    u_bf = u.astype(BF16)
    tiles_per_part = N_STATE // MXU_DIM
    ch_per_tile = SSM_WIDTH // tiles_per_part
    for n in range(2 * tiles_per_part):
        ch0 = ((n % tiles_per_part) * ch_per_tile) // LANES * LANES
        lanes = slice(n * MXU_DIM, (n + 1) * MXU_DIM)
        bu_ref[:, lanes] = jnp.dot(u_bf[:, ch0:ch0 + LANES], bblk_ref[ch0:ch0 + LANES, lanes],
                                   preferred_element_type=F32)

    for j in range(N_STATE // SCAN_LANES):
        re = slice(j * SCAN_LANES, (j + 1) * SCAN_LANES)
        im = slice(N_STATE + j * SCAN_LANES, N_STATE + (j + 1) * SCAN_LANES)
        ar = a_ref[:, re]
        ai = a_ref[:, im]

        def step(t, carry, re=re, im=im, ar=ar, ai=ai):
            xr, xi = carry
            rows = pl.ds(pl.multiple_of(t * SUBLANES, SUBLANES), SUBLANES)
            nr = ar * xr - ai * xi + bu_ref[rows, re]
            ni = ar * xi + ai * xr + bu_ref[rows, im]
            bu_ref[rows, re] = nr
            bu_ref[rows, im] = ni
            return nr, ni

        xr, xi = lax.fori_loop(0, chunk, step, (st_ref[:, re], st_ref[:, im]), unroll=4)
        st_ref[:, re] = xr
        st_ref[:, im] = xi

    n_out = SSM_WIDTH // MXU_DIM
    lanes_per_out = N_STATE // n_out
    ys = []
    for j in range(n_out):
        cols = slice(j * MXU_DIM, (j + 1) * MXU_DIM)
        acc = None
        for part in range(2):
            lanes = slice(part * N_STATE + j * lanes_per_out, part * N_STATE + (j + 1) * lanes_per_out)
            term = jnp.dot(bu_ref[:, lanes].astype(BF16), cblk_ref[lanes, cols], preferred_element_type=F32)
            acc = term if acc is None else acc + term
        ys.append(acc)
    y = jnp.concatenate(ys, axis=1) + d_ref[...] * u
    z = jax.nn.gelu(y)
    gate = jax.nn.sigmoid(jnp.dot(z.astype(BF16), wglu_ref[...], preferred_element_type=F32) + bglu_ref[...])
    o_ref[...] = (z * gate).astype(BF16)


def _ssm_params(lambda_re, lambda_im, log_dt, b_re, b_im, c_re, c_im):
    dt = jnp.exp(log_dt.astype(F32))[:, None]
    lr = jnp.minimum(lambda_re.astype(F32), -1e-4)
    li = lambda_im.astype(F32)
    mag = jnp.exp(lr * dt)
    abar_re = mag * jnp.cos(li * dt)
    abar_im = mag * jnp.sin(li * dt)
    den = lr * lr + li * li
    nr = abar_re - 1.0
    coef_re = (nr * lr + abar_im * li) / den
    coef_im = (abar_im * lr - nr * li) / den
    br = b_re.astype(F32)
    bi = b_im.astype(F32)
    bbar_re = coef_re[..., None] * br - coef_im[..., None] * bi
    bbar_im = coef_re[..., None] * bi + coef_im[..., None] * br
    eye = jnp.eye(SSM_GROUPS, dtype=F32)

    def expand_b(b):
        return jnp.einsum('gph,gk->ghkp', b, eye).reshape(SSM_WIDTH, N_STATE)

    def expand_c(c):
        return jnp.einsum('ghp,gk->gpkh', c, eye).reshape(N_STATE, SSM_WIDTH)

    bblk = jnp.concatenate([expand_b(bbar_re), expand_b(bbar_im)], axis=1).astype(BF16)
    cblk = jnp.concatenate([expand_c(c_re.astype(F32)), -expand_c(c_im.astype(F32))], axis=0).astype(BF16)
    a_row = jnp.concatenate([abar_re.reshape(-1), abar_im.reshape(-1)])[None, :]
    return bblk, cblk, jnp.broadcast_to(a_row, (SUBLANES, 2 * N_STATE))


def _ssm(u_tb, seq, bblk, cblk, a_tile, d_skip, w_glu_bf, b_glu):
    chunk = min(SSM_CHUNK, seq)
    rows = chunk * SUBLANES
    return pl.pallas_call(
        _ssm_kernel,
        grid=(seq // chunk,),
        in_specs=[pl.BlockSpec((rows, SSM_WIDTH), lambda c: (c, 0)),
                  _full(bblk.shape), _full(a_tile.shape), _full(cblk.shape),
                  _full((1, SSM_WIDTH)), _full(w_glu_bf.shape), _full((1, SSM_WIDTH))],
        out_specs=pl.BlockSpec((rows, SSM_WIDTH), lambda c: (c, 0)),
        out_shape=jax.ShapeDtypeStruct(u_tb.shape, BF16),
        scratch_shapes=[pltpu.VMEM((rows, 2 * N_STATE), F32), pltpu.VMEM((SUBLANES, 2 * N_STATE), F32)],
        compiler_params=_cparams(("arbitrary",)),
        name="ssm",
    )(u_tb, bblk, a_tile, cblk, d_skip, w_glu_bf, b_glu)


def _attn_kernel(lam_ref, sg_ref, q_ref, k_ref, vt_ref, o_ref, acc_ref, *, out_scale, blk, nh):
    seq = q_ref.shape[1]
    hw = 2 * HEAD_DIM
    lane = lax.broadcasted_iota(jnp.int32, (blk, hw), 1)
    key_i = lax.broadcasted_iota(jnp.int32, (blk, blk), 0)
    qry_i = lax.broadcasted_iota(jnp.int32, (blk, blk), 1)
    keep = key_i <= qry_i
    contract_last = (((1,), (1,)), ((), ()))
    lam = lam_ref[0]
    n_chain = 2 * nh

    def q_block(qi, _):
        qrows = pl.ds(pl.multiple_of(qi * blk, blk), blk)
        qs = []
        for hh in range(nh):
            q = q_ref[0, qrows, hh * hw:(hh + 1) * hw]
            zero = jnp.zeros_like(q)
            qs += [jnp.where(lane < HEAD_DIM, q, zero), jnp.where(lane >= HEAD_DIM, q, zero)]
        acc_ref[...] = jnp.zeros_like(acc_ref)

        def kv_block(kb, carry, masked):
            krows = pl.ds(pl.multiple_of(kb * blk, blk), blk)
            scores = []
            for c in range(n_chain):
                hh = c // 2
                k = k_ref[0, krows, hh * hw:(hh + 1) * hw]
                scores.append(lax.dot_general(k, qs[c], contract_last, preferred_element_type=F32))
            out, probs, alphas = [], [], []
            for c in range(n_chain):
                m, l = carry[2 * c], carry[2 * c + 1]
                s = jnp.where(keep, scores[c], _NEG) if masked else scores[c]
                m_new = jnp.maximum(m, jnp.max(s, axis=0, keepdims=True))
                p = jnp.exp2(s - m_new)
                alpha = jnp.exp2(m - m_new)
                out += [m_new, alpha * l + jnp.sum(p, axis=0, keepdims=True)]
                probs.append(p.astype(BF16))
                alphas.append(alpha)
            for c in range(n_chain):
                hh = c // 2
                vt = vt_ref[hh * hw:(hh + 1) * hw, krows]
                acc_ref[c] = alphas[c] * acc_ref[c] + jnp.dot(vt, probs[c], preferred_element_type=F32)
            return tuple(out)

        carry = (jnp.full((1, blk), _NEG, F32), jnp.zeros((1, blk), F32)) * n_chain
        carry = lax.fori_loop(0, qi, lambda kb, c: kv_block(kb, c, False), carry)
        carry = kv_block(qi, carry, True)
        for hh in range(nh):
            l1, l2 = carry[4 * hh + 1], carry[4 * hh + 3]
            ot = acc_ref[2 * hh] * (1.0 / l1) - acc_ref[2 * hh + 1] * (lam / l2)
            ot = ot * lax.rsqrt(jnp.mean(ot * ot, axis=0, keepdims=True) + NORM_EPS)
            o_ref[0, qrows, hh * hw:(hh + 1) * hw] = (ot.T * sg_ref[...] * out_scale).astype(BF16)
        return 0

    lax.fori_loop(0, seq // blk, q_block, 0)


def _attention(q, k, vt, lam, subln_gain, lambda_init, bsz, seq):
    blk = min(ATTN_BLOCK, seq)
    nh = ATTN_HEADS_PER_STEP
    hw = 2 * HEAD_DIM
    q3, k3 = (a.reshape(bsz, seq, D_MODEL) for a in (q, k))
    tok_spec = pl.BlockSpec((1, seq, nh * hw), lambda b, h: (b, 0, h))
    out = pl.pallas_call(
        functools.partial(_attn_kernel, out_scale=1.0 - lambda_init, blk=blk, nh=nh),
        grid=(bsz, HEADS // nh),
        in_specs=[pl.BlockSpec(memory_space=pltpu.SMEM), _full((1, hw)), tok_spec, tok_spec,
                  pl.BlockSpec((nh * hw, seq), lambda b, h: (h, b))],
        out_specs=tok_spec,
        out_shape=jax.ShapeDtypeStruct((bsz, seq, D_MODEL), BF16),
        scratch_shapes=[pltpu.VMEM((2 * nh, hw, blk), F32)],
        compiler_params=_cparams(("parallel", "parallel")),
        name="diff_attn",
    )(lam, subln_gain.astype(F32)[None, :], q3, k3, vt)
    return out.reshape(bsz * seq, D_MODEL)


def _pack_rows(y):
    bits = lax.bitcast_convert_type(y.astype(BF16).astype(F32), jnp.uint32)
    return (bits[:, :PACK_W] >> 16) | (bits[:, PACK_W:] & jnp.uint32(0xFFFF0000))


def _unpack_rows(w):
    lo = lax.bitcast_convert_type(w << 16, F32)
    hi = lax.bitcast_convert_type(w & jnp.uint32(0xFFFF0000), F32)
    return jnp.concatenate([lo, hi], axis=1)


def _merge_kernel(so_ref, ao_ref, gs_ref, ga_ref, x_ref, wps_ref, wpa_ref, wo_ref, g2_ref,
                  wrh_ref, wrl_ref, br_ref, tri_ref,
                  x1_ref, hlo_ref, hhi_ref, route_ref, gate_ref, cnt_ref, run_ref, *, region):
    @pl.when(pl.program_id(0) == 0)
    def _():
        run_ref[...] = jnp.zeros_like(run_ref)

    tm = x_ref.shape[0]
    n_part = 2 if tm % (2 * MXU_DIM) == 0 else 1
    rows_per = tm // n_part
    lane = lax.broadcasted_iota(jnp.int32, (rows_per, LANES), 1)
    lane_f = lane.astype(F32)
    parts = [dict(rows=slice(p * rows_per, (p + 1) * rows_per)) for p in range(n_part)]

    def stage_proj(st):
        def matmul():
            return (jnp.dot(so_ref[st["rows"], :], wps_ref[...], preferred_element_type=F32),
                    jnp.dot(ao_ref[st["rows"], :], wpa_ref[...], preferred_element_type=F32))

        def epilogue(r):
            ps, pa = r
            merged = gs_ref[st["rows"], :].astype(F32) * ps + ga_ref[st["rows"], :].astype(F32) * pa
            st["merged"] = merged.astype(BF16)
        return matmul, epilogue

    def stage_out(st):
        def matmul():
            return jnp.dot(st["merged"], wo_ref[...], preferred_element_type=F32)

        def epilogue(r):
            x1 = x_ref[st["rows"], :] + r
            x1_ref[st["rows"], :] = x1
            h2 = x1 * lax.rsqrt(jnp.mean(x1 * x1, axis=-1, keepdims=True) + NORM_EPS) * g2_ref[...]
            words = _pack_rows(h2)
            hlo_ref[st["rows"], :] = words[:, :PACK_HALF]
            hhi_ref[st["rows"], :] = words[:, PACK_HALF:]
            st["h_hi"] = h2.astype(BF16)
            st["h_lo"] = (h2 - st["h_hi"].astype(F32)).astype(BF16)
        return matmul, epilogue

    def stage_router(st):
        def matmul():
            return (jnp.dot(st["h_hi"], wrh_ref[...], preferred_element_type=F32)
                    + jnp.dot(st["h_lo"], wrh_ref[...], preferred_element_type=F32)
                    + jnp.dot(st["h_hi"], wrl_ref[...], preferred_element_type=F32))

        def epilogue(r):
            work = r + br_ref[...]
            onehots, vals, ids = [], [], []
            for _ in range(TOP_K):
                m = jnp.max(work, axis=-1, keepdims=True)
                idx = jnp.min(jnp.where(work == m, lane_f, float(LANES)), axis=-1, keepdims=True)
                oh = lane_f == idx
                onehots.append(oh)
                vals.append(m)
                ids.append(idx.astype(jnp.int32))
                work = jnp.where(oh, -jnp.inf, work)
            exps = [jnp.exp(v - vals[0]) for v in vals]
            den = exps[0] + exps[1] + exps[2] + exps[3]
            st.update(onehots=onehots, ids=ids, gates=[e / den for e in exps],
                      multi=(onehots[0] | onehots[1] | onehots[2] | onehots[3]).astype(F32))
        return matmul, epilogue

    _run_lookahead([stage(st) for stage in (stage_proj, stage_out, stage_router) for st in parts])

    multi = jnp.concatenate([st["multi"] for st in parts], axis=0)
    before = jnp.dot(tri_ref[...], multi.astype(BF16), preferred_element_type=F32) + run_ref[...]
    for st in parts:
        route = jnp.zeros((rows_per, LANES), jnp.int32)
        gates = jnp.zeros((rows_per, LANES), F32)
        for kk in range(TOP_K):
            rank = jnp.sum(jnp.where(st["onehots"][kk], before[st["rows"], :], 0.0), axis=-1, keepdims=True)
            route = jnp.where(lane == kk, st["ids"][kk] * region + rank.astype(jnp.int32), route)
            gates = jnp.where(lane == kk, st["gates"][kk], gates)
        route_ref[st["rows"], :] = route
        gate_ref[st["rows"], :] = gates
    run = run_ref[...] + jnp.sum(multi, axis=0, keepdims=True)
    run_ref[...] = run
    cnt_ref[...] = run.astype(jnp.int32)


def _merge_route(so, ao, gs, ga, x2d, wps, wpa, wo, gain2, w_router, b_router, region):
    n_tok = x2d.shape[0]
    tm = min(ROW_TILE, n_tok)
    wr = jnp.zeros((D_MODEL, LANES), F32).at[:, :N_EXPERTS].set(w_router.astype(F32))
    wr_hi = wr.astype(BF16)
    wr_lo = (wr - wr_hi.astype(F32)).astype(BF16)
    br = jnp.full((1, LANES), -jnp.inf, F32).at[0, :N_EXPERTS].set(b_router.astype(F32))
    tri = (jnp.arange(tm)[:, None] > jnp.arange(tm)[None, :]).astype(BF16)
    row = lambda w: pl.BlockSpec((tm, w), lambda i: (i, 0))
    out_shapes = [jax.ShapeDtypeStruct((n_tok, D_MODEL), F32),
                  jax.ShapeDtypeStruct((n_tok, PACK_HALF), jnp.uint32),
                  jax.ShapeDtypeStruct((n_tok, PACK_HALF), jnp.uint32),
                  jax.ShapeDtypeStruct((n_tok, LANES), jnp.int32),
                  jax.ShapeDtypeStruct((n_tok, LANES), F32),
                  jax.ShapeDtypeStruct((1, LANES), jnp.int32)]
    return pl.pallas_call(
        functools.partial(_merge_kernel, region=region),
        grid=(n_tok // tm,),
        in_specs=[row(SSM_WIDTH), row(D_MODEL), row(D_MODEL), row(D_MODEL), row(D_MODEL),
                  _full(wps.shape), _full(wpa.shape), _full(wo.shape), _full((1, D_MODEL)),
                  _full(wr_hi.shape), _full(wr_lo.shape), _full((1, LANES)), _full((tm, tm))],
        out_specs=[row(D_MODEL), row(PACK_HALF), row(PACK_HALF), row(LANES), row(LANES), _full((1, LANES))],
        out_shape=out_shapes,
        scratch_shapes=[pltpu.VMEM((1, LANES), F32)],
        compiler_params=_cparams(("arbitrary",)),
        name="merge_route",
    )(so, ao, gs, ga, x2d, wps, wpa, wo, gain2, wr_hi, wr_lo, br, tri)


def _sc_scatter_rows(rows, dest, n_slots):
    n_tok, width = rows.shape
    mesh = plsc.VectorSubcoreMesh(core_axis_name="core", subcore_axis_name="subcore")

    @pl.kernel(out_type=jax.ShapeDtypeStruct((n_slots, width), rows.dtype), mesh=mesh, scratch_types=[])
    def scatter(rows_hbm, dest_hbm, out_hbm):
        def body(rows_vmem, dest_vmem):
            pltpu.sync_copy(rows_vmem, out_hbm.at[dest_vmem.at[0]])

        pltpu.emit_pipeline(
            body,
            grid=(TOP_K, n_tok // SC_WINDOW),
            in_specs=[pl.BlockSpec((SC_WINDOW, width), lambda k, i: (i, 0)),
                      pl.BlockSpec((1, SC_WINDOW), lambda k, i: (k, i))],
            out_specs=[],
            core_axis_name=("core", "subcore"),
            dimension_semantics=(pltpu.PARALLEL, pltpu.PARALLEL),
        )(rows_hbm, dest_hbm)

    return scatter(rows, dest)


def _sc_gather_rows(table, idx):
    n = idx.shape[1]
    width = table.shape[1]
    mesh = plsc.VectorSubcoreMesh(core_axis_name="core", subcore_axis_name="subcore")

    @pl.kernel(out_type=jax.ShapeDtypeStruct((n, width), table.dtype), mesh=mesh, scratch_types=[])
    def gather(table_hbm, idx_hbm, out_hbm):
        def body(idx_vmem, out_vmem):
            pltpu.sync_copy(table_hbm.at[idx_vmem.at[0]], out_vmem)

        pltpu.emit_pipeline(
            body,
            grid=(n // SC_WINDOW,),
            in_specs=[pl.BlockSpec((1, SC_WINDOW), lambda i: (0, i))],
            out_specs=[pl.BlockSpec((SC_WINDOW, width), lambda i: (i, 0))],
            core_axis_name=("core", "subcore"),
            dimension_semantics=(pltpu.PARALLEL,),
        )(idx_hbm, out_hbm)

    return gather(table, idx)


def _expert_kernel(cnt_ref, xlo_hbm, xhi_hbm, w1_ref, b1_ref, w2_ref, b2_ref, perm_ref, ylo_hbm, yhi_hbm,
                   w1p_ref, w2b_ref, act_ref, xin_ref, yout_ref, in_sem, out_sem, *, region):
    e = pl.program_id(0)
    n_blk = (cnt_ref[e] + EXPERT_BLOCK - 1) // EXPERT_BLOCK
    base = e * region
    n_chunks = (2 * D_FF) // MXU_DIM
    x_hbm = (xlo_hbm, xhi_hbm)
    y_hbm = (ylo_hbm, yhi_hbm)

    def rows_of(j):
        return pl.ds(pl.multiple_of(base + j * EXPERT_BLOCK, EXPERT_BLOCK), EXPERT_BLOCK)

    def in_copy(j, slot, half):
        return pltpu.make_async_copy(x_hbm[half].at[rows_of(j), :], xin_ref.at[slot, half], in_sem.at[slot, half])

    def out_copy(j, slot, half):
        return pltpu.make_async_copy(yout_ref.at[slot, half], y_hbm[half].at[rows_of(j), :], out_sem.at[slot, half])

    @pl.when(n_blk > 0)
    def _():
        for half in range(2):
            in_copy(0, 0, half).start()

    perm = perm_ref[...]
    for c in range(n_chunks):
        cols = slice(c * MXU_DIM, (c + 1) * MXU_DIM)
        w1p_ref[:, cols] = jnp.dot(w1_ref[0, :, cols].astype(BF16), perm,
                                   preferred_element_type=F32).astype(BF16)
    w2b_ref[...] = w2_ref[0].astype(BF16)

    def block(j, _):
        slot = j % 2
        for half in range(2):
            in_copy(j, slot, half).wait()

        @pl.when(j + 1 < n_blk)
        def _():
            for half in range(2):
                in_copy(j + 1, 1 - slot, half).start()

        @pl.when(j >= 2)
        def _():
            for half in range(2):
                out_copy(j - 2, slot, half).wait()

        words = jnp.concatenate([xin_ref[slot, 0], xin_ref[slot, 1]], axis=1)
        x = _unpack_rows(words).astype(BF16)

        def up_task(c):
            cols = slice(c * MXU_DIM, (c + 1) * MXU_DIM)

            def epilogue(r):
                h = r + b1_ref[0, :, cols]
                gate = jnp.minimum(h[:, :LANES], SWIGLU_LIMIT)
                up = jnp.clip(h[:, LANES:], -SWIGLU_LIMIT, SWIGLU_LIMIT)
                glu = gate * jax.nn.sigmoid(SWIGLU_ALPHA * gate)
                act_ref[:, c * LANES:(c + 1) * LANES] = ((up + 1.0) * glu).astype(BF16)
            return (lambda: jnp.dot(x, w1p_ref[:, cols], preferred_element_type=F32)), epilogue

        _run_lookahead([up_task(c) for c in range(n_chunks)])
        y = jnp.dot(act_ref[...], w2b_ref[...], preferred_element_type=F32) + b2_ref[0]
        words = _pack_rows(y)
        yout_ref[slot, 0] = words[:, :PACK_HALF]
        yout_ref[slot, 1] = words[:, PACK_HALF:]
        for half in range(2):
            out_copy(j, slot, half).start()
        return 0

    lax.fori_loop(0, n_blk, block, 0)

    for back in (2, 1):
        @pl.when(n_blk >= back)
        def _(back=back):
            j = n_blk - back
            for half in range(2):
                out_copy(j, j % 2, half).wait()


def _gate_up_order():
    j = jnp.arange(MXU_DIM)
    within = jnp.where(j < LANES, 2 * j, 2 * (j - LANES) + 1)
    return within


def _experts(xs_lo, xs_hi, w1, b1, w2, b2, counts, region):
    within = _gate_up_order()
    perm = (jnp.arange(MXU_DIM)[:, None] == within[None, :]).astype(BF16)
    order = (jnp.arange(0, 2 * D_FF, MXU_DIM)[:, None] + within[None, :]).reshape(-1)
    b1p = b1.astype(F32)[:, order][:, None, :]
    b2r = b2.astype(F32)[:, None, :]
    tb = EXPERT_BLOCK
    hbm = pl.BlockSpec(memory_space=pl.ANY)
    wspec = lambda shape: pl.BlockSpec((1,) + shape, lambda e, cnt: (e, 0, 0))
    grid_spec = pltpu.PrefetchScalarGridSpec(
        num_scalar_prefetch=1,
        grid=(N_EXPERTS,),
        in_specs=[hbm, hbm, wspec((D_MODEL, 2 * D_FF)), wspec((1, 2 * D_FF)),
                  wspec((D_FF, D_MODEL)), wspec((1, D_MODEL)),
                  pl.BlockSpec((MXU_DIM, MXU_DIM), lambda e, cnt: (0, 0))],
        out_specs=[hbm, hbm],
        scratch_shapes=[pltpu.VMEM((D_MODEL, 2 * D_FF), BF16), pltpu.VMEM((D_FF, D_MODEL), BF16),
                        pltpu.VMEM((tb, D_FF), BF16),
                        pltpu.VMEM((2, 2, tb, PACK_HALF), jnp.uint32), pltpu.VMEM((2, 2, tb, PACK_HALF), jnp.uint32),
                        pltpu.SemaphoreType.DMA((2, 2)), pltpu.SemaphoreType.DMA((2, 2))],
    )
    return pl.pallas_call(
        functools.partial(_expert_kernel, region=region),
        grid_spec=grid_spec,
        out_shape=[jax.ShapeDtypeStruct(xs_lo.shape, jnp.uint32)] * 2,
        compiler_params=_cparams(("arbitrary",)),
        name="experts",
    )(counts, xs_lo, xs_hi, w1, b1p, w2, b2r, perm)


def _combine_kernel(x1_ref, gate_ref, *refs):
    lo_refs, hi_refs, o_ref = refs[:TOP_K], refs[TOP_K:2 * TOP_K], refs[2 * TOP_K]
    acc = x1_ref[...]
    gates = gate_ref[...]
    for kk in range(TOP_K):
        words = jnp.concatenate([lo_refs[kk][...], hi_refs[kk][...]], axis=1)
        acc = acc + gates[:, kk:kk + 1] * _unpack_rows(words)
    o_ref[...] = acc


def _combine(x1, gates, yg_lo, yg_hi):
    n_tok = x1.shape[0]
    tm = min(ROW_TILE, n_tok)
    nblk = n_tok // tm
    row = lambda w: pl.BlockSpec((tm, w), lambda i: (i, 0))
    plane = lambda kk: pl.BlockSpec((tm, PACK_HALF), lambda i, kk=kk: (kk * nblk + i, 0))
    planes = [plane(kk) for kk in range(TOP_K)]
    return pl.pallas_call(
        _combine_kernel,
        grid=(nblk,),
        in_specs=[row(D_MODEL), row(LANES)] + planes + planes,
        out_specs=row(D_MODEL),
        out_shape=jax.ShapeDtypeStruct((n_tok, D_MODEL), F32),
        compiler_params=_cparams(("parallel",)),
        name="combine",
    )(x1, gates, *([yg_lo] * TOP_K), *([yg_hi] * TOP_K))


def kernel(x, norm1_gain, w_in, lambda_re, lambda_im, log_dt, ssm_b_re, ssm_b_im, ssm_c_re, ssm_c_im, ssm_d, w_glu, b_glu, q_norm_gain, k_norm_gain, lambda_q1, lambda_k1, lambda_q2, lambda_k2, subln_gain, w_proj_ssm, w_proj_attn, w_out, norm2_gain, w_router, b_router, w_exp1, b_exp1, w_exp2, b_exp2):
    bsz, seq, d = x.shape
    n_tok = bsz * seq
    depth = norm1_gain.shape[0]
    row1 = lambda a: a.astype(F32).reshape(1, -1)
    for l in range(depth):
        lambda_init = 0.8 - 0.6 * math.exp(-0.3 * l)
        x2d = x.reshape(n_tok, d)

        u, q, k, vt, gs, ga = _in_proj(x2d, row1(norm1_gain[l]), w_in[l], q_norm_gain[l], k_norm_gain[l])

        bblk, cblk, a_tile = _ssm_params(lambda_re[l], lambda_im[l], log_dt[l], ssm_b_re[l], ssm_b_im[l],
                                         ssm_c_re[l], ssm_c_im[l])
        u_tb = u.reshape(bsz, seq, SSM_WIDTH).transpose(1, 0, 2).reshape(n_tok, SSM_WIDTH)
        so_tb = _ssm(u_tb, seq, bblk, cblk, a_tile, row1(ssm_d[l]), w_glu[l].astype(BF16), row1(b_glu[l]))
        so = so_tb.reshape(seq, bsz, SSM_WIDTH).transpose(1, 0, 2).reshape(n_tok, SSM_WIDTH)

        lam = (jnp.exp(jnp.sum(lambda_q1[l].astype(F32) * lambda_k1[l].astype(F32)))
               - jnp.exp(jnp.sum(lambda_q2[l].astype(F32) * lambda_k2[l].astype(F32)))
               + lambda_init).reshape(1)
        ao = _attention(q, k, vt, lam, subln_gain[l], lambda_init, bsz, seq)

        region = n_tok
        x1, h_lo, h_hi, route, gates, counts = _merge_route(
            so, ao, gs, ga, x2d, w_proj_ssm[l].astype(BF16), w_proj_attn[l].astype(BF16),
            w_out[l].astype(BF16), row1(norm2_gain[l]), w_router[l], b_router[l], region)

        dest = route[:, :TOP_K].T
        n_slots = N_EXPERTS * region
        xs_lo = _sc_scatter_rows(h_lo, dest, n_slots)
        xs_hi = _sc_scatter_rows(h_hi, dest, n_slots)

        ys_lo, ys_hi = _experts(xs_lo, xs_hi, w_exp1[l], b_exp1[l], w_exp2[l], b_exp2[l],
                                counts[0, :N_EXPERTS], region)

        flat = dest.reshape(1, TOP_K * n_tok)
        yg_lo = _sc_gather_rows(ys_lo, flat)
        yg_hi = _sc_gather_rows(ys_hi, flat)
        x = _combine(x1, gates, yg_lo, yg_hi).reshape(bsz, seq, d)
    return x
```

```python
import functools
import math

import jax
import jax.numpy as jnp
from jax import lax
from jax.experimental import pallas as pl
from jax.experimental.pallas import tpu as pltpu
from jax.experimental.pallas import tpu_sc as plsc

F32 = jnp.float32
BF16 = jnp.bfloat16

D_MODEL = 1024
NORM_EPS = 1e-5
SSM_WIDTH = 512
SSM_GROUP = 16
SSM_GROUPS = 32
SSM_STATE = 64
N_STATE = SSM_GROUPS * SSM_STATE
HEADS = 8
HEAD_DIM = 64
N_EXPERTS = 32
TOP_K = 4
D_FF = 1024
SWIGLU_ALPHA = 1.702
SWIGLU_LIMIT = 7.0

LANES = 128
SUBLANES = 8
MXU_DIM = 256
VMEM_LIMIT = 56 * 1024 * 1024

ROW_TILE = 512
SSM_CHUNK = 128
SCAN_LANES = 512
ATTN_BLOCK = 256
ATTN_HEADS_PER_STEP = 8
EXPERT_BLOCK = 512
SC_WINDOW = 128
PACK_W = D_MODEL // 2
PACK_HALF = PACK_W // 2

_NEG = -1e30
Q_SCALE = math.log2(math.e) / math.sqrt(HEAD_DIM)


def _cparams(sem):
    return pltpu.CompilerParams(dimension_semantics=sem, vmem_limit_bytes=VMEM_LIMIT)


def _full(shape):
    nd = len(shape)
    return pl.BlockSpec(shape, lambda *_: (0,) * nd)


def _run_lookahead(tasks):
    pending = tasks[0][0]()
    for i, (_, epilogue) in enumerate(tasks):
        result = pending
        if i + 1 < len(tasks):
            pending = tasks[i + 1][0]()
        epilogue(result)


def _inproj_kernel(x_ref, g1_ref, w_ref, wvt_ref, qg_ref, kg_ref, seg_ref,
                   u_ref, q_ref, k_ref, vt_ref, gs_ref, ga_ref):
    x = x_ref[...]
    ms = jnp.mean(x * x, axis=-1, keepdims=True)
    h = (x * lax.rsqrt(ms + NORM_EPS) * g1_ref[...]).astype(BF16)

    def proj(c0, width):
        return lambda: jnp.dot(h, w_ref[:, c0:c0 + width], preferred_element_type=F32)

    seg = seg_ref[...]
    q0 = SSM_WIDTH
    k0 = q0 + D_MODEL
    gs0 = k0 + D_MODEL
    ga0 = gs0 + D_MODEL
    half = D_MODEL // 2
    tasks = []

    def store(out_ref, cols, fn):
        def epilogue(r):
            out_ref[:, cols] = fn(r)
        return epilogue

    tasks.append((proj(0, SSM_WIDTH), store(u_ref, slice(None), lambda r: r)))

    def head_norm_tasks(base, gain_ref, out_ref, scale, c):
        cols = slice(c * MXU_DIM, (c + 1) * MXU_DIM)
        kept = {}

        def after_proj(y):
            kept["y"] = y
            kept["sq"] = (y * y).astype(BF16)

        def after_sum(ss):
            yn = kept["y"] * lax.rsqrt(ss * (1.0 / HEAD_DIM) + NORM_EPS) * gain_ref[:, cols]
            out_ref[:, cols] = (yn * scale).astype(BF16)

        return ((proj(base + c * MXU_DIM, MXU_DIM), after_proj),
                (lambda: jnp.dot(kept["sq"], seg, preferred_element_type=F32), after_sum))

    pairs = [head_norm_tasks(base, gain_ref, out_ref, scale, c)
             for base, gain_ref, out_ref, scale in ((q0, qg_ref, q_ref, Q_SCALE), (k0, kg_ref, k_ref, 1.0))
             for c in range(D_MODEL // MXU_DIM)]
    tasks.append(pairs[0][0])
    for prev, cur in zip(pairs, pairs[1:]):
        tasks += [cur[0], prev[1]]
    tasks.append(pairs[-1][1])

    to_gate = lambda r: jax.nn.sigmoid(r).astype(BF16)
    for c in range(2):
        cols = slice(c * half, (c + 1) * half)
        tasks.append((proj(gs0 + c * half, half), store(gs_ref, cols, to_gate)))
        tasks.append((proj(ga0 + c * half, half), store(ga_ref, cols, to_gate)))

    def vt_task(r0):
        def matmul():
            return lax.dot_general(wvt_ref[r0:r0 + MXU_DIM, :], h, (((1,), (1,)), ((), ())),
                                   preferred_element_type=F32)

        def epilogue(r):
            vt_ref[r0:r0 + MXU_DIM, :] = r.astype(BF16)
        return matmul, epilogue

    tasks += [vt_task(r0) for r0 in range(0, D_MODEL, MXU_DIM)]
    _run_lookahead(tasks)


def _in_proj(x2d, gain1, w_in, q_gain, k_gain):
    n_tok = x2d.shape[0]
    tm = min(ROW_TILE, n_tok)
    v0 = SSM_WIDTH + 2 * D_MODEL
    w_bf = w_in.astype(BF16)
    w_main = jnp.concatenate([w_bf[:, :v0], w_bf[:, v0 + D_MODEL:]], axis=1)
    w_vt = w_bf[:, v0:v0 + D_MODEL].T
    in_width = w_main.shape[1]
    seg = (jnp.arange(MXU_DIM)[:, None] // HEAD_DIM == jnp.arange(MXU_DIM)[None, :] // HEAD_DIM).astype(BF16)
    reps = D_MODEL // HEAD_DIM
    qg = jnp.tile(q_gain.astype(F32), reps)[None, :]
    kg = jnp.tile(k_gain.astype(F32), reps)[None, :]
    row = lambda w: pl.BlockSpec((tm, w), lambda i: (i, 0))
    tok = jax.ShapeDtypeStruct((n_tok, D_MODEL), BF16)
    out_shapes = [jax.ShapeDtypeStruct((n_tok, SSM_WIDTH), F32), tok, tok,
                  jax.ShapeDtypeStruct((D_MODEL, n_tok), BF16), tok, tok]
    vt_spec = pl.BlockSpec((D_MODEL, tm), lambda i: (0, i))
    return pl.pallas_call(
        _inproj_kernel,
        grid=(n_tok // tm,),
        in_specs=[row(D_MODEL), _full((1, D_MODEL)), _full((D_MODEL, in_width)), _full((D_MODEL, D_MODEL)),
                  _full((1, D_MODEL)), _full((1, D_MODEL)), _full((MXU_DIM, MXU_DIM))],
        out_specs=[row(SSM_WIDTH), row(D_MODEL), row(D_MODEL), vt_spec, row(D_MODEL), row(D_MODEL)],
        out_shape=out_shapes,
        compiler_params=_cparams(("parallel",)),
        name="in_proj",
    )(x2d, gain1, w_main, w_vt, qg, kg, seg)


def _ssm_kernel(u_ref, bblk_ref, a_ref, cblk_ref, d_ref, wglu_ref, bglu_ref, o_ref, bu_ref, st_ref):
    chunk = u_ref.shape[0] // SUBLANES

    @pl.when(pl.program_id(0) == 0)
    def _():
        st_ref[...] = jnp.zeros_like(st_ref)

    u = u_ref[...]
    u_bf = u.astype(BF16)
    tiles_per_part = N_STATE // MXU_DIM
    ch_per_tile = SSM_WIDTH // tiles_per_part
    for n in range(2 * tiles_per_part):
        ch0 = ((n % tiles_per_part) * ch_per_tile) // LANES * LANES
        lanes = slice(n * MXU_DIM, (n + 1) * MXU_DIM)
        bu_ref[:, lanes] = jnp.dot(u_bf[:, ch0:ch0 + LANES], bblk_ref[ch0:ch0 + LANES, lanes],
                                   preferred_element_type=F32)

    for j in range(N_STATE // SCAN_LANES):
        re = slice(j * SCAN_LANES, (j + 1) * SCAN_LANES)
        im = slice(N_STATE + j * SCAN_LANES, N_STATE + (j + 1) * SCAN_LANES)
        ar = a_ref[:, re]
        ai = a_ref[:, im]

        def step(t, carry, re=re, im=im, ar=ar, ai=ai):
            xr, xi = carry
            rows = pl.ds(pl.multiple_of(t * SUBLANES, SUBLANES), SUBLANES)
            nr = ar * xr - ai * xi + bu_ref[rows, re]
            ni = ar * xi + ai * xr + bu_ref[rows, im]
            bu_ref[rows, re] = nr
            bu_ref[rows, im] = ni
            return nr, ni

        xr, xi = lax.fori_loop(0, chunk, step, (st_ref[:, re], st_ref[:, im]), unroll=4)
        st_ref[:, re] = xr
        st_ref[:, im] = xi

    n_out = SSM_WIDTH // MXU_DIM
    lanes_per_out = N_STATE // n_out
    ys = []
    for j in range(n_out):
        cols = slice(j * MXU_DIM, (j + 1) * MXU_DIM)
        acc = None
        for part in range(2):
            lanes = slice(part * N_STATE + j * lanes_per_out, part * N_STATE + (j + 1) * lanes_per_out)
            term = jnp.dot(bu_ref[:, lanes].astype(BF16), cblk_ref[lanes, cols], preferred_element_type=F32)
            acc = term if acc is None else acc + term
        ys.append(acc)
    y = jnp.concatenate(ys, axis=1) + d_ref[...] * u
    z = jax.nn.gelu(y)
    gate = jax.nn.sigmoid(jnp.dot(z.astype(BF16), wglu_ref[...], preferred_element_type=F32) + bglu_ref[...])
    o_ref[...] = (z * gate).astype(BF16)


def _ssm_params(lambda_re, lambda_im, log_dt, b_re, b_im, c_re, c_im):
    dt = jnp.exp(log_dt.astype(F32))[:, None]
    lr = jnp.minimum(lambda_re.astype(F32), -1e-4)
    li = lambda_im.astype(F32)
    mag = jnp.exp(lr * dt)
    abar_re = mag * jnp.cos(li * dt)
    abar_im = mag * jnp.sin(li * dt)
    den = lr * lr + li * li
    nr = abar_re - 1.0
    coef_re = (nr * lr + abar_im * li) / den
    coef_im = (abar_im * lr - nr * li) / den
    br = b_re.astype(F32)
    bi = b_im.astype(F32)
    bbar_re = coef_re[..., None] * br - coef_im[..., None] * bi
    bbar_im = coef_re[..., None] * bi + coef_im[..., None] * br
    eye = jnp.eye(SSM_GROUPS, dtype=F32)

    def expand_b(b):
        return jnp.einsum('gph,gk->ghkp', b, eye).reshape(SSM_WIDTH, N_STATE)

    def expand_c(c):
        return jnp.einsum('ghp,gk->gpkh', c, eye).reshape(N_STATE, SSM_WIDTH)

    bblk = jnp.concatenate([expand_b(bbar_re), expand_b(bbar_im)], axis=1).astype(BF16)
    cblk = jnp.concatenate([expand_c(c_re.astype(F32)), -expand_c(c_im.astype(F32))], axis=0).astype(BF16)
    a_row = jnp.concatenate([abar_re.reshape(-1), abar_im.reshape(-1)])[None, :]
    return bblk, cblk, jnp.broadcast_to(a_row, (SUBLANES, 2 * N_STATE))


def _ssm(u_tb, seq, bblk, cblk, a_tile, d_skip, w_glu_bf, b_glu):
    chunk = min(SSM_CHUNK, seq)
    rows = chunk * SUBLANES
    return pl.pallas_call(
        _ssm_kernel,
        grid=(seq // chunk,),
        in_specs=[pl.BlockSpec((rows, SSM_WIDTH), lambda c: (c, 0)),
                  _full(bblk.shape), _full(a_tile.shape), _full(cblk.shape),
                  _full((1, SSM_WIDTH)), _full(w_glu_bf.shape), _full((1, SSM_WIDTH))],
        out_specs=pl.BlockSpec((rows, SSM_WIDTH), lambda c: (c, 0)),
        out_shape=jax.ShapeDtypeStruct(u_tb.shape, BF16),
        scratch_shapes=[pltpu.VMEM((rows, 2 * N_STATE), F32), pltpu.VMEM((SUBLANES, 2 * N_STATE), F32)],
        compiler_params=_cparams(("arbitrary",)),
        name="ssm",
    )(u_tb, bblk, a_tile, cblk, d_skip, w_glu_bf, b_glu)


def _attn_kernel(lam_ref, sg_ref, q_ref, k_ref, vt_ref, o_ref, acc_ref, *, out_scale, blk, nh):
    seq = q_ref.shape[1]
    hw = 2 * HEAD_DIM
    lane = lax.broadcasted_iota(jnp.int32, (blk, hw), 1)
    key_i = lax.broadcasted_iota(jnp.int32, (blk, blk), 0)
    qry_i = lax.broadcasted_iota(jnp.int32, (blk, blk), 1)
    keep = key_i <= qry_i
    contract_last = (((1,), (1,)), ((), ()))
    lam = lam_ref[0]
    n_chain = 2 * nh

    def q_block(qi, _):
        qrows = pl.ds(pl.multiple_of(qi * blk, blk), blk)
        qs = []
        for hh in range(nh):
            q = q_ref[0, qrows, hh * hw:(hh + 1) * hw]
            zero = jnp.zeros_like(q)
            qs += [jnp.where(lane < HEAD_DIM, q, zero), jnp.where(lane >= HEAD_DIM, q, zero)]
        acc_ref[...] = jnp.zeros_like(acc_ref)

        def kv_block(kb, carry, masked):
            krows = pl.ds(pl.multiple_of(kb * blk, blk), blk)
            scores = []
            for c in range(n_chain):
                hh = c // 2
                k = k_ref[0, krows, hh * hw:(hh + 1) * hw]
                scores.append(lax.dot_general(k, qs[c], contract_last, preferred_element_type=F32))
            out, probs, alphas = [], [], []
            for c in range(n_chain):
                m, l = carry[2 * c], carry[2 * c + 1]
                s = jnp.where(keep, scores[c], _NEG) if masked else scores[c]
                m_new = jnp.maximum(m, jnp.max(s, axis=0, keepdims=True))
                p = jnp.exp2(s - m_new)
                alpha = jnp.exp2(m - m_new)
                out += [m_new, alpha * l + jnp.sum(p, axis=0, keepdims=True)]
                probs.append(p.astype(BF16))
                alphas.append(alpha)
            for c in range(n_chain):
                hh = c // 2
                vt = vt_ref[hh * hw:(hh + 1) * hw, krows]
                acc_ref[c] = alphas[c] * acc_ref[c] + jnp.dot(vt, probs[c], preferred_element_type=F32)
            return tuple(out)

        carry = (jnp.full((1, blk), _NEG, F32), jnp.zeros((1, blk), F32)) * n_chain
        carry = lax.fori_loop(0, qi, lambda kb, c: kv_block(kb, c, False), carry)
        carry = kv_block(qi, carry, True)
        for hh in range(nh):
            l1, l2 = carry[4 * hh + 1], carry[4 * hh + 3]
            ot = acc_ref[2 * hh] * (1.0 / l1) - acc_ref[2 * hh + 1] * (lam / l2)
            ot = ot * lax.rsqrt(jnp.mean(ot * ot, axis=0, keepdims=True) + NORM_EPS)
            o_ref[0, qrows, hh * hw:(hh + 1) * hw] = (ot.T * sg_ref[...] * out_scale).astype(BF16)
        return 0

    lax.fori_loop(0, seq // blk, q_block, 0)


def _attention(q, k, vt, lam, subln_gain, lambda_init, bsz, seq):
    blk = min(ATTN_BLOCK, seq)
    nh = ATTN_HEADS_PER_STEP
    hw = 2 * HEAD_DIM
    q3, k3 = (a.reshape(bsz, seq, D_MODEL) for a in (q, k))
    tok_spec = pl.BlockSpec((1, seq, nh * hw), lambda b, h: (b, 0, h))
    out = pl.pallas_call(
        functools.partial(_attn_kernel, out_scale=1.0 - lambda_init, blk=blk, nh=nh),
        grid=(bsz, HEADS // nh),
        in_specs=[pl.BlockSpec(memory_space=pltpu.SMEM), _full((1, hw)), tok_spec, tok_spec,
                  pl.BlockSpec((nh * hw, seq), lambda b, h: (h, b))],
        out_specs=tok_spec,
        out_shape=jax.ShapeDtypeStruct((bsz, seq, D_MODEL), BF16),
        scratch_shapes=[pltpu.VMEM((2 * nh, hw, blk), F32)],
        compiler_params=_cparams(("parallel", "parallel")),
        name="diff_attn",
    )(lam, subln_gain.astype(F32)[None, :], q3, k3, vt)
    return out.reshape(bsz * seq, D_MODEL)


def _pack_rows(y):
    bits = lax.bitcast_convert_type(y.astype(BF16).astype(F32), jnp.uint32)
    return (bits[:, :PACK_W] >> 16) | (bits[:, PACK_W:] & jnp.uint32(0xFFFF0000))


def _unpack_rows(w):
    lo = lax.bitcast_convert_type(w << 16, F32)
    hi = lax.bitcast_convert_type(w & jnp.uint32(0xFFFF0000), F32)
    return jnp.concatenate([lo, hi], axis=1)


def _merge_kernel(so_ref, ao_ref, gs_ref, ga_ref, x_ref, wps_ref, wpa_ref, wo_ref, g2_ref,
                  wrh_ref, wrl_ref, br_ref, tri_ref,
                  x1_ref, hlo_ref, hhi_ref, route_ref, gate_ref, cnt_ref, run_ref, *, region):
    @pl.when(pl.program_id(0) == 0)
    def _():
        run_ref[...] = jnp.zeros_like(run_ref)

    tm = x_ref.shape[0]
    n_part = 2 if tm % (2 * MXU_DIM) == 0 else 1
    rows_per = tm // n_part
    lane = lax.broadcasted_iota(jnp.int32, (rows_per, LANES), 1)
    lane_f = lane.astype(F32)
    parts = [dict(rows=slice(p * rows_per, (p + 1) * rows_per)) for p in range(n_part)]

    def stage_proj(st):
        def matmul():
            return (jnp.dot(so_ref[st["rows"], :], wps_ref[...], preferred_element_type=F32),
                    jnp.dot(ao_ref[st["rows"], :], wpa_ref[...], preferred_element_type=F32))

        def epilogue(r):
            ps, pa = r
            merged = gs_ref[st["rows"], :].astype(F32) * ps + ga_ref[st["rows"], :].astype(F32) * pa
            st["merged"] = merged.astype(BF16)
        return matmul, epilogue

    def stage_out(st):
        def matmul():
            return jnp.dot(st["merged"], wo_ref[...], preferred_element_type=F32)

        def epilogue(r):
            x1 = x_ref[st["rows"], :] + r
            x1_ref[st["rows"], :] = x1
            h2 = x1 * lax.rsqrt(jnp.mean(x1 * x1, axis=-1, keepdims=True) + NORM_EPS) * g2_ref[...]
            words = _pack_rows(h2)
            hlo_ref[st["rows"], :] = words[:, :PACK_HALF]
            hhi_ref[st["rows"], :] = words[:, PACK_HALF:]
            st["h_hi"] = h2.astype(BF16)
            st["h_lo"] = (h2 - st["h_hi"].astype(F32)).astype(BF16)
        return matmul, epilogue

    def stage_router(st):
        def matmul():
            return (jnp.dot(st["h_hi"], wrh_ref[...], preferred_element_type=F32)
                    + jnp.dot(st["h_lo"], wrh_ref[...], preferred_element_type=F32)
                    + jnp.dot(st["h_hi"], wrl_ref[...], preferred_element_type=F32))

        def epilogue(r):
            work = r + br_ref[...]
            onehots, vals, ids = [], [], []
            for _ in range(TOP_K):
                m = jnp.max(work, axis=-1, keepdims=True)
                idx = jnp.min(jnp.where(work == m, lane_f, float(LANES)), axis=-1, keepdims=True)
                oh = lane_f == idx
                onehots.append(oh)
                vals.append(m)
                ids.append(idx.astype(jnp.int32))
                work = jnp.where(oh, -jnp.inf, work)
            exps = [jnp.exp(v - vals[0]) for v in vals]
            den = exps[0] + exps[1] + exps[2] + exps[3]
            st.update(onehots=onehots, ids=ids, gates=[e / den for e in exps],
                      multi=(onehots[0] | onehots[1] | onehots[2] | onehots[3]).astype(F32))
        return matmul, epilogue

    _run_lookahead([stage(st) for stage in (stage_proj, stage_out, stage_router) for st in parts])

    multi = jnp.concatenate([st["multi"] for st in parts], axis=0)
    before = jnp.dot(tri_ref[...], multi.astype(BF16), preferred_element_type=F32) + run_ref[...]
    for st in parts:
        route = jnp.zeros((rows_per, LANES), jnp.int32)
        gates = jnp.zeros((rows_per, LANES), F32)
        for kk in range(TOP_K):
            rank = jnp.sum(jnp.where(st["onehots"][kk], before[st["rows"], :], 0.0), axis=-1, keepdims=True)
            route = jnp.where(lane == kk, st["ids"][kk] * region + rank.astype(jnp.int32), route)
            gates = jnp.where(lane == kk, st["gates"][kk], gates)
        route_ref[st["rows"], :] = route
        gate_ref[st["rows"], :] = gates
    run = run_ref[...] + jnp.sum(multi, axis=0, keepdims=True)
    run_ref[...] = run
    cnt_ref[...] = run.astype(jnp.int32)


def _merge_route(so, ao, gs, ga, x2d, wps, wpa, wo, gain2, w_router, b_router, region):
    n_tok = x2d.shape[0]
    tm = min(ROW_TILE, n_tok)
    wr = jnp.zeros((D_MODEL, LANES), F32).at[:, :N_EXPERTS].set(w_router.astype(F32))
    wr_hi = wr.astype(BF16)
    wr_lo = (wr - wr_hi.astype(F32)).astype(BF16)
    br = jnp.full((1, LANES), -jnp.inf, F32).at[0, :N_EXPERTS].set(b_router.astype(F32))
    tri = (jnp.arange(tm)[:, None] > jnp.arange(tm)[None, :]).astype(BF16)
    row = lambda w: pl.BlockSpec((tm, w), lambda i: (i, 0))
    out_shapes = [jax.ShapeDtypeStruct((n_tok, D_MODEL), F32),
                  jax.ShapeDtypeStruct((n_tok, PACK_HALF), jnp.uint32),
                  jax.ShapeDtypeStruct((n_tok, PACK_HALF), jnp.uint32),
                  jax.ShapeDtypeStruct((n_tok, LANES), jnp.int32),
                  jax.ShapeDtypeStruct((n_tok, LANES), F32),
                  jax.ShapeDtypeStruct((1, LANES), jnp.int32)]
    return pl.pallas_call(
        functools.partial(_merge_kernel, region=region),
        grid=(n_tok // tm,),
        in_specs=[row(SSM_WIDTH), row(D_MODEL), row(D_MODEL), row(D_MODEL), row(D_MODEL),
                  _full(wps.shape), _full(wpa.shape), _full(wo.shape), _full((1, D_MODEL)),
                  _full(wr_hi.shape), _full(wr_lo.shape), _full((1, LANES)), _full((tm, tm))],
        out_specs=[row(D_MODEL), row(PACK_HALF), row(PACK_HALF), row(LANES), row(LANES), _full((1, LANES))],
        out_shape=out_shapes,
        scratch_shapes=[pltpu.VMEM((1, LANES), F32)],
        compiler_params=_cparams(("arbitrary",)),
        name="merge_route",
    )(so, ao, gs, ga, x2d, wps, wpa, wo, gain2, wr_hi, wr_lo, br, tri)


def _sc_scatter_rows(rows, dest, n_slots):
    n_tok, width = rows.shape
    mesh = plsc.VectorSubcoreMesh(core_axis_name="core", subcore_axis_name="subcore")

    @pl.kernel(out_type=jax.ShapeDtypeStruct((n_slots, width), rows.dtype), mesh=mesh, scratch_types=[])
    def scatter(rows_hbm, dest_hbm, out_hbm):
        def body(rows_vmem, dest_vmem):
            pltpu.sync_copy(rows_vmem, out_hbm.at[dest_vmem.at[0]])

        pltpu.emit_pipeline(
            body,
            grid=(TOP_K, n_tok // SC_WINDOW),
            in_specs=[pl.BlockSpec((SC_WINDOW, width), lambda k, i: (i, 0)),
                      pl.BlockSpec((1, SC_WINDOW), lambda k, i: (k, i))],
            out_specs=[],
            core_axis_name=("core", "subcore"),
            dimension_semantics=(pltpu.PARALLEL, pltpu.PARALLEL),
        )(rows_hbm, dest_hbm)

    return scatter(rows, dest)


def _sc_gather_rows(table, idx):
    n = idx.shape[1]
    width = table.shape[1]
    mesh = plsc.VectorSubcoreMesh(core_axis_name="core", subcore_axis_name="subcore")

    @pl.kernel(out_type=jax.ShapeDtypeStruct((n, width), table.dtype), mesh=mesh, scratch_types=[])
    def gather(table_hbm, idx_hbm, out_hbm):
        def body(idx_vmem, out_vmem):
            pltpu.sync_copy(table_hbm.at[idx_vmem.at[0]], out_vmem)

        pltpu.emit_pipeline(
            body,
            grid=(n // SC_WINDOW,),
            in_specs=[pl.BlockSpec((1, SC_WINDOW), lambda i: (0, i))],
            out_specs=[pl.BlockSpec((SC_WINDOW, width), lambda i: (i, 0))],
            core_axis_name=("core", "subcore"),
            dimension_semantics=(pltpu.PARALLEL,),
        )(idx_hbm, out_hbm)

    return gather(table, idx)


def _expert_kernel(cnt_ref, xlo_hbm, xhi_hbm, w1_ref, b1_ref, w2_ref, b2_ref, perm_ref, ylo_hbm, yhi_hbm,
                   w1p_ref, w2b_ref, act_ref, xin_ref, yout_ref, in_sem, out_sem, *, region):
    e = pl.program_id(0)
    n_blk = (cnt_ref[e] + EXPERT_BLOCK - 1) // EXPERT_BLOCK
    base = e * region
    n_chunks = (2 * D_FF) // MXU_DIM
    x_hbm = (xlo_hbm, xhi_hbm)
    y_hbm = (ylo_hbm, yhi_hbm)

    def rows_of(j):
        return pl.ds(pl.multiple_of(base + j * EXPERT_BLOCK, EXPERT_BLOCK), EXPERT_BLOCK)

    def in_copy(j, slot, half):
        return pltpu.make_async_copy(x_hbm[half].at[rows_of(j), :], xin_ref.at[slot, half], in_sem.at[slot, half])

    def out_copy(j, slot, half):
        return pltpu.make_async_copy(yout_ref.at[slot, half], y_hbm[half].at[rows_of(j), :], out_sem.at[slot, half])

    @pl.when(n_blk > 0)
    def _():
        for half in range(2):
            in_copy(0, 0, half).start()

    perm = perm_ref[...]
    for c in range(n_chunks):
        cols = slice(c * MXU_DIM, (c + 1) * MXU_DIM)
        w1p_ref[:, cols] = jnp.dot(w1_ref[0, :, cols].astype(BF16), perm,
                                   preferred_element_type=F32).astype(BF16)
    w2b_ref[...] = w2_ref[0].astype(BF16)

    def block(j, _):
        slot = j % 2
        for half in range(2):
            in_copy(j, slot, half).wait()

        @pl.when(j + 1 < n_blk)
        def _():
            for half in range(2):
                in_copy(j + 1, 1 - slot, half).start()

        @pl.when(j >= 2)
        def _():
            for half in range(2):
                out_copy(j - 2, slot, half).wait()

        words = jnp.concatenate([xin_ref[slot, 0], xin_ref[slot, 1]], axis=1)
        x = _unpack_rows(words).astype(BF16)

        def up_task(c):
            cols = slice(c * MXU_DIM, (c + 1) * MXU_DIM)

            def epilogue(r):
                h = r + b1_ref[0, :, cols]
                gate = jnp.minimum(h[:, :LANES], SWIGLU_LIMIT)
                up = jnp.clip(h[:, LANES:], -SWIGLU_LIMIT, SWIGLU_LIMIT)
                glu = gate * jax.nn.sigmoid(SWIGLU_ALPHA * gate)
                act_ref[:, c * LANES:(c + 1) * LANES] = ((up + 1.0) * glu).astype(BF16)
            return (lambda: jnp.dot(x, w1p_ref[:, cols], preferred_element_type=F32)), epilogue

        _run_lookahead([up_task(c) for c in range(n_chunks)])
        y = jnp.dot(act_ref[...], w2b_ref[...], preferred_element_type=F32) + b2_ref[0]
        words = _pack_rows(y)
        yout_ref[slot, 0] = words[:, :PACK_HALF]
        yout_ref[slot, 1] = words[:, PACK_HALF:]
        for half in range(2):
            out_copy(j, slot, half).start()
        return 0

    lax.fori_loop(0, n_blk, block, 0)

    for back in (2, 1):
        @pl.when(n_blk >= back)
        def _(back=back):
            j = n_blk - back
            for half in range(2):
                out_copy(j, j % 2, half).wait()


def _gate_up_order():
    j = jnp.arange(MXU_DIM)
    within = jnp.where(j < LANES, 2 * j, 2 * (j - LANES) + 1)
    return within


def _experts(xs_lo, xs_hi, w1, b1, w2, b2, counts, region):
    within = _gate_up_order()
    perm = (jnp.arange(MXU_DIM)[:, None] == within[None, :]).astype(BF16)
    order = (jnp.arange(0, 2 * D_FF, MXU_DIM)[:, None] + within[None, :]).reshape(-1)
    b1p = b1.astype(F32)[:, order][:, None, :]
    b2r = b2.astype(F32)[:, None, :]
    tb = EXPERT_BLOCK
    hbm = pl.BlockSpec(memory_space=pl.ANY)
    wspec = lambda shape: pl.BlockSpec((1,) + shape, lambda e, cnt: (e, 0, 0))
    grid_spec = pltpu.PrefetchScalarGridSpec(
        num_scalar_prefetch=1,
        grid=(N_EXPERTS,),
        in_specs=[hbm, hbm, wspec((D_MODEL, 2 * D_FF)), wspec((1, 2 * D_FF)),
                  wspec((D_FF, D_MODEL)), wspec((1, D_MODEL)),
                  pl.BlockSpec((MXU_DIM, MXU_DIM), lambda e, cnt: (0, 0))],
        out_specs=[hbm, hbm],
        scratch_shapes=[pltpu.VMEM((D_MODEL, 2 * D_FF), BF16), pltpu.VMEM((D_FF, D_MODEL), BF16),
                        pltpu.VMEM((tb, D_FF), BF16),
                        pltpu.VMEM((2, 2, tb, PACK_HALF), jnp.uint32), pltpu.VMEM((2, 2, tb, PACK_HALF), jnp.uint32),
                        pltpu.SemaphoreType.DMA((2, 2)), pltpu.SemaphoreType.DMA((2, 2))],
    )
    return pl.pallas_call(
        functools.partial(_expert_kernel, region=region),
        grid_spec=grid_spec,
        out_shape=[jax.ShapeDtypeStruct(xs_lo.shape, jnp.uint32)] * 2,
        compiler_params=_cparams(("arbitrary",)),
        name="experts",
    )(counts, xs_lo, xs_hi, w1, b1p, w2, b2r, perm)


def _combine_kernel(x1_ref, gate_ref, *refs):
    lo_refs, hi_refs, o_ref = refs[:TOP_K], refs[TOP_K:2 * TOP_K], refs[2 * TOP_K]
    acc = x1_ref[...]
    gates = gate_ref[...]
    for kk in range(TOP_K):
        words = jnp.concatenate([lo_refs[kk][...], hi_refs[kk][...]], axis=1)
        acc = acc + gates[:, kk:kk + 1] * _unpack_rows(words)
    o_ref[...] = acc


def _combine(x1, gates, yg_lo, yg_hi):
    n_tok = x1.shape[0]
    tm = min(ROW_TILE, n_tok)
    nblk = n_tok // tm
    row = lambda w: pl.BlockSpec((tm, w), lambda i: (i, 0))
    plane = lambda kk: pl.BlockSpec((tm, PACK_HALF), lambda i, kk=kk: (kk * nblk + i, 0))
    planes = [plane(kk) for kk in range(TOP_K)]
    return pl.pallas_call(
        _combine_kernel,
        grid=(nblk,),
        in_specs=[row(D_MODEL), row(LANES)] + planes + planes,
        out_specs=row(D_MODEL),
        out_shape=jax.ShapeDtypeStruct((n_tok, D_MODEL), F32),
        compiler_params=_cparams(("parallel",)),
        name="combine",
    )(x1, gates, *([yg_lo] * TOP_K), *([yg_hi] * TOP_K))


def kernel(x, norm1_gain, w_in, lambda_re, lambda_im, log_dt, ssm_b_re, ssm_b_im, ssm_c_re, ssm_c_im, ssm_d, w_glu, b_glu, q_norm_gain, k_norm_gain, lambda_q1, lambda_k1, lambda_q2, lambda_k2, subln_gain, w_proj_ssm, w_proj_attn, w_out, norm2_gain, w_router, b_router, w_exp1, b_exp1, w_exp2, b_exp2):
    bsz, seq, d = x.shape
    n_tok = bsz * seq
    depth = norm1_gain.shape[0]
    row1 = lambda a: a.astype(F32).reshape(1, -1)
    for l in range(depth):
        lambda_init = 0.8 - 0.6 * math.exp(-0.3 * l)
        x2d = x.reshape(n_tok, d)

        u, q, k, vt, gs, ga = _in_proj(x2d, row1(norm1_gain[l]), w_in[l], q_norm_gain[l], k_norm_gain[l])

        bblk, cblk, a_tile = _ssm_params(lambda_re[l], lambda_im[l], log_dt[l], ssm_b_re[l], ssm_b_im[l],
                                         ssm_c_re[l], ssm_c_im[l])
        u_tb = u.reshape(bsz, seq, SSM_WIDTH).transpose(1, 0, 2).reshape(n_tok, SSM_WIDTH)
        so_tb = _ssm(u_tb, seq, bblk, cblk, a_tile, row1(ssm_d[l]), w_glu[l].astype(BF16), row1(b_glu[l]))
        so = so_tb.reshape(seq, bsz, SSM_WIDTH).transpose(1, 0, 2).reshape(n_tok, SSM_WIDTH)

        lam = (jnp.exp(jnp.sum(lambda_q1[l].astype(F32) * lambda_k1[l].astype(F32)))
               - jnp.exp(jnp.sum(lambda_q2[l].astype(F32) * lambda_k2[l].astype(F32)))
               + lambda_init).reshape(1)
        ao = _attention(q, k, vt, lam, subln_gain[l], lambda_init, bsz, seq)

        region = n_tok
        x1, h_lo, h_hi, route, gates, counts = _merge_route(
            so, ao, gs, ga, x2d, w_proj_ssm[l].astype(BF16), w_proj_attn[l].astype(BF16),
            w_out[l].astype(BF16), row1(norm2_gain[l]), w_router[l], b_router[l], region)

        dest = route[:, :TOP_K].T
        n_slots = N_EXPERTS * region
        xs_lo = _sc_scatter_rows(h_lo, dest, n_slots)
        xs_hi = _sc_scatter_rows(h_hi, dest, n_slots)

        ys_lo, ys_hi = _experts(xs_lo, xs_hi, w_exp1[l], b_exp1[l], w_exp2[l], b_exp2[l],
                                counts[0, :N_EXPERTS], region)

        flat = dest.reshape(1, TOP_K * n_tok)
        yg_lo = _sc_gather_rows(ys_lo, flat)
        yg_hi = _sc_gather_rows(ys_hi, flat)
        x = _combine(x1, gates, yg_lo, yg_hi).reshape(bsz, seq, d)
    return x
```

```python
import functools
import math

import jax
import jax.numpy as jnp
from jax import lax
from jax.experimental import pallas as pl
from jax.experimental.pallas import tpu as pltpu
from jax.experimental.pallas import tpu_sc as plsc

F32 = jnp.float32
BF16 = jnp.bfloat16

D_MODEL = 1024
NORM_EPS = 1e-5
SSM_WIDTH = 512
SSM_GROUP = 16
SSM_GROUPS = 32
SSM_STATE = 64
N_STATE = SSM_GROUPS * SSM_STATE
HEADS = 8
HEAD_DIM = 64
N_EXPERTS = 32
TOP_K = 4
D_FF = 1024
SWIGLU_ALPHA = 1.702
SWIGLU_LIMIT = 7.0

LANES = 128
SUBLANES = 8
MXU_DIM = 256
VMEM_LIMIT = 56 * 1024 * 1024

ROW_TILE = 512
SSM_CHUNK = 128
SCAN_LANES = 512
ATTN_BLOCK = 256
ATTN_HEADS_PER_STEP = 8
EXPERT_BLOCK = 512
SC_WINDOW = 128
PACK_W = D_MODEL // 2
PACK_HALF = PACK_W // 2

_NEG = -1e30
Q_SCALE = math.log2(math.e) / math.sqrt(HEAD_DIM)


def _cparams(sem):
    return pltpu.CompilerParams(dimension_semantics=sem, vmem_limit_bytes=VMEM_LIMIT)


def _full(shape):
    nd = len(shape)
    return pl.BlockSpec(shape, lambda *_: (0,) * nd)


def _sigmoid(x):
    return 0.5 * jnp.tanh(0.5 * x) + 0.5


def _run_lookahead(tasks):
    pending = tasks[0][0]()
    for i, (_, epilogue) in enumerate(tasks):
        result = pending
        if i + 1 < len(tasks):
            pending = tasks[i + 1][0]()
        epilogue(result)


def _inproj_kernel(x_ref, g1_ref, w_ref, wvt_ref, qg_ref, kg_ref, seg_ref,
                   u_ref, q_ref, k_ref, vt_ref, gs_ref, ga_ref):
    x = x_ref[...]
    ms = jnp.mean(x * x, axis=-1, keepdims=True)
    h = (x * lax.rsqrt(ms + NORM_EPS) * g1_ref[...]).astype(BF16)

    def proj(c0, width):
        return lambda: jnp.dot(h, w_ref[:, c0:c0 + width], preferred_element_type=F32)

    seg = seg_ref[...]
    q0 = SSM_WIDTH
    k0 = q0 + D_MODEL
    gs0 = k0 + D_MODEL
    ga0 = gs0 + D_MODEL
    half = D_MODEL // 2
    tasks = []

    def store(out_ref, cols, fn):
        def epilogue(r):
            out_ref[:, cols] = fn(r)
        return epilogue

    tasks.append((proj(0, SSM_WIDTH), store(u_ref, slice(None), lambda r: r)))

    def head_norm_tasks(base, gain_ref, out_ref, scale, c):
        cols = slice(c * MXU_DIM, (c + 1) * MXU_DIM)
        kept = {}

        def after_proj(y):
            kept["y"] = y
            kept["sq"] = (y * y).astype(BF16)

        def after_sum(ss):
            yn = kept["y"] * lax.rsqrt(ss * (1.0 / HEAD_DIM) + NORM_EPS) * gain_ref[:, cols]
            out_ref[:, cols] = (yn * scale).astype(BF16)

        return ((proj(base + c * MXU_DIM, MXU_DIM), after_proj),
                (lambda: jnp.dot(kept["sq"], seg, preferred_element_type=F32), after_sum))

    pairs = [head_norm_tasks(base, gain_ref, out_ref, scale, c)
             for base, gain_ref, out_ref, scale in ((q0, qg_ref, q_ref, Q_SCALE), (k0, kg_ref, k_ref, 1.0))
             for c in range(D_MODEL // MXU_DIM)]
    tasks.append(pairs[0][0])
    for prev, cur in zip(pairs, pairs[1:]):
        tasks += [cur[0], prev[1]]
    tasks.append(pairs[-1][1])

    to_gate = lambda r: _sigmoid(r).astype(BF16)
    for c in range(2):
        cols = slice(c * half, (c + 1) * half)
        tasks.append((proj(gs0 + c * half, half), store(gs_ref, cols, to_gate)))
        tasks.append((proj(ga0 + c * half, half), store(ga_ref, cols, to_gate)))

    def vt_task(r0):
        def matmul():
            return lax.dot_general(wvt_ref[r0:r0 + MXU_DIM, :], h, (((1,), (1,)), ((), ())),
                                   preferred_element_type=F32)

        def epilogue(r):
            vt_ref[r0:r0 + MXU_DIM, :] = r.astype(BF16)
        return matmul, epilogue

    tasks += [vt_task(r0) for r0 in range(0, D_MODEL, MXU_DIM)]
    _run_lookahead(tasks)


def _in_proj(x2d, gain1, w_in, q_gain, k_gain):
    n_tok = x2d.shape[0]
    tm = min(ROW_TILE, n_tok)
    v0 = SSM_WIDTH + 2 * D_MODEL
    w_bf = w_in.astype(BF16)
    w_main = jnp.concatenate([w_bf[:, :v0], w_bf[:, v0 + D_MODEL:]], axis=1)
    w_vt = w_bf[:, v0:v0 + D_MODEL].T
    in_width = w_main.shape[1]
    seg = (jnp.arange(MXU_DIM)[:, None] // HEAD_DIM == jnp.arange(MXU_DIM)[None, :] // HEAD_DIM).astype(BF16)
    reps = D_MODEL // HEAD_DIM
    qg = jnp.tile(q_gain.astype(F32), reps)[None, :]
    kg = jnp.tile(k_gain.astype(F32), reps)[None, :]
    row = lambda w: pl.BlockSpec((tm, w), lambda i: (i, 0))
    tok = jax.ShapeDtypeStruct((n_tok, D_MODEL), BF16)
    out_shapes = [jax.ShapeDtypeStruct((n_tok, SSM_WIDTH), F32), tok, tok,
                  jax.ShapeDtypeStruct((D_MODEL, n_tok), BF16), tok, tok]
    vt_spec = pl.BlockSpec((D_MODEL, tm), lambda i: (0, i))
    return pl.pallas_call(
        _inproj_kernel,
        grid=(n_tok // tm,),
        in_specs=[row(D_MODEL), _full((1, D_MODEL)), _full((D_MODEL, in_width)), _full((D_MODEL, D_MODEL)),
                  _full((1, D_MODEL)), _full((1, D_MODEL)), _full((MXU_DIM, MXU_DIM))],
        out_specs=[row(SSM_WIDTH), row(D_MODEL), row(D_MODEL), vt_spec, row(D_MODEL), row(D_MODEL)],
        out_shape=out_shapes,
        compiler_params=_cparams(("parallel",)),
        name="in_proj",
    )(x2d, gain1, w_main, w_vt, qg, kg, seg)


def _ssm_kernel(u_ref, bblk_ref, a_ref, cblk_ref, d_ref, wglu_ref, bglu_ref, o_ref, bu_ref, st_ref):
    chunk = u_ref.shape[0] // SUBLANES

    @pl.when(pl.program_id(0) == 0)
    def _():
        st_ref[...] = jnp.zeros_like(st_ref)

    u = u_ref[...]
    u_bf = u.astype(BF16)
    tiles_per_part = N_STATE // MXU_DIM
    ch_per_tile = SSM_WIDTH // tiles_per_part
    for n in range(2 * tiles_per_part):
        ch0 = ((n % tiles_per_part) * ch_per_tile) // LANES * LANES
        lanes = slice(n * MXU_DIM, (n + 1) * MXU_DIM)
        bu_ref[:, lanes] = jnp.dot(u_bf[:, ch0:ch0 + LANES], bblk_ref[ch0:ch0 + LANES, lanes],
                                   preferred_element_type=F32)

    for j in range(N_STATE // SCAN_LANES):
        re = slice(j * SCAN_LANES, (j + 1) * SCAN_LANES)
        im = slice(N_STATE + j * SCAN_LANES, N_STATE + (j + 1) * SCAN_LANES)
        ar = a_ref[:, re]
        ai = a_ref[:, im]

        def step(t, carry, re=re, im=im, ar=ar, ai=ai):
            xr, xi = carry
            rows = pl.ds(pl.multiple_of(t * SUBLANES, SUBLANES), SUBLANES)
            nr = ar * xr - ai * xi + bu_ref[rows, re]
            ni = ar * xi + ai * xr + bu_ref[rows, im]
            bu_ref[rows, re] = nr
            bu_ref[rows, im] = ni
            return nr, ni

        xr, xi = lax.fori_loop(0, chunk, step, (st_ref[:, re], st_ref[:, im]), unroll=4)
        st_ref[:, re] = xr
        st_ref[:, im] = xi

    n_out = SSM_WIDTH // MXU_DIM
    lanes_per_out = N_STATE // n_out
    ys = []
    for j in range(n_out):
        cols = slice(j * MXU_DIM, (j + 1) * MXU_DIM)
        acc = None
        for part in range(2):
            lanes = slice(part * N_STATE + j * lanes_per_out, part * N_STATE + (j + 1) * lanes_per_out)
            term = jnp.dot(bu_ref[:, lanes].astype(BF16), cblk_ref[lanes, cols], preferred_element_type=F32)
            acc = term if acc is None else acc + term
        ys.append(acc)
    y = jnp.concatenate(ys, axis=1) + d_ref[...] * u
    z = jax.nn.gelu(y)
    gate = _sigmoid(jnp.dot(z.astype(BF16), wglu_ref[...], preferred_element_type=F32) + bglu_ref[...])
    o_ref[...] = (z * gate).astype(BF16)


def _ssm_params(lambda_re, lambda_im, log_dt, b_re, b_im, c_re, c_im):
    dt = jnp.exp(log_dt.astype(F32))[:, None]
    lr = jnp.minimum(lambda_re.astype(F32), -1e-4)
    li = lambda_im.astype(F32)
    mag = jnp.exp(lr * dt)
    abar_re = mag * jnp.cos(li * dt)
    abar_im = mag * jnp.sin(li * dt)
    den = lr * lr + li * li
    nr = abar_re - 1.0
    coef_re = (nr * lr + abar_im * li) / den
    coef_im = (abar_im * lr - nr * li) / den
    br = b_re.astype(F32)
    bi = b_im.astype(F32)
    bbar_re = coef_re[..., None] * br - coef_im[..., None] * bi
    bbar_im = coef_re[..., None] * bi + coef_im[..., None] * br
    eye = jnp.eye(SSM_GROUPS, dtype=F32)

    def expand_b(b):
        return jnp.einsum('gph,gk->ghkp', b, eye).reshape(SSM_WIDTH, N_STATE)

    def expand_c(c):
        return jnp.einsum('ghp,gk->gpkh', c, eye).reshape(N_STATE, SSM_WIDTH)

    bblk = jnp.concatenate([expand_b(bbar_re), expand_b(bbar_im)], axis=1).astype(BF16)
    cblk = jnp.concatenate([expand_c(c_re.astype(F32)), -expand_c(c_im.astype(F32))], axis=0).astype(BF16)
    a_row = jnp.concatenate([abar_re.reshape(-1), abar_im.reshape(-1)])[None, :]
    return bblk, cblk, jnp.broadcast_to(a_row, (SUBLANES, 2 * N_STATE))


def _ssm(u_tb, seq, bblk, cblk, a_tile, d_skip, w_glu_bf, b_glu):
    chunk = min(SSM_CHUNK, seq)
    rows = chunk * SUBLANES
    return pl.pallas_call(
        _ssm_kernel,
        grid=(seq // chunk,),
        in_specs=[pl.BlockSpec((rows, SSM_WIDTH), lambda c: (c, 0)),
                  _full(bblk.shape), _full(a_tile.shape), _full(cblk.shape),
                  _full((1, SSM_WIDTH)), _full(w_glu_bf.shape), _full((1, SSM_WIDTH))],
        out_specs=pl.BlockSpec((rows, SSM_WIDTH), lambda c: (c, 0)),
        out_shape=jax.ShapeDtypeStruct(u_tb.shape, BF16),
        scratch_shapes=[pltpu.VMEM((rows, 2 * N_STATE), F32), pltpu.VMEM((SUBLANES, 2 * N_STATE), F32)],
        compiler_params=_cparams(("arbitrary",)),
        name="ssm",
    )(u_tb, bblk, a_tile, cblk, d_skip, w_glu_bf, b_glu)


def _attn_kernel(lam_ref, sg_ref, q_ref, k_ref, vt_ref, o_ref, acc_ref, *, out_scale, blk, nh):
    seq = q_ref.shape[1]
    hw = 2 * HEAD_DIM
    lane = lax.broadcasted_iota(jnp.int32, (blk, hw), 1)
    key_i = lax.broadcasted_iota(jnp.int32, (blk, blk), 0)
    qry_i = lax.broadcasted_iota(jnp.int32, (blk, blk), 1)
    keep = key_i <= qry_i
    contract_last = (((1,), (1,)), ((), ()))
    lam = lam_ref[0]
    n_chain = 2 * nh

    def q_block(qi, _):
        qrows = pl.ds(pl.multiple_of(qi * blk, blk), blk)
        qs = []
        for hh in range(nh):
            q = q_ref[0, qrows, hh * hw:(hh + 1) * hw]
            zero = jnp.zeros_like(q)
            qs += [jnp.where(lane < HEAD_DIM, q, zero), jnp.where(lane >= HEAD_DIM, q, zero)]
        acc_ref[...] = jnp.zeros_like(acc_ref)

        def kv_block(kb, carry, masked):
            krows = pl.ds(pl.multiple_of(kb * blk, blk), blk)
            scores = []
            for c in range(n_chain):
                hh = c // 2
                k = k_ref[0, krows, hh * hw:(hh + 1) * hw]
                scores.append(lax.dot_general(k, qs[c], contract_last, preferred_element_type=F32))
            out, probs, alphas = [], [], []
            for c in range(n_chain):
                m, l = carry[2 * c], carry[2 * c + 1]
                s = jnp.where(keep, scores[c], _NEG) if masked else scores[c]
                m_new = jnp.maximum(m, jnp.max(s, axis=0, keepdims=True))
                p = jnp.exp2(s - m_new)
                alpha = jnp.exp2(m - m_new)
                out += [m_new, alpha * l + jnp.sum(p, axis=0, keepdims=True)]
                probs.append(p.astype(BF16))
                alphas.append(alpha)
            for c in range(n_chain):
                hh = c // 2
                vt = vt_ref[hh * hw:(hh + 1) * hw, krows]
                acc_ref[c] = alphas[c] * acc_ref[c] + jnp.dot(vt, probs[c], preferred_element_type=F32)
            return tuple(out)

        carry = (jnp.full((1, blk), _NEG, F32), jnp.zeros((1, blk), F32)) * n_chain
        carry = lax.fori_loop(0, qi, lambda kb, c: kv_block(kb, c, False), carry)
        carry = kv_block(qi, carry, True)
        for hh in range(nh):
            l1, l2 = carry[4 * hh + 1], carry[4 * hh + 3]
            ot = acc_ref[2 * hh] * (1.0 / l1) - acc_ref[2 * hh + 1] * (lam / l2)
            ot = ot * lax.rsqrt(jnp.mean(ot * ot, axis=0, keepdims=True) + NORM_EPS)
            o_ref[0, qrows, hh * hw:(hh + 1) * hw] = (ot.T * sg_ref[...] * out_scale).astype(BF16)
        return 0

    lax.fori_loop(0, seq // blk, q_block, 0)


def _attention(q, k, vt, lam, subln_gain, lambda_init, bsz, seq):
    blk = min(ATTN_BLOCK, seq)
    nh = ATTN_HEADS_PER_STEP
    hw = 2 * HEAD_DIM
    q3, k3 = (a.reshape(bsz, seq, D_MODEL) for a in (q, k))
    tok_spec = pl.BlockSpec((1, seq, nh * hw), lambda b, h: (b, 0, h))
    out = pl.pallas_call(
        functools.partial(_attn_kernel, out_scale=1.0 - lambda_init, blk=blk, nh=nh),
        grid=(bsz, HEADS // nh),
        in_specs=[pl.BlockSpec(memory_space=pltpu.SMEM), _full((1, hw)), tok_spec, tok_spec,
                  pl.BlockSpec((nh * hw, seq), lambda b, h: (h, b))],
        out_specs=tok_spec,
        out_shape=jax.ShapeDtypeStruct((bsz, seq, D_MODEL), BF16),
        scratch_shapes=[pltpu.VMEM((2 * nh, hw, blk), F32)],
        compiler_params=_cparams(("parallel", "parallel")),
        name="diff_attn",
    )(lam, subln_gain.astype(F32)[None, :], q3, k3, vt)
    return out.reshape(bsz * seq, D_MODEL)


def _pack_rows(y):
    bits = lax.bitcast_convert_type(y.astype(BF16).astype(F32), jnp.uint32)
    return (bits[:, :PACK_W] >> 16) | (bits[:, PACK_W:] & jnp.uint32(0xFFFF0000))


def _unpack_rows(w):
    lo = lax.bitcast_convert_type(w << 16, F32)
    hi = lax.bitcast_convert_type(w & jnp.uint32(0xFFFF0000), F32)
    return jnp.concatenate([lo, hi], axis=1)


def _merge_kernel(so_ref, ao_ref, gs_ref, ga_ref, x_ref, wps_ref, wpa_ref, wo_ref, g2_ref,
                  wrh_ref, wrl_ref, br_ref, tri_ref,
                  x1_ref, hlo_ref, hhi_ref, route_ref, gate_ref, cnt_ref, run_ref, *, region):
    @pl.when(pl.program_id(0) == 0)
    def _():
        run_ref[...] = jnp.zeros_like(run_ref)

    tm = x_ref.shape[0]
    n_part = 2 if tm % (2 * MXU_DIM) == 0 else 1
    rows_per = tm // n_part
    lane = lax.broadcasted_iota(jnp.int32, (rows_per, LANES), 1)
    lane_f = lane.astype(F32)
    parts = [dict(rows=slice(p * rows_per, (p + 1) * rows_per)) for p in range(n_part)]

    def stage_proj(st):
        def matmul():
            return (jnp.dot(so_ref[st["rows"], :], wps_ref[...], preferred_element_type=F32),
                    jnp.dot(ao_ref[st["rows"], :], wpa_ref[...], preferred_element_type=F32))

        def epilogue(r):
            ps, pa = r
            merged = gs_ref[st["rows"], :].astype(F32) * ps + ga_ref[st["rows"], :].astype(F32) * pa
            st["merged"] = merged.astype(BF16)
        return matmul, epilogue

    def stage_out(st):
        def matmul():
            return jnp.dot(st["merged"], wo_ref[...], preferred_element_type=F32)

        def epilogue(r):
            x1 = x_ref[st["rows"], :] + r
            x1_ref[st["rows"], :] = x1
            h2 = x1 * lax.rsqrt(jnp.mean(x1 * x1, axis=-1, keepdims=True) + NORM_EPS) * g2_ref[...]
            words = _pack_rows(h2)
            hlo_ref[st["rows"], :] = words[:, :PACK_HALF]
            hhi_ref[st["rows"], :] = words[:, PACK_HALF:]
            st["h_hi"] = h2.astype(BF16)
            st["h_lo"] = (h2 - st["h_hi"].astype(F32)).astype(BF16)
        return matmul, epilogue

    def stage_router(st):
        def matmul():
            return (jnp.dot(st["h_hi"], wrh_ref[...], preferred_element_type=F32)
                    + jnp.dot(st["h_lo"], wrh_ref[...], preferred_element_type=F32)
                    + jnp.dot(st["h_hi"], wrl_ref[...], preferred_element_type=F32))

        def epilogue(r):
            work = r + br_ref[...]
            onehots, vals, ids = [], [], []
            for _ in range(TOP_K):
                m = jnp.max(work, axis=-1, keepdims=True)
                idx = jnp.min(jnp.where(work == m, lane_f, float(LANES)), axis=-1, keepdims=True)
                oh = lane_f == idx
                onehots.append(oh)
                vals.append(m)
                ids.append(idx.astype(jnp.int32))
                work = jnp.where(oh, -jnp.inf, work)
            exps = [jnp.exp(v - vals[0]) for v in vals]
            den = exps[0] + exps[1] + exps[2] + exps[3]
            st.update(onehots=onehots, ids=ids, gates=[e / den for e in exps],
                      multi=(onehots[0] | onehots[1] | onehots[2] | onehots[3]).astype(F32))
        return matmul, epilogue

    _run_lookahead([stage(st) for stage in (stage_proj, stage_out, stage_router) for st in parts])

    multi = jnp.concatenate([st["multi"] for st in parts], axis=0)
    before = jnp.dot(tri_ref[...], multi.astype(BF16), preferred_element_type=F32) + run_ref[...]
    for st in parts:
        route = jnp.zeros((rows_per, LANES), jnp.int32)
        gates = jnp.zeros((rows_per, LANES), F32)
        for kk in range(TOP_K):
            rank = jnp.sum(jnp.where(st["onehots"][kk], before[st["rows"], :], 0.0), axis=-1, keepdims=True)
            route = jnp.where(lane == kk, st["ids"][kk] * region + rank.astype(jnp.int32), route)
            gates = jnp.where(lane == kk, st["gates"][kk], gates)
        route_ref[st["rows"], :] = route
        gate_ref[st["rows"], :] = gates
    run = run_ref[...] + jnp.sum(multi, axis=0, keepdims=True)
    run_ref[...] = run
    cnt_ref[...] = run.astype(jnp.int32)


def _merge_route(so, ao, gs, ga, x2d, wps, wpa, wo, gain2, w_router, b_router, region):
    n_tok = x2d.shape[0]
    tm = min(ROW_TILE, n_tok)
    wr = jnp.zeros((D_MODEL, LANES), F32).at[:, :N_EXPERTS].set(w_router.astype(F32))
    wr_hi = wr.astype(BF16)
    wr_lo = (wr - wr_hi.astype(F32)).astype(BF16)
    br = jnp.full((1, LANES), -jnp.inf, F32).at[0, :N_EXPERTS].set(b_router.astype(F32))
    tri = (jnp.arange(tm)[:, None] > jnp.arange(tm)[None, :]).astype(BF16)
    row = lambda w: pl.BlockSpec((tm, w), lambda i: (i, 0))
    out_shapes = [jax.ShapeDtypeStruct((n_tok, D_MODEL), F32),
                  jax.ShapeDtypeStruct((n_tok, PACK_HALF), jnp.uint32),
                  jax.ShapeDtypeStruct((n_tok, PACK_HALF), jnp.uint32),
                  jax.ShapeDtypeStruct((n_tok, LANES), jnp.int32),
                  jax.ShapeDtypeStruct((n_tok, LANES), F32),
                  jax.ShapeDtypeStruct((1, LANES), jnp.int32)]
    return pl.pallas_call(
        functools.partial(_merge_kernel, region=region),
        grid=(n_tok // tm,),
        in_specs=[row(SSM_WIDTH), row(D_MODEL), row(D_MODEL), row(D_MODEL), row(D_MODEL),
                  _full(wps.shape), _full(wpa.shape), _full(wo.shape), _full((1, D_MODEL)),
                  _full(wr_hi.shape), _full(wr_lo.shape), _full((1, LANES)), _full((tm, tm))],
        out_specs=[row(D_MODEL), row(PACK_HALF), row(PACK_HALF), row(LANES), row(LANES), _full((1, LANES))],
        out_shape=out_shapes,
        scratch_shapes=[pltpu.VMEM((1, LANES), F32)],
        compiler_params=_cparams(("arbitrary",)),
        name="merge_route",
    )(so, ao, gs, ga, x2d, wps, wpa, wo, gain2, wr_hi, wr_lo, br, tri)


def _sc_scatter_rows(rows, dest, n_slots):
    n_tok, width = rows.shape
    mesh = plsc.VectorSubcoreMesh(core_axis_name="core", subcore_axis_name="subcore")

    @pl.kernel(out_type=jax.ShapeDtypeStruct((n_slots, width), rows.dtype), mesh=mesh, scratch_types=[])
    def scatter(rows_hbm, dest_hbm, out_hbm):
        def body(rows_vmem, dest_vmem):
            pltpu.sync_copy(rows_vmem, out_hbm.at[dest_vmem.at[0]])

        pltpu.emit_pipeline(
            body,
            grid=(TOP_K, n_tok // SC_WINDOW),
            in_specs=[pl.BlockSpec((SC_WINDOW, width), lambda k, i: (i, 0)),
                      pl.BlockSpec((1, SC_WINDOW), lambda k, i: (k, i))],
            out_specs=[],
            core_axis_name=("core", "subcore"),
            dimension_semantics=(pltpu.PARALLEL, pltpu.PARALLEL),
        )(rows_hbm, dest_hbm)

    return scatter(rows, dest)


def _sc_gather_rows(table, idx):
    n = idx.shape[1]
    width = table.shape[1]
    mesh = plsc.VectorSubcoreMesh(core_axis_name="core", subcore_axis_name="subcore")

    @pl.kernel(out_type=jax.ShapeDtypeStruct((n, width), table.dtype), mesh=mesh, scratch_types=[])
    def gather(table_hbm, idx_hbm, out_hbm):
        def body(idx_vmem, out_vmem):
            pltpu.sync_copy(table_hbm.at[idx_vmem.at[0]], out_vmem)

        pltpu.emit_pipeline(
            body,
            grid=(n // SC_WINDOW,),
            in_specs=[pl.BlockSpec((1, SC_WINDOW), lambda i: (0, i))],
            out_specs=[pl.BlockSpec((SC_WINDOW, width), lambda i: (i, 0))],
            core_axis_name=("core", "subcore"),
            dimension_semantics=(pltpu.PARALLEL,),
        )(idx_hbm, out_hbm)

    return gather(table, idx)


def _expert_kernel(cnt_ref, xlo_hbm, xhi_hbm, w1_ref, b1_ref, w2_ref, b2_ref, perm_ref, ylo_hbm, yhi_hbm,
                   w1p_ref, w2b_ref, act_ref, xin_ref, yout_ref, in_sem, out_sem, *, region):
    e = pl.program_id(0)
    n_blk = (cnt_ref[e] + EXPERT_BLOCK - 1) // EXPERT_BLOCK
    base = e * region
    n_chunks = (2 * D_FF) // MXU_DIM
    x_hbm = (xlo_hbm, xhi_hbm)
    y_hbm = (ylo_hbm, yhi_hbm)

    def rows_of(j):
        return pl.ds(pl.multiple_of(base + j * EXPERT_BLOCK, EXPERT_BLOCK), EXPERT_BLOCK)

    def in_copy(j, slot, half):
        return pltpu.make_async_copy(x_hbm[half].at[rows_of(j), :], xin_ref.at[slot, half], in_sem.at[slot, half])

    def out_copy(j, slot, half):
        return pltpu.make_async_copy(yout_ref.at[slot, half], y_hbm[half].at[rows_of(j), :], out_sem.at[slot, half])

    @pl.when(n_blk > 0)
    def _():
        for half in range(2):
            in_copy(0, 0, half).start()

    perm = perm_ref[...]
    for c in range(n_chunks):
        cols = slice(c * MXU_DIM, (c + 1) * MXU_DIM)
        w1p_ref[:, cols] = jnp.dot(w1_ref[0, :, cols].astype(BF16), perm,
                                   preferred_element_type=F32).astype(BF16)
    w2b_ref[...] = w2_ref[0].astype(BF16)

    def block(j, _):
        slot = j % 2
        for half in range(2):
            in_copy(j, slot, half).wait()

        @pl.when(j + 1 < n_blk)
        def _():
            for half in range(2):
                in_copy(j + 1, 1 - slot, half).start()

        @pl.when(j >= 2)
        def _():
            for half in range(2):
                out_copy(j - 2, slot, half).wait()

        words = jnp.concatenate([xin_ref[slot, 0], xin_ref[slot, 1]], axis=1)
        x = _unpack_rows(words).astype(BF16)

        def up_task(c):
            cols = slice(c * MXU_DIM, (c + 1) * MXU_DIM)

            def epilogue(r):
                h = r + b1_ref[0, :, cols]
                gate = jnp.minimum(h[:, :LANES], SWIGLU_LIMIT)
                up = jnp.clip(h[:, LANES:], -SWIGLU_LIMIT, SWIGLU_LIMIT)
                glu = gate * _sigmoid(SWIGLU_ALPHA * gate)
                act_ref[:, c * LANES:(c + 1) * LANES] = ((up + 1.0) * glu).astype(BF16)
            return (lambda: jnp.dot(x, w1p_ref[:, cols], preferred_element_type=F32)), epilogue

        _run_lookahead([up_task(c) for c in range(n_chunks)])
        y = jnp.dot(act_ref[...], w2b_ref[...], preferred_element_type=F32) + b2_ref[0]
        words = _pack_rows(y)
        yout_ref[slot, 0] = words[:, :PACK_HALF]
        yout_ref[slot, 1] = words[:, PACK_HALF:]
        for half in range(2):
            out_copy(j, slot, half).start()
        return 0

    lax.fori_loop(0, n_blk, block, 0)

    for back in (2, 1):
        @pl.when(n_blk >= back)
        def _(back=back):
            j = n_blk - back
            for half in range(2):
                out_copy(j, j % 2, half).wait()


def _gate_up_order():
    j = jnp.arange(MXU_DIM)
    within = jnp.where(j < LANES, 2 * j, 2 * (j - LANES) + 1)
    return within


def _experts(xs_lo, xs_hi, w1, b1, w2, b2, counts, region):
    within = _gate_up_order()
    perm = (jnp.arange(MXU_DIM)[:, None] == within[None, :]).astype(BF16)
    order = (jnp.arange(0, 2 * D_FF, MXU_DIM)[:, None] + within[None, :]).reshape(-1)
    b1p = b1.astype(F32)[:, order][:, None, :]
    b2r = b2.astype(F32)[:, None, :]
    tb = EXPERT_BLOCK
    hbm = pl.BlockSpec(memory_space=pl.ANY)
    wspec = lambda shape: pl.BlockSpec((1,) + shape, lambda e, cnt: (e, 0, 0))
    grid_spec = pltpu.PrefetchScalarGridSpec(
        num_scalar_prefetch=1,
        grid=(N_EXPERTS,),
        in_specs=[hbm, hbm, wspec((D_MODEL, 2 * D_FF)), wspec((1, 2 * D_FF)),
                  wspec((D_FF, D_MODEL)), wspec((1, D_MODEL)),
                  pl.BlockSpec((MXU_DIM, MXU_DIM), lambda e, cnt: (0, 0))],
        out_specs=[hbm, hbm],
        scratch_shapes=[pltpu.VMEM((D_MODEL, 2 * D_FF), BF16), pltpu.VMEM((D_FF, D_MODEL), BF16),
                        pltpu.VMEM((tb, D_FF), BF16),
                        pltpu.VMEM((2, 2, tb, PACK_HALF), jnp.uint32), pltpu.VMEM((2, 2, tb, PACK_HALF), jnp.uint32),
                        pltpu.SemaphoreType.DMA((2, 2)), pltpu.SemaphoreType.DMA((2, 2))],
    )
    return pl.pallas_call(
        functools.partial(_expert_kernel, region=region),
        grid_spec=grid_spec,
        out_shape=[jax.ShapeDtypeStruct(xs_lo.shape, jnp.uint32)] * 2,
        compiler_params=_cparams(("arbitrary",)),
        name="experts",
    )(counts, xs_lo, xs_hi, w1, b1p, w2, b2r, perm)


def _combine_kernel(x1_ref, gate_ref, *refs):
    lo_refs, hi_refs, o_ref = refs[:TOP_K], refs[TOP_K:2 * TOP_K], refs[2 * TOP_K]
    acc = x1_ref[...]
    gates = gate_ref[...]
    for kk in range(TOP_K):
        words = jnp.concatenate([lo_refs[kk][...], hi_refs[kk][...]], axis=1)
        acc = acc + gates[:, kk:kk + 1] * _unpack_rows(words)
    o_ref[...] = acc


def _combine(x1, gates, yg_lo, yg_hi):
    n_tok = x1.shape[0]
    tm = min(ROW_TILE, n_tok)
    nblk = n_tok // tm
    row = lambda w: pl.BlockSpec((tm, w), lambda i: (i, 0))
    plane = lambda kk: pl.BlockSpec((tm, PACK_HALF), lambda i, kk=kk: (kk * nblk + i, 0))
    planes = [plane(kk) for kk in range(TOP_K)]
    return pl.pallas_call(
        _combine_kernel,
        grid=(nblk,),
        in_specs=[row(D_MODEL), row(LANES)] + planes + planes,
        out_specs=row(D_MODEL),
        out_shape=jax.ShapeDtypeStruct((n_tok, D_MODEL), F32),
        compiler_params=_cparams(("parallel",)),
        name="combine",
    )(x1, gates, *([yg_lo] * TOP_K), *([yg_hi] * TOP_K))


def kernel(x, norm1_gain, w_in, lambda_re, lambda_im, log_dt, ssm_b_re, ssm_b_im, ssm_c_re, ssm_c_im, ssm_d, w_glu, b_glu, q_norm_gain, k_norm_gain, lambda_q1, lambda_k1, lambda_q2, lambda_k2, subln_gain, w_proj_ssm, w_proj_attn, w_out, norm2_gain, w_router, b_router, w_exp1, b_exp1, w_exp2, b_exp2):
    bsz, seq, d = x.shape
    n_tok = bsz * seq
    depth = norm1_gain.shape[0]
    row1 = lambda a: a.astype(F32).reshape(1, -1)
    for l in range(depth):
        lambda_init = 0.8 - 0.6 * math.exp(-0.3 * l)
        x2d = x.reshape(n_tok, d)

        u, q, k, vt, gs, ga = _in_proj(x2d, row1(norm1_gain[l]), w_in[l], q_norm_gain[l], k_norm_gain[l])

        bblk, cblk, a_tile = _ssm_params(lambda_re[l], lambda_im[l], log_dt[l], ssm_b_re[l], ssm_b_im[l],
                                         ssm_c_re[l], ssm_c_im[l])
        u_tb = u.reshape(bsz, seq, SSM_WIDTH).transpose(1, 0, 2).reshape(n_tok, SSM_WIDTH)
        so_tb = _ssm(u_tb, seq, bblk, cblk, a_tile, row1(ssm_d[l]), w_glu[l].astype(BF16), row1(b_glu[l]))
        so = so_tb.reshape(seq, bsz, SSM_WIDTH).transpose(1, 0, 2).reshape(n_tok, SSM_WIDTH)

        lam = (jnp.exp(jnp.sum(lambda_q1[l].astype(F32) * lambda_k1[l].astype(F32)))
               - jnp.exp(jnp.sum(lambda_q2[l].astype(F32) * lambda_k2[l].astype(F32)))
               + lambda_init).reshape(1)
        ao = _attention(q, k, vt, lam, subln_gain[l], lambda_init, bsz, seq)

        region = n_tok
        x1, h_lo, h_hi, route, gates, counts = _merge_route(
            so, ao, gs, ga, x2d, w_proj_ssm[l].astype(BF16), w_proj_attn[l].astype(BF16),
            w_out[l].astype(BF16), row1(norm2_gain[l]), w_router[l], b_router[l], region)

        dest = route[:, :TOP_K].T
        n_slots = N_EXPERTS * region
        xs_lo = _sc_scatter_rows(h_lo, dest, n_slots)
        xs_hi = _sc_scatter_rows(h_hi, dest, n_slots)

        ys_lo, ys_hi = _experts(xs_lo, xs_hi, w_exp1[l], b_exp1[l], w_exp2[l], b_exp2[l],
                                counts[0, :N_EXPERTS], region)

        flat = dest.reshape(1, TOP_K * n_tok)
        yg_lo = _sc_gather_rows(ys_lo, flat)
        yg_hi = _sc_gather_rows(ys_hi, flat)
        x = _combine(x1, gates, yg_lo, yg_hi).reshape(bsz, seq, d)
    return x
```

```python
import functools
import math

import jax
import jax.numpy as jnp
from jax import lax
from jax.experimental import pallas as pl
from jax.experimental.pallas import tpu as pltpu
from jax.experimental.pallas import tpu_sc as plsc

F32 = jnp.float32
BF16 = jnp.bfloat16

D_MODEL = 1024
NORM_EPS = 1e-5
SSM_WIDTH = 512
SSM_GROUP = 16
SSM_GROUPS = 32
SSM_STATE = 64
N_STATE = SSM_GROUPS * SSM_STATE
HEADS = 8
HEAD_DIM = 64
N_EXPERTS = 32
TOP_K = 4
D_FF = 1024
SWIGLU_ALPHA = 1.702
SWIGLU_LIMIT = 7.0

LANES = 128
SUBLANES = 8
MXU_DIM = 256
VMEM_LIMIT = 56 * 1024 * 1024

ROW_TILE = 512
SSM_CHUNK = 128
SCAN_LANES = 512
ATTN_BLOCK = 256
ATTN_HEADS_PER_STEP = 8
EXPERT_BLOCK = 512
ROW_DMA_PRIORITY = 1
SC_WINDOW = 128
PACK_W = D_MODEL // 2
PACK_HALF = PACK_W // 2

_NEG = -1e30
Q_SCALE = math.log2(math.e) / math.sqrt(HEAD_DIM)


def _cparams(sem):
    return pltpu.CompilerParams(dimension_semantics=sem, vmem_limit_bytes=VMEM_LIMIT)


def _full(shape):
    nd = len(shape)
    return pl.BlockSpec(shape, lambda *_: (0,) * nd)


def _sigmoid(x):
    return 0.5 * jnp.tanh(0.5 * x) + 0.5


def _run_lookahead(tasks):
    pending = tasks[0][0]()
    for i, (_, epilogue) in enumerate(tasks):
        result = pending
        if i + 1 < len(tasks):
            pending = tasks[i + 1][0]()
        epilogue(result)


def _inproj_kernel(x_ref, g1_ref, w_ref, wvt_ref, qg_ref, kg_ref, seg_ref,
                   u_ref, q_ref, k_ref, vt_ref, gs_ref, ga_ref):
    x = x_ref[...]
    ms = jnp.mean(x * x, axis=-1, keepdims=True)
    h = (x * lax.rsqrt(ms + NORM_EPS) * g1_ref[...]).astype(BF16)

    def proj(c0, width):
        return lambda: jnp.dot(h, w_ref[:, c0:c0 + width], preferred_element_type=F32)

    seg = seg_ref[...]
    q0 = SSM_WIDTH
    k0 = q0 + D_MODEL
    gs0 = k0 + D_MODEL
    ga0 = gs0 + D_MODEL
    half = D_MODEL // 2
    tasks = []

    def store(out_ref, cols, fn):
        def epilogue(r):
            out_ref[:, cols] = fn(r)
        return epilogue

    tasks.append((proj(0, SSM_WIDTH), store(u_ref, slice(None), lambda r: r)))

    def head_norm_tasks(base, gain_ref, out_ref, scale, c):
        cols = slice(c * MXU_DIM, (c + 1) * MXU_DIM)
        kept = {}

        def after_proj(y):
            kept["y"] = y
            kept["sq"] = (y * y).astype(BF16)

        def after_sum(ss):
            yn = kept["y"] * lax.rsqrt(ss * (1.0 / HEAD_DIM) + NORM_EPS) * gain_ref[:, cols]
            out_ref[:, cols] = (yn * scale).astype(BF16)

        return ((proj(base + c * MXU_DIM, MXU_DIM), after_proj),
                (lambda: jnp.dot(kept["sq"], seg, preferred_element_type=F32), after_sum))

    pairs = [head_norm_tasks(base, gain_ref, out_ref, scale, c)
             for base, gain_ref, out_ref, scale in ((q0, qg_ref, q_ref, Q_SCALE), (k0, kg_ref, k_ref, 1.0))
             for c in range(D_MODEL // MXU_DIM)]
    tasks.append(pairs[0][0])
    for prev, cur in zip(pairs, pairs[1:]):
        tasks += [cur[0], prev[1]]
    tasks.append(pairs[-1][1])

    to_gate = lambda r: _sigmoid(r).astype(BF16)
    for c in range(2):
        cols = slice(c * half, (c + 1) * half)
        tasks.append((proj(gs0 + c * half, half), store(gs_ref, cols, to_gate)))
        tasks.append((proj(ga0 + c * half, half), store(ga_ref, cols, to_gate)))

    def vt_task(r0):
        def matmul():
            return lax.dot_general(wvt_ref[r0:r0 + MXU_DIM, :], h, (((1,), (1,)), ((), ())),
                                   preferred_element_type=F32)

        def epilogue(r):
            vt_ref[r0:r0 + MXU_DIM, :] = r.astype(BF16)
        return matmul, epilogue

    tasks += [vt_task(r0) for r0 in range(0, D_MODEL, MXU_DIM)]
    _run_lookahead(tasks)


def _in_proj(x2d, gain1, w_in, q_gain, k_gain):
    n_tok = x2d.shape[0]
    tm = min(ROW_TILE, n_tok)
    v0 = SSM_WIDTH + 2 * D_MODEL
    w_bf = w_in.astype(BF16)
    w_main = jnp.concatenate([w_bf[:, :v0], w_bf[:, v0 + D_MODEL:]], axis=1)
    w_vt = w_bf[:, v0:v0 + D_MODEL].T
    in_width = w_main.shape[1]
    seg = (jnp.arange(MXU_DIM)[:, None] // HEAD_DIM == jnp.arange(MXU_DIM)[None, :] // HEAD_DIM).astype(BF16)
    reps = D_MODEL // HEAD_DIM
    qg = jnp.tile(q_gain.astype(F32), reps)[None, :]
    kg = jnp.tile(k_gain.astype(F32), reps)[None, :]
    row = lambda w: pl.BlockSpec((tm, w), lambda i: (i, 0))
    tok = jax.ShapeDtypeStruct((n_tok, D_MODEL), BF16)
    out_shapes = [jax.ShapeDtypeStruct((n_tok, SSM_WIDTH), F32), tok, tok,
                  jax.ShapeDtypeStruct((D_MODEL, n_tok), BF16), tok, tok]
    vt_spec = pl.BlockSpec((D_MODEL, tm), lambda i: (0, i))
    return pl.pallas_call(
        _inproj_kernel,
        grid=(n_tok // tm,),
        in_specs=[row(D_MODEL), _full((1, D_MODEL)), _full((D_MODEL, in_width)), _full((D_MODEL, D_MODEL)),
                  _full((1, D_MODEL)), _full((1, D_MODEL)), _full((MXU_DIM, MXU_DIM))],
        out_specs=[row(SSM_WIDTH), row(D_MODEL), row(D_MODEL), vt_spec, row(D_MODEL), row(D_MODEL)],
        out_shape=out_shapes,
        compiler_params=_cparams(("parallel",)),
        name="in_proj",
    )(x2d, gain1, w_main, w_vt, qg, kg, seg)


def _ssm_kernel(u_ref, bblk_ref, a_ref, cblk_ref, d_ref, wglu_ref, bglu_ref, o_ref, bu_ref, st_ref):
    chunk = u_ref.shape[0] // SUBLANES

    @pl.when(pl.program_id(0) == 0)
    def _():
        st_ref[...] = jnp.zeros_like(st_ref)

    u = u_ref[...]
    u_bf = u.astype(BF16)
    tiles_per_part = N_STATE // MXU_DIM
    ch_per_tile = SSM_WIDTH // tiles_per_part
    for n in range(2 * tiles_per_part):
        ch0 = ((n % tiles_per_part) * ch_per_tile) // LANES * LANES
        lanes = slice(n * MXU_DIM, (n + 1) * MXU_DIM)
        bu_ref[:, lanes] = jnp.dot(u_bf[:, ch0:ch0 + LANES], bblk_ref[ch0:ch0 + LANES, lanes],
                                   preferred_element_type=F32)

    for j in range(N_STATE // SCAN_LANES):
        re = slice(j * SCAN_LANES, (j + 1) * SCAN_LANES)
        im = slice(N_STATE + j * SCAN_LANES, N_STATE + (j + 1) * SCAN_LANES)
        ar = a_ref[:, re]
        ai = a_ref[:, im]

        def step(t, carry, re=re, im=im, ar=ar, ai=ai):
            xr, xi = carry
            rows = pl.ds(pl.multiple_of(t * SUBLANES, SUBLANES), SUBLANES)
            nr = ar * xr - ai * xi + bu_ref[rows, re]
            ni = ar * xi + ai * xr + bu_ref[rows, im]
            bu_ref[rows, re] = nr
            bu_ref[rows, im] = ni
            return nr, ni

        xr, xi = lax.fori_loop(0, chunk, step, (st_ref[:, re], st_ref[:, im]), unroll=4)
        st_ref[:, re] = xr
        st_ref[:, im] = xi

    n_out = SSM_WIDTH // MXU_DIM
    lanes_per_out = N_STATE // n_out
    ys = []
    for j in range(n_out):
        cols = slice(j * MXU_DIM, (j + 1) * MXU_DIM)
        acc = None
        for part in range(2):
            lanes = slice(part * N_STATE + j * lanes_per_out, part * N_STATE + (j + 1) * lanes_per_out)
            term = jnp.dot(bu_ref[:, lanes].astype(BF16), cblk_ref[lanes, cols], preferred_element_type=F32)
            acc = term if acc is None else acc + term
        ys.append(acc)
    y = jnp.concatenate(ys, axis=1) + d_ref[...] * u
    z = jax.nn.gelu(y)
    gate = _sigmoid(jnp.dot(z.astype(BF16), wglu_ref[...], preferred_element_type=F32) + bglu_ref[...])
    o_ref[...] = (z * gate).astype(BF16)


def _ssm_params(lambda_re, lambda_im, log_dt, b_re, b_im, c_re, c_im):
    dt = jnp.exp(log_dt.astype(F32))[:, None]
    lr = jnp.minimum(lambda_re.astype(F32), -1e-4)
    li = lambda_im.astype(F32)
    mag = jnp.exp(lr * dt)
    abar_re = mag * jnp.cos(li * dt)
    abar_im = mag * jnp.sin(li * dt)
    den = lr * lr + li * li
    nr = abar_re - 1.0
    coef_re = (nr * lr + abar_im * li) / den
    coef_im = (abar_im * lr - nr * li) / den
    br = b_re.astype(F32)
    bi = b_im.astype(F32)
    bbar_re = coef_re[..., None] * br - coef_im[..., None] * bi
    bbar_im = coef_re[..., None] * bi + coef_im[..., None] * br
    eye = jnp.eye(SSM_GROUPS, dtype=F32)

    def expand_b(b):
        return jnp.einsum('gph,gk->ghkp', b, eye).reshape(SSM_WIDTH, N_STATE)

    def expand_c(c):
        return jnp.einsum('ghp,gk->gpkh', c, eye).reshape(N_STATE, SSM_WIDTH)

    bblk = jnp.concatenate([expand_b(bbar_re), expand_b(bbar_im)], axis=1).astype(BF16)
    cblk = jnp.concatenate([expand_c(c_re.astype(F32)), -expand_c(c_im.astype(F32))], axis=0).astype(BF16)
    a_row = jnp.concatenate([abar_re.reshape(-1), abar_im.reshape(-1)])[None, :]
    return bblk, cblk, jnp.broadcast_to(a_row, (SUBLANES, 2 * N_STATE))


def _ssm(u_tb, seq, bblk, cblk, a_tile, d_skip, w_glu_bf, b_glu):
    chunk = min(SSM_CHUNK, seq)
    rows = chunk * SUBLANES
    return pl.pallas_call(
        _ssm_kernel,
        grid=(seq // chunk,),
        in_specs=[pl.BlockSpec((rows, SSM_WIDTH), lambda c: (c, 0)),
                  _full(bblk.shape), _full(a_tile.shape), _full(cblk.shape),
                  _full((1, SSM_WIDTH)), _full(w_glu_bf.shape), _full((1, SSM_WIDTH))],
        out_specs=pl.BlockSpec((rows, SSM_WIDTH), lambda c: (c, 0)),
        out_shape=jax.ShapeDtypeStruct(u_tb.shape, BF16),
        scratch_shapes=[pltpu.VMEM((rows, 2 * N_STATE), F32), pltpu.VMEM((SUBLANES, 2 * N_STATE), F32)],
        compiler_params=_cparams(("arbitrary",)),
        name="ssm",
    )(u_tb, bblk, a_tile, cblk, d_skip, w_glu_bf, b_glu)


def _attn_kernel(lam_ref, sg_ref, q_ref, k_ref, vt_ref, o_ref, acc_ref, *, out_scale, blk, nh):
    seq = q_ref.shape[1]
    hw = 2 * HEAD_DIM
    lane = lax.broadcasted_iota(jnp.int32, (blk, hw), 1)
    key_i = lax.broadcasted_iota(jnp.int32, (blk, blk), 0)
    qry_i = lax.broadcasted_iota(jnp.int32, (blk, blk), 1)
    keep = key_i <= qry_i
    contract_last = (((1,), (1,)), ((), ()))
    lam = lam_ref[0]
    n_chain = 2 * nh

    def q_block(qi, _):
        qrows = pl.ds(pl.multiple_of(qi * blk, blk), blk)
        qs = []
        for hh in range(nh):
            q = q_ref[0, qrows, hh * hw:(hh + 1) * hw]
            zero = jnp.zeros_like(q)
            qs += [jnp.where(lane < HEAD_DIM, q, zero), jnp.where(lane >= HEAD_DIM, q, zero)]
        acc_ref[...] = jnp.zeros_like(acc_ref)

        def kv_block(kb, carry, masked):
            krows = pl.ds(pl.multiple_of(kb * blk, blk), blk)
            scores = []
            for c in range(n_chain):
                hh = c // 2
                k = k_ref[0, krows, hh * hw:(hh + 1) * hw]
                scores.append(lax.dot_general(k, qs[c], contract_last, preferred_element_type=F32))
            out, probs, alphas = [], [], []
            for c in range(n_chain):
                m, l = carry[2 * c], carry[2 * c + 1]
                s = jnp.where(keep, scores[c], _NEG) if masked else scores[c]
                m_new = jnp.maximum(m, jnp.max(s, axis=0, keepdims=True))
                p = jnp.exp2(s - m_new)
                alpha = jnp.exp2(m - m_new)
                out += [m_new, alpha * l + jnp.sum(p, axis=0, keepdims=True)]
                probs.append(p.astype(BF16))
                alphas.append(alpha)
            for c in range(n_chain):
                hh = c // 2
                vt = vt_ref[hh * hw:(hh + 1) * hw, krows]
                acc_ref[c] = alphas[c] * acc_ref[c] + jnp.dot(vt, probs[c], preferred_element_type=F32)
            return tuple(out)

        carry = (jnp.full((1, blk), _NEG, F32), jnp.zeros((1, blk), F32)) * n_chain
        carry = lax.fori_loop(0, qi, lambda kb, c: kv_block(kb, c, False), carry)
        carry = kv_block(qi, carry, True)
        for hh in range(nh):
            l1, l2 = carry[4 * hh + 1], carry[4 * hh + 3]
            ot = acc_ref[2 * hh] * (1.0 / l1) - acc_ref[2 * hh + 1] * (lam / l2)
            ot = ot * lax.rsqrt(jnp.mean(ot * ot, axis=0, keepdims=True) + NORM_EPS)
            o_ref[0, qrows, hh * hw:(hh + 1) * hw] = (ot.T * sg_ref[...] * out_scale).astype(BF16)
        return 0

    lax.fori_loop(0, seq // blk, q_block, 0)


def _attention(q, k, vt, lam, subln_gain, lambda_init, bsz, seq):
    blk = min(ATTN_BLOCK, seq)
    nh = ATTN_HEADS_PER_STEP
    hw = 2 * HEAD_DIM
    q3, k3 = (a.reshape(bsz, seq, D_MODEL) for a in (q, k))
    tok_spec = pl.BlockSpec((1, seq, nh * hw), lambda b, h: (b, 0, h))
    out = pl.pallas_call(
        functools.partial(_attn_kernel, out_scale=1.0 - lambda_init, blk=blk, nh=nh),
        grid=(bsz, HEADS // nh),
        in_specs=[pl.BlockSpec(memory_space=pltpu.SMEM), _full((1, hw)), tok_spec, tok_spec,
                  pl.BlockSpec((nh * hw, seq), lambda b, h: (h, b))],
        out_specs=tok_spec,
        out_shape=jax.ShapeDtypeStruct((bsz, seq, D_MODEL), BF16),
        scratch_shapes=[pltpu.VMEM((2 * nh, hw, blk), F32)],
        compiler_params=_cparams(("parallel", "parallel")),
        name="diff_attn",
    )(lam, subln_gain.astype(F32)[None, :], q3, k3, vt)
    return out.reshape(bsz * seq, D_MODEL)


def _pack_rows(y):
    bits = lax.bitcast_convert_type(y.astype(BF16).astype(F32), jnp.uint32)
    return (bits[:, :PACK_W] >> 16) | (bits[:, PACK_W:] & jnp.uint32(0xFFFF0000))


def _unpack_rows(w):
    lo = lax.bitcast_convert_type(w << 16, F32)
    hi = lax.bitcast_convert_type(w & jnp.uint32(0xFFFF0000), F32)
    return jnp.concatenate([lo, hi], axis=1)


def _merge_kernel(so_ref, ao_ref, gs_ref, ga_ref, x_ref, wps_ref, wpa_ref, wo_ref, g2_ref,
                  wrh_ref, wrl_ref, br_ref, tri_ref,
                  x1_ref, hlo_ref, hhi_ref, route_ref, gate_ref, cnt_ref, run_ref, *, region):
    @pl.when(pl.program_id(0) == 0)
    def _():
        run_ref[...] = jnp.zeros_like(run_ref)

    tm = x_ref.shape[0]
    n_part = 2 if tm % (2 * MXU_DIM) == 0 else 1
    rows_per = tm // n_part
    lane = lax.broadcasted_iota(jnp.int32, (rows_per, LANES), 1)
    lane_f = lane.astype(F32)
    parts = [dict(rows=slice(p * rows_per, (p + 1) * rows_per)) for p in range(n_part)]

    def stage_proj(st):
        def matmul():
            return (jnp.dot(so_ref[st["rows"], :], wps_ref[...], preferred_element_type=F32),
                    jnp.dot(ao_ref[st["rows"], :], wpa_ref[...], preferred_element_type=F32))

        def epilogue(r):
            ps, pa = r
            merged = gs_ref[st["rows"], :].astype(F32) * ps + ga_ref[st["rows"], :].astype(F32) * pa
            st["merged"] = merged.astype(BF16)
        return matmul, epilogue

    def stage_out(st):
        def matmul():
            return jnp.dot(st["merged"], wo_ref[...], preferred_element_type=F32)

        def epilogue(r):
            x1 = x_ref[st["rows"], :] + r
            x1_ref[st["rows"], :] = x1
            h2 = x1 * lax.rsqrt(jnp.mean(x1 * x1, axis=-1, keepdims=True) + NORM_EPS) * g2_ref[...]
            words = _pack_rows(h2)
            hlo_ref[st["rows"], :] = words[:, :PACK_HALF]
            hhi_ref[st["rows"], :] = words[:, PACK_HALF:]
            st["h_hi"] = h2.astype(BF16)
            st["h_lo"] = (h2 - st["h_hi"].astype(F32)).astype(BF16)
        return matmul, epilogue

    def stage_router(st):
        def matmul():
            return (jnp.dot(st["h_hi"], wrh_ref[...], preferred_element_type=F32)
                    + jnp.dot(st["h_lo"], wrh_ref[...], preferred_element_type=F32)
                    + jnp.dot(st["h_hi"], wrl_ref[...], preferred_element_type=F32))

        def epilogue(r):
            work = r + br_ref[...]
            onehots, vals, ids = [], [], []
            for _ in range(TOP_K):
                m = jnp.max(work, axis=-1, keepdims=True)
                idx = jnp.min(jnp.where(work == m, lane_f, float(LANES)), axis=-1, keepdims=True)
                oh = lane_f == idx
                onehots.append(oh)
                vals.append(m)
                ids.append(idx.astype(jnp.int32))
                work = jnp.where(oh, -jnp.inf, work)
            exps = [jnp.exp(v - vals[0]) for v in vals]
            den = exps[0] + exps[1] + exps[2] + exps[3]
            st.update(onehots=onehots, ids=ids, gates=[e / den for e in exps],
                      multi=(onehots[0] | onehots[1] | onehots[2] | onehots[3]).astype(F32))
        return matmul, epilogue

    _run_lookahead([stage(st) for stage in (stage_proj, stage_out, stage_router) for st in parts])

    multi = jnp.concatenate([st["multi"] for st in parts], axis=0)
    before = jnp.dot(tri_ref[...], multi.astype(BF16), preferred_element_type=F32) + run_ref[...]
    for st in parts:
        route = jnp.zeros((rows_per, LANES), jnp.int32)
        gates = jnp.zeros((rows_per, LANES), F32)
        for kk in range(TOP_K):
            rank = jnp.sum(jnp.where(st["onehots"][kk], before[st["rows"], :], 0.0), axis=-1, keepdims=True)
            route = jnp.where(lane == kk, st["ids"][kk] * region + rank.astype(jnp.int32), route)
            gates = jnp.where(lane == kk, st["gates"][kk], gates)
        route_ref[st["rows"], :] = route
        gate_ref[st["rows"], :] = gates
    run = run_ref[...] + jnp.sum(multi, axis=0, keepdims=True)
    run_ref[...] = run
    cnt_ref[...] = run.astype(jnp.int32)


def _merge_route(so, ao, gs, ga, x2d, wps, wpa, wo, gain2, w_router, b_router, region):
    n_tok = x2d.shape[0]
    tm = min(ROW_TILE, n_tok)
    wr = jnp.zeros((D_MODEL, LANES), F32).at[:, :N_EXPERTS].set(w_router.astype(F32))
    wr_hi = wr.astype(BF16)
    wr_lo = (wr - wr_hi.astype(F32)).astype(BF16)
    br = jnp.full((1, LANES), -jnp.inf, F32).at[0, :N_EXPERTS].set(b_router.astype(F32))
    tri = (jnp.arange(tm)[:, None] > jnp.arange(tm)[None, :]).astype(BF16)
    row = lambda w: pl.BlockSpec((tm, w), lambda i: (i, 0))
    out_shapes = [jax.ShapeDtypeStruct((n_tok, D_MODEL), F32),
                  jax.ShapeDtypeStruct((n_tok, PACK_HALF), jnp.uint32),
                  jax.ShapeDtypeStruct((n_tok, PACK_HALF), jnp.uint32),
                  jax.ShapeDtypeStruct((n_tok, LANES), jnp.int32),
                  jax.ShapeDtypeStruct((n_tok, LANES), F32),
                  jax.ShapeDtypeStruct((1, LANES), jnp.int32)]
    return pl.pallas_call(
        functools.partial(_merge_kernel, region=region),
        grid=(n_tok // tm,),
        in_specs=[row(SSM_WIDTH), row(D_MODEL), row(D_MODEL), row(D_MODEL), row(D_MODEL),
                  _full(wps.shape), _full(wpa.shape), _full(wo.shape), _full((1, D_MODEL)),
                  _full(wr_hi.shape), _full(wr_lo.shape), _full((1, LANES)), _full((tm, tm))],
        out_specs=[row(D_MODEL), row(PACK_HALF), row(PACK_HALF), row(LANES), row(LANES), _full((1, LANES))],
        out_shape=out_shapes,
        scratch_shapes=[pltpu.VMEM((1, LANES), F32)],
        compiler_params=_cparams(("arbitrary",)),
        name="merge_route",
    )(so, ao, gs, ga, x2d, wps, wpa, wo, gain2, wr_hi, wr_lo, br, tri)


def _sc_scatter_rows(rows, dest, n_slots):
    n_tok, width = rows.shape
    mesh = plsc.VectorSubcoreMesh(core_axis_name="core", subcore_axis_name="subcore")

    @pl.kernel(out_type=jax.ShapeDtypeStruct((n_slots, width), rows.dtype), mesh=mesh, scratch_types=[])
    def scatter(rows_hbm, dest_hbm, out_hbm):
        def body(rows_vmem, dest_vmem):
            pltpu.sync_copy(rows_vmem, out_hbm.at[dest_vmem.at[0]])

        pltpu.emit_pipeline(
            body,
            grid=(TOP_K, n_tok // SC_WINDOW),
            in_specs=[pl.BlockSpec((SC_WINDOW, width), lambda k, i: (i, 0)),
                      pl.BlockSpec((1, SC_WINDOW), lambda k, i: (k, i))],
            out_specs=[],
            core_axis_name=("core", "subcore"),
            dimension_semantics=(pltpu.PARALLEL, pltpu.PARALLEL),
        )(rows_hbm, dest_hbm)

    return scatter(rows, dest)


def _sc_gather_rows(table, idx):
    n = idx.shape[1]
    width = table.shape[1]
    mesh = plsc.VectorSubcoreMesh(core_axis_name="core", subcore_axis_name="subcore")

    @pl.kernel(out_type=jax.ShapeDtypeStruct((n, width), table.dtype), mesh=mesh, scratch_types=[])
    def gather(table_hbm, idx_hbm, out_hbm):
        def body(idx_vmem, out_vmem):
            pltpu.sync_copy(table_hbm.at[idx_vmem.at[0]], out_vmem)

        pltpu.emit_pipeline(
            body,
            grid=(n // SC_WINDOW,),
            in_specs=[pl.BlockSpec((1, SC_WINDOW), lambda i: (0, i))],
            out_specs=[pl.BlockSpec((SC_WINDOW, width), lambda i: (i, 0))],
            core_axis_name=("core", "subcore"),
            dimension_semantics=(pltpu.PARALLEL,),
        )(idx_hbm, out_hbm)

    return gather(table, idx)


def _expert_kernel(cnt_ref, xlo_hbm, xhi_hbm, w1_ref, b1_ref, w2_ref, b2_ref, perm_ref, ylo_hbm, yhi_hbm,
                   w1p_ref, w2b_ref, act_ref, xin_ref, yout_ref, in_sem, out_sem, *, region):
    e = pl.program_id(0)
    n_blk = (cnt_ref[e] + EXPERT_BLOCK - 1) // EXPERT_BLOCK
    base = e * region
    n_chunks = (2 * D_FF) // MXU_DIM
    x_hbm = (xlo_hbm, xhi_hbm)
    y_hbm = (ylo_hbm, yhi_hbm)

    def rows_of(j):
        return pl.ds(pl.multiple_of(base + j * EXPERT_BLOCK, EXPERT_BLOCK), EXPERT_BLOCK)

    def in_copy(j, slot, half):
        return pltpu.make_async_copy(x_hbm[half].at[rows_of(j), :], xin_ref.at[slot, half], in_sem.at[slot, half])

    def out_copy(j, slot, half):
        return pltpu.make_async_copy(yout_ref.at[slot, half], y_hbm[half].at[rows_of(j), :], out_sem.at[slot, half])

    @pl.when(n_blk > 0)
    def _():
        for half in range(2):
            in_copy(0, 0, half).start(priority=ROW_DMA_PRIORITY)

    perm = perm_ref[...]
    for c in range(n_chunks):
        cols = slice(c * MXU_DIM, (c + 1) * MXU_DIM)
        w1p_ref[:, cols] = jnp.dot(w1_ref[0, :, cols].astype(BF16), perm,
                                   preferred_element_type=F32).astype(BF16)
    w2b_ref[...] = w2_ref[0].astype(BF16)

    def block(j, _):
        slot = j % 2
        for half in range(2):
            in_copy(j, slot, half).wait()

        @pl.when(j + 1 < n_blk)
        def _():
            for half in range(2):
                in_copy(j + 1, 1 - slot, half).start(priority=ROW_DMA_PRIORITY)

        @pl.when(j >= 2)
        def _():
            for half in range(2):
                out_copy(j - 2, slot, half).wait()

        words = jnp.concatenate([xin_ref[slot, 0], xin_ref[slot, 1]], axis=1)
        x = _unpack_rows(words).astype(BF16)

        def up_task(c):
            cols = slice(c * MXU_DIM, (c + 1) * MXU_DIM)

            def epilogue(r):
                h = r + b1_ref[0, :, cols]
                gate = jnp.minimum(h[:, :LANES], SWIGLU_LIMIT)
                up = jnp.clip(h[:, LANES:], -SWIGLU_LIMIT, SWIGLU_LIMIT)
                glu = gate * _sigmoid(SWIGLU_ALPHA * gate)
                act_ref[:, c * LANES:(c + 1) * LANES] = ((up + 1.0) * glu).astype(BF16)
            return (lambda: jnp.dot(x, w1p_ref[:, cols], preferred_element_type=F32)), epilogue

        _run_lookahead([up_task(c) for c in range(n_chunks)])
        y = jnp.dot(act_ref[...], w2b_ref[...], preferred_element_type=F32) + b2_ref[0]
        words = _pack_rows(y)
        yout_ref[slot, 0] = words[:, :PACK_HALF]
        yout_ref[slot, 1] = words[:, PACK_HALF:]
        for half in range(2):
            out_copy(j, slot, half).start(priority=ROW_DMA_PRIORITY)
        return 0

    lax.fori_loop(0, n_blk, block, 0)

    for back in (2, 1):
        @pl.when(n_blk >= back)
        def _(back=back):
            j = n_blk - back
            for half in range(2):
                out_copy(j, j % 2, half).wait()


def _gate_up_order():
    j = jnp.arange(MXU_DIM)
    within = jnp.where(j < LANES, 2 * j, 2 * (j - LANES) + 1)
    return within


def _experts(xs_lo, xs_hi, w1, b1, w2, b2, counts, region):
    within = _gate_up_order()
    perm = (jnp.arange(MXU_DIM)[:, None] == within[None, :]).astype(BF16)
    order = (jnp.arange(0, 2 * D_FF, MXU_DIM)[:, None] + within[None, :]).reshape(-1)
    b1p = b1.astype(F32)[:, order][:, None, :]
    b2r = b2.astype(F32)[:, None, :]
    tb = EXPERT_BLOCK
    hbm = pl.BlockSpec(memory_space=pl.ANY)
    wspec = lambda shape: pl.BlockSpec((1,) + shape, lambda e, cnt: (e, 0, 0))
    grid_spec = pltpu.PrefetchScalarGridSpec(
        num_scalar_prefetch=1,
        grid=(N_EXPERTS,),
        in_specs=[hbm, hbm, wspec((D_MODEL, 2 * D_FF)), wspec((1, 2 * D_FF)),
                  wspec((D_FF, D_MODEL)), wspec((1, D_MODEL)),
                  pl.BlockSpec((MXU_DIM, MXU_DIM), lambda e, cnt: (0, 0))],
        out_specs=[hbm, hbm],
        scratch_shapes=[pltpu.VMEM((D_MODEL, 2 * D_FF), BF16), pltpu.VMEM((D_FF, D_MODEL), BF16),
                        pltpu.VMEM((tb, D_FF), BF16),
                        pltpu.VMEM((2, 2, tb, PACK_HALF), jnp.uint32), pltpu.VMEM((2, 2, tb, PACK_HALF), jnp.uint32),
                        pltpu.SemaphoreType.DMA((2, 2)), pltpu.SemaphoreType.DMA((2, 2))],
    )
    return pl.pallas_call(
        functools.partial(_expert_kernel, region=region),
        grid_spec=grid_spec,
        out_shape=[jax.ShapeDtypeStruct(xs_lo.shape, jnp.uint32)] * 2,
        compiler_params=_cparams(("arbitrary",)),
        name="experts",
    )(counts, xs_lo, xs_hi, w1, b1p, w2, b2r, perm)


def _combine_kernel(x1_ref, gate_ref, *refs):
    lo_refs, hi_refs, o_ref = refs[:TOP_K], refs[TOP_K:2 * TOP_K], refs[2 * TOP_K]
    acc = x1_ref[...]
    gates = gate_ref[...]
    for kk in range(TOP_K):
        words = jnp.concatenate([lo_refs[kk][...], hi_refs[kk][...]], axis=1)
        acc = acc + gates[:, kk:kk + 1] * _unpack_rows(words)
    o_ref[...] = acc


def _combine(x1, gates, yg_lo, yg_hi):
    n_tok = x1.shape[0]
    tm = min(ROW_TILE, n_tok)
    nblk = n_tok // tm
    row = lambda w: pl.BlockSpec((tm, w), lambda i: (i, 0))
    plane = lambda kk: pl.BlockSpec((tm, PACK_HALF), lambda i, kk=kk: (kk * nblk + i, 0))
    planes = [plane(kk) for kk in range(TOP_K)]
    return pl.pallas_call(
        _combine_kernel,
        grid=(nblk,),
        in_specs=[row(D_MODEL), row(LANES)] + planes + planes,
        out_specs=row(D_MODEL),
        out_shape=jax.ShapeDtypeStruct((n_tok, D_MODEL), F32),
        compiler_params=_cparams(("parallel",)),
        name="combine",
    )(x1, gates, *([yg_lo] * TOP_K), *([yg_hi] * TOP_K))


def kernel(x, norm1_gain, w_in, lambda_re, lambda_im, log_dt, ssm_b_re, ssm_b_im, ssm_c_re, ssm_c_im, ssm_d, w_glu, b_glu, q_norm_gain, k_norm_gain, lambda_q1, lambda_k1, lambda_q2, lambda_k2, subln_gain, w_proj_ssm, w_proj_attn, w_out, norm2_gain, w_router, b_router, w_exp1, b_exp1, w_exp2, b_exp2):
    bsz, seq, d = x.shape
    n_tok = bsz * seq
    depth = norm1_gain.shape[0]
    row1 = lambda a: a.astype(F32).reshape(1, -1)
    for l in range(depth):
        lambda_init = 0.8 - 0.6 * math.exp(-0.3 * l)
        x2d = x.reshape(n_tok, d)

        u, q, k, vt, gs, ga = _in_proj(x2d, row1(norm1_gain[l]), w_in[l], q_norm_gain[l], k_norm_gain[l])

        bblk, cblk, a_tile = _ssm_params(lambda_re[l], lambda_im[l], log_dt[l], ssm_b_re[l], ssm_b_im[l],
                                         ssm_c_re[l], ssm_c_im[l])
        u_tb = u.reshape(bsz, seq, SSM_WIDTH).transpose(1, 0, 2).reshape(n_tok, SSM_WIDTH)
        so_tb = _ssm(u_tb, seq, bblk, cblk, a_tile, row1(ssm_d[l]), w_glu[l].astype(BF16), row1(b_glu[l]))
        so = so_tb.reshape(seq, bsz, SSM_WIDTH).transpose(1, 0, 2).reshape(n_tok, SSM_WIDTH)

        lam = (jnp.exp(jnp.sum(lambda_q1[l].astype(F32) * lambda_k1[l].astype(F32)))
               - jnp.exp(jnp.sum(lambda_q2[l].astype(F32) * lambda_k2[l].astype(F32)))
               + lambda_init).reshape(1)
        ao = _attention(q, k, vt, lam, subln_gain[l], lambda_init, bsz, seq)

        region = n_tok
        x1, h_lo, h_hi, route, gates, counts = _merge_route(
            so, ao, gs, ga, x2d, w_proj_ssm[l].astype(BF16), w_proj_attn[l].astype(BF16),
            w_out[l].astype(BF16), row1(norm2_gain[l]), w_router[l], b_router[l], region)

        dest = route[:, :TOP_K].T
        n_slots = N_EXPERTS * region
        xs_lo = _sc_scatter_rows(h_lo, dest, n_slots)
        xs_hi = _sc_scatter_rows(h_hi, dest, n_slots)

        ys_lo, ys_hi = _experts(xs_lo, xs_hi, w_exp1[l], b_exp1[l], w_exp2[l], b_exp2[l],
                                counts[0, :N_EXPERTS], region)

        flat = dest.reshape(1, TOP_K * n_tok)
        yg_lo = _sc_gather_rows(ys_lo, flat)
        yg_hi = _sc_gather_rows(ys_hi, flat)
        x = _combine(x1, gates, yg_lo, yg_hi).reshape(bsz, seq, d)
    return x
```

```python
import functools
import math

import jax
import jax.numpy as jnp
from jax import lax
from jax.experimental import pallas as pl
from jax.experimental.pallas import tpu as pltpu
from jax.experimental.pallas import tpu_sc as plsc

F32 = jnp.float32
BF16 = jnp.bfloat16

D_MODEL = 1024
NORM_EPS = 1e-5
SSM_WIDTH = 512
SSM_GROUP = 16
SSM_GROUPS = 32
SSM_STATE = 64
N_STATE = SSM_GROUPS * SSM_STATE
HEADS = 8
HEAD_DIM = 64
N_EXPERTS = 32
TOP_K = 4
D_FF = 1024
SWIGLU_ALPHA = 1.702
SWIGLU_LIMIT = 7.0

LANES = 128
SUBLANES = 8
MXU_DIM = 256
VMEM_LIMIT = 56 * 1024 * 1024

ROW_TILE = 512
SSM_CHUNK = 128
SCAN_LANES = 512
ATTN_BLOCK = 256
ATTN_HEADS_PER_STEP = 8
EXPERT_BLOCK = 512
ROW_DMA_PRIORITY = 1
SC_WINDOW = 128
PACK_W = D_MODEL // 2
PACK_HALF = PACK_W // 2

_NEG = -1e30
Q_SCALE = math.log2(math.e) / math.sqrt(HEAD_DIM)


def _cparams(sem):
    return pltpu.CompilerParams(dimension_semantics=sem, vmem_limit_bytes=VMEM_LIMIT)


def _full(shape):
    nd = len(shape)
    return pl.BlockSpec(shape, lambda *_: (0,) * nd)


def _sigmoid(x):
    return 0.5 * jnp.tanh(0.5 * x) + 0.5


def _run_lookahead(tasks):
    pending = tasks[0][0]()
    for i, (_, epilogue) in enumerate(tasks):
        result = pending
        if i + 1 < len(tasks):
            pending = tasks[i + 1][0]()
        epilogue(result)


def _inproj_kernel(x_ref, g1_ref, w_ref, wvt_ref, qg_ref, kg_ref, seg_ref,
                   u_ref, q_ref, k_ref, vt_ref, gs_ref, ga_ref):
    x = x_ref[...]
    ms = jnp.mean(x * x, axis=-1, keepdims=True)
    h = (x * lax.rsqrt(ms + NORM_EPS) * g1_ref[...]).astype(BF16)

    def proj(c0, width):
        return lambda: jnp.dot(h, w_ref[:, c0:c0 + width], preferred_element_type=F32)

    seg = seg_ref[...]
    q0 = SSM_WIDTH
    k0 = q0 + D_MODEL
    gs0 = k0 + D_MODEL
    ga0 = gs0 + D_MODEL
    half = D_MODEL // 2
    tasks = []

    def store(out_ref, cols, fn):
        def epilogue(r):
            out_ref[:, cols] = fn(r)
        return epilogue

    tasks.append((proj(0, SSM_WIDTH), store(u_ref, slice(None), lambda r: r)))

    def head_norm_tasks(base, gain_ref, out_ref, scale, c):
        cols = slice(c * MXU_DIM, (c + 1) * MXU_DIM)
        kept = {}

        def after_proj(y):
            kept["y"] = y
            kept["sq"] = (y * y).astype(BF16)

        def after_sum(ss):
            yn = kept["y"] * lax.rsqrt(ss * (1.0 / HEAD_DIM) + NORM_EPS) * gain_ref[:, cols]
            out_ref[:, cols] = (yn * scale).astype(BF16)

        return ((proj(base + c * MXU_DIM, MXU_DIM), after_proj),
                (lambda: jnp.dot(kept["sq"], seg, preferred_element_type=F32), after_sum))

    pairs = [head_norm_tasks(base, gain_ref, out_ref, scale, c)
             for base, gain_ref, out_ref, scale in ((q0, qg_ref, q_ref, Q_SCALE), (k0, kg_ref, k_ref, 1.0))
             for c in range(D_MODEL // MXU_DIM)]
    tasks.append(pairs[0][0])
    for prev, cur in zip(pairs, pairs[1:]):
        tasks += [cur[0], prev[1]]
    tasks.append(pairs[-1][1])

    to_gate = lambda r: _sigmoid(r).astype(BF16)
    for c in range(2):
        cols = slice(c * half, (c + 1) * half)
        tasks.append((proj(gs0 + c * half, half), store(gs_ref, cols, to_gate)))
        tasks.append((proj(ga0 + c * half, half), store(ga_ref, cols, to_gate)))

    def vt_task(r0):
        def matmul():
            return lax.dot_general(wvt_ref[r0:r0 + MXU_DIM, :], h, (((1,), (1,)), ((), ())),
                                   preferred_element_type=F32)

        def epilogue(r):
            vt_ref[r0:r0 + MXU_DIM, :] = r.astype(BF16)
        return matmul, epilogue

    tasks += [vt_task(r0) for r0 in range(0, D_MODEL, MXU_DIM)]
    _run_lookahead(tasks)


def _in_proj(x2d, gain1, w_in, q_gain, k_gain):
    n_tok = x2d.shape[0]
    tm = min(ROW_TILE, n_tok)
    v0 = SSM_WIDTH + 2 * D_MODEL
    w_bf = w_in.astype(BF16)
    w_main = jnp.concatenate([w_bf[:, :v0], w_bf[:, v0 + D_MODEL:]], axis=1)
    w_vt = w_bf[:, v0:v0 + D_MODEL].T
    in_width = w_main.shape[1]
    seg = (jnp.arange(MXU_DIM)[:, None] // HEAD_DIM == jnp.arange(MXU_DIM)[None, :] // HEAD_DIM).astype(BF16)
    reps = D_MODEL // HEAD_DIM
    qg = jnp.tile(q_gain.astype(F32), reps)[None, :]
    kg = jnp.tile(k_gain.astype(F32), reps)[None, :]
    row = lambda w: pl.BlockSpec((tm, w), lambda i: (i, 0))
    tok = jax.ShapeDtypeStruct((n_tok, D_MODEL), BF16)
    out_shapes = [jax.ShapeDtypeStruct((n_tok, SSM_WIDTH), F32), tok, tok,
                  jax.ShapeDtypeStruct((D_MODEL, n_tok), BF16), tok, tok]
    vt_spec = pl.BlockSpec((D_MODEL, tm), lambda i: (0, i))
    return pl.pallas_call(
        _inproj_kernel,
        grid=(n_tok // tm,),
        in_specs=[row(D_MODEL), _full((1, D_MODEL)), _full((D_MODEL, in_width)), _full((D_MODEL, D_MODEL)),
                  _full((1, D_MODEL)), _full((1, D_MODEL)), _full((MXU_DIM, MXU_DIM))],
        out_specs=[row(SSM_WIDTH), row(D_MODEL), row(D_MODEL), vt_spec, row(D_MODEL), row(D_MODEL)],
        out_shape=out_shapes,
        compiler_params=_cparams(("parallel",)),
        name="in_proj",
    )(x2d, gain1, w_main, w_vt, qg, kg, seg)


def _ssm_kernel(u_ref, bblk_ref, a_ref, cblk_ref, d_ref, wglu_ref, bglu_ref, o_ref, bu_ref, st_ref):
    chunk = u_ref.shape[0] // SUBLANES

    @pl.when(pl.program_id(0) == 0)
    def _():
        st_ref[...] = jnp.zeros_like(st_ref)

    u = u_ref[...]
    u_bf = u.astype(BF16)
    tiles_per_part = N_STATE // MXU_DIM
    ch_per_tile = SSM_WIDTH // tiles_per_part
    for n in range(2 * tiles_per_part):
        ch0 = ((n % tiles_per_part) * ch_per_tile) // LANES * LANES
        lanes = slice(n * MXU_DIM, (n + 1) * MXU_DIM)
        bu_ref[:, lanes] = jnp.dot(u_bf[:, ch0:ch0 + LANES], bblk_ref[ch0:ch0 + LANES, lanes],
                                   preferred_element_type=F32)

    for j in range(N_STATE // SCAN_LANES):
        re = slice(j * SCAN_LANES, (j + 1) * SCAN_LANES)
        im = slice(N_STATE + j * SCAN_LANES, N_STATE + (j + 1) * SCAN_LANES)
        ar = a_ref[:, re]
        ai = a_ref[:, im]

        def step(t, carry, re=re, im=im, ar=ar, ai=ai):
            xr, xi = carry
            rows = pl.ds(pl.multiple_of(t * SUBLANES, SUBLANES), SUBLANES)
            nr = ar * xr - ai * xi + bu_ref[rows, re]
            ni = ar * xi + ai * xr + bu_ref[rows, im]
            bu_ref[rows, re] = nr
            bu_ref[rows, im] = ni
            return nr, ni

        xr, xi = lax.fori_loop(0, chunk, step, (st_ref[:, re], st_ref[:, im]), unroll=4)
        st_ref[:, re] = xr
        st_ref[:, im] = xi

    n_out = SSM_WIDTH // MXU_DIM
    lanes_per_out = N_STATE // n_out
    ys = []
    for j in range(n_out):
        cols = slice(j * MXU_DIM, (j + 1) * MXU_DIM)
        acc = None
        for part in range(2):
            lanes = slice(part * N_STATE + j * lanes_per_out, part * N_STATE + (j + 1) * lanes_per_out)
            term = jnp.dot(bu_ref[:, lanes].astype(BF16), cblk_ref[lanes, cols], preferred_element_type=F32)
            acc = term if acc is None else acc + term
        ys.append(acc)
    y = jnp.concatenate(ys, axis=1) + d_ref[...] * u
    z = jax.nn.gelu(y)
    gate = _sigmoid(jnp.dot(z.astype(BF16), wglu_ref[...], preferred_element_type=F32) + bglu_ref[...])
    o_ref[...] = (z * gate).astype(BF16)


def _ssm_params(lambda_re, lambda_im, log_dt, b_re, b_im, c_re, c_im):
    dt = jnp.exp(log_dt.astype(F32))[:, None]
    lr = jnp.minimum(lambda_re.astype(F32), -1e-4)
    li = lambda_im.astype(F32)
    mag = jnp.exp(lr * dt)
    abar_re = mag * jnp.cos(li * dt)
    abar_im = mag * jnp.sin(li * dt)
    den = lr * lr + li * li
    nr = abar_re - 1.0
    coef_re = (nr * lr + abar_im * li) / den
    coef_im = (abar_im * lr - nr * li) / den
    br = b_re.astype(F32)
    bi = b_im.astype(F32)
    bbar_re = coef_re[..., None] * br - coef_im[..., None] * bi
    bbar_im = coef_re[..., None] * bi + coef_im[..., None] * br
    eye = jnp.eye(SSM_GROUPS, dtype=F32)

    def expand_b(b):
        return jnp.einsum('gph,gk->ghkp', b, eye).reshape(SSM_WIDTH, N_STATE)

    def expand_c(c):
        return jnp.einsum('ghp,gk->gpkh', c, eye).reshape(N_STATE, SSM_WIDTH)

    bblk = jnp.concatenate([expand_b(bbar_re), expand_b(bbar_im)], axis=1).astype(BF16)
    cblk = jnp.concatenate([expand_c(c_re.astype(F32)), -expand_c(c_im.astype(F32))], axis=0).astype(BF16)
    a_row = jnp.concatenate([abar_re.reshape(-1), abar_im.reshape(-1)])[None, :]
    return bblk, cblk, jnp.broadcast_to(a_row, (SUBLANES, 2 * N_STATE))


def _ssm(u_tb, seq, bblk, cblk, a_tile, d_skip, w_glu_bf, b_glu):
    chunk = min(SSM_CHUNK, seq)
    rows = chunk * SUBLANES
    return pl.pallas_call(
        _ssm_kernel,
        grid=(seq // chunk,),
        in_specs=[pl.BlockSpec((rows, SSM_WIDTH), lambda c: (c, 0)),
                  _full(bblk.shape), _full(a_tile.shape), _full(cblk.shape),
                  _full((1, SSM_WIDTH)), _full(w_glu_bf.shape), _full((1, SSM_WIDTH))],
        out_specs=pl.BlockSpec((rows, SSM_WIDTH), lambda c: (c, 0)),
        out_shape=jax.ShapeDtypeStruct(u_tb.shape, BF16),
        scratch_shapes=[pltpu.VMEM((rows, 2 * N_STATE), F32), pltpu.VMEM((SUBLANES, 2 * N_STATE), F32)],
        compiler_params=_cparams(("arbitrary",)),
        name="ssm",
    )(u_tb, bblk, a_tile, cblk, d_skip, w_glu_bf, b_glu)


def _attn_kernel(lam_ref, sg_ref, q_ref, k_ref, vt_ref, o_ref, acc_ref, *, out_scale, blk, nh):
    seq = q_ref.shape[1]
    hw = 2 * HEAD_DIM
    lane = lax.broadcasted_iota(jnp.int32, (blk, hw), 1)
    key_i = lax.broadcasted_iota(jnp.int32, (blk, blk), 0)
    qry_i = lax.broadcasted_iota(jnp.int32, (blk, blk), 1)
    keep = key_i <= qry_i
    contract_last = (((1,), (1,)), ((), ()))
    lam = lam_ref[0]
    n_chain = 2 * nh

    def q_block(qi, _):
        qrows = pl.ds(pl.multiple_of(qi * blk, blk), blk)
        qs = []
        for hh in range(nh):
            q = q_ref[0, qrows, hh * hw:(hh + 1) * hw]
            zero = jnp.zeros_like(q)
            qs += [jnp.where(lane < HEAD_DIM, q, zero), jnp.where(lane >= HEAD_DIM, q, zero)]
        acc_ref[...] = jnp.zeros_like(acc_ref)

        def kv_block(kb, carry, masked):
            krows = pl.ds(pl.multiple_of(kb * blk, blk), blk)
            scores = []
            for c in range(n_chain):
                hh = c // 2
                k = k_ref[0, krows, hh * hw:(hh + 1) * hw]
                scores.append(lax.dot_general(k, qs[c], contract_last, preferred_element_type=F32))
            out, probs, alphas = [], [], []
            for c in range(n_chain):
                m, l = carry[2 * c], carry[2 * c + 1]
                s = jnp.where(keep, scores[c], _NEG) if masked else scores[c]
                m_new = jnp.maximum(m, jnp.max(s, axis=0, keepdims=True))
                p = jnp.exp2(s - m_new)
                alpha = jnp.exp2(m - m_new)
                out += [m_new, alpha * l + jnp.sum(p, axis=0, keepdims=True)]
                probs.append(p.astype(BF16))
                alphas.append(alpha)
            for c in range(n_chain):
                hh = c // 2
                vt = vt_ref[hh * hw:(hh + 1) * hw, krows]
                acc_ref[c] = alphas[c] * acc_ref[c] + jnp.dot(vt, probs[c], preferred_element_type=F32)
            return tuple(out)

        carry = (jnp.full((1, blk), _NEG, F32), jnp.zeros((1, blk), F32)) * n_chain
        carry = lax.fori_loop(0, qi, lambda kb, c: kv_block(kb, c, False), carry)
        carry = kv_block(qi, carry, True)
        for hh in range(nh):
            l1, l2 = carry[4 * hh + 1], carry[4 * hh + 3]
            ot = acc_ref[2 * hh] * (1.0 / l1) - acc_ref[2 * hh + 1] * (lam / l2)
            ot = ot * lax.rsqrt(jnp.mean(ot * ot, axis=0, keepdims=True) + NORM_EPS)
            o_ref[0, qrows, hh * hw:(hh + 1) * hw] = (ot.T * sg_ref[...] * out_scale).astype(BF16)
        return 0

    lax.fori_loop(0, seq // blk, q_block, 0)


def _attention(q, k, vt, lam, subln_gain, lambda_init, bsz, seq):
    blk = min(ATTN_BLOCK, seq)
    nh = ATTN_HEADS_PER_STEP
    hw = 2 * HEAD_DIM
    q3, k3 = (a.reshape(bsz, seq, D_MODEL) for a in (q, k))
    tok_spec = pl.BlockSpec((1, seq, nh * hw), lambda b, h: (b, 0, h))
    out = pl.pallas_call(
        functools.partial(_attn_kernel, out_scale=1.0 - lambda_init, blk=blk, nh=nh),
        grid=(bsz, HEADS // nh),
        in_specs=[pl.BlockSpec(memory_space=pltpu.SMEM), _full((1, hw)), tok_spec, tok_spec,
                  pl.BlockSpec((nh * hw, seq), lambda b, h: (h, b))],
        out_specs=tok_spec,
        out_shape=jax.ShapeDtypeStruct((bsz, seq, D_MODEL), BF16),
        scratch_shapes=[pltpu.VMEM((2 * nh, hw, blk), F32)],
        compiler_params=_cparams(("parallel", "parallel")),
        name="diff_attn",
    )(lam, subln_gain.astype(F32)[None, :], q3, k3, vt)
    return out.reshape(bsz * seq, D_MODEL)


def _pack_rows(y):
    bits = lax.bitcast_convert_type(y.astype(BF16).astype(F32), jnp.uint32)
    return (bits[:, :PACK_W] >> 16) | (bits[:, PACK_W:] & jnp.uint32(0xFFFF0000))


def _unpack_rows(w):
    lo = lax.bitcast_convert_type(w << 16, F32)
    hi = lax.bitcast_convert_type(w & jnp.uint32(0xFFFF0000), F32)
    return jnp.concatenate([lo, hi], axis=1)


def _merge_kernel(so_ref, ao_ref, gs_ref, ga_ref, x_ref, wps_ref, wpa_ref, wo_ref, g2_ref,
                  wrh_ref, wrl_ref, br_ref, tri_ref,
                  x1_ref, hlo_ref, hhi_ref, route_ref, gate_ref, cnt_ref, run_ref, *, region):
    @pl.when(pl.program_id(0) == 0)
    def _():
        run_ref[...] = jnp.zeros_like(run_ref)

    tm = x_ref.shape[0]
    n_part = 2 if tm % (2 * MXU_DIM) == 0 else 1
    rows_per = tm // n_part
    lane = lax.broadcasted_iota(jnp.int32, (rows_per, LANES), 1)
    lane_f = lane.astype(F32)
    parts = [dict(rows=slice(p * rows_per, (p + 1) * rows_per)) for p in range(n_part)]

    def stage_proj(st):
        def matmul():
            return (jnp.dot(so_ref[st["rows"], :], wps_ref[...], preferred_element_type=F32),
                    jnp.dot(ao_ref[st["rows"], :], wpa_ref[...], preferred_element_type=F32))

        def epilogue(r):
            ps, pa = r
            merged = gs_ref[st["rows"], :].astype(F32) * ps + ga_ref[st["rows"], :].astype(F32) * pa
            st["merged"] = merged.astype(BF16)
        return matmul, epilogue

    def stage_out(st):
        def matmul():
            return jnp.dot(st["merged"], wo_ref[...], preferred_element_type=F32)

        def epilogue(r):
            x1 = x_ref[st["rows"], :] + r
            x1_ref[st["rows"], :] = x1
            h2 = x1 * lax.rsqrt(jnp.mean(x1 * x1, axis=-1, keepdims=True) + NORM_EPS) * g2_ref[...]
            words = _pack_rows(h2)
            hlo_ref[st["rows"], :] = words[:, :PACK_HALF]
            hhi_ref[st["rows"], :] = words[:, PACK_HALF:]
            st["h_hi"] = h2.astype(BF16)
            st["h_lo"] = (h2 - st["h_hi"].astype(F32)).astype(BF16)
        return matmul, epilogue

    def stage_router(st):
        def matmul():
            return (jnp.dot(st["h_hi"], wrh_ref[...], preferred_element_type=F32)
                    + jnp.dot(st["h_lo"], wrh_ref[...], preferred_element_type=F32)
                    + jnp.dot(st["h_hi"], wrl_ref[...], preferred_element_type=F32))

        def epilogue(r):
            work = r + br_ref[...]
            onehots, vals, ids = [], [], []
            for _ in range(TOP_K):
                m = jnp.max(work, axis=-1, keepdims=True)
                idx = jnp.min(jnp.where(work == m, lane_f, float(LANES)), axis=-1, keepdims=True)
                oh = lane_f == idx
                onehots.append(oh)
                vals.append(m)
                ids.append(idx.astype(jnp.int32))
                work = jnp.where(oh, -jnp.inf, work)
            exps = [jnp.exp(v - vals[0]) for v in vals]
            den = exps[0] + exps[1] + exps[2] + exps[3]
            st.update(onehots=onehots, ids=ids, gates=[e / den for e in exps],
                      multi=(onehots[0] | onehots[1] | onehots[2] | onehots[3]).astype(F32))
        return matmul, epilogue

    _run_lookahead([stage(st) for stage in (stage_proj, stage_out, stage_router) for st in parts])

    multi = jnp.concatenate([st["multi"] for st in parts], axis=0)
    before = jnp.dot(tri_ref[...], multi.astype(BF16), preferred_element_type=F32) + run_ref[...]
    for st in parts:
        route = jnp.zeros((rows_per, LANES), jnp.int32)
        gates = jnp.zeros((rows_per, LANES), F32)
        for kk in range(TOP_K):
            rank = jnp.sum(jnp.where(st["onehots"][kk], before[st["rows"], :], 0.0), axis=-1, keepdims=True)
            route = jnp.where(lane == kk, st["ids"][kk] * region + rank.astype(jnp.int32), route)
            gates = jnp.where(lane == kk, st["gates"][kk], gates)
        route_ref[st["rows"], :] = route
        gate_ref[st["rows"], :] = gates
    run = run_ref[...] + jnp.sum(multi, axis=0, keepdims=True)
    run_ref[...] = run
    cnt_ref[...] = run.astype(jnp.int32)


def _merge_route(so, ao, gs, ga, x2d, wps, wpa, wo, gain2, w_router, b_router, region):
    n_tok = x2d.shape[0]
    tm = min(ROW_TILE, n_tok)
    wr = jnp.zeros((D_MODEL, LANES), F32).at[:, :N_EXPERTS].set(w_router.astype(F32))
    wr_hi = wr.astype(BF16)
    wr_lo = (wr - wr_hi.astype(F32)).astype(BF16)
    br = jnp.full((1, LANES), -jnp.inf, F32).at[0, :N_EXPERTS].set(b_router.astype(F32))
    tri = (jnp.arange(tm)[:, None] > jnp.arange(tm)[None, :]).astype(BF16)
    row = lambda w: pl.BlockSpec((tm, w), lambda i: (i, 0))
    out_shapes = [jax.ShapeDtypeStruct((n_tok, D_MODEL), F32),
                  jax.ShapeDtypeStruct((n_tok, PACK_HALF), jnp.uint32),
                  jax.ShapeDtypeStruct((n_tok, PACK_HALF), jnp.uint32),
                  jax.ShapeDtypeStruct((n_tok, LANES), jnp.int32),
                  jax.ShapeDtypeStruct((n_tok, LANES), F32),
                  jax.ShapeDtypeStruct((1, LANES), jnp.int32)]
    return pl.pallas_call(
        functools.partial(_merge_kernel, region=region),
        grid=(n_tok // tm,),
        in_specs=[row(SSM_WIDTH), row(D_MODEL), row(D_MODEL), row(D_MODEL), row(D_MODEL),
                  _full(wps.shape), _full(wpa.shape), _full(wo.shape), _full((1, D_MODEL)),
                  _full(wr_hi.shape), _full(wr_lo.shape), _full((1, LANES)), _full((tm, tm))],
        out_specs=[row(D_MODEL), row(PACK_HALF), row(PACK_HALF), row(LANES), row(LANES), _full((1, LANES))],
        out_shape=out_shapes,
        scratch_shapes=[pltpu.VMEM((1, LANES), F32)],
        compiler_params=_cparams(("arbitrary",)),
        name="merge_route",
    )(so, ao, gs, ga, x2d, wps, wpa, wo, gain2, wr_hi, wr_lo, br, tri)


def _sc_scatter_rows(rows, dest, n_slots):
    n_tok, width = rows.shape
    mesh = plsc.VectorSubcoreMesh(core_axis_name="core", subcore_axis_name="subcore")

    @pl.kernel(out_type=jax.ShapeDtypeStruct((n_slots, width), rows.dtype), mesh=mesh, scratch_types=[])
    def scatter(rows_hbm, dest_hbm, out_hbm):
        def body(rows_vmem, dest_vmem):
            pltpu.sync_copy(rows_vmem, out_hbm.at[dest_vmem.at[0]])

        pltpu.emit_pipeline(
            body,
            grid=(TOP_K, n_tok // SC_WINDOW),
            in_specs=[pl.BlockSpec((SC_WINDOW, width), lambda k, i: (i, 0)),
                      pl.BlockSpec((1, SC_WINDOW), lambda k, i: (k, i))],
            out_specs=[],
            core_axis_name=("core", "subcore"),
            dimension_semantics=(pltpu.PARALLEL, pltpu.PARALLEL),
        )(rows_hbm, dest_hbm)

    return scatter(rows, dest)


def _sc_gather_rows(table, idx):
    n = idx.shape[1]
    width = table.shape[1]
    mesh = plsc.VectorSubcoreMesh(core_axis_name="core", subcore_axis_name="subcore")

    @pl.kernel(out_type=jax.ShapeDtypeStruct((n, width), table.dtype), mesh=mesh, scratch_types=[])
    def gather(table_hbm, idx_hbm, out_hbm):
        def body(idx_vmem, out_vmem):
            pltpu.sync_copy(table_hbm.at[idx_vmem.at[0]], out_vmem)

        pltpu.emit_pipeline(
            body,
            grid=(n // SC_WINDOW,),
            in_specs=[pl.BlockSpec((1, SC_WINDOW), lambda i: (0, i))],
            out_specs=[pl.BlockSpec((SC_WINDOW, width), lambda i: (i, 0))],
            core_axis_name=("core", "subcore"),
            dimension_semantics=(pltpu.PARALLEL,),
        )(idx_hbm, out_hbm)

    return gather(table, idx)


def _expert_kernel(cnt_ref, xlo_hbm, xhi_hbm, w1_ref, b1_ref, w2_ref, b2_ref, perm_ref, ylo_hbm, yhi_hbm,
                   w1p_ref, w2b_ref, x_ref, act_ref, xin_ref, yout_ref, in_sem, out_sem, *, region):
    e = pl.program_id(0)
    n_blk = (cnt_ref[e] + EXPERT_BLOCK - 1) // EXPERT_BLOCK
    base = e * region
    n_chunks = (2 * D_FF) // MXU_DIM
    x_hbm = (xlo_hbm, xhi_hbm)
    y_hbm = (ylo_hbm, yhi_hbm)

    def rows_of(j):
        return pl.ds(pl.multiple_of(base + j * EXPERT_BLOCK, EXPERT_BLOCK), EXPERT_BLOCK)

    def in_copy(j, slot, half):
        return pltpu.make_async_copy(x_hbm[half].at[rows_of(j), :], xin_ref.at[slot, half], in_sem.at[slot, half])

    def out_copy(j, slot, half):
        return pltpu.make_async_copy(yout_ref.at[slot, half], y_hbm[half].at[rows_of(j), :], out_sem.at[slot, half])

    @pl.when(n_blk > 0)
    def _():
        for half in range(2):
            in_copy(0, 0, half).start(priority=ROW_DMA_PRIORITY)

    perm = perm_ref[...]
    for c in range(n_chunks):
        cols = slice(c * MXU_DIM, (c + 1) * MXU_DIM)
        w1p_ref[c] = jnp.dot(w1_ref[0, :, cols].astype(BF16), perm, preferred_element_type=F32).astype(BF16)
    for c in range(D_MODEL // MXU_DIM):
        w2b_ref[c] = w2_ref[0, :, c * MXU_DIM:(c + 1) * MXU_DIM].astype(BF16)

    def block(j, _):
        slot = j % 2
        for half in range(2):
            in_copy(j, slot, half).wait()

        @pl.when(j + 1 < n_blk)
        def _():
            for half in range(2):
                in_copy(j + 1, 1 - slot, half).start(priority=ROW_DMA_PRIORITY)

        @pl.when(j >= 2)
        def _():
            for half in range(2):
                out_copy(j - 2, slot, half).wait()

        words = jnp.concatenate([xin_ref[slot, 0], xin_ref[slot, 1]], axis=1)
        x_ref[...] = _unpack_rows(words).astype(BF16)

        def up_task(c):
            cols = slice(c * MXU_DIM, (c + 1) * MXU_DIM)

            def epilogue(r):
                h = r + b1_ref[0, :, cols]
                gate = jnp.minimum(h[:, :LANES], SWIGLU_LIMIT)
                up = jnp.clip(h[:, LANES:], -SWIGLU_LIMIT, SWIGLU_LIMIT)
                glu = gate * _sigmoid(SWIGLU_ALPHA * gate)
                act_ref[:, c * LANES:(c + 1) * LANES] = ((up + 1.0) * glu).astype(BF16)
            return (lambda: jnp.dot(x_ref[...], w1p_ref[c], preferred_element_type=F32)), epilogue

        _run_lookahead([up_task(c) for c in range(n_chunks)])

        def down_task(c):
            cols = slice(c * MXU_DIM, (c + 1) * MXU_DIM)
            half, high = c % 2, c // 2

            def epilogue(r):
                bits = lax.bitcast_convert_type((r + b2_ref[0, :, cols]).astype(BF16).astype(F32), jnp.uint32)
                if high:
                    yout_ref[slot, half] = yout_ref[slot, half] | (bits & jnp.uint32(0xFFFF0000))
                else:
                    yout_ref[slot, half] = bits >> 16
            return (lambda: jnp.dot(act_ref[...], w2b_ref[c], preferred_element_type=F32)), epilogue

        _run_lookahead([down_task(c) for c in range(D_MODEL // MXU_DIM)])
        for half in range(2):
            out_copy(j, slot, half).start(priority=ROW_DMA_PRIORITY)
        return 0

    lax.fori_loop(0, n_blk, block, 0)

    for back in (2, 1):
        @pl.when(n_blk >= back)
        def _(back=back):
            j = n_blk - back
            for half in range(2):
                out_copy(j, j % 2, half).wait()


def _gate_up_order():
    j = jnp.arange(MXU_DIM)
    within = jnp.where(j < LANES, 2 * j, 2 * (j - LANES) + 1)
    return within


def _experts(xs_lo, xs_hi, w1, b1, w2, b2, counts, region):
    within = _gate_up_order()
    perm = (jnp.arange(MXU_DIM)[:, None] == within[None, :]).astype(BF16)
    order = (jnp.arange(0, 2 * D_FF, MXU_DIM)[:, None] + within[None, :]).reshape(-1)
    b1p = b1.astype(F32)[:, order][:, None, :]
    b2r = b2.astype(F32)[:, None, :]
    tb = EXPERT_BLOCK
    hbm = pl.BlockSpec(memory_space=pl.ANY)
    wspec = lambda shape: pl.BlockSpec((1,) + shape, lambda e, cnt: (e, 0, 0))
    grid_spec = pltpu.PrefetchScalarGridSpec(
        num_scalar_prefetch=1,
        grid=(N_EXPERTS,),
        in_specs=[hbm, hbm, wspec((D_MODEL, 2 * D_FF)), wspec((1, 2 * D_FF)),
                  wspec((D_FF, D_MODEL)), wspec((1, D_MODEL)),
                  pl.BlockSpec((MXU_DIM, MXU_DIM), lambda e, cnt: (0, 0))],
        out_specs=[hbm, hbm],
        scratch_shapes=[pltpu.VMEM((2 * D_FF // MXU_DIM, D_MODEL, MXU_DIM), BF16),
                        pltpu.VMEM((D_MODEL // MXU_DIM, D_FF, MXU_DIM), BF16),
                        pltpu.VMEM((tb, D_MODEL), BF16), pltpu.VMEM((tb, D_FF), BF16),
                        pltpu.VMEM((2, 2, tb, PACK_HALF), jnp.uint32), pltpu.VMEM((2, 2, tb, PACK_HALF), jnp.uint32),
                        pltpu.SemaphoreType.DMA((2, 2)), pltpu.SemaphoreType.DMA((2, 2))],
    )
    return pl.pallas_call(
        functools.partial(_expert_kernel, region=region),
        grid_spec=grid_spec,
        out_shape=[jax.ShapeDtypeStruct(xs_lo.shape, jnp.uint32)] * 2,
        compiler_params=_cparams(("arbitrary",)),
        name="experts",
    )(counts, xs_lo, xs_hi, w1, b1p, w2, b2r, perm)


def _combine_kernel(x1_ref, gate_ref, *refs):
    lo_refs, hi_refs, o_ref = refs[:TOP_K], refs[TOP_K:2 * TOP_K], refs[2 * TOP_K]
    acc = x1_ref[...]
    gates = gate_ref[...]
    for kk in range(TOP_K):
        words = jnp.concatenate([lo_refs[kk][...], hi_refs[kk][...]], axis=1)
        acc = acc + gates[:, kk:kk + 1] * _unpack_rows(words)
    o_ref[...] = acc


def _combine(x1, gates, yg_lo, yg_hi):
    n_tok = x1.shape[0]
    tm = min(ROW_TILE, n_tok)
    nblk = n_tok // tm
    row = lambda w: pl.BlockSpec((tm, w), lambda i: (i, 0))
    plane = lambda kk: pl.BlockSpec((tm, PACK_HALF), lambda i, kk=kk: (kk * nblk + i, 0))
    planes = [plane(kk) for kk in range(TOP_K)]
    return pl.pallas_call(
        _combine_kernel,
        grid=(nblk,),
        in_specs=[row(D_MODEL), row(LANES)] + planes + planes,
        out_specs=row(D_MODEL),
        out_shape=jax.ShapeDtypeStruct((n_tok, D_MODEL), F32),
        compiler_params=_cparams(("parallel",)),
        name="combine",
    )(x1, gates, *([yg_lo] * TOP_K), *([yg_hi] * TOP_K))


def kernel(x, norm1_gain, w_in, lambda_re, lambda_im, log_dt, ssm_b_re, ssm_b_im, ssm_c_re, ssm_c_im, ssm_d, w_glu, b_glu, q_norm_gain, k_norm_gain, lambda_q1, lambda_k1, lambda_q2, lambda_k2, subln_gain, w_proj_ssm, w_proj_attn, w_out, norm2_gain, w_router, b_router, w_exp1, b_exp1, w_exp2, b_exp2):
    bsz, seq, d = x.shape
    n_tok = bsz * seq
    depth = norm1_gain.shape[0]
    row1 = lambda a: a.astype(F32).reshape(1, -1)
    for l in range(depth):
        lambda_init = 0.8 - 0.6 * math.exp(-0.3 * l)
        x2d = x.reshape(n_tok, d)

        u, q, k, vt, gs, ga = _in_proj(x2d, row1(norm1_gain[l]), w_in[l], q_norm_gain[l], k_norm_gain[l])

        bblk, cblk, a_tile = _ssm_params(lambda_re[l], lambda_im[l], log_dt[l], ssm_b_re[l], ssm_b_im[l],
                                         ssm_c_re[l], ssm_c_im[l])
        u_tb = u.reshape(bsz, seq, SSM_WIDTH).transpose(1, 0, 2).reshape(n_tok, SSM_WIDTH)
        so_tb = _ssm(u_tb, seq, bblk, cblk, a_tile, row1(ssm_d[l]), w_glu[l].astype(BF16), row1(b_glu[l]))
        so = so_tb.reshape(seq, bsz, SSM_WIDTH).transpose(1, 0, 2).reshape(n_tok, SSM_WIDTH)

        lam = (jnp.exp(jnp.sum(lambda_q1[l].astype(F32) * lambda_k1[l].astype(F32)))
               - jnp.exp(jnp.sum(lambda_q2[l].astype(F32) * lambda_k2[l].astype(F32)))
               + lambda_init).reshape(1)
        ao = _attention(q, k, vt, lam, subln_gain[l], lambda_init, bsz, seq)

        region = n_tok
        x1, h_lo, h_hi, route, gates, counts = _merge_route(
            so, ao, gs, ga, x2d, w_proj_ssm[l].astype(BF16), w_proj_attn[l].astype(BF16),
            w_out[l].astype(BF16), row1(norm2_gain[l]), w_router[l], b_router[l], region)

        dest = route[:, :TOP_K].T
        n_slots = N_EXPERTS * region
        xs_lo = _sc_scatter_rows(h_lo, dest, n_slots)
        xs_hi = _sc_scatter_rows(h_hi, dest, n_slots)

        ys_lo, ys_hi = _experts(xs_lo, xs_hi, w_exp1[l], b_exp1[l], w_exp2[l], b_exp2[l],
                                counts[0, :N_EXPERTS], region)

        flat = dest.reshape(1, TOP_K * n_tok)
        yg_lo = _sc_gather_rows(ys_lo, flat)
        yg_hi = _sc_gather_rows(ys_hi, flat)
        x = _combine(x1, gates, yg_lo, yg_hi).reshape(bsz, seq, d)
    return x
```

```python
import functools
import math

import jax
import jax.numpy as jnp
from jax import lax
from jax.experimental import pallas as pl
from jax.experimental.pallas import tpu as pltpu
from jax.experimental.pallas import tpu_sc as plsc

F32 = jnp.float32
BF16 = jnp.bfloat16

D_MODEL = 1024
NORM_EPS = 1e-5
SSM_WIDTH = 512
SSM_GROUP = 16
SSM_GROUPS = 32
SSM_STATE = 64
N_STATE = SSM_GROUPS * SSM_STATE
HEADS = 8
HEAD_DIM = 64
N_EXPERTS = 32
TOP_K = 4
D_FF = 1024
SWIGLU_ALPHA = 1.702
SWIGLU_LIMIT = 7.0

LANES = 128
SUBLANES = 8
MXU_DIM = 256
VMEM_LIMIT = 56 * 1024 * 1024

ROW_TILE = 512
SSM_CHUNK = 128
SCAN_LANES = 512
ATTN_BLOCK = 256
ATTN_HEADS_PER_STEP = 8
EXPERT_BLOCK = 512
ROW_DMA_PRIORITY = 1
SC_WINDOW = 128
PACK_W = D_MODEL // 2
PACK_HALF = PACK_W // 2

_NEG = -1e30
Q_SCALE = math.log2(math.e) / math.sqrt(HEAD_DIM)


def _cparams(sem):
    return pltpu.CompilerParams(dimension_semantics=sem, vmem_limit_bytes=VMEM_LIMIT)


def _full(shape):
    nd = len(shape)
    return pl.BlockSpec(shape, lambda *_: (0,) * nd)


def _sigmoid(x):
    return 0.5 * jnp.tanh(0.5 * x) + 0.5


def _run_lookahead(tasks):
    pending = tasks[0][0]()
    for i, (_, epilogue) in enumerate(tasks):
        result = pending
        if i + 1 < len(tasks):
            pending = tasks[i + 1][0]()
        epilogue(result)


_IN_CHUNKS = {"u": (0, SSM_WIDTH // MXU_DIM)}
for _name in ("q", "k", "v", "gs", "ga"):
    _start = max(first + count for first, count in _IN_CHUNKS.values())
    _IN_CHUNKS[_name] = (_start, D_MODEL // MXU_DIM)
N_IN_CHUNKS = max(first + count for first, count in _IN_CHUNKS.values())


def _prep_w_in_kernel(w_ref, wc_ref, wvt_ref):
    c = pl.program_id(0)
    w = w_ref[...]
    wc_ref[0] = w.astype(BF16)
    v_first, v_count = _IN_CHUNKS["v"]

    @pl.when((c >= v_first) & (c < v_first + v_count))
    def _():
        wvt_ref[0] = w.T.astype(BF16)


def _prep_w_in(w_in):
    v_first, v_count = _IN_CHUNKS["v"]
    return pl.pallas_call(
        _prep_w_in_kernel,
        grid=(N_IN_CHUNKS,),
        in_specs=[pl.BlockSpec((D_MODEL, MXU_DIM), lambda c: (0, c))],
        out_specs=[pl.BlockSpec((1, D_MODEL, MXU_DIM), lambda c: (c, 0, 0)),
                   pl.BlockSpec((1, MXU_DIM, D_MODEL), lambda c: (jnp.clip(c - v_first, 0, v_count - 1), 0, 0))],
        out_shape=[jax.ShapeDtypeStruct((N_IN_CHUNKS, D_MODEL, MXU_DIM), BF16),
                   jax.ShapeDtypeStruct((v_count, MXU_DIM, D_MODEL), BF16)],
        compiler_params=_cparams(("arbitrary",)),
        name="prep_w_in",
    )(w_in)


def _inproj_kernel(x_ref, g1_ref, w_ref, wvt_ref, qg_ref, kg_ref, seg_ref,
                   u_ref, q_ref, k_ref, vt_ref, gs_ref, ga_ref):
    x = x_ref[...]
    ms = jnp.mean(x * x, axis=-1, keepdims=True)
    h = (x * lax.rsqrt(ms + NORM_EPS) * g1_ref[...]).astype(BF16)

    def proj(name, c):
        chunk = _IN_CHUNKS[name][0] + c
        return lambda: jnp.dot(h, w_ref[chunk], preferred_element_type=F32)

    seg = seg_ref[...]
    tasks = []

    def plain_task(name, out_ref, c, fn):
        cols = slice(c * MXU_DIM, (c + 1) * MXU_DIM)

        def epilogue(r):
            out_ref[:, cols] = fn(r)
        return proj(name, c), epilogue

    tasks += [plain_task("u", u_ref, c, lambda r: r) for c in range(_IN_CHUNKS["u"][1])]

    def head_norm_tasks(name, gain_ref, out_ref, scale, c):
        cols = slice(c * MXU_DIM, (c + 1) * MXU_DIM)
        kept = {}

        def after_proj(y):
            kept["y"] = y
            kept["sq"] = (y * y).astype(BF16)

        def after_sum(ss):
            yn = kept["y"] * lax.rsqrt(ss * (1.0 / HEAD_DIM) + NORM_EPS) * gain_ref[:, cols]
            out_ref[:, cols] = (yn * scale).astype(BF16)

        return ((proj(name, c), after_proj),
                (lambda: jnp.dot(kept["sq"], seg, preferred_element_type=F32), after_sum))

    pairs = [head_norm_tasks(name, gain_ref, out_ref, scale, c)
             for name, gain_ref, out_ref, scale in (("q", qg_ref, q_ref, Q_SCALE), ("k", kg_ref, k_ref, 1.0))
             for c in range(_IN_CHUNKS[name][1])]
    tasks.append(pairs[0][0])
    for prev, cur in zip(pairs, pairs[1:]):
        tasks += [cur[0], prev[1]]
    tasks.append(pairs[-1][1])

    to_gate = lambda r: _sigmoid(r).astype(BF16)
    for c in range(_IN_CHUNKS["gs"][1]):
        tasks.append(plain_task("gs", gs_ref, c, to_gate))
        tasks.append(plain_task("ga", ga_ref, c, to_gate))

    def vt_task(c):
        def matmul():
            return lax.dot_general(wvt_ref[c], h, (((1,), (1,)), ((), ())), preferred_element_type=F32)

        def epilogue(r):
            vt_ref[c * MXU_DIM:(c + 1) * MXU_DIM, :] = r.astype(BF16)
        return matmul, epilogue

    tasks += [vt_task(c) for c in range(_IN_CHUNKS["v"][1])]
    _run_lookahead(tasks)


def _in_proj(x2d, gain1, w_in, q_gain, k_gain):
    n_tok = x2d.shape[0]
    tm = min(ROW_TILE, n_tok)
    w_chunks, w_vt = _prep_w_in(w_in)
    seg = (jnp.arange(MXU_DIM)[:, None] // HEAD_DIM == jnp.arange(MXU_DIM)[None, :] // HEAD_DIM).astype(BF16)
    reps = D_MODEL // HEAD_DIM
    qg = jnp.tile(q_gain.astype(F32), reps)[None, :]
    kg = jnp.tile(k_gain.astype(F32), reps)[None, :]
    row = lambda w: pl.BlockSpec((tm, w), lambda i: (i, 0))
    tok = jax.ShapeDtypeStruct((n_tok, D_MODEL), BF16)
    out_shapes = [jax.ShapeDtypeStruct((n_tok, SSM_WIDTH), F32), tok, tok,
                  jax.ShapeDtypeStruct((D_MODEL, n_tok), BF16), tok, tok]
    vt_spec = pl.BlockSpec((D_MODEL, tm), lambda i: (0, i))
    return pl.pallas_call(
        _inproj_kernel,
        grid=(n_tok // tm,),
        in_specs=[row(D_MODEL), _full((1, D_MODEL)), _full(w_chunks.shape), _full(w_vt.shape),
                  _full((1, D_MODEL)), _full((1, D_MODEL)), _full((MXU_DIM, MXU_DIM))],
        out_specs=[row(SSM_WIDTH), row(D_MODEL), row(D_MODEL), vt_spec, row(D_MODEL), row(D_MODEL)],
        out_shape=out_shapes,
        compiler_params=_cparams(("parallel",)),
        name="in_proj",
    )(x2d, gain1, w_chunks, w_vt, qg, kg, seg)


def _ssm_kernel(u_ref, bblk_ref, a_ref, cblk_ref, d_ref, wglu_ref, bglu_ref, o_ref, bu_ref, st_ref, tb_ref):
    n_batch, chunk = u_ref.shape[0], u_ref.shape[1]

    @pl.when(pl.program_id(0) == 0)
    def _():
        st_ref[...] = jnp.zeros_like(st_ref)

    n_planes = SSM_WIDTH // LANES
    for b in range(n_batch):
        for j in range(n_planes):
            tb_ref[j, pl.ds(b, chunk, stride=n_batch), :] = u_ref[b, :, j * LANES:(j + 1) * LANES]
    u = jnp.concatenate([tb_ref[j] for j in range(n_planes)], axis=1)
    u_bf = u.astype(BF16)
    tiles_per_part = N_STATE // MXU_DIM
    ch_per_tile = SSM_WIDTH // tiles_per_part
    for n in range(2 * tiles_per_part):
        ch0 = ((n % tiles_per_part) * ch_per_tile) // LANES * LANES
        lanes = slice(n * MXU_DIM, (n + 1) * MXU_DIM)
        bu_ref[:, lanes] = jnp.dot(u_bf[:, ch0:ch0 + LANES], bblk_ref[ch0:ch0 + LANES, lanes],
                                   preferred_element_type=F32)

    for j in range(N_STATE // SCAN_LANES):
        re = slice(j * SCAN_LANES, (j + 1) * SCAN_LANES)
        im = slice(N_STATE + j * SCAN_LANES, N_STATE + (j + 1) * SCAN_LANES)
        ar = a_ref[:, re]
        ai = a_ref[:, im]

        def step(t, carry, re=re, im=im, ar=ar, ai=ai):
            xr, xi = carry
            rows = pl.ds(pl.multiple_of(t * SUBLANES, SUBLANES), SUBLANES)
            nr = ar * xr - ai * xi + bu_ref[rows, re]
            ni = ar * xi + ai * xr + bu_ref[rows, im]
            bu_ref[rows, re] = nr
            bu_ref[rows, im] = ni
            return nr, ni

        xr, xi = lax.fori_loop(0, chunk, step, (st_ref[:, re], st_ref[:, im]), unroll=4)
        st_ref[:, re] = xr
        st_ref[:, im] = xi

    n_out = SSM_WIDTH // MXU_DIM
    lanes_per_out = N_STATE // n_out
    ys = []
    for j in range(n_out):
        cols = slice(j * MXU_DIM, (j + 1) * MXU_DIM)
        acc = None
        for part in range(2):
            lanes = slice(part * N_STATE + j * lanes_per_out, part * N_STATE + (j + 1) * lanes_per_out)
            term = jnp.dot(bu_ref[:, lanes].astype(BF16), cblk_ref[lanes, cols], preferred_element_type=F32)
            acc = term if acc is None else acc + term
        ys.append(acc)
    y = jnp.concatenate(ys, axis=1) + d_ref[...] * u
    z = jax.nn.gelu(y)
    gate = _sigmoid(jnp.dot(z.astype(BF16), wglu_ref[...], preferred_element_type=F32) + bglu_ref[...])
    out = z * gate
    for j in range(n_planes):
        tb_ref[j] = out[:, j * LANES:(j + 1) * LANES]
    for b in range(n_batch):
        for j in range(n_planes):
            o_ref[b, :, j * LANES:(j + 1) * LANES] = tb_ref[j, pl.ds(b, chunk, stride=n_batch), :].astype(BF16)


def _ssm_params(lambda_re, lambda_im, log_dt, b_re, b_im, c_re, c_im):
    dt = jnp.exp(log_dt.astype(F32))[:, None]
    lr = jnp.minimum(lambda_re.astype(F32), -1e-4)
    li = lambda_im.astype(F32)
    mag = jnp.exp(lr * dt)
    abar_re = mag * jnp.cos(li * dt)
    abar_im = mag * jnp.sin(li * dt)
    den = lr * lr + li * li
    nr = abar_re - 1.0
    coef_re = (nr * lr + abar_im * li) / den
    coef_im = (abar_im * lr - nr * li) / den
    br = b_re.astype(F32)
    bi = b_im.astype(F32)
    bbar_re = coef_re[..., None] * br - coef_im[..., None] * bi
    bbar_im = coef_re[..., None] * bi + coef_im[..., None] * br
    ch_group = jnp.arange(SSM_WIDTH) // SSM_GROUP
    lane_group = jnp.arange(N_STATE) // SSM_STATE

    def expand_b(b):
        rows = b.transpose(0, 2, 1).reshape(SSM_WIDTH, SSM_STATE)
        return jnp.where(ch_group[:, None] == lane_group[None, :], jnp.tile(rows, (1, SSM_GROUPS)), 0.0)

    def expand_c(c):
        rows = c.transpose(0, 2, 1).reshape(N_STATE, SSM_GROUP)
        return jnp.where(lane_group[:, None] == ch_group[None, :], jnp.tile(rows, (1, SSM_GROUPS)), 0.0)

    bblk = jnp.concatenate([expand_b(bbar_re), expand_b(bbar_im)], axis=1).astype(BF16)
    cblk = jnp.concatenate([expand_c(c_re.astype(F32)), -expand_c(c_im.astype(F32))], axis=0).astype(BF16)
    a_row = jnp.concatenate([abar_re.reshape(-1), abar_im.reshape(-1)])[None, :]
    return bblk, cblk, jnp.broadcast_to(a_row, (SUBLANES, 2 * N_STATE))


def _ssm(u, bblk, cblk, a_tile, d_skip, w_glu_bf, b_glu):
    bsz, seq, _ = u.shape
    assert bsz == SUBLANES
    chunk = min(SSM_CHUNK, seq)
    rows = chunk * SUBLANES
    tok_spec = pl.BlockSpec((bsz, chunk, SSM_WIDTH), lambda c: (0, c, 0))
    return pl.pallas_call(
        _ssm_kernel,
        grid=(seq // chunk,),
        in_specs=[tok_spec, _full(bblk.shape), _full(a_tile.shape), _full(cblk.shape),
                  _full((1, SSM_WIDTH)), _full(w_glu_bf.shape), _full((1, SSM_WIDTH))],
        out_specs=tok_spec,
        out_shape=jax.ShapeDtypeStruct(u.shape, BF16),
        scratch_shapes=[pltpu.VMEM((rows, 2 * N_STATE), F32), pltpu.VMEM((SUBLANES, 2 * N_STATE), F32),
                        pltpu.VMEM((SSM_WIDTH // LANES, rows, LANES), F32)],
        compiler_params=_cparams(("arbitrary",)),
        name="ssm",
    )(u, bblk, a_tile, cblk, d_skip, w_glu_bf, b_glu)


def _attn_kernel(lam_ref, sg_ref, q_ref, k_ref, vt_ref, o_ref, acc_ref, *, out_scale, blk, nh):
    seq = q_ref.shape[1]
    hw = 2 * HEAD_DIM
    lane = lax.broadcasted_iota(jnp.int32, (blk, hw), 1)
    key_i = lax.broadcasted_iota(jnp.int32, (blk, blk), 0)
    qry_i = lax.broadcasted_iota(jnp.int32, (blk, blk), 1)
    keep = key_i <= qry_i
    contract_last = (((1,), (1,)), ((), ()))
    lam = lam_ref[0]
    n_chain = 2 * nh

    def q_block(qi, _):
        qrows = pl.ds(pl.multiple_of(qi * blk, blk), blk)
        qs = []
        for hh in range(nh):
            q = q_ref[0, qrows, hh * hw:(hh + 1) * hw]
            zero = jnp.zeros_like(q)
            qs += [jnp.where(lane < HEAD_DIM, q, zero), jnp.where(lane >= HEAD_DIM, q, zero)]
        acc_ref[...] = jnp.zeros_like(acc_ref)

        def kv_block(kb, carry, masked):
            krows = pl.ds(pl.multiple_of(kb * blk, blk), blk)
            scores = []
            for c in range(n_chain):
                hh = c // 2
                k = k_ref[0, krows, hh * hw:(hh + 1) * hw]
                scores.append(lax.dot_general(k, qs[c], contract_last, preferred_element_type=F32))
            out, probs, alphas = [], [], []
            for c in range(n_chain):
                m, l = carry[2 * c], carry[2 * c + 1]
                s = jnp.where(keep, scores[c], _NEG) if masked else scores[c]
                m_new = jnp.maximum(m, jnp.max(s, axis=0, keepdims=True))
                p = jnp.exp2(s - m_new)
                alpha = jnp.exp2(m - m_new)
                out += [m_new, alpha * l + jnp.sum(p, axis=0, keepdims=True)]
                probs.append(p.astype(BF16))
                alphas.append(alpha)
            for c in range(n_chain):
                hh = c // 2
                vt = vt_ref[hh * hw:(hh + 1) * hw, krows]
                acc_ref[c] = alphas[c] * acc_ref[c] + jnp.dot(vt, probs[c], preferred_element_type=F32)
            return tuple(out)

        carry = (jnp.full((1, blk), _NEG, F32), jnp.zeros((1, blk), F32)) * n_chain
        carry = lax.fori_loop(0, qi, lambda kb, c: kv_block(kb, c, False), carry)
        carry = kv_block(qi, carry, True)
        for hh in range(nh):
            l1, l2 = carry[4 * hh + 1], carry[4 * hh + 3]
            ot = acc_ref[2 * hh] * (1.0 / l1) - acc_ref[2 * hh + 1] * (lam / l2)
            ot = ot * lax.rsqrt(jnp.mean(ot * ot, axis=0, keepdims=True) + NORM_EPS)
            o_ref[0, qrows, hh * hw:(hh + 1) * hw] = (ot.T * sg_ref[...] * out_scale).astype(BF16)
        return 0

    lax.fori_loop(0, seq // blk, q_block, 0)


def _attention(q, k, vt, lam, subln_gain, lambda_init, bsz, seq):
    blk = min(ATTN_BLOCK, seq)
    nh = ATTN_HEADS_PER_STEP
    hw = 2 * HEAD_DIM
    q3, k3 = (a.reshape(bsz, seq, D_MODEL) for a in (q, k))
    tok_spec = pl.BlockSpec((1, seq, nh * hw), lambda b, h: (b, 0, h))
    out = pl.pallas_call(
        functools.partial(_attn_kernel, out_scale=1.0 - lambda_init, blk=blk, nh=nh),
        grid=(bsz, HEADS // nh),
        in_specs=[pl.BlockSpec(memory_space=pltpu.SMEM), _full((1, hw)), tok_spec, tok_spec,
                  pl.BlockSpec((nh * hw, seq), lambda b, h: (h, b))],
        out_specs=tok_spec,
        out_shape=jax.ShapeDtypeStruct((bsz, seq, D_MODEL), BF16),
        scratch_shapes=[pltpu.VMEM((2 * nh, hw, blk), F32)],
        compiler_params=_cparams(("parallel", "parallel")),
        name="diff_attn",
    )(lam, subln_gain.astype(F32)[None, :], q3, k3, vt)
    return out.reshape(bsz * seq, D_MODEL)


def _pack_rows(y):
    bits = lax.bitcast_convert_type(y.astype(BF16).astype(F32), jnp.uint32)
    return (bits[:, :PACK_W] >> 16) | (bits[:, PACK_W:] & jnp.uint32(0xFFFF0000))


def _unpack_rows(w):
    lo = lax.bitcast_convert_type(w << 16, F32)
    hi = lax.bitcast_convert_type(w & jnp.uint32(0xFFFF0000), F32)
    return jnp.concatenate([lo, hi], axis=1)


def _merge_kernel(so_ref, ao_ref, gs_ref, ga_ref, x_ref, wps_ref, wpa_ref, wo_ref, g2_ref,
                  wrh_ref, wrl_ref, br_ref, tri_ref,
                  x1_ref, hlo_ref, hhi_ref, route_ref, gate_ref, cnt_ref, run_ref, *, region):
    @pl.when(pl.program_id(0) == 0)
    def _():
        run_ref[...] = jnp.zeros_like(run_ref)

    tm = x_ref.shape[0]
    n_part = 2 if tm % (2 * MXU_DIM) == 0 else 1
    rows_per = tm // n_part
    lane = lax.broadcasted_iota(jnp.int32, (rows_per, LANES), 1)
    lane_f = lane.astype(F32)
    parts = [dict(rows=slice(p * rows_per, (p + 1) * rows_per)) for p in range(n_part)]

    def stage_proj(st):
        def matmul():
            return (jnp.dot(so_ref[st["rows"], :], wps_ref[...], preferred_element_type=F32),
                    jnp.dot(ao_ref[st["rows"], :], wpa_ref[...], preferred_element_type=F32))

        def epilogue(r):
            ps, pa = r
            merged = gs_ref[st["rows"], :].astype(F32) * ps + ga_ref[st["rows"], :].astype(F32) * pa
            st["merged"] = merged.astype(BF16)
        return matmul, epilogue

    def stage_out(st):
        def matmul():
            return jnp.dot(st["merged"], wo_ref[...], preferred_element_type=F32)

        def epilogue(r):
            x1 = x_ref[st["rows"], :] + r
            x1_ref[st["rows"], :] = x1
            h2 = x1 * lax.rsqrt(jnp.mean(x1 * x1, axis=-1, keepdims=True) + NORM_EPS) * g2_ref[...]
            words = _pack_rows(h2)
            hlo_ref[st["rows"], :] = words[:, :PACK_HALF]
            hhi_ref[st["rows"], :] = words[:, PACK_HALF:]
            st["h_hi"] = h2.astype(BF16)
            st["h_lo"] = (h2 - st["h_hi"].astype(F32)).astype(BF16)
        return matmul, epilogue

    def stage_router(st):
        def matmul():
            return (jnp.dot(st["h_hi"], wrh_ref[...], preferred_element_type=F32)
                    + jnp.dot(st["h_lo"], wrh_ref[...], preferred_element_type=F32)
                    + jnp.dot(st["h_hi"], wrl_ref[...], preferred_element_type=F32))

        def epilogue(r):
            work = r + br_ref[...]
            onehots, vals, ids = [], [], []
            for _ in range(TOP_K):
                m = jnp.max(work, axis=-1, keepdims=True)
                idx = jnp.min(jnp.where(work == m, lane_f, float(LANES)), axis=-1, keepdims=True)
                oh = lane_f == idx
                onehots.append(oh)
                vals.append(m)
                ids.append(idx.astype(jnp.int32))
                work = jnp.where(oh, -jnp.inf, work)
            exps = [jnp.exp(v - vals[0]) for v in vals]
            den = exps[0] + exps[1] + exps[2] + exps[3]
            st.update(onehots=onehots, ids=ids, gates=[e / den for e in exps],
                      multi=(onehots[0] | onehots[1] | onehots[2] | onehots[3]).astype(F32))
        return matmul, epilogue

    _run_lookahead([stage(st) for stage in (stage_proj, stage_out, stage_router) for st in parts])

    multi = jnp.concatenate([st["multi"] for st in parts], axis=0)
    before = jnp.dot(tri_ref[...], multi.astype(BF16), preferred_element_type=F32) + run_ref[...]
    for st in parts:
        route = jnp.zeros((rows_per, LANES), jnp.int32)
        gates = jnp.zeros((rows_per, LANES), F32)
        for kk in range(TOP_K):
            rank = jnp.sum(jnp.where(st["onehots"][kk], before[st["rows"], :], 0.0), axis=-1, keepdims=True)
            route = jnp.where(lane == kk, st["ids"][kk] * region + rank.astype(jnp.int32), route)
            gates = jnp.where(lane == kk, st["gates"][kk], gates)
        route_ref[st["rows"], :] = route
        gate_ref[st["rows"], :] = gates
    run = run_ref[...] + jnp.sum(multi, axis=0, keepdims=True)
    run_ref[...] = run
    cnt_ref[...] = run.astype(jnp.int32)


def _merge_route(so, ao, gs, ga, x2d, wps, wpa, wo, gain2, w_router, b_router, region):
    n_tok = x2d.shape[0]
    tm = min(ROW_TILE, n_tok)
    wr = jnp.zeros((D_MODEL, LANES), F32).at[:, :N_EXPERTS].set(w_router.astype(F32))
    wr_hi = wr.astype(BF16)
    wr_lo = (wr - wr_hi.astype(F32)).astype(BF16)
    br = jnp.full((1, LANES), -jnp.inf, F32).at[0, :N_EXPERTS].set(b_router.astype(F32))
    tri = (jnp.arange(tm)[:, None] > jnp.arange(tm)[None, :]).astype(BF16)
    row = lambda w: pl.BlockSpec((tm, w), lambda i: (i, 0))
    out_shapes = [jax.ShapeDtypeStruct((n_tok, D_MODEL), F32),
                  jax.ShapeDtypeStruct((n_tok, PACK_HALF), jnp.uint32),
                  jax.ShapeDtypeStruct((n_tok, PACK_HALF), jnp.uint32),
                  jax.ShapeDtypeStruct((n_tok, LANES), jnp.int32),
                  jax.ShapeDtypeStruct((n_tok, LANES), F32),
                  jax.ShapeDtypeStruct((1, LANES), jnp.int32)]
    return pl.pallas_call(
        functools.partial(_merge_kernel, region=region),
        grid=(n_tok // tm,),
        in_specs=[row(SSM_WIDTH), row(D_MODEL), row(D_MODEL), row(D_MODEL), row(D_MODEL),
                  _full(wps.shape), _full(wpa.shape), _full(wo.shape), _full((1, D_MODEL)),
                  _full(wr_hi.shape), _full(wr_lo.shape), _full((1, LANES)), _full((tm, tm))],
        out_specs=[row(D_MODEL), row(PACK_HALF), row(PACK_HALF), row(LANES), row(LANES), _full((1, LANES))],
        out_shape=out_shapes,
        scratch_shapes=[pltpu.VMEM((1, LANES), F32)],
        compiler_params=_cparams(("arbitrary",)),
        name="merge_route",
    )(so, ao, gs, ga, x2d, wps, wpa, wo, gain2, wr_hi, wr_lo, br, tri)


def _sc_scatter_rows(rows, dest, n_slots):
    n_tok, width = rows.shape
    mesh = plsc.VectorSubcoreMesh(core_axis_name="core", subcore_axis_name="subcore")

    @pl.kernel(out_type=jax.ShapeDtypeStruct((n_slots, width), rows.dtype), mesh=mesh, scratch_types=[])
    def scatter(rows_hbm, dest_hbm, out_hbm):
        def body(rows_vmem, dest_vmem):
            pltpu.sync_copy(rows_vmem, out_hbm.at[dest_vmem.at[0]])

        pltpu.emit_pipeline(
            body,
            grid=(TOP_K, n_tok // SC_WINDOW),
            in_specs=[pl.BlockSpec((SC_WINDOW, width), lambda k, i: (i, 0)),
                      pl.BlockSpec((1, SC_WINDOW), lambda k, i: (k, i))],
            out_specs=[],
            core_axis_name=("core", "subcore"),
            dimension_semantics=(pltpu.PARALLEL, pltpu.PARALLEL),
        )(rows_hbm, dest_hbm)

    return scatter(rows, dest)


def _sc_gather_rows(table, idx):
    n = idx.shape[1]
    width = table.shape[1]
    mesh = plsc.VectorSubcoreMesh(core_axis_name="core", subcore_axis_name="subcore")

    @pl.kernel(out_type=jax.ShapeDtypeStruct((n, width), table.dtype), mesh=mesh, scratch_types=[])
    def gather(table_hbm, idx_hbm, out_hbm):
        def body(idx_vmem, out_vmem):
            pltpu.sync_copy(table_hbm.at[idx_vmem.at[0]], out_vmem)

        pltpu.emit_pipeline(
            body,
            grid=(n // SC_WINDOW,),
            in_specs=[pl.BlockSpec((1, SC_WINDOW), lambda i: (0, i))],
            out_specs=[pl.BlockSpec((SC_WINDOW, width), lambda i: (i, 0))],
            core_axis_name=("core", "subcore"),
            dimension_semantics=(pltpu.PARALLEL,),
        )(idx_hbm, out_hbm)

    return gather(table, idx)


def _expert_kernel(cnt_ref, xlo_hbm, xhi_hbm, w1_ref, b1_ref, w2_ref, b2_ref, perm_ref, ylo_hbm, yhi_hbm,
                   w1p_ref, w2b_ref, x_ref, act_ref, xin_ref, yout_ref, in_sem, out_sem, *, region):
    e = pl.program_id(0)
    n_blk = (cnt_ref[e] + EXPERT_BLOCK - 1) // EXPERT_BLOCK
    base = e * region
    n_chunks = (2 * D_FF) // MXU_DIM
    x_hbm = (xlo_hbm, xhi_hbm)
    y_hbm = (ylo_hbm, yhi_hbm)

    def rows_of(j):
        return pl.ds(pl.multiple_of(base + j * EXPERT_BLOCK, EXPERT_BLOCK), EXPERT_BLOCK)

    def in_copy(j, slot, half):
        return pltpu.make_async_copy(x_hbm[half].at[rows_of(j), :], xin_ref.at[slot, half], in_sem.at[slot, half])

    def out_copy(j, slot, half):
        return pltpu.make_async_copy(yout_ref.at[slot, half], y_hbm[half].at[rows_of(j), :], out_sem.at[slot, half])

    @pl.when(n_blk > 0)
    def _():
        for half in range(2):
            in_copy(0, 0, half).start(priority=ROW_DMA_PRIORITY)

    perm = perm_ref[...]
    for c in range(n_chunks):
        cols = slice(c * MXU_DIM, (c + 1) * MXU_DIM)
        w1p_ref[c] = jnp.dot(w1_ref[0, :, cols].astype(BF16), perm, preferred_element_type=F32).astype(BF16)
    for c in range(D_MODEL // MXU_DIM):
        w2b_ref[c] = w2_ref[0, :, c * MXU_DIM:(c + 1) * MXU_DIM].astype(BF16)

    def block(j, _):
        slot = j % 2
        for half in range(2):
            in_copy(j, slot, half).wait()

        @pl.when(j + 1 < n_blk)
        def _():
            for half in range(2):
                in_copy(j + 1, 1 - slot, half).start(priority=ROW_DMA_PRIORITY)

        @pl.when(j >= 2)
        def _():
            for half in range(2):
                out_copy(j - 2, slot, half).wait()

        words = jnp.concatenate([xin_ref[slot, 0], xin_ref[slot, 1]], axis=1)
        x_ref[...] = _unpack_rows(words).astype(BF16)

        def up_task(c):
            cols = slice(c * MXU_DIM, (c + 1) * MXU_DIM)

            def epilogue(r):
                h = r + b1_ref[0, :, cols]
                gate = jnp.minimum(h[:, :LANES], SWIGLU_LIMIT)
                up = jnp.clip(h[:, LANES:], -SWIGLU_LIMIT, SWIGLU_LIMIT)
                glu = gate * _sigmoid(SWIGLU_ALPHA * gate)
                act_ref[:, c * LANES:(c + 1) * LANES] = ((up + 1.0) * glu).astype(BF16)
            return (lambda: jnp.dot(x_ref[...], w1p_ref[c], preferred_element_type=F32)), epilogue

        _run_lookahead([up_task(c) for c in range(n_chunks)])

        def down_task(c):
            cols = slice(c * MXU_DIM, (c + 1) * MXU_DIM)
            half, high = c % 2, c // 2

            def epilogue(r):
                bits = lax.bitcast_convert_type((r + b2_ref[0, :, cols]).astype(BF16).astype(F32), jnp.uint32)
                if high:
                    yout_ref[slot, half] = yout_ref[slot, half] | (bits & jnp.uint32(0xFFFF0000))
                else:
                    yout_ref[slot, half] = bits >> 16
            return (lambda: jnp.dot(act_ref[...], w2b_ref[c], preferred_element_type=F32)), epilogue

        _run_lookahead([down_task(c) for c in range(D_MODEL // MXU_DIM)])
        for half in range(2):
            out_copy(j, slot, half).start(priority=ROW_DMA_PRIORITY)
        return 0

    lax.fori_loop(0, n_blk, block, 0)

    for back in (2, 1):
        @pl.when(n_blk >= back)
        def _(back=back):
            j = n_blk - back
            for half in range(2):
                out_copy(j, j % 2, half).wait()


def _gate_up_order():
    j = jnp.arange(MXU_DIM)
    within = jnp.where(j < LANES, 2 * j, 2 * (j - LANES) + 1)
    return within


def _experts(xs_lo, xs_hi, w1, b1, w2, b2, counts, region):
    within = _gate_up_order()
    perm = (jnp.arange(MXU_DIM)[:, None] == within[None, :]).astype(BF16)
    order = (jnp.arange(0, 2 * D_FF, MXU_DIM)[:, None] + within[None, :]).reshape(-1)
    b1p = b1.astype(F32)[:, order][:, None, :]
    b2r = b2.astype(F32)[:, None, :]
    tb = EXPERT_BLOCK
    hbm = pl.BlockSpec(memory_space=pl.ANY)
    wspec = lambda shape: pl.BlockSpec((1,) + shape, lambda e, cnt: (e, 0, 0))
    grid_spec = pltpu.PrefetchScalarGridSpec(
        num_scalar_prefetch=1,
        grid=(N_EXPERTS,),
        in_specs=[hbm, hbm, wspec((D_MODEL, 2 * D_FF)), wspec((1, 2 * D_FF)),
                  wspec((D_FF, D_MODEL)), wspec((1, D_MODEL)),
                  pl.BlockSpec((MXU_DIM, MXU_DIM), lambda e, cnt: (0, 0))],
        out_specs=[hbm, hbm],
        scratch_shapes=[pltpu.VMEM((2 * D_FF // MXU_DIM, D_MODEL, MXU_DIM), BF16),
                        pltpu.VMEM((D_MODEL // MXU_DIM, D_FF, MXU_DIM), BF16),
                        pltpu.VMEM((tb, D_MODEL), BF16), pltpu.VMEM((tb, D_FF), BF16),
                        pltpu.VMEM((2, 2, tb, PACK_HALF), jnp.uint32), pltpu.VMEM((2, 2, tb, PACK_HALF), jnp.uint32),
                        pltpu.SemaphoreType.DMA((2, 2)), pltpu.SemaphoreType.DMA((2, 2))],
    )
    return pl.pallas_call(
        functools.partial(_expert_kernel, region=region),
        grid_spec=grid_spec,
        out_shape=[jax.ShapeDtypeStruct(xs_lo.shape, jnp.uint32)] * 2,
        compiler_params=_cparams(("arbitrary",)),
        name="experts",
    )(counts, xs_lo, xs_hi, w1, b1p, w2, b2r, perm)


def _combine_kernel(x1_ref, gate_ref, *refs):
    lo_refs, hi_refs, o_ref = refs[:TOP_K], refs[TOP_K:2 * TOP_K], refs[2 * TOP_K]
    acc = x1_ref[...]
    gates = gate_ref[...]
    for kk in range(TOP_K):
        words = jnp.concatenate([lo_refs[kk][...], hi_refs[kk][...]], axis=1)
        acc = acc + gates[:, kk:kk + 1] * _unpack_rows(words)
    o_ref[...] = acc


def _combine(x1, gates, yg_lo, yg_hi):
    n_tok = x1.shape[0]
    tm = min(ROW_TILE, n_tok)
    nblk = n_tok // tm
    row = lambda w: pl.BlockSpec((tm, w), lambda i: (i, 0))
    plane = lambda kk: pl.BlockSpec((tm, PACK_HALF), lambda i, kk=kk: (kk * nblk + i, 0))
    planes = [plane(kk) for kk in range(TOP_K)]
    return pl.pallas_call(
        _combine_kernel,
        grid=(nblk,),
        in_specs=[row(D_MODEL), row(LANES)] + planes + planes,
        out_specs=row(D_MODEL),
        out_shape=jax.ShapeDtypeStruct((n_tok, D_MODEL), F32),
        compiler_params=_cparams(("parallel",)),
        name="combine",
    )(x1, gates, *([yg_lo] * TOP_K), *([yg_hi] * TOP_K))


def kernel(x, norm1_gain, w_in, lambda_re, lambda_im, log_dt, ssm_b_re, ssm_b_im, ssm_c_re, ssm_c_im, ssm_d, w_glu, b_glu, q_norm_gain, k_norm_gain, lambda_q1, lambda_k1, lambda_q2, lambda_k2, subln_gain, w_proj_ssm, w_proj_attn, w_out, norm2_gain, w_router, b_router, w_exp1, b_exp1, w_exp2, b_exp2):
    bsz, seq, d = x.shape
    n_tok = bsz * seq
    depth = norm1_gain.shape[0]
    row1 = lambda a: a.astype(F32).reshape(1, -1)
    for l in range(depth):
        lambda_init = 0.8 - 0.6 * math.exp(-0.3 * l)
        x2d = x.reshape(n_tok, d)

        u, q, k, vt, gs, ga = _in_proj(x2d, row1(norm1_gain[l]), w_in[l], q_norm_gain[l], k_norm_gain[l])

        bblk, cblk, a_tile = _ssm_params(lambda_re[l], lambda_im[l], log_dt[l], ssm_b_re[l], ssm_b_im[l],
                                         ssm_c_re[l], ssm_c_im[l])
        so = _ssm(u.reshape(bsz, seq, SSM_WIDTH), bblk, cblk, a_tile, row1(ssm_d[l]),
                  w_glu[l].astype(BF16), row1(b_glu[l])).reshape(n_tok, SSM_WIDTH)

        lam = (jnp.exp(jnp.sum(lambda_q1[l].astype(F32) * lambda_k1[l].astype(F32)))
               - jnp.exp(jnp.sum(lambda_q2[l].astype(F32) * lambda_k2[l].astype(F32)))
               + lambda_init).reshape(1)
        ao = _attention(q, k, vt, lam, subln_gain[l], lambda_init, bsz, seq)

        region = n_tok
        x1, h_lo, h_hi, route, gates, counts = _merge_route(
            so, ao, gs, ga, x2d, w_proj_ssm[l].astype(BF16), w_proj_attn[l].astype(BF16),
            w_out[l].astype(BF16), row1(norm2_gain[l]), w_router[l], b_router[l], region)

        dest = route[:, :TOP_K].T
        n_slots = N_EXPERTS * region
        xs_lo = _sc_scatter_rows(h_lo, dest, n_slots)
        xs_hi = _sc_scatter_rows(h_hi, dest, n_slots)

        ys_lo, ys_hi = _experts(xs_lo, xs_hi, w_exp1[l], b_exp1[l], w_exp2[l], b_exp2[l],
                                counts[0, :N_EXPERTS], region)

        flat = dest.reshape(1, TOP_K * n_tok)
        yg_lo = _sc_gather_rows(ys_lo, flat)
        yg_hi = _sc_gather_rows(ys_hi, flat)
        x = _combine(x1, gates, yg_lo, yg_hi).reshape(bsz, seq, d)
    return x
```

```python
import functools
import math

import jax
import jax.numpy as jnp
from jax import lax
from jax.experimental import pallas as pl
from jax.experimental.pallas import tpu as pltpu
from jax.experimental.pallas import tpu_sc as plsc

F32 = jnp.float32
BF16 = jnp.bfloat16

D_MODEL = 1024
NORM_EPS = 1e-5
SSM_WIDTH = 512
SSM_GROUP = 16
SSM_GROUPS = 32
SSM_STATE = 64
N_STATE = SSM_GROUPS * SSM_STATE
HEADS = 8
HEAD_DIM = 64
N_EXPERTS = 32
TOP_K = 4
D_FF = 1024
SWIGLU_ALPHA = 1.702
SWIGLU_LIMIT = 7.0

LANES = 128
SUBLANES = 8
MXU_DIM = 256
VMEM_LIMIT = 56 * 1024 * 1024

ROW_TILE = 512
SSM_CHUNK = 128
SCAN_LANES = 512
ATTN_BLOCK = 256
ATTN_HEADS_PER_STEP = 8
ATTN_LOOKAHEAD = 16
ROWSUM_ROWS = 16
EXPERT_BLOCK = 512
ROW_DMA_PRIORITY = 1
SC_WINDOW = 128
PACK_W = D_MODEL // 2
PACK_HALF = PACK_W // 2

_NEG = -1e30
Q_SCALE = math.log2(math.e) / math.sqrt(HEAD_DIM)


def _cparams(sem):
    return pltpu.CompilerParams(dimension_semantics=sem, vmem_limit_bytes=VMEM_LIMIT)


def _full(shape):
    nd = len(shape)
    return pl.BlockSpec(shape, lambda *_: (0,) * nd)


def _sigmoid(x):
    return 0.5 * jnp.tanh(0.5 * x) + 0.5


def _run_lookahead(tasks):
    pending = tasks[0][0]()
    for i, (_, epilogue) in enumerate(tasks):
        result = pending
        if i + 1 < len(tasks):
            pending = tasks[i + 1][0]()
        epilogue(result)


_IN_CHUNKS = {"u": (0, SSM_WIDTH // MXU_DIM)}
for _name in ("q", "k", "v", "gs", "ga"):
    _start = max(first + count for first, count in _IN_CHUNKS.values())
    _IN_CHUNKS[_name] = (_start, D_MODEL // MXU_DIM)
N_IN_CHUNKS = max(first + count for first, count in _IN_CHUNKS.values())


def _prep_w_in_kernel(w_ref, wc_ref, wvt_ref):
    c = pl.program_id(0)
    w = w_ref[...]
    wc_ref[0] = w.astype(BF16)
    v_first, v_count = _IN_CHUNKS["v"]

    @pl.when((c >= v_first) & (c < v_first + v_count))
    def _():
        wvt_ref[0] = w.T.astype(BF16)


def _prep_w_in(w_in):
    v_first, v_count = _IN_CHUNKS["v"]
    return pl.pallas_call(
        _prep_w_in_kernel,
        grid=(N_IN_CHUNKS,),
        in_specs=[pl.BlockSpec((D_MODEL, MXU_DIM), lambda c: (0, c))],
        out_specs=[pl.BlockSpec((1, D_MODEL, MXU_DIM), lambda c: (c, 0, 0)),
                   pl.BlockSpec((1, MXU_DIM, D_MODEL), lambda c: (jnp.clip(c - v_first, 0, v_count - 1), 0, 0))],
        out_shape=[jax.ShapeDtypeStruct((N_IN_CHUNKS, D_MODEL, MXU_DIM), BF16),
                   jax.ShapeDtypeStruct((v_count, MXU_DIM, D_MODEL), BF16)],
        compiler_params=_cparams(("arbitrary",)),
        name="prep_w_in",
    )(w_in)


def _inproj_kernel(x_ref, g1_ref, w_ref, wvt_ref, qg_ref, kg_ref, seg_ref,
                   u_ref, q_ref, k_ref, vt_ref, gs_ref, ga_ref):
    x = x_ref[...]
    ms = jnp.mean(x * x, axis=-1, keepdims=True)
    h = (x * lax.rsqrt(ms + NORM_EPS) * g1_ref[...]).astype(BF16)

    def proj(name, c):
        chunk = _IN_CHUNKS[name][0] + c
        return lambda: jnp.dot(h, w_ref[chunk], preferred_element_type=F32)

    seg = seg_ref[...]
    tasks = []

    def plain_task(name, out_ref, c, fn):
        cols = slice(c * MXU_DIM, (c + 1) * MXU_DIM)

        def epilogue(r):
            out_ref[:, cols] = fn(r)
        return proj(name, c), epilogue

    tasks += [plain_task("u", u_ref, c, lambda r: r) for c in range(_IN_CHUNKS["u"][1])]

    def head_norm_tasks(name, gain_ref, out_ref, scale, c):
        cols = slice(c * MXU_DIM, (c + 1) * MXU_DIM)
        kept = {}

        def after_proj(y):
            kept["y"] = y
            kept["sq"] = (y * y).astype(BF16)

        def after_sum(ss):
            yn = kept["y"] * lax.rsqrt(ss * (1.0 / HEAD_DIM) + NORM_EPS) * gain_ref[:, cols]
            out_ref[:, cols] = (yn * scale).astype(BF16)

        return ((proj(name, c), after_proj),
                (lambda: jnp.dot(kept["sq"], seg, preferred_element_type=F32), after_sum))

    pairs = [head_norm_tasks(name, gain_ref, out_ref, scale, c)
             for name, gain_ref, out_ref, scale in (("q", qg_ref, q_ref, Q_SCALE), ("k", kg_ref, k_ref, 1.0))
             for c in range(_IN_CHUNKS[name][1])]
    tasks.append(pairs[0][0])
    for prev, cur in zip(pairs, pairs[1:]):
        tasks += [cur[0], prev[1]]
    tasks.append(pairs[-1][1])

    to_gate = lambda r: _sigmoid(r).astype(BF16)
    for c in range(_IN_CHUNKS["gs"][1]):
        tasks.append(plain_task("gs", gs_ref, c, to_gate))
        tasks.append(plain_task("ga", ga_ref, c, to_gate))

    def vt_task(c):
        def matmul():
            return lax.dot_general(wvt_ref[c], h, (((1,), (1,)), ((), ())), preferred_element_type=F32)

        def epilogue(r):
            vt_ref[c * MXU_DIM:(c + 1) * MXU_DIM, :] = r.astype(BF16)
        return matmul, epilogue

    tasks += [vt_task(c) for c in range(_IN_CHUNKS["v"][1])]
    _run_lookahead(tasks)


def _in_proj(x2d, gain1, w_in, q_gain, k_gain):
    n_tok = x2d.shape[0]
    tm = min(ROW_TILE, n_tok)
    w_chunks, w_vt = _prep_w_in(w_in)
    seg = (jnp.arange(MXU_DIM)[:, None] // HEAD_DIM == jnp.arange(MXU_DIM)[None, :] // HEAD_DIM).astype(BF16)
    reps = D_MODEL // HEAD_DIM
    qg = jnp.tile(q_gain.astype(F32), reps)[None, :]
    kg = jnp.tile(k_gain.astype(F32), reps)[None, :]
    row = lambda w: pl.BlockSpec((tm, w), lambda i: (i, 0))
    tok = jax.ShapeDtypeStruct((n_tok, D_MODEL), BF16)
    out_shapes = [jax.ShapeDtypeStruct((n_tok, SSM_WIDTH), F32), tok, tok,
                  jax.ShapeDtypeStruct((D_MODEL, n_tok), BF16), tok, tok]
    vt_spec = pl.BlockSpec((D_MODEL, tm), lambda i: (0, i))
    return pl.pallas_call(
        _inproj_kernel,
        grid=(n_tok // tm,),
        in_specs=[row(D_MODEL), _full((1, D_MODEL)), _full(w_chunks.shape), _full(w_vt.shape),
                  _full((1, D_MODEL)), _full((1, D_MODEL)), _full((MXU_DIM, MXU_DIM))],
        out_specs=[row(SSM_WIDTH), row(D_MODEL), row(D_MODEL), vt_spec, row(D_MODEL), row(D_MODEL)],
        out_shape=out_shapes,
        compiler_params=_cparams(("parallel",)),
        name="in_proj",
    )(x2d, gain1, w_chunks, w_vt, qg, kg, seg)


def _ssm_kernel(u_ref, bblk_ref, a_ref, cblk_ref, d_ref, wglu_ref, bglu_ref, o_ref, bu_ref, st_ref, tb_ref):
    n_batch, chunk = u_ref.shape[0], u_ref.shape[1]

    @pl.when(pl.program_id(0) == 0)
    def _():
        st_ref[...] = jnp.zeros_like(st_ref)

    n_planes = SSM_WIDTH // LANES
    for b in range(n_batch):
        for j in range(n_planes):
            tb_ref[j, pl.ds(b, chunk, stride=n_batch), :] = u_ref[b, :, j * LANES:(j + 1) * LANES]
    u = jnp.concatenate([tb_ref[j] for j in range(n_planes)], axis=1)
    u_bf = u.astype(BF16)
    tiles_per_part = N_STATE // MXU_DIM
    ch_per_tile = SSM_WIDTH // tiles_per_part
    for n in range(2 * tiles_per_part):
        ch0 = ((n % tiles_per_part) * ch_per_tile) // LANES * LANES
        lanes = slice(n * MXU_DIM, (n + 1) * MXU_DIM)
        bu_ref[:, lanes] = jnp.dot(u_bf[:, ch0:ch0 + LANES], bblk_ref[ch0:ch0 + LANES, lanes],
                                   preferred_element_type=F32)

    for j in range(N_STATE // SCAN_LANES):
        re = slice(j * SCAN_LANES, (j + 1) * SCAN_LANES)
        im = slice(N_STATE + j * SCAN_LANES, N_STATE + (j + 1) * SCAN_LANES)
        ar = a_ref[:, re]
        ai = a_ref[:, im]

        def step(t, carry, re=re, im=im, ar=ar, ai=ai):
            xr, xi = carry
            rows = pl.ds(pl.multiple_of(t * SUBLANES, SUBLANES), SUBLANES)
            nr = ar * xr - ai * xi + bu_ref[rows, re]
            ni = ar * xi + ai * xr + bu_ref[rows, im]
            bu_ref[rows, re] = nr
            bu_ref[rows, im] = ni
            return nr, ni

        xr, xi = lax.fori_loop(0, chunk, step, (st_ref[:, re], st_ref[:, im]), unroll=4)
        st_ref[:, re] = xr
        st_ref[:, im] = xi

    n_out = SSM_WIDTH // MXU_DIM
    lanes_per_out = N_STATE // n_out
    ys = []
    for j in range(n_out):
        cols = slice(j * MXU_DIM, (j + 1) * MXU_DIM)
        acc = None
        for part in range(2):
            lanes = slice(part * N_STATE + j * lanes_per_out, part * N_STATE + (j + 1) * lanes_per_out)
            term = jnp.dot(bu_ref[:, lanes].astype(BF16), cblk_ref[lanes, cols], preferred_element_type=F32)
            acc = term if acc is None else acc + term
        ys.append(acc)
    y = jnp.concatenate(ys, axis=1) + d_ref[...] * u
    z = jax.nn.gelu(y)
    gate = _sigmoid(jnp.dot(z.astype(BF16), wglu_ref[...], preferred_element_type=F32) + bglu_ref[...])
    out = z * gate
    for j in range(n_planes):
        tb_ref[j] = out[:, j * LANES:(j + 1) * LANES]
    for b in range(n_batch):
        for j in range(n_planes):
            o_ref[b, :, j * LANES:(j + 1) * LANES] = tb_ref[j, pl.ds(b, chunk, stride=n_batch), :].astype(BF16)


def _ssm_params(lambda_re, lambda_im, log_dt, b_re, b_im, c_re, c_im):
    dt = jnp.exp(log_dt.astype(F32))[:, None]
    lr = jnp.minimum(lambda_re.astype(F32), -1e-4)
    li = lambda_im.astype(F32)
    mag = jnp.exp(lr * dt)
    abar_re = mag * jnp.cos(li * dt)
    abar_im = mag * jnp.sin(li * dt)
    den = lr * lr + li * li
    nr = abar_re - 1.0
    coef_re = (nr * lr + abar_im * li) / den
    coef_im = (abar_im * lr - nr * li) / den
    br = b_re.astype(F32)
    bi = b_im.astype(F32)
    bbar_re = coef_re[..., None] * br - coef_im[..., None] * bi
    bbar_im = coef_re[..., None] * bi + coef_im[..., None] * br
    ch_group = jnp.arange(SSM_WIDTH) // SSM_GROUP
    lane_group = jnp.arange(N_STATE) // SSM_STATE

    def expand_b(b):
        rows = b.transpose(0, 2, 1).reshape(SSM_WIDTH, SSM_STATE)
        return jnp.where(ch_group[:, None] == lane_group[None, :], jnp.tile(rows, (1, SSM_GROUPS)), 0.0)

    def expand_c(c):
        rows = c.transpose(0, 2, 1).reshape(N_STATE, SSM_GROUP)
        return jnp.where(lane_group[:, None] == ch_group[None, :], jnp.tile(rows, (1, SSM_GROUPS)), 0.0)

    bblk = jnp.concatenate([expand_b(bbar_re), expand_b(bbar_im)], axis=1).astype(BF16)
    cblk = jnp.concatenate([expand_c(c_re.astype(F32)), -expand_c(c_im.astype(F32))], axis=0).astype(BF16)
    a_row = jnp.concatenate([abar_re.reshape(-1), abar_im.reshape(-1)])[None, :]
    return bblk, cblk, jnp.broadcast_to(a_row, (SUBLANES, 2 * N_STATE))


def _ssm(u, bblk, cblk, a_tile, d_skip, w_glu_bf, b_glu):
    bsz, seq, _ = u.shape
    assert bsz == SUBLANES
    chunk = min(SSM_CHUNK, seq)
    rows = chunk * SUBLANES
    tok_spec = pl.BlockSpec((bsz, chunk, SSM_WIDTH), lambda c: (0, c, 0))
    return pl.pallas_call(
        _ssm_kernel,
        grid=(seq // chunk,),
        in_specs=[tok_spec, _full(bblk.shape), _full(a_tile.shape), _full(cblk.shape),
                  _full((1, SSM_WIDTH)), _full(w_glu_bf.shape), _full((1, SSM_WIDTH))],
        out_specs=tok_spec,
        out_shape=jax.ShapeDtypeStruct(u.shape, BF16),
        scratch_shapes=[pltpu.VMEM((rows, 2 * N_STATE), F32), pltpu.VMEM((SUBLANES, 2 * N_STATE), F32),
                        pltpu.VMEM((SSM_WIDTH // LANES, rows, LANES), F32)],
        compiler_params=_cparams(("arbitrary",)),
        name="ssm",
    )(u, bblk, a_tile, cblk, d_skip, w_glu_bf, b_glu)


def _attn_kernel(lam_ref, sg_ref, q_ref, k_ref, vt_ref, o_ref, acc_ref, vta_ref, *, out_scale, blk, nh):
    seq = q_ref.shape[1]
    hw = 2 * HEAD_DIM
    lane = lax.broadcasted_iota(jnp.int32, (blk, hw), 1)
    key_i = lax.broadcasted_iota(jnp.int32, (blk, blk), 0)
    qry_i = lax.broadcasted_iota(jnp.int32, (blk, blk), 1)
    keep = key_i <= qry_i
    contract_last = (((1,), (1,)), ((), ()))
    lam = lam_ref[0]
    n_chain = 2 * nh

    for hh in range(nh):
        vta_ref[hh, :hw, :] = vt_ref[hh * hw:(hh + 1) * hw, :]
        vta_ref[hh, hw:, :] = jnp.ones((ROWSUM_ROWS, seq), BF16)

    def q_block(qi, _):
        qrows = pl.ds(pl.multiple_of(qi * blk, blk), blk)
        qs = []
        for hh in range(nh):
            q = q_ref[0, qrows, hh * hw:(hh + 1) * hw]
            zero = jnp.zeros_like(q)
            qs += [jnp.where(lane < HEAD_DIM, q, zero), jnp.where(lane >= HEAD_DIM, q, zero)]
        acc_ref[...] = jnp.zeros_like(acc_ref)

        def kv_block(kb, carry, masked):
            krows = pl.ds(pl.multiple_of(kb * blk, blk), blk)
            def score(c):
                hh = c // 2
                k = k_ref[0, krows, hh * hw:(hh + 1) * hw]
                return lax.dot_general(k, qs[c], contract_last, preferred_element_type=F32)

            scores = {c: score(c) for c in range(min(ATTN_LOOKAHEAD, n_chain))}
            out = []
            for c in range(n_chain):
                m = carry[c]
                s = scores.pop(c)
                if masked:
                    s = jnp.where(keep, s, _NEG)
                m_new = jnp.maximum(m, jnp.max(s, axis=0, keepdims=True))
                p = jnp.exp2(s - m_new).astype(BF16)
                alpha = jnp.exp2(m - m_new)
                out.append(m_new)
                if c + ATTN_LOOKAHEAD < n_chain:
                    scores[c + ATTN_LOOKAHEAD] = score(c + ATTN_LOOKAHEAD)
                vt = vta_ref[c // 2, :, krows]
                acc_ref[c] = alpha * acc_ref[c] + jnp.dot(vt, p, preferred_element_type=F32)
            return tuple(out)

        carry = (jnp.full((1, blk), _NEG, F32),) * n_chain
        carry = lax.fori_loop(0, qi, lambda kb, c: kv_block(kb, c, False), carry)
        kv_block(qi, carry, True)
        for hh in range(nh):
            a1, a2 = acc_ref[2 * hh], acc_ref[2 * hh + 1]
            l1, l2 = a1[hw:hw + 1, :], a2[hw:hw + 1, :]
            ot = a1[:hw, :] * (1.0 / l1) - a2[:hw, :] * (lam / l2)
            ot = ot * lax.rsqrt(jnp.mean(ot * ot, axis=0, keepdims=True) + NORM_EPS)
            o_ref[0, qrows, hh * hw:(hh + 1) * hw] = (ot.T * sg_ref[...] * out_scale).astype(BF16)
        return 0

    lax.fori_loop(0, seq // blk, q_block, 0)


def _attention(q, k, vt, lam, subln_gain, lambda_init, bsz, seq):
    blk = min(ATTN_BLOCK, seq)
    nh = ATTN_HEADS_PER_STEP
    hw = 2 * HEAD_DIM
    q3, k3 = (a.reshape(bsz, seq, D_MODEL) for a in (q, k))
    tok_spec = pl.BlockSpec((1, seq, nh * hw), lambda b, h: (b, 0, h))
    out = pl.pallas_call(
        functools.partial(_attn_kernel, out_scale=1.0 - lambda_init, blk=blk, nh=nh),
        grid=(bsz, HEADS // nh),
        in_specs=[pl.BlockSpec(memory_space=pltpu.SMEM), _full((1, hw)), tok_spec, tok_spec,
                  pl.BlockSpec((nh * hw, seq), lambda b, h: (h, b))],
        out_specs=tok_spec,
        out_shape=jax.ShapeDtypeStruct((bsz, seq, D_MODEL), BF16),
        scratch_shapes=[pltpu.VMEM((2 * nh, hw + ROWSUM_ROWS, blk), F32),
                        pltpu.VMEM((nh, hw + ROWSUM_ROWS, seq), BF16)],
        compiler_params=_cparams(("parallel", "parallel")),
        name="diff_attn",
    )(lam, subln_gain.astype(F32)[None, :], q3, k3, vt)
    return out.reshape(bsz * seq, D_MODEL)


def _pack_rows(y):
    bits = lax.bitcast_convert_type(y.astype(BF16).astype(F32), jnp.uint32)
    return (bits[:, :PACK_W] >> 16) | (bits[:, PACK_W:] & jnp.uint32(0xFFFF0000))


def _unpack_rows(w):
    lo = lax.bitcast_convert_type(w << 16, F32)
    hi = lax.bitcast_convert_type(w & jnp.uint32(0xFFFF0000), F32)
    return jnp.concatenate([lo, hi], axis=1)


def _merge_kernel(so_ref, ao_ref, gs_ref, ga_ref, x_ref, wps_ref, wpa_ref, wo_ref, g2_ref,
                  wrh_ref, wrl_ref, br_ref, tri_ref,
                  x1_ref, hlo_ref, hhi_ref, route_ref, gate_ref, cnt_ref, run_ref, *, region):
    @pl.when(pl.program_id(0) == 0)
    def _():
        run_ref[...] = jnp.zeros_like(run_ref)

    tm = x_ref.shape[0]
    n_part = 2 if tm % (2 * MXU_DIM) == 0 else 1
    rows_per = tm // n_part
    lane = lax.broadcasted_iota(jnp.int32, (rows_per, LANES), 1)
    lane_f = lane.astype(F32)
    parts = [dict(rows=slice(p * rows_per, (p + 1) * rows_per)) for p in range(n_part)]

    def stage_proj(st):
        def matmul():
            return (jnp.dot(so_ref[st["rows"], :], wps_ref[...], preferred_element_type=F32),
                    jnp.dot(ao_ref[st["rows"], :], wpa_ref[...], preferred_element_type=F32))

        def epilogue(r):
            ps, pa = r
            merged = gs_ref[st["rows"], :].astype(F32) * ps + ga_ref[st["rows"], :].astype(F32) * pa
            st["merged"] = merged.astype(BF16)
        return matmul, epilogue

    def stage_out(st):
        def matmul():
            return jnp.dot(st["merged"], wo_ref[...], preferred_element_type=F32)

        def epilogue(r):
            x1 = x_ref[st["rows"], :] + r
            x1_ref[st["rows"], :] = x1
            h2 = x1 * lax.rsqrt(jnp.mean(x1 * x1, axis=-1, keepdims=True) + NORM_EPS) * g2_ref[...]
            words = _pack_rows(h2)
            hlo_ref[st["rows"], :] = words[:, :PACK_HALF]
            hhi_ref[st["rows"], :] = words[:, PACK_HALF:]
            st["h_hi"] = h2.astype(BF16)
            st["h_lo"] = (h2 - st["h_hi"].astype(F32)).astype(BF16)
        return matmul, epilogue

    def stage_router(st):
        def matmul():
            return (jnp.dot(st["h_hi"], wrh_ref[...], preferred_element_type=F32)
                    + jnp.dot(st["h_lo"], wrh_ref[...], preferred_element_type=F32)
                    + jnp.dot(st["h_hi"], wrl_ref[...], preferred_element_type=F32))

        def epilogue(r):
            work = r + br_ref[...]
            onehots, vals, ids = [], [], []
            for _ in range(TOP_K):
                m = jnp.max(work, axis=-1, keepdims=True)
                idx = jnp.min(jnp.where(work == m, lane_f, float(LANES)), axis=-1, keepdims=True)
                oh = lane_f == idx
                onehots.append(oh)
                vals.append(m)
                ids.append(idx.astype(jnp.int32))
                work = jnp.where(oh, -jnp.inf, work)
            exps = [jnp.exp(v - vals[0]) for v in vals]
            den = exps[0] + exps[1] + exps[2] + exps[3]
            st.update(onehots=onehots, ids=ids, gates=[e / den for e in exps],
                      multi=(onehots[0] | onehots[1] | onehots[2] | onehots[3]).astype(F32))
        return matmul, epilogue

    _run_lookahead([stage(st) for stage in (stage_proj, stage_out, stage_router) for st in parts])

    multi = jnp.concatenate([st["multi"] for st in parts], axis=0)
    before = jnp.dot(tri_ref[...], multi.astype(BF16), preferred_element_type=F32) + run_ref[...]
    for st in parts:
        route = jnp.zeros((rows_per, LANES), jnp.int32)
        gates = jnp.zeros((rows_per, LANES), F32)
        for kk in range(TOP_K):
            rank = jnp.sum(jnp.where(st["onehots"][kk], before[st["rows"], :], 0.0), axis=-1, keepdims=True)
            route = jnp.where(lane == kk, st["ids"][kk] * region + rank.astype(jnp.int32), route)
            gates = jnp.where(lane == kk, st["gates"][kk], gates)
        route_ref[st["rows"], :] = route
        gate_ref[st["rows"], :] = gates
    run = run_ref[...] + jnp.sum(multi, axis=0, keepdims=True)
    run_ref[...] = run
    cnt_ref[...] = run.astype(jnp.int32)


def _merge_route(so, ao, gs, ga, x2d, wps, wpa, wo, gain2, w_router, b_router, region):
    n_tok = x2d.shape[0]
    tm = min(ROW_TILE, n_tok)
    wr = jnp.zeros((D_MODEL, LANES), F32).at[:, :N_EXPERTS].set(w_router.astype(F32))
    wr_hi = wr.astype(BF16)
    wr_lo = (wr - wr_hi.astype(F32)).astype(BF16)
    br = jnp.full((1, LANES), -jnp.inf, F32).at[0, :N_EXPERTS].set(b_router.astype(F32))
    tri = (jnp.arange(tm)[:, None] > jnp.arange(tm)[None, :]).astype(BF16)
    row = lambda w: pl.BlockSpec((tm, w), lambda i: (i, 0))
    out_shapes = [jax.ShapeDtypeStruct((n_tok, D_MODEL), F32),
                  jax.ShapeDtypeStruct((n_tok, PACK_HALF), jnp.uint32),
                  jax.ShapeDtypeStruct((n_tok, PACK_HALF), jnp.uint32),
                  jax.ShapeDtypeStruct((n_tok, LANES), jnp.int32),
                  jax.ShapeDtypeStruct((n_tok, LANES), F32),
                  jax.ShapeDtypeStruct((1, LANES), jnp.int32)]
    return pl.pallas_call(
        functools.partial(_merge_kernel, region=region),
        grid=(n_tok // tm,),
        in_specs=[row(SSM_WIDTH), row(D_MODEL), row(D_MODEL), row(D_MODEL), row(D_MODEL),
                  _full(wps.shape), _full(wpa.shape), _full(wo.shape), _full((1, D_MODEL)),
                  _full(wr_hi.shape), _full(wr_lo.shape), _full((1, LANES)), _full((tm, tm))],
        out_specs=[row(D_MODEL), row(PACK_HALF), row(PACK_HALF), row(LANES), row(LANES), _full((1, LANES))],
        out_shape=out_shapes,
        scratch_shapes=[pltpu.VMEM((1, LANES), F32)],
        compiler_params=_cparams(("arbitrary",)),
        name="merge_route",
    )(so, ao, gs, ga, x2d, wps, wpa, wo, gain2, wr_hi, wr_lo, br, tri)


def _sc_scatter_rows(rows, dest, n_slots):
    n_tok, width = rows.shape
    mesh = plsc.VectorSubcoreMesh(core_axis_name="core", subcore_axis_name="subcore")

    @pl.kernel(out_type=jax.ShapeDtypeStruct((n_slots, width), rows.dtype), mesh=mesh, scratch_types=[])
    def scatter(rows_hbm, dest_hbm, out_hbm):
        def body(rows_vmem, dest_vmem):
            pltpu.sync_copy(rows_vmem, out_hbm.at[dest_vmem.at[0]])

        pltpu.emit_pipeline(
            body,
            grid=(TOP_K, n_tok // SC_WINDOW),
            in_specs=[pl.BlockSpec((SC_WINDOW, width), lambda k, i: (i, 0)),
                      pl.BlockSpec((1, SC_WINDOW), lambda k, i: (k, i))],
            out_specs=[],
            core_axis_name=("core", "subcore"),
            dimension_semantics=(pltpu.PARALLEL, pltpu.PARALLEL),
        )(rows_hbm, dest_hbm)

    return scatter(rows, dest)


def _sc_gather_rows(table, idx):
    n = idx.shape[1]
    width = table.shape[1]
    mesh = plsc.VectorSubcoreMesh(core_axis_name="core", subcore_axis_name="subcore")

    @pl.kernel(out_type=jax.ShapeDtypeStruct((n, width), table.dtype), mesh=mesh, scratch_types=[])
    def gather(table_hbm, idx_hbm, out_hbm):
        def body(idx_vmem, out_vmem):
            pltpu.sync_copy(table_hbm.at[idx_vmem.at[0]], out_vmem)

        pltpu.emit_pipeline(
            body,
            grid=(n // SC_WINDOW,),
            in_specs=[pl.BlockSpec((1, SC_WINDOW), lambda i: (0, i))],
            out_specs=[pl.BlockSpec((SC_WINDOW, width), lambda i: (i, 0))],
            core_axis_name=("core", "subcore"),
            dimension_semantics=(pltpu.PARALLEL,),
        )(idx_hbm, out_hbm)

    return gather(table, idx)


def _expert_kernel(cnt_ref, xlo_hbm, xhi_hbm, w1_ref, b1_ref, w2_ref, b2_ref, perm_ref, ylo_hbm, yhi_hbm,
                   w1p_ref, w2b_ref, x_ref, act_ref, xin_ref, yout_ref, in_sem, out_sem, *, region):
    e = pl.program_id(0)
    n_blk = (cnt_ref[e] + EXPERT_BLOCK - 1) // EXPERT_BLOCK
    base = e * region
    n_chunks = (2 * D_FF) // MXU_DIM
    x_hbm = (xlo_hbm, xhi_hbm)
    y_hbm = (ylo_hbm, yhi_hbm)

    def rows_of(j):
        return pl.ds(pl.multiple_of(base + j * EXPERT_BLOCK, EXPERT_BLOCK), EXPERT_BLOCK)

    def in_copy(j, slot, half):
        return pltpu.make_async_copy(x_hbm[half].at[rows_of(j), :], xin_ref.at[slot, half], in_sem.at[slot, half])

    def out_copy(j, slot, half):
        return pltpu.make_async_copy(yout_ref.at[slot, half], y_hbm[half].at[rows_of(j), :], out_sem.at[slot, half])

    @pl.when(n_blk > 0)
    def _():
        for half in range(2):
            in_copy(0, 0, half).start(priority=ROW_DMA_PRIORITY)

    perm = perm_ref[...]
    for c in range(n_chunks):
        cols = slice(c * MXU_DIM, (c + 1) * MXU_DIM)
        w1p_ref[c] = jnp.dot(w1_ref[0, :, cols].astype(BF16), perm, preferred_element_type=F32).astype(BF16)
    for c in range(D_MODEL // MXU_DIM):
        w2b_ref[c] = w2_ref[0, :, c * MXU_DIM:(c + 1) * MXU_DIM].astype(BF16)

    def block(j, _):
        slot = j % 2
        for half in range(2):
            in_copy(j, slot, half).wait()

        @pl.when(j + 1 < n_blk)
        def _():
            for half in range(2):
                in_copy(j + 1, 1 - slot, half).start(priority=ROW_DMA_PRIORITY)

        @pl.when(j >= 2)
        def _():
            for half in range(2):
                out_copy(j - 2, slot, half).wait()

        words = jnp.concatenate([xin_ref[slot, 0], xin_ref[slot, 1]], axis=1)
        x_ref[...] = _unpack_rows(words).astype(BF16)

        def up_task(c):
            cols = slice(c * MXU_DIM, (c + 1) * MXU_DIM)

            def epilogue(r):
                h = r + b1_ref[0, :, cols]
                gate = jnp.minimum(h[:, :LANES], SWIGLU_LIMIT)
                up = jnp.clip(h[:, LANES:], -SWIGLU_LIMIT, SWIGLU_LIMIT)
                glu = gate * _sigmoid(SWIGLU_ALPHA * gate)
                act_ref[:, c * LANES:(c + 1) * LANES] = ((up + 1.0) * glu).astype(BF16)
            return (lambda: jnp.dot(x_ref[...], w1p_ref[c], preferred_element_type=F32)), epilogue

        _run_lookahead([up_task(c) for c in range(n_chunks)])

        def down_task(c):
            cols = slice(c * MXU_DIM, (c + 1) * MXU_DIM)
            half, high = c % 2, c // 2

            def epilogue(r):
                bits = lax.bitcast_convert_type((r + b2_ref[0, :, cols]).astype(BF16).astype(F32), jnp.uint32)
                if high:
                    yout_ref[slot, half] = yout_ref[slot, half] | (bits & jnp.uint32(0xFFFF0000))
                else:
                    yout_ref[slot, half] = bits >> 16
            return (lambda: jnp.dot(act_ref[...], w2b_ref[c], preferred_element_type=F32)), epilogue

        _run_lookahead([down_task(c) for c in range(D_MODEL // MXU_DIM)])
        for half in range(2):
            out_copy(j, slot, half).start(priority=ROW_DMA_PRIORITY)
        return 0

    lax.fori_loop(0, n_blk, block, 0)

    for back in (2, 1):
        @pl.when(n_blk >= back)
        def _(back=back):
            j = n_blk - back
            for half in range(2):
                out_copy(j, j % 2, half).wait()


def _gate_up_order():
    j = jnp.arange(MXU_DIM)
    within = jnp.where(j < LANES, 2 * j, 2 * (j - LANES) + 1)
    return within


def _experts(xs_lo, xs_hi, w1, b1, w2, b2, counts, region):
    within = _gate_up_order()
    perm = (jnp.arange(MXU_DIM)[:, None] == within[None, :]).astype(BF16)
    order = (jnp.arange(0, 2 * D_FF, MXU_DIM)[:, None] + within[None, :]).reshape(-1)
    b1p = b1.astype(F32)[:, order][:, None, :]
    b2r = b2.astype(F32)[:, None, :]
    tb = EXPERT_BLOCK
    hbm = pl.BlockSpec(memory_space=pl.ANY)
    wspec = lambda shape: pl.BlockSpec((1,) + shape, lambda e, cnt: (e, 0, 0))
    grid_spec = pltpu.PrefetchScalarGridSpec(
        num_scalar_prefetch=1,
        grid=(N_EXPERTS,),
        in_specs=[hbm, hbm, wspec((D_MODEL, 2 * D_FF)), wspec((1, 2 * D_FF)),
                  wspec((D_FF, D_MODEL)), wspec((1, D_MODEL)),
                  pl.BlockSpec((MXU_DIM, MXU_DIM), lambda e, cnt: (0, 0))],
        out_specs=[hbm, hbm],
        scratch_shapes=[pltpu.VMEM((2 * D_FF // MXU_DIM, D_MODEL, MXU_DIM), BF16),
                        pltpu.VMEM((D_MODEL // MXU_DIM, D_FF, MXU_DIM), BF16),
                        pltpu.VMEM((tb, D_MODEL), BF16), pltpu.VMEM((tb, D_FF), BF16),
                        pltpu.VMEM((2, 2, tb, PACK_HALF), jnp.uint32), pltpu.VMEM((2, 2, tb, PACK_HALF), jnp.uint32),
                        pltpu.SemaphoreType.DMA((2, 2)), pltpu.SemaphoreType.DMA((2, 2))],
    )
    return pl.pallas_call(
        functools.partial(_expert_kernel, region=region),
        grid_spec=grid_spec,
        out_shape=[jax.ShapeDtypeStruct(xs_lo.shape, jnp.uint32)] * 2,
        compiler_params=_cparams(("arbitrary",)),
        name="experts",
    )(counts, xs_lo, xs_hi, w1, b1p, w2, b2r, perm)


def _combine_kernel(x1_ref, gate_ref, *refs):
    lo_refs, hi_refs, o_ref = refs[:TOP_K], refs[TOP_K:2 * TOP_K], refs[2 * TOP_K]
    acc = x1_ref[...]
    gates = gate_ref[...]
    for kk in range(TOP_K):
        words = jnp.concatenate([lo_refs[kk][...], hi_refs[kk][...]], axis=1)
        acc = acc + gates[:, kk:kk + 1] * _unpack_rows(words)
    o_ref[...] = acc


def _combine(x1, gates, yg_lo, yg_hi):
    n_tok = x1.shape[0]
    tm = min(ROW_TILE, n_tok)
    nblk = n_tok // tm
    row = lambda w: pl.BlockSpec((tm, w), lambda i: (i, 0))
    plane = lambda kk: pl.BlockSpec((tm, PACK_HALF), lambda i, kk=kk: (kk * nblk + i, 0))
    planes = [plane(kk) for kk in range(TOP_K)]
    return pl.pallas_call(
        _combine_kernel,
        grid=(nblk,),
        in_specs=[row(D_MODEL), row(LANES)] + planes + planes,
        out_specs=row(D_MODEL),
        out_shape=jax.ShapeDtypeStruct((n_tok, D_MODEL), F32),
        compiler_params=_cparams(("parallel",)),
        name="combine",
    )(x1, gates, *([yg_lo] * TOP_K), *([yg_hi] * TOP_K))


def kernel(x, norm1_gain, w_in, lambda_re, lambda_im, log_dt, ssm_b_re, ssm_b_im, ssm_c_re, ssm_c_im, ssm_d, w_glu, b_glu, q_norm_gain, k_norm_gain, lambda_q1, lambda_k1, lambda_q2, lambda_k2, subln_gain, w_proj_ssm, w_proj_attn, w_out, norm2_gain, w_router, b_router, w_exp1, b_exp1, w_exp2, b_exp2):
    bsz, seq, d = x.shape
    n_tok = bsz * seq
    depth = norm1_gain.shape[0]
    row1 = lambda a: a.astype(F32).reshape(1, -1)
    for l in range(depth):
        lambda_init = 0.8 - 0.6 * math.exp(-0.3 * l)
        x2d = x.reshape(n_tok, d)

        u, q, k, vt, gs, ga = _in_proj(x2d, row1(norm1_gain[l]), w_in[l], q_norm_gain[l], k_norm_gain[l])

        bblk, cblk, a_tile = _ssm_params(lambda_re[l], lambda_im[l], log_dt[l], ssm_b_re[l], ssm_b_im[l],
                                         ssm_c_re[l], ssm_c_im[l])
        so = _ssm(u.reshape(bsz, seq, SSM_WIDTH), bblk, cblk, a_tile, row1(ssm_d[l]),
                  w_glu[l].astype(BF16), row1(b_glu[l])).reshape(n_tok, SSM_WIDTH)

        lam = (jnp.exp(jnp.sum(lambda_q1[l].astype(F32) * lambda_k1[l].astype(F32)))
               - jnp.exp(jnp.sum(lambda_q2[l].astype(F32) * lambda_k2[l].astype(F32)))
               + lambda_init).reshape(1)
        ao = _attention(q, k, vt, lam, subln_gain[l], lambda_init, bsz, seq)

        region = n_tok
        x1, h_lo, h_hi, route, gates, counts = _merge_route(
            so, ao, gs, ga, x2d, w_proj_ssm[l].astype(BF16), w_proj_attn[l].astype(BF16),
            w_out[l].astype(BF16), row1(norm2_gain[l]), w_router[l], b_router[l], region)

        dest = route[:, :TOP_K].T
        n_slots = N_EXPERTS * region
        xs_lo = _sc_scatter_rows(h_lo, dest, n_slots)
        xs_hi = _sc_scatter_rows(h_hi, dest, n_slots)

        ys_lo, ys_hi = _experts(xs_lo, xs_hi, w_exp1[l], b_exp1[l], w_exp2[l], b_exp2[l],
                                counts[0, :N_EXPERTS], region)

        flat = dest.reshape(1, TOP_K * n_tok)
        yg_lo = _sc_gather_rows(ys_lo, flat)
        yg_hi = _sc_gather_rows(ys_hi, flat)
        x = _combine(x1, gates, yg_lo, yg_hi).reshape(bsz, seq, d)
    return x
```

```python
import functools
import math

import jax
import jax.numpy as jnp
from jax import lax
from jax.experimental import pallas as pl
from jax.experimental.pallas import tpu as pltpu
from jax.experimental.pallas import tpu_sc as plsc

F32 = jnp.float32
BF16 = jnp.bfloat16

D_MODEL = 1024
NORM_EPS = 1e-5
SSM_WIDTH = 512
SSM_GROUP = 16
SSM_GROUPS = 32
SSM_STATE = 64
N_STATE = SSM_GROUPS * SSM_STATE
HEADS = 8
HEAD_DIM = 64
N_EXPERTS = 32
TOP_K = 4
D_FF = 1024
SWIGLU_ALPHA = 1.702
SWIGLU_LIMIT = 7.0

LANES = 128
SUBLANES = 8
MXU_DIM = 256
VMEM_LIMIT = 56 * 1024 * 1024

ROW_TILE = 512
SSM_CHUNK = 128
SCAN_LANES = 512
ATTN_BLOCK = 256
ATTN_HEADS_PER_STEP = 8
ATTN_LOOKAHEAD = 16
ROWSUM_ROWS = 16
EXPERT_BLOCK = 512
UP_CHUNKS_PER_TASK = 2
ROW_DMA_PRIORITY = 1
COLLECT_PARTS = 1
SC_WINDOW = 128
PACK_W = D_MODEL // 2
PACK_HALF = PACK_W // 2

_NEG = -1e30
Q_SCALE = math.log2(math.e) / math.sqrt(HEAD_DIM)


def _cparams(sem):
    return pltpu.CompilerParams(dimension_semantics=sem, vmem_limit_bytes=VMEM_LIMIT)


def _full(shape):
    nd = len(shape)
    return pl.BlockSpec(shape, lambda *_: (0,) * nd)


def _sigmoid(x):
    return 0.5 * jnp.tanh(0.5 * x) + 0.5


def _run_lookahead(tasks):
    pending = tasks[0][0]()
    for i, (_, epilogue) in enumerate(tasks):
        result = pending
        if i + 1 < len(tasks):
            pending = tasks[i + 1][0]()
        epilogue(result)


_IN_CHUNKS = {"u": (0, SSM_WIDTH // MXU_DIM)}
for _name in ("q", "k", "v", "gs", "ga"):
    _start = max(first + count for first, count in _IN_CHUNKS.values())
    _IN_CHUNKS[_name] = (_start, D_MODEL // MXU_DIM)
N_IN_CHUNKS = max(first + count for first, count in _IN_CHUNKS.values())


def _prep_w_in_kernel(w_ref, wc_ref, wvt_ref):
    c = pl.program_id(0)
    w = w_ref[...]
    wc_ref[0] = w.astype(BF16)
    v_first, v_count = _IN_CHUNKS["v"]

    @pl.when((c >= v_first) & (c < v_first + v_count))
    def _():
        wvt_ref[0] = w.T.astype(BF16)


def _prep_w_in(w_in):
    v_first, v_count = _IN_CHUNKS["v"]
    return pl.pallas_call(
        _prep_w_in_kernel,
        grid=(N_IN_CHUNKS,),
        in_specs=[pl.BlockSpec((D_MODEL, MXU_DIM), lambda c: (0, c))],
        out_specs=[pl.BlockSpec((1, D_MODEL, MXU_DIM), lambda c: (c, 0, 0)),
                   pl.BlockSpec((1, MXU_DIM, D_MODEL), lambda c: (jnp.clip(c - v_first, 0, v_count - 1), 0, 0))],
        out_shape=[jax.ShapeDtypeStruct((N_IN_CHUNKS, D_MODEL, MXU_DIM), BF16),
                   jax.ShapeDtypeStruct((v_count, MXU_DIM, D_MODEL), BF16)],
        compiler_params=_cparams(("arbitrary",)),
        name="prep_w_in",
    )(w_in)


def _inproj_kernel(x_ref, g1_ref, w_ref, wvt_ref, qg_ref, kg_ref, seg_ref,
                   u_ref, q_ref, k_ref, vt_ref, gs_ref, ga_ref):
    x = x_ref[...]
    ms = jnp.mean(x * x, axis=-1, keepdims=True)
    h = (x * lax.rsqrt(ms + NORM_EPS) * g1_ref[...]).astype(BF16)

    def proj(name, c):
        chunk = _IN_CHUNKS[name][0] + c
        return lambda: jnp.dot(h, w_ref[chunk], preferred_element_type=F32)

    seg = seg_ref[...]
    tasks = []

    def plain_task(name, out_ref, c, fn):
        cols = slice(c * MXU_DIM, (c + 1) * MXU_DIM)

        def epilogue(r):
            out_ref[:, cols] = fn(r)
        return proj(name, c), epilogue

    tasks += [plain_task("u", u_ref, c, lambda r: r) for c in range(_IN_CHUNKS["u"][1])]

    def head_norm_tasks(name, gain_ref, out_ref, scale, c):
        cols = slice(c * MXU_DIM, (c + 1) * MXU_DIM)
        kept = {}

        def after_proj(y):
            kept["y"] = y
            kept["sq"] = (y * y).astype(BF16)

        def after_sum(ss):
            yn = kept["y"] * lax.rsqrt(ss * (1.0 / HEAD_DIM) + NORM_EPS) * gain_ref[:, cols]
            out_ref[:, cols] = (yn * scale).astype(BF16)

        return ((proj(name, c), after_proj),
                (lambda: jnp.dot(kept["sq"], seg, preferred_element_type=F32), after_sum))

    pairs = [head_norm_tasks(name, gain_ref, out_ref, scale, c)
             for name, gain_ref, out_ref, scale in (("q", qg_ref, q_ref, Q_SCALE), ("k", kg_ref, k_ref, 1.0))
             for c in range(_IN_CHUNKS[name][1])]
    tasks.append(pairs[0][0])
    for prev, cur in zip(pairs, pairs[1:]):
        tasks += [cur[0], prev[1]]
    tasks.append(pairs[-1][1])

    to_gate = lambda r: _sigmoid(r).astype(BF16)
    for c in range(_IN_CHUNKS["gs"][1]):
        tasks.append(plain_task("gs", gs_ref, c, to_gate))
        tasks.append(plain_task("ga", ga_ref, c, to_gate))

    def vt_task(c):
        def matmul():
            return lax.dot_general(wvt_ref[c], h, (((1,), (1,)), ((), ())), preferred_element_type=F32)

        def epilogue(r):
            vt_ref[c * MXU_DIM:(c + 1) * MXU_DIM, :] = r.astype(BF16)
        return matmul, epilogue

    tasks += [vt_task(c) for c in range(_IN_CHUNKS["v"][1])]
    _run_lookahead(tasks)


def _in_proj(x2d, gain1, w_in, q_gain, k_gain):
    n_tok = x2d.shape[0]
    tm = min(ROW_TILE, n_tok)
    w_chunks, w_vt = _prep_w_in(w_in)
    seg = (jnp.arange(MXU_DIM)[:, None] // HEAD_DIM == jnp.arange(MXU_DIM)[None, :] // HEAD_DIM).astype(BF16)
    reps = D_MODEL // HEAD_DIM
    qg = jnp.tile(q_gain.astype(F32), reps)[None, :]
    kg = jnp.tile(k_gain.astype(F32), reps)[None, :]
    row = lambda w: pl.BlockSpec((tm, w), lambda i: (i, 0))
    tok = jax.ShapeDtypeStruct((n_tok, D_MODEL), BF16)
    out_shapes = [jax.ShapeDtypeStruct((n_tok, SSM_WIDTH), F32), tok, tok,
                  jax.ShapeDtypeStruct((D_MODEL, n_tok), BF16), tok, tok]
    vt_spec = pl.BlockSpec((D_MODEL, tm), lambda i: (0, i))
    return pl.pallas_call(
        _inproj_kernel,
        grid=(n_tok // tm,),
        in_specs=[row(D_MODEL), _full((1, D_MODEL)), _full(w_chunks.shape), _full(w_vt.shape),
                  _full((1, D_MODEL)), _full((1, D_MODEL)), _full((MXU_DIM, MXU_DIM))],
        out_specs=[row(SSM_WIDTH), row(D_MODEL), row(D_MODEL), vt_spec, row(D_MODEL), row(D_MODEL)],
        out_shape=out_shapes,
        compiler_params=_cparams(("parallel",)),
        name="in_proj",
    )(x2d, gain1, w_chunks, w_vt, qg, kg, seg)


def _ssm_kernel(u_ref, bblk_ref, a_ref, cblk_ref, d_ref, wglu_ref, bglu_ref, o_ref, bu_ref, st_ref, tb_ref):
    n_batch, chunk = u_ref.shape[0], u_ref.shape[1]

    @pl.when(pl.program_id(0) == 0)
    def _():
        st_ref[...] = jnp.zeros_like(st_ref)

    n_planes = SSM_WIDTH // LANES
    for b in range(n_batch):
        for j in range(n_planes):
            tb_ref[j, pl.ds(b, chunk, stride=n_batch), :] = u_ref[b, :, j * LANES:(j + 1) * LANES]
    u = jnp.concatenate([tb_ref[j] for j in range(n_planes)], axis=1)
    u_bf = u.astype(BF16)
    tiles_per_part = N_STATE // MXU_DIM
    ch_per_tile = SSM_WIDTH // tiles_per_part
    for n in range(2 * tiles_per_part):
        ch0 = ((n % tiles_per_part) * ch_per_tile) // LANES * LANES
        lanes = slice(n * MXU_DIM, (n + 1) * MXU_DIM)
        bu_ref[:, lanes] = jnp.dot(u_bf[:, ch0:ch0 + LANES], bblk_ref[ch0:ch0 + LANES, lanes],
                                   preferred_element_type=F32)

    for j in range(N_STATE // SCAN_LANES):
        re = slice(j * SCAN_LANES, (j + 1) * SCAN_LANES)
        im = slice(N_STATE + j * SCAN_LANES, N_STATE + (j + 1) * SCAN_LANES)
        ar = a_ref[:, re]
        ai = a_ref[:, im]

        def step(t, carry, re=re, im=im, ar=ar, ai=ai):
            xr, xi = carry
            rows = pl.ds(pl.multiple_of(t * SUBLANES, SUBLANES), SUBLANES)
            nr = ar * xr - ai * xi + bu_ref[rows, re]
            ni = ar * xi + ai * xr + bu_ref[rows, im]
            bu_ref[rows, re] = nr
            bu_ref[rows, im] = ni
            return nr, ni

        xr, xi = lax.fori_loop(0, chunk, step, (st_ref[:, re], st_ref[:, im]), unroll=4)
        st_ref[:, re] = xr
        st_ref[:, im] = xi

    n_out = SSM_WIDTH // MXU_DIM
    lanes_per_out = N_STATE // n_out
    ys = []
    for j in range(n_out):
        cols = slice(j * MXU_DIM, (j + 1) * MXU_DIM)
        acc = None
        for part in range(2):
            lanes = slice(part * N_STATE + j * lanes_per_out, part * N_STATE + (j + 1) * lanes_per_out)
            term = jnp.dot(bu_ref[:, lanes].astype(BF16), cblk_ref[lanes, cols], preferred_element_type=F32)
            acc = term if acc is None else acc + term
        ys.append(acc)
    y = jnp.concatenate(ys, axis=1) + d_ref[...] * u
    z = jax.nn.gelu(y)
    gate = _sigmoid(jnp.dot(z.astype(BF16), wglu_ref[...], preferred_element_type=F32) + bglu_ref[...])
    out = z * gate
    for j in range(n_planes):
        tb_ref[j] = out[:, j * LANES:(j + 1) * LANES]
    for b in range(n_batch):
        for j in range(n_planes):
            o_ref[b, :, j * LANES:(j + 1) * LANES] = tb_ref[j, pl.ds(b, chunk, stride=n_batch), :].astype(BF16)


def _ssm_params(lambda_re, lambda_im, log_dt, b_re, b_im, c_re, c_im):
    dt = jnp.exp(log_dt.astype(F32))[:, None]
    lr = jnp.minimum(lambda_re.astype(F32), -1e-4)
    li = lambda_im.astype(F32)
    mag = jnp.exp(lr * dt)
    abar_re = mag * jnp.cos(li * dt)
    abar_im = mag * jnp.sin(li * dt)
    den = lr * lr + li * li
    nr = abar_re - 1.0
    coef_re = (nr * lr + abar_im * li) / den
    coef_im = (abar_im * lr - nr * li) / den
    br = b_re.astype(F32)
    bi = b_im.astype(F32)
    bbar_re = coef_re[..., None] * br - coef_im[..., None] * bi
    bbar_im = coef_re[..., None] * bi + coef_im[..., None] * br
    ch_group = jnp.arange(SSM_WIDTH) // SSM_GROUP
    lane_group = jnp.arange(N_STATE) // SSM_STATE

    def expand_b(b):
        rows = b.transpose(0, 2, 1).reshape(SSM_WIDTH, SSM_STATE)
        return jnp.where(ch_group[:, None] == lane_group[None, :], jnp.tile(rows, (1, SSM_GROUPS)), 0.0)

    def expand_c(c):
        rows = c.transpose(0, 2, 1).reshape(N_STATE, SSM_GROUP)
        return jnp.where(lane_group[:, None] == ch_group[None, :], jnp.tile(rows, (1, SSM_GROUPS)), 0.0)

    bblk = jnp.concatenate([expand_b(bbar_re), expand_b(bbar_im)], axis=1).astype(BF16)
    cblk = jnp.concatenate([expand_c(c_re.astype(F32)), -expand_c(c_im.astype(F32))], axis=0).astype(BF16)
    a_row = jnp.concatenate([abar_re.reshape(-1), abar_im.reshape(-1)])[None, :]
    return bblk, cblk, jnp.broadcast_to(a_row, (SUBLANES, 2 * N_STATE))


def _ssm(u, bblk, cblk, a_tile, d_skip, w_glu_bf, b_glu):
    bsz, seq, _ = u.shape
    assert bsz == SUBLANES
    chunk = min(SSM_CHUNK, seq)
    rows = chunk * SUBLANES
    tok_spec = pl.BlockSpec((bsz, chunk, SSM_WIDTH), lambda c: (0, c, 0))
    return pl.pallas_call(
        _ssm_kernel,
        grid=(seq // chunk,),
        in_specs=[tok_spec, _full(bblk.shape), _full(a_tile.shape), _full(cblk.shape),
                  _full((1, SSM_WIDTH)), _full(w_glu_bf.shape), _full((1, SSM_WIDTH))],
        out_specs=tok_spec,
        out_shape=jax.ShapeDtypeStruct(u.shape, BF16),
        scratch_shapes=[pltpu.VMEM((rows, 2 * N_STATE), F32), pltpu.VMEM((SUBLANES, 2 * N_STATE), F32),
                        pltpu.VMEM((SSM_WIDTH // LANES, rows, LANES), F32)],
        compiler_params=_cparams(("arbitrary",)),
        name="ssm",
    )(u, bblk, a_tile, cblk, d_skip, w_glu_bf, b_glu)


def _attn_kernel(lam_ref, sg_ref, q_ref, k_ref, vt_ref, o_ref, acc_ref, vta_ref, *, out_scale, blk, nh):
    seq = q_ref.shape[1]
    hw = 2 * HEAD_DIM
    lane = lax.broadcasted_iota(jnp.int32, (blk, hw), 1)
    key_i = lax.broadcasted_iota(jnp.int32, (blk, blk), 0)
    qry_i = lax.broadcasted_iota(jnp.int32, (blk, blk), 1)
    keep = key_i <= qry_i
    contract_last = (((1,), (1,)), ((), ()))
    lam = lam_ref[0]
    n_chain = 2 * nh

    for hh in range(nh):
        vta_ref[hh, :hw, :] = vt_ref[hh * hw:(hh + 1) * hw, :]
        vta_ref[hh, hw:, :] = jnp.ones((ROWSUM_ROWS, seq), BF16)

    def q_block(qi, _):
        qrows = pl.ds(pl.multiple_of(qi * blk, blk), blk)
        qs = []
        for hh in range(nh):
            q = q_ref[0, qrows, hh * hw:(hh + 1) * hw]
            zero = jnp.zeros_like(q)
            qs += [jnp.where(lane < HEAD_DIM, q, zero), jnp.where(lane >= HEAD_DIM, q, zero)]
        acc_ref[...] = jnp.zeros_like(acc_ref)

        def kv_block(kb, carry, masked):
            krows = pl.ds(pl.multiple_of(kb * blk, blk), blk)
            def score(c):
                hh = c // 2
                k = k_ref[0, krows, hh * hw:(hh + 1) * hw]
                return lax.dot_general(k, qs[c], contract_last, preferred_element_type=F32)

            scores = {c: score(c) for c in range(min(ATTN_LOOKAHEAD, n_chain))}
            out = []
            for c in range(n_chain):
                m = carry[c]
                s = scores.pop(c)
                if masked:
                    s = jnp.where(keep, s, _NEG)
                m_new = jnp.maximum(m, jnp.max(s, axis=0, keepdims=True))
                p = jnp.exp2(s - m_new).astype(BF16)
                alpha = jnp.exp2(m - m_new)
                out.append(m_new)
                if c + ATTN_LOOKAHEAD < n_chain:
                    scores[c + ATTN_LOOKAHEAD] = score(c + ATTN_LOOKAHEAD)
                vt = vta_ref[c // 2, :, krows]
                acc_ref[c] = alpha * acc_ref[c] + jnp.dot(vt, p, preferred_element_type=F32)
            return tuple(out)

        carry = (jnp.full((1, blk), _NEG, F32),) * n_chain
        carry = lax.fori_loop(0, qi, lambda kb, c: kv_block(kb, c, False), carry)
        kv_block(qi, carry, True)
        for hh in range(nh):
            a1, a2 = acc_ref[2 * hh], acc_ref[2 * hh + 1]
            l1, l2 = a1[hw:hw + 1, :], a2[hw:hw + 1, :]
            ot = a1[:hw, :] * (1.0 / l1) - a2[:hw, :] * (lam / l2)
            ot = ot * lax.rsqrt(jnp.mean(ot * ot, axis=0, keepdims=True) + NORM_EPS)
            o_ref[0, qrows, hh * hw:(hh + 1) * hw] = (ot.T * sg_ref[...] * out_scale).astype(BF16)
        return 0

    lax.fori_loop(0, seq // blk, q_block, 0)


def _attention(q, k, vt, lam, subln_gain, lambda_init, bsz, seq):
    blk = min(ATTN_BLOCK, seq)
    nh = ATTN_HEADS_PER_STEP
    hw = 2 * HEAD_DIM
    q3, k3 = (a.reshape(bsz, seq, D_MODEL) for a in (q, k))
    tok_spec = pl.BlockSpec((1, seq, nh * hw), lambda b, h: (b, 0, h))
    out = pl.pallas_call(
        functools.partial(_attn_kernel, out_scale=1.0 - lambda_init, blk=blk, nh=nh),
        grid=(bsz, HEADS // nh),
        in_specs=[pl.BlockSpec(memory_space=pltpu.SMEM), _full((1, hw)), tok_spec, tok_spec,
                  pl.BlockSpec((nh * hw, seq), lambda b, h: (h, b))],
        out_specs=tok_spec,
        out_shape=jax.ShapeDtypeStruct((bsz, seq, D_MODEL), BF16),
        scratch_shapes=[pltpu.VMEM((2 * nh, hw + ROWSUM_ROWS, blk), F32),
                        pltpu.VMEM((nh, hw + ROWSUM_ROWS, seq), BF16)],
        compiler_params=_cparams(("parallel", "parallel")),
        name="diff_attn",
    )(lam, subln_gain.astype(F32)[None, :], q3, k3, vt)
    return out.reshape(bsz * seq, D_MODEL)


def _pack_rows(y):
    bits = lax.bitcast_convert_type(y.astype(BF16).astype(F32), jnp.uint32)
    return (bits[:, :PACK_W] >> 16) | (bits[:, PACK_W:] & jnp.uint32(0xFFFF0000))


def _unpack_rows(w):
    lo = lax.bitcast_convert_type(w << 16, F32)
    hi = lax.bitcast_convert_type(w & jnp.uint32(0xFFFF0000), F32)
    return jnp.concatenate([lo, hi], axis=1)


def _merge_kernel(so_ref, ao_ref, gs_ref, ga_ref, x_ref, wps_ref, wpa_ref, wo_ref, g2_ref,
                  wrh_ref, wrl_ref, br_ref, tri_ref,
                  x1_ref, hlo_ref, hhi_ref, route_ref, gate_ref, cnt_ref, run_ref, *, region):
    @pl.when(pl.program_id(0) == 0)
    def _():
        run_ref[...] = jnp.zeros_like(run_ref)

    tm = x_ref.shape[0]
    n_part = 2 if tm % (2 * MXU_DIM) == 0 else 1
    rows_per = tm // n_part
    lane = lax.broadcasted_iota(jnp.int32, (rows_per, LANES), 1)
    lane_f = lane.astype(F32)
    parts = [dict(rows=slice(p * rows_per, (p + 1) * rows_per)) for p in range(n_part)]

    def stage_proj(st):
        def matmul():
            return (jnp.dot(so_ref[st["rows"], :], wps_ref[...], preferred_element_type=F32),
                    jnp.dot(ao_ref[st["rows"], :], wpa_ref[...], preferred_element_type=F32))

        def epilogue(r):
            ps, pa = r
            merged = gs_ref[st["rows"], :].astype(F32) * ps + ga_ref[st["rows"], :].astype(F32) * pa
            st["merged"] = merged.astype(BF16)
        return matmul, epilogue

    def stage_out(st):
        def matmul():
            return jnp.dot(st["merged"], wo_ref[...], preferred_element_type=F32)

        def epilogue(r):
            x1 = x_ref[st["rows"], :] + r
            x1_ref[st["rows"], :] = x1
            h2 = x1 * lax.rsqrt(jnp.mean(x1 * x1, axis=-1, keepdims=True) + NORM_EPS) * g2_ref[...]
            words = _pack_rows(h2)
            hlo_ref[st["rows"], :] = words[:, :PACK_HALF]
            hhi_ref[st["rows"], :] = words[:, PACK_HALF:]
            st["h_hi"] = h2.astype(BF16)
            st["h_lo"] = (h2 - st["h_hi"].astype(F32)).astype(BF16)
        return matmul, epilogue

    def stage_router(st):
        def matmul():
            return (jnp.dot(st["h_hi"], wrh_ref[...], preferred_element_type=F32)
                    + jnp.dot(st["h_lo"], wrh_ref[...], preferred_element_type=F32)
                    + jnp.dot(st["h_hi"], wrl_ref[...], preferred_element_type=F32))

        def epilogue(r):
            work = r + br_ref[...]
            onehots, vals, ids = [], [], []
            for _ in range(TOP_K):
                m = jnp.max(work, axis=-1, keepdims=True)
                idx = jnp.min(jnp.where(work == m, lane_f, float(LANES)), axis=-1, keepdims=True)
                oh = lane_f == idx
                onehots.append(oh)
                vals.append(m)
                ids.append(idx.astype(jnp.int32))
                work = jnp.where(oh, -jnp.inf, work)
            exps = [jnp.exp(v - vals[0]) for v in vals]
            den = exps[0] + exps[1] + exps[2] + exps[3]
            st.update(onehots=onehots, ids=ids, gates=[e / den for e in exps],
                      multi=(onehots[0] | onehots[1] | onehots[2] | onehots[3]).astype(F32))
        return matmul, epilogue

    _run_lookahead([stage(st) for stage in (stage_proj, stage_out, stage_router) for st in parts])

    multi = jnp.concatenate([st["multi"] for st in parts], axis=0)
    before = jnp.dot(tri_ref[...], multi.astype(BF16), preferred_element_type=F32) + run_ref[...]
    for st in parts:
        route = jnp.zeros((rows_per, LANES), jnp.int32)
        gates = jnp.zeros((rows_per, LANES), F32)
        for kk in range(TOP_K):
            rank = jnp.sum(jnp.where(st["onehots"][kk], before[st["rows"], :], 0.0), axis=-1, keepdims=True)
            route = jnp.where(lane == kk, st["ids"][kk] * region + rank.astype(jnp.int32), route)
            gates = jnp.where(lane == kk, st["gates"][kk], gates)
        route_ref[st["rows"], :] = route
        gate_ref[st["rows"], :] = gates
    run = run_ref[...] + jnp.sum(multi, axis=0, keepdims=True)
    run_ref[...] = run
    cnt_ref[...] = run.astype(jnp.int32)


def _merge_route(so, ao, gs, ga, x2d, wps, wpa, wo, gain2, w_router, b_router, region):
    n_tok = x2d.shape[0]
    tm = min(ROW_TILE, n_tok)
    wr = jnp.zeros((D_MODEL, LANES), F32).at[:, :N_EXPERTS].set(w_router.astype(F32))
    wr_hi = wr.astype(BF16)
    wr_lo = (wr - wr_hi.astype(F32)).astype(BF16)
    br = jnp.full((1, LANES), -jnp.inf, F32).at[0, :N_EXPERTS].set(b_router.astype(F32))
    tri = (jnp.arange(tm)[:, None] > jnp.arange(tm)[None, :]).astype(BF16)
    row = lambda w: pl.BlockSpec((tm, w), lambda i: (i, 0))
    out_shapes = [jax.ShapeDtypeStruct((n_tok, D_MODEL), F32),
                  jax.ShapeDtypeStruct((n_tok, PACK_HALF), jnp.uint32),
                  jax.ShapeDtypeStruct((n_tok, PACK_HALF), jnp.uint32),
                  jax.ShapeDtypeStruct((n_tok, LANES), jnp.int32),
                  jax.ShapeDtypeStruct((n_tok, LANES), F32),
                  jax.ShapeDtypeStruct((1, LANES), jnp.int32)]
    return pl.pallas_call(
        functools.partial(_merge_kernel, region=region),
        grid=(n_tok // tm,),
        in_specs=[row(SSM_WIDTH), row(D_MODEL), row(D_MODEL), row(D_MODEL), row(D_MODEL),
                  _full(wps.shape), _full(wpa.shape), _full(wo.shape), _full((1, D_MODEL)),
                  _full(wr_hi.shape), _full(wr_lo.shape), _full((1, LANES)), _full((tm, tm))],
        out_specs=[row(D_MODEL), row(PACK_HALF), row(PACK_HALF), row(LANES), row(LANES), _full((1, LANES))],
        out_shape=out_shapes,
        scratch_shapes=[pltpu.VMEM((1, LANES), F32)],
        compiler_params=_cparams(("arbitrary",)),
        name="merge_route",
    )(so, ao, gs, ga, x2d, wps, wpa, wo, gain2, wr_hi, wr_lo, br, tri)


def _sc_scatter_rows(rows, dest, n_slots):
    n_tok, width = rows.shape
    mesh = plsc.VectorSubcoreMesh(core_axis_name="core", subcore_axis_name="subcore")

    @pl.kernel(out_type=jax.ShapeDtypeStruct((n_slots, width), rows.dtype), mesh=mesh, scratch_types=[])
    def scatter(rows_hbm, dest_hbm, out_hbm):
        def body(rows_vmem, dest_vmem):
            for k in range(TOP_K):
                pltpu.sync_copy(rows_vmem, out_hbm.at[dest_vmem.at[k]])

        pltpu.emit_pipeline(
            body,
            grid=(n_tok // SC_WINDOW,),
            in_specs=[pl.BlockSpec((SC_WINDOW, width), lambda i: (i, 0)),
                      pl.BlockSpec((TOP_K, SC_WINDOW), lambda i: (0, i))],
            out_specs=[],
            core_axis_name=("core", "subcore"),
            dimension_semantics=(pltpu.PARALLEL,),
        )(rows_hbm, dest_hbm)

    return scatter(rows, dest)


def _sc_gather_rows(table, idx):
    n = idx.shape[1]
    width = table.shape[1]
    mesh = plsc.VectorSubcoreMesh(core_axis_name="core", subcore_axis_name="subcore")

    @pl.kernel(out_type=jax.ShapeDtypeStruct((n, width), table.dtype), mesh=mesh, scratch_types=[])
    def gather(table_hbm, idx_hbm, out_hbm):
        def body(idx_vmem, out_vmem):
            pltpu.sync_copy(table_hbm.at[idx_vmem.at[0]], out_vmem)

        pltpu.emit_pipeline(
            body,
            grid=(n // SC_WINDOW,),
            in_specs=[pl.BlockSpec((1, SC_WINDOW), lambda i: (0, i))],
            out_specs=[pl.BlockSpec((SC_WINDOW, width), lambda i: (i, 0))],
            core_axis_name=("core", "subcore"),
            dimension_semantics=(pltpu.PARALLEL,),
        )(idx_hbm, out_hbm)

    return gather(table, idx)


def _expert_kernel(cnt_ref, xlo_hbm, xhi_hbm, w1_ref, b1_ref, w2_ref, b2_ref, perm_ref, ylo_hbm, yhi_hbm,
                   w1p_ref, w2b_ref, x_ref, act_ref, xin_ref, yout_ref, in_sem, out_sem, *, region):
    e = pl.program_id(0)
    n_blk = (cnt_ref[e] + EXPERT_BLOCK - 1) // EXPERT_BLOCK
    base = e * region
    n_chunks = (2 * D_FF) // MXU_DIM
    x_hbm = (xlo_hbm, xhi_hbm)
    y_hbm = (ylo_hbm, yhi_hbm)

    def rows_of(j):
        return pl.ds(pl.multiple_of(base + j * EXPERT_BLOCK, EXPERT_BLOCK), EXPERT_BLOCK)

    def in_copy(j, slot, half):
        return pltpu.make_async_copy(x_hbm[half].at[rows_of(j), :], xin_ref.at[slot, half], in_sem.at[slot, half])

    def out_copy(j, slot, half):
        return pltpu.make_async_copy(yout_ref.at[slot, half], y_hbm[half].at[rows_of(j), :], out_sem.at[slot, half])

    @pl.when(n_blk > 0)
    def _():
        for half in range(2):
            in_copy(0, 0, half).start(priority=ROW_DMA_PRIORITY)

    perm = perm_ref[...]
    for c in range(n_chunks):
        cols = slice(c * MXU_DIM, (c + 1) * MXU_DIM)
        w1p_ref[c] = jnp.dot(w1_ref[0, :, cols].astype(BF16), perm, preferred_element_type=F32).astype(BF16)
    for c in range(D_MODEL // MXU_DIM):
        w2b_ref[c] = w2_ref[0, :, c * MXU_DIM:(c + 1) * MXU_DIM].astype(BF16)

    def block(j, _):
        slot = j % 2
        for half in range(2):
            in_copy(j, slot, half).wait()

        @pl.when(j + 1 < n_blk)
        def _():
            for half in range(2):
                in_copy(j + 1, 1 - slot, half).start(priority=ROW_DMA_PRIORITY)

        @pl.when(j >= 2)
        def _():
            for half in range(2):
                out_copy(j - 2, slot, half).wait()

        words = jnp.concatenate([xin_ref[slot, 0], xin_ref[slot, 1]], axis=1)
        x_ref[...] = _unpack_rows(words).astype(BF16)

        def up_task(c0):
            chunks = range(c0, c0 + UP_CHUNKS_PER_TASK)

            def matmul():
                return [jnp.dot(x_ref[...], w1p_ref[c], preferred_element_type=F32) for c in chunks]

            def epilogue(results):
                gates, ups = [], []
                for c, r in zip(chunks, results):
                    h = r + b1_ref[0, :, c * MXU_DIM:(c + 1) * MXU_DIM]
                    gates.append(jnp.minimum(h[:, :LANES], SWIGLU_LIMIT))
                    ups.append(jnp.clip(h[:, LANES:], -SWIGLU_LIMIT, SWIGLU_LIMIT))
                gate = jnp.concatenate(gates, axis=1)
                up = jnp.concatenate(ups, axis=1)
                glu = gate * _sigmoid(SWIGLU_ALPHA * gate)
                act_ref[:, c0 * LANES:(c0 + UP_CHUNKS_PER_TASK) * LANES] = ((up + 1.0) * glu).astype(BF16)
            return matmul, epilogue

        _run_lookahead([up_task(c0) for c0 in range(0, n_chunks, UP_CHUNKS_PER_TASK)])

        def down_task(c):
            cols = slice(c * MXU_DIM, (c + 1) * MXU_DIM)
            half, high = c % 2, c // 2

            def epilogue(r):
                bits = lax.bitcast_convert_type((r + b2_ref[0, :, cols]).astype(BF16).astype(F32), jnp.uint32)
                if high:
                    yout_ref[slot, half] = yout_ref[slot, half] | (bits & jnp.uint32(0xFFFF0000))
                else:
                    yout_ref[slot, half] = bits >> 16
            return (lambda: jnp.dot(act_ref[...], w2b_ref[c], preferred_element_type=F32)), epilogue

        _run_lookahead([down_task(c) for c in range(D_MODEL // MXU_DIM)])
        for half in range(2):
            out_copy(j, slot, half).start(priority=ROW_DMA_PRIORITY)
        return 0

    lax.fori_loop(0, n_blk, block, 0)

    for back in (2, 1):
        @pl.when(n_blk >= back)
        def _(back=back):
            j = n_blk - back
            for half in range(2):
                out_copy(j, j % 2, half).wait()


def _gate_up_order():
    j = jnp.arange(MXU_DIM)
    within = jnp.where(j < LANES, 2 * j, 2 * (j - LANES) + 1)
    return within


def _experts(xs_lo, xs_hi, w1, b1, w2, b2, counts, region):
    within = _gate_up_order()
    perm = (jnp.arange(MXU_DIM)[:, None] == within[None, :]).astype(BF16)
    order = (jnp.arange(0, 2 * D_FF, MXU_DIM)[:, None] + within[None, :]).reshape(-1)
    b1p = b1.astype(F32)[:, order][:, None, :]
    b2r = b2.astype(F32)[:, None, :]
    tb = EXPERT_BLOCK
    hbm = pl.BlockSpec(memory_space=pl.ANY)
    wspec = lambda shape: pl.BlockSpec((1,) + shape, lambda e, cnt: (e, 0, 0))
    grid_spec = pltpu.PrefetchScalarGridSpec(
        num_scalar_prefetch=1,
        grid=(N_EXPERTS,),
        in_specs=[hbm, hbm, wspec((D_MODEL, 2 * D_FF)), wspec((1, 2 * D_FF)),
                  wspec((D_FF, D_MODEL)), wspec((1, D_MODEL)),
                  pl.BlockSpec((MXU_DIM, MXU_DIM), lambda e, cnt: (0, 0))],
        out_specs=[hbm, hbm],
        scratch_shapes=[pltpu.VMEM((2 * D_FF // MXU_DIM, D_MODEL, MXU_DIM), BF16),
                        pltpu.VMEM((D_MODEL // MXU_DIM, D_FF, MXU_DIM), BF16),
                        pltpu.VMEM((tb, D_MODEL), BF16), pltpu.VMEM((tb, D_FF), BF16),
                        pltpu.VMEM((2, 2, tb, PACK_HALF), jnp.uint32), pltpu.VMEM((2, 2, tb, PACK_HALF), jnp.uint32),
                        pltpu.SemaphoreType.DMA((2, 2)), pltpu.SemaphoreType.DMA((2, 2))],
    )
    return pl.pallas_call(
        functools.partial(_expert_kernel, region=region),
        grid_spec=grid_spec,
        out_shape=[jax.ShapeDtypeStruct(xs_lo.shape, jnp.uint32)] * 2,
        compiler_params=_cparams(("arbitrary",)),
        name="experts",
    )(counts, xs_lo, xs_hi, w1, b1p, w2, b2r, perm)


def _combine_kernel(x1_ref, gate_ref, *refs):
    lo_refs, hi_refs, o_ref = refs[:TOP_K], refs[TOP_K:2 * TOP_K], refs[-1]
    acc = x1_ref[...]
    gates = gate_ref[...]
    for kk in range(TOP_K):
        words = jnp.concatenate([lo_refs[kk][...], hi_refs[kk][...]], axis=1)
        acc = acc + gates[:, kk:kk + 1] * _unpack_rows(words)
    o_ref[...] = acc


def _combine_part(x1, gates, yg_lo, yg_hi, part, n_parts, out_so_far):
    n_tok = x1.shape[0]
    part_tok = n_tok // n_parts
    tm = min(ROW_TILE, part_tok)
    nblk = part_tok // tm
    row = lambda w: pl.BlockSpec((tm, w), lambda i: (part * nblk + i, 0))
    plane = lambda kk: pl.BlockSpec((tm, PACK_HALF), lambda i, kk=kk: (kk * nblk + i, 0))
    planes = [plane(kk) for kk in range(TOP_K)]
    operands = [x1, gates, *([yg_lo] * TOP_K), *([yg_hi] * TOP_K)]
    in_specs = [row(D_MODEL), row(LANES)] + planes + planes
    aliases = {}
    if out_so_far is not None:
        aliases = {len(operands): 0}
        operands.append(out_so_far)
        in_specs.append(pl.BlockSpec(memory_space=pl.ANY))
    return pl.pallas_call(
        _combine_kernel,
        grid=(nblk,),
        in_specs=in_specs,
        out_specs=row(D_MODEL),
        out_shape=jax.ShapeDtypeStruct((n_tok, D_MODEL), F32),
        input_output_aliases=aliases,
        compiler_params=_cparams(("parallel",)),
        name="combine",
    )(*operands)


def kernel(x, norm1_gain, w_in, lambda_re, lambda_im, log_dt, ssm_b_re, ssm_b_im, ssm_c_re, ssm_c_im, ssm_d, w_glu, b_glu, q_norm_gain, k_norm_gain, lambda_q1, lambda_k1, lambda_q2, lambda_k2, subln_gain, w_proj_ssm, w_proj_attn, w_out, norm2_gain, w_router, b_router, w_exp1, b_exp1, w_exp2, b_exp2):
    bsz, seq, d = x.shape
    n_tok = bsz * seq
    depth = norm1_gain.shape[0]
    row1 = lambda a: a.astype(F32).reshape(1, -1)
    for l in range(depth):
        lambda_init = 0.8 - 0.6 * math.exp(-0.3 * l)
        x2d = x.reshape(n_tok, d)

        u, q, k, vt, gs, ga = _in_proj(x2d, row1(norm1_gain[l]), w_in[l], q_norm_gain[l], k_norm_gain[l])

        bblk, cblk, a_tile = _ssm_params(lambda_re[l], lambda_im[l], log_dt[l], ssm_b_re[l], ssm_b_im[l],
                                         ssm_c_re[l], ssm_c_im[l])
        so = _ssm(u.reshape(bsz, seq, SSM_WIDTH), bblk, cblk, a_tile, row1(ssm_d[l]),
                  w_glu[l].astype(BF16), row1(b_glu[l])).reshape(n_tok, SSM_WIDTH)

        lam = (jnp.exp(jnp.sum(lambda_q1[l].astype(F32) * lambda_k1[l].astype(F32)))
               - jnp.exp(jnp.sum(lambda_q2[l].astype(F32) * lambda_k2[l].astype(F32)))
               + lambda_init).reshape(1)
        ao = _attention(q, k, vt, lam, subln_gain[l], lambda_init, bsz, seq)

        region = n_tok
        x1, h_lo, h_hi, route, gates, counts = _merge_route(
            so, ao, gs, ga, x2d, w_proj_ssm[l].astype(BF16), w_proj_attn[l].astype(BF16),
            w_out[l].astype(BF16), row1(norm2_gain[l]), w_router[l], b_router[l], region)

        dest = route[:, :TOP_K].T
        n_slots = N_EXPERTS * region
        xs_lo = _sc_scatter_rows(h_lo, dest, n_slots)
        xs_hi = _sc_scatter_rows(h_hi, dest, n_slots)

        ys_lo, ys_hi = _experts(xs_lo, xs_hi, w_exp1[l], b_exp1[l], w_exp2[l], b_exp2[l],
                                counts[0, :N_EXPERTS], region)

        part_tok = n_tok // COLLECT_PARTS
        out = None
        for part in range(COLLECT_PARTS):
            flat = dest[:, part * part_tok:(part + 1) * part_tok].reshape(1, TOP_K * part_tok)
            out = _combine_part(x1, gates, _sc_gather_rows(ys_lo, flat), _sc_gather_rows(ys_hi, flat),
                                part, COLLECT_PARTS, out)
        x = out.reshape(bsz, seq, d)
    return x
```

```python
import functools
import math

import jax
import jax.numpy as jnp
from jax import lax
from jax.experimental import pallas as pl
from jax.experimental.pallas import tpu as pltpu
from jax.experimental.pallas import tpu_sc as plsc

F32 = jnp.float32
BF16 = jnp.bfloat16

D_MODEL = 1024
NORM_EPS = 1e-5
SSM_WIDTH = 512
SSM_GROUP = 16
SSM_GROUPS = 32
SSM_STATE = 64
N_STATE = SSM_GROUPS * SSM_STATE
HEADS = 8
HEAD_DIM = 64
N_EXPERTS = 32
TOP_K = 4
D_FF = 1024
SWIGLU_ALPHA = 1.702
SWIGLU_LIMIT = 7.0

LANES = 128
SUBLANES = 8
MXU_DIM = 256
VMEM_LIMIT = 56 * 1024 * 1024

ROW_TILE = 512
SSM_CHUNK = 128
SCAN_LANES = 1024
ATTN_BLOCK = 256
ATTN_HEADS_PER_STEP = 8
ATTN_LOOKAHEAD = 16
ROWSUM_ROWS = 16
EXPERT_BLOCK = 512
UP_CHUNKS_PER_TASK = 2
ROW_DMA_PRIORITY = 1
COLLECT_PARTS = 1
SC_WINDOW = 128
PACK_W = D_MODEL // 2
PACK_HALF = PACK_W // 2

_NEG = -1e30
Q_SCALE = math.log2(math.e) / math.sqrt(HEAD_DIM)


def _cparams(sem):
    return pltpu.CompilerParams(dimension_semantics=sem, vmem_limit_bytes=VMEM_LIMIT)


def _full(shape):
    nd = len(shape)
    return pl.BlockSpec(shape, lambda *_: (0,) * nd)


def _sigmoid(x):
    return 0.5 * jnp.tanh(0.5 * x) + 0.5


def _run_lookahead(tasks):
    pending = tasks[0][0]()
    for i, (_, epilogue) in enumerate(tasks):
        result = pending
        if i + 1 < len(tasks):
            pending = tasks[i + 1][0]()
        epilogue(result)


_IN_CHUNKS = {"u": (0, SSM_WIDTH // MXU_DIM)}
for _name in ("q", "k", "v", "gs", "ga"):
    _start = max(first + count for first, count in _IN_CHUNKS.values())
    _IN_CHUNKS[_name] = (_start, D_MODEL // MXU_DIM)
N_IN_CHUNKS = max(first + count for first, count in _IN_CHUNKS.values())


def _prep_w_in_kernel(w_ref, wc_ref, wvt_ref):
    c = pl.program_id(0)
    w = w_ref[...]
    wc_ref[0] = w.astype(BF16)
    v_first, v_count = _IN_CHUNKS["v"]

    @pl.when((c >= v_first) & (c < v_first + v_count))
    def _():
        wvt_ref[0] = w.T.astype(BF16)


def _prep_w_in(w_in):
    v_first, v_count = _IN_CHUNKS["v"]
    return pl.pallas_call(
        _prep_w_in_kernel,
        grid=(N_IN_CHUNKS,),
        in_specs=[pl.BlockSpec((D_MODEL, MXU_DIM), lambda c: (0, c))],
        out_specs=[pl.BlockSpec((1, D_MODEL, MXU_DIM), lambda c: (c, 0, 0)),
                   pl.BlockSpec((1, MXU_DIM, D_MODEL), lambda c: (jnp.clip(c - v_first, 0, v_count - 1), 0, 0))],
        out_shape=[jax.ShapeDtypeStruct((N_IN_CHUNKS, D_MODEL, MXU_DIM), BF16),
                   jax.ShapeDtypeStruct((v_count, MXU_DIM, D_MODEL), BF16)],
        compiler_params=_cparams(("arbitrary",)),
        name="prep_w_in",
    )(w_in)


def _inproj_kernel(x_ref, g1_ref, w_ref, wvt_ref, qg_ref, kg_ref, seg_ref,
                   u_ref, q_ref, k_ref, vt_ref, gs_ref, ga_ref):
    x = x_ref[...]
    ms = jnp.mean(x * x, axis=-1, keepdims=True)
    h = (x * lax.rsqrt(ms + NORM_EPS) * g1_ref[...]).astype(BF16)

    def proj(name, c):
        chunk = _IN_CHUNKS[name][0] + c
        return lambda: jnp.dot(h, w_ref[chunk], preferred_element_type=F32)

    seg = seg_ref[...]
    tasks = []

    def plain_task(name, out_ref, c, fn):
        cols = slice(c * MXU_DIM, (c + 1) * MXU_DIM)

        def epilogue(r):
            out_ref[:, cols] = fn(r)
        return proj(name, c), epilogue

    tasks += [plain_task("u", u_ref, c, lambda r: r) for c in range(_IN_CHUNKS["u"][1])]

    def head_norm_tasks(name, gain_ref, out_ref, scale, c):
        cols = slice(c * MXU_DIM, (c + 1) * MXU_DIM)
        kept = {}

        def after_proj(y):
            kept["y"] = y
            kept["sq"] = (y * y).astype(BF16)

        def after_sum(ss):
            yn = kept["y"] * lax.rsqrt(ss * (1.0 / HEAD_DIM) + NORM_EPS) * gain_ref[:, cols]
            out_ref[:, cols] = (yn * scale).astype(BF16)

        return ((proj(name, c), after_proj),
                (lambda: jnp.dot(kept["sq"], seg, preferred_element_type=F32), after_sum))

    pairs = [head_norm_tasks(name, gain_ref, out_ref, scale, c)
             for name, gain_ref, out_ref, scale in (("q", qg_ref, q_ref, Q_SCALE), ("k", kg_ref, k_ref, 1.0))
             for c in range(_IN_CHUNKS[name][1])]
    tasks.append(pairs[0][0])
    for prev, cur in zip(pairs, pairs[1:]):
        tasks += [cur[0], prev[1]]
    tasks.append(pairs[-1][1])

    to_gate = lambda r: _sigmoid(r).astype(BF16)
    for c in range(_IN_CHUNKS["gs"][1]):
        tasks.append(plain_task("gs", gs_ref, c, to_gate))
        tasks.append(plain_task("ga", ga_ref, c, to_gate))

    def vt_task(c):
        def matmul():
            return lax.dot_general(wvt_ref[c], h, (((1,), (1,)), ((), ())), preferred_element_type=F32)

        def epilogue(r):
            vt_ref[c * MXU_DIM:(c + 1) * MXU_DIM, :] = r.astype(BF16)
        return matmul, epilogue

    tasks += [vt_task(c) for c in range(_IN_CHUNKS["v"][1])]
    _run_lookahead(tasks)


def _in_proj(x2d, gain1, w_in, q_gain, k_gain):
    n_tok = x2d.shape[0]
    tm = min(ROW_TILE, n_tok)
    w_chunks, w_vt = _prep_w_in(w_in)
    seg = (jnp.arange(MXU_DIM)[:, None] // HEAD_DIM == jnp.arange(MXU_DIM)[None, :] // HEAD_DIM).astype(BF16)
    reps = D_MODEL // HEAD_DIM
    qg = jnp.tile(q_gain.astype(F32), reps)[None, :]
    kg = jnp.tile(k_gain.astype(F32), reps)[None, :]
    row = lambda w: pl.BlockSpec((tm, w), lambda i: (i, 0))
    tok = jax.ShapeDtypeStruct((n_tok, D_MODEL), BF16)
    out_shapes = [jax.ShapeDtypeStruct((n_tok, SSM_WIDTH), F32), tok, tok,
                  jax.ShapeDtypeStruct((D_MODEL, n_tok), BF16), tok, tok]
    vt_spec = pl.BlockSpec((D_MODEL, tm), lambda i: (0, i))
    return pl.pallas_call(
        _inproj_kernel,
        grid=(n_tok // tm,),
        in_specs=[row(D_MODEL), _full((1, D_MODEL)), _full(w_chunks.shape), _full(w_vt.shape),
                  _full((1, D_MODEL)), _full((1, D_MODEL)), _full((MXU_DIM, MXU_DIM))],
        out_specs=[row(SSM_WIDTH), row(D_MODEL), row(D_MODEL), vt_spec, row(D_MODEL), row(D_MODEL)],
        out_shape=out_shapes,
        compiler_params=_cparams(("parallel",)),
        name="in_proj",
    )(x2d, gain1, w_chunks, w_vt, qg, kg, seg)


def _ssm_kernel(u_ref, bblk_ref, a_ref, cblk_ref, d_ref, wglu_ref, bglu_ref, o_ref, bu_ref, st_ref, tb_ref):
    n_batch, chunk = u_ref.shape[0], u_ref.shape[1]

    @pl.when(pl.program_id(0) == 0)
    def _():
        st_ref[...] = jnp.zeros_like(st_ref)

    n_planes = SSM_WIDTH // LANES
    for b in range(n_batch):
        for j in range(n_planes):
            tb_ref[j, pl.ds(b, chunk, stride=n_batch), :] = u_ref[b, :, j * LANES:(j + 1) * LANES]
    u = jnp.concatenate([tb_ref[j] for j in range(n_planes)], axis=1)
    u_bf = u.astype(BF16)
    tiles_per_part = N_STATE // MXU_DIM
    ch_per_tile = SSM_WIDTH // tiles_per_part
    n_groups = N_STATE // SCAN_LANES
    tiles_per_group = SCAN_LANES // MXU_DIM
    assert SCAN_LANES * SSM_WIDTH == N_STATE * MXU_DIM

    def input_tiles(g):
        for part in range(2):
            for t in range(tiles_per_group):
                tile = g * tiles_per_group + t
                ch0 = (tile * ch_per_tile) // LANES * LANES
                lanes = slice(part * N_STATE + tile * MXU_DIM, part * N_STATE + (tile + 1) * MXU_DIM)
                bu_ref[:, lanes] = jnp.dot(u_bf[:, ch0:ch0 + LANES], bblk_ref[ch0:ch0 + LANES, lanes],
                                           preferred_element_type=F32)

    def scan(g):
        re = slice(g * SCAN_LANES, (g + 1) * SCAN_LANES)
        im = slice(N_STATE + g * SCAN_LANES, N_STATE + (g + 1) * SCAN_LANES)
        ar, ai = a_ref[:, re], a_ref[:, im]
        xr, xi = st_ref[:, re], st_ref[:, im]
        for t in range(chunk):
            rows = slice(t * SUBLANES, (t + 1) * SUBLANES)
            xr, xi = ar * xr - ai * xi + bu_ref[rows, re], ar * xi + ai * xr + bu_ref[rows, im]
            bu_ref[rows, re] = xr
            bu_ref[rows, im] = xi
        st_ref[:, re] = xr
        st_ref[:, im] = xi

    def output_tile(g):
        cols = slice(g * MXU_DIM, (g + 1) * MXU_DIM)
        acc = None
        for part in range(2):
            lanes = slice(part * N_STATE + g * SCAN_LANES, part * N_STATE + (g + 1) * SCAN_LANES)
            term = jnp.dot(bu_ref[:, lanes].astype(BF16), cblk_ref[lanes, cols], preferred_element_type=F32)
            acc = term if acc is None else acc + term
        return acc

    input_tiles(0)
    ys = []
    for g in range(n_groups):
        if g + 1 < n_groups:
            input_tiles(g + 1)
        scan(g)
        ys.append(output_tile(g))
    y = jnp.concatenate(ys, axis=1) + d_ref[...] * u
    z = jax.nn.gelu(y)
    gate = _sigmoid(jnp.dot(z.astype(BF16), wglu_ref[...], preferred_element_type=F32) + bglu_ref[...])
    out = z * gate
    for j in range(n_planes):
        tb_ref[j] = out[:, j * LANES:(j + 1) * LANES]
    for b in range(n_batch):
        for j in range(n_planes):
            o_ref[b, :, j * LANES:(j + 1) * LANES] = tb_ref[j, pl.ds(b, chunk, stride=n_batch), :].astype(BF16)


def _ssm_params(lambda_re, lambda_im, log_dt, b_re, b_im, c_re, c_im):
    dt = jnp.exp(log_dt.astype(F32))[:, None]
    lr = jnp.minimum(lambda_re.astype(F32), -1e-4)
    li = lambda_im.astype(F32)
    mag = jnp.exp(lr * dt)
    abar_re = mag * jnp.cos(li * dt)
    abar_im = mag * jnp.sin(li * dt)
    den = lr * lr + li * li
    nr = abar_re - 1.0
    coef_re = (nr * lr + abar_im * li) / den
    coef_im = (abar_im * lr - nr * li) / den
    br = b_re.astype(F32)
    bi = b_im.astype(F32)
    bbar_re = coef_re[..., None] * br - coef_im[..., None] * bi
    bbar_im = coef_re[..., None] * bi + coef_im[..., None] * br
    ch_group = jnp.arange(SSM_WIDTH) // SSM_GROUP
    lane_group = jnp.arange(N_STATE) // SSM_STATE

    def expand_b(b):
        rows = b.transpose(0, 2, 1).reshape(SSM_WIDTH, SSM_STATE)
        return jnp.where(ch_group[:, None] == lane_group[None, :], jnp.tile(rows, (1, SSM_GROUPS)), 0.0)

    def expand_c(c):
        rows = c.transpose(0, 2, 1).reshape(N_STATE, SSM_GROUP)
        return jnp.where(lane_group[:, None] == ch_group[None, :], jnp.tile(rows, (1, SSM_GROUPS)), 0.0)

    bblk = jnp.concatenate([expand_b(bbar_re), expand_b(bbar_im)], axis=1).astype(BF16)
    cblk = jnp.concatenate([expand_c(c_re.astype(F32)), -expand_c(c_im.astype(F32))], axis=0).astype(BF16)
    a_row = jnp.concatenate([abar_re.reshape(-1), abar_im.reshape(-1)])[None, :]
    return bblk, cblk, jnp.broadcast_to(a_row, (SUBLANES, 2 * N_STATE))


def _ssm(u, bblk, cblk, a_tile, d_skip, w_glu_bf, b_glu):
    bsz, seq, _ = u.shape
    assert bsz == SUBLANES
    chunk = min(SSM_CHUNK, seq)
    rows = chunk * SUBLANES
    tok_spec = pl.BlockSpec((bsz, chunk, SSM_WIDTH), lambda c: (0, c, 0))
    return pl.pallas_call(
        _ssm_kernel,
        grid=(seq // chunk,),
        in_specs=[tok_spec, _full(bblk.shape), _full(a_tile.shape), _full(cblk.shape),
                  _full((1, SSM_WIDTH)), _full(w_glu_bf.shape), _full((1, SSM_WIDTH))],
        out_specs=tok_spec,
        out_shape=jax.ShapeDtypeStruct(u.shape, BF16),
        scratch_shapes=[pltpu.VMEM((rows, 2 * N_STATE), F32), pltpu.VMEM((SUBLANES, 2 * N_STATE), F32),
                        pltpu.VMEM((SSM_WIDTH // LANES, rows, LANES), F32)],
        compiler_params=_cparams(("arbitrary",)),
        name="ssm",
    )(u, bblk, a_tile, cblk, d_skip, w_glu_bf, b_glu)


def _attn_kernel(lam_ref, sg_ref, q_ref, k_ref, vt_ref, o_ref, acc_ref, vta_ref, *, out_scale, blk, nh):
    seq = q_ref.shape[1]
    hw = 2 * HEAD_DIM
    lane = lax.broadcasted_iota(jnp.int32, (blk, hw), 1)
    key_i = lax.broadcasted_iota(jnp.int32, (blk, blk), 0)
    qry_i = lax.broadcasted_iota(jnp.int32, (blk, blk), 1)
    keep = key_i <= qry_i
    contract_last = (((1,), (1,)), ((), ()))
    lam = lam_ref[0]
    n_chain = 2 * nh

    for hh in range(nh):
        vta_ref[hh, :hw, :] = vt_ref[hh * hw:(hh + 1) * hw, :]
        vta_ref[hh, hw:, :] = jnp.ones((ROWSUM_ROWS, seq), BF16)

    def q_block(qi, _):
        qrows = pl.ds(pl.multiple_of(qi * blk, blk), blk)
        qs = []
        for hh in range(nh):
            q = q_ref[0, qrows, hh * hw:(hh + 1) * hw]
            zero = jnp.zeros_like(q)
            qs += [jnp.where(lane < HEAD_DIM, q, zero), jnp.where(lane >= HEAD_DIM, q, zero)]
        acc_ref[...] = jnp.zeros_like(acc_ref)

        def kv_block(kb, carry, masked):
            krows = pl.ds(pl.multiple_of(kb * blk, blk), blk)
            def score(c):
                hh = c // 2
                k = k_ref[0, krows, hh * hw:(hh + 1) * hw]
                return lax.dot_general(k, qs[c], contract_last, preferred_element_type=F32)

            scores = {c: score(c) for c in range(min(ATTN_LOOKAHEAD, n_chain))}
            out = []
            for c in range(n_chain):
                m = carry[c]
                s = scores.pop(c)
                if masked:
                    s = jnp.where(keep, s, _NEG)
                m_new = jnp.maximum(m, jnp.max(s, axis=0, keepdims=True))
                p = jnp.exp2(s - m_new).astype(BF16)
                alpha = jnp.exp2(m - m_new)
                out.append(m_new)
                if c + ATTN_LOOKAHEAD < n_chain:
                    scores[c + ATTN_LOOKAHEAD] = score(c + ATTN_LOOKAHEAD)
                vt = vta_ref[c // 2, :, krows]
                acc_ref[c] = alpha * acc_ref[c] + jnp.dot(vt, p, preferred_element_type=F32)
            return tuple(out)

        carry = (jnp.full((1, blk), _NEG, F32),) * n_chain
        carry = lax.fori_loop(0, qi, lambda kb, c: kv_block(kb, c, False), carry)
        kv_block(qi, carry, True)
        for hh in range(nh):
            a1, a2 = acc_ref[2 * hh], acc_ref[2 * hh + 1]
            l1, l2 = a1[hw:hw + 1, :], a2[hw:hw + 1, :]
            ot = a1[:hw, :] * (1.0 / l1) - a2[:hw, :] * (lam / l2)
            ot = ot * lax.rsqrt(jnp.mean(ot * ot, axis=0, keepdims=True) + NORM_EPS)
            o_ref[0, qrows, hh * hw:(hh + 1) * hw] = (ot.T * sg_ref[...] * out_scale).astype(BF16)
        return 0

    lax.fori_loop(0, seq // blk, q_block, 0)


def _attention(q, k, vt, lam, subln_gain, lambda_init, bsz, seq):
    blk = min(ATTN_BLOCK, seq)
    nh = ATTN_HEADS_PER_STEP
    hw = 2 * HEAD_DIM
    q3, k3 = (a.reshape(bsz, seq, D_MODEL) for a in (q, k))
    tok_spec = pl.BlockSpec((1, seq, nh * hw), lambda b, h: (b, 0, h))
    out = pl.pallas_call(
        functools.partial(_attn_kernel, out_scale=1.0 - lambda_init, blk=blk, nh=nh),
        grid=(bsz, HEADS // nh),
        in_specs=[pl.BlockSpec(memory_space=pltpu.SMEM), _full((1, hw)), tok_spec, tok_spec,
                  pl.BlockSpec((nh * hw, seq), lambda b, h: (h, b))],
        out_specs=tok_spec,
        out_shape=jax.ShapeDtypeStruct((bsz, seq, D_MODEL), BF16),
        scratch_shapes=[pltpu.VMEM((2 * nh, hw + ROWSUM_ROWS, blk), F32),
                        pltpu.VMEM((nh, hw + ROWSUM_ROWS, seq), BF16)],
        compiler_params=_cparams(("parallel", "parallel")),
        name="diff_attn",
    )(lam, subln_gain.astype(F32)[None, :], q3, k3, vt)
    return out.reshape(bsz * seq, D_MODEL)


def _pack_rows(y):
    bits = lax.bitcast_convert_type(y.astype(BF16).astype(F32), jnp.uint32)
    return (bits[:, :PACK_W] >> 16) | (bits[:, PACK_W:] & jnp.uint32(0xFFFF0000))


def _unpack_rows(w):
    lo = lax.bitcast_convert_type(w << 16, F32)
    hi = lax.bitcast_convert_type(w & jnp.uint32(0xFFFF0000), F32)
    return jnp.concatenate([lo, hi], axis=1)


def _merge_kernel(so_ref, ao_ref, gs_ref, ga_ref, x_ref, wps_ref, wpa_ref, wo_ref, g2_ref,
                  wrh_ref, wrl_ref, br_ref, tri_ref,
                  x1_ref, hlo_ref, hhi_ref, route_ref, gate_ref, cnt_ref, run_ref, *, region):
    @pl.when(pl.program_id(0) == 0)
    def _():
        run_ref[...] = jnp.zeros_like(run_ref)

    tm = x_ref.shape[0]
    n_part = 2 if tm % (2 * MXU_DIM) == 0 else 1
    rows_per = tm // n_part
    lane = lax.broadcasted_iota(jnp.int32, (rows_per, LANES), 1)
    lane_f = lane.astype(F32)
    parts = [dict(rows=slice(p * rows_per, (p + 1) * rows_per)) for p in range(n_part)]

    def stage_proj(st):
        def matmul():
            return (jnp.dot(so_ref[st["rows"], :], wps_ref[...], preferred_element_type=F32),
                    jnp.dot(ao_ref[st["rows"], :], wpa_ref[...], preferred_element_type=F32))

        def epilogue(r):
            ps, pa = r
            merged = gs_ref[st["rows"], :].astype(F32) * ps + ga_ref[st["rows"], :].astype(F32) * pa
            st["merged"] = merged.astype(BF16)
        return matmul, epilogue

    def stage_out(st):
        def matmul():
            return jnp.dot(st["merged"], wo_ref[...], preferred_element_type=F32)

        def epilogue(r):
            x1 = x_ref[st["rows"], :] + r
            x1_ref[st["rows"], :] = x1
            h2 = x1 * lax.rsqrt(jnp.mean(x1 * x1, axis=-1, keepdims=True) + NORM_EPS) * g2_ref[...]
            words = _pack_rows(h2)
            hlo_ref[st["rows"], :] = words[:, :PACK_HALF]
            hhi_ref[st["rows"], :] = words[:, PACK_HALF:]
            st["h_hi"] = h2.astype(BF16)
            st["h_lo"] = (h2 - st["h_hi"].astype(F32)).astype(BF16)
        return matmul, epilogue

    def stage_router(st):
        def matmul():
            return (jnp.dot(st["h_hi"], wrh_ref[...], preferred_element_type=F32)
                    + jnp.dot(st["h_lo"], wrh_ref[...], preferred_element_type=F32)
                    + jnp.dot(st["h_hi"], wrl_ref[...], preferred_element_type=F32))

        def epilogue(r):
            work = r + br_ref[...]
            onehots, vals, ids = [], [], []
            for _ in range(TOP_K):
                m = jnp.max(work, axis=-1, keepdims=True)
                idx = jnp.min(jnp.where(work == m, lane_f, float(LANES)), axis=-1, keepdims=True)
                oh = lane_f == idx
                onehots.append(oh)
                vals.append(m)
                ids.append(idx.astype(jnp.int32))
                work = jnp.where(oh, -jnp.inf, work)
            exps = [jnp.exp(v - vals[0]) for v in vals]
            den = exps[0] + exps[1] + exps[2] + exps[3]
            st.update(onehots=onehots, ids=ids, gates=[e / den for e in exps],
                      multi=(onehots[0] | onehots[1] | onehots[2] | onehots[3]).astype(F32))
        return matmul, epilogue

    _run_lookahead([stage(st) for stage in (stage_proj, stage_out, stage_router) for st in parts])

    multi = jnp.concatenate([st["multi"] for st in parts], axis=0)
    before = jnp.dot(tri_ref[...], multi.astype(BF16), preferred_element_type=F32) + run_ref[...]
    for st in parts:
        route = jnp.zeros((rows_per, LANES), jnp.int32)
        gates = jnp.zeros((rows_per, LANES), F32)
        for kk in range(TOP_K):
            rank = jnp.sum(jnp.where(st["onehots"][kk], before[st["rows"], :], 0.0), axis=-1, keepdims=True)
            route = jnp.where(lane == kk, st["ids"][kk] * region + rank.astype(jnp.int32), route)
            gates = jnp.where(lane == kk, st["gates"][kk], gates)
        route_ref[st["rows"], :] = route
        gate_ref[st["rows"], :] = gates
    run = run_ref[...] + jnp.sum(multi, axis=0, keepdims=True)
    run_ref[...] = run
    cnt_ref[...] = run.astype(jnp.int32)


def _merge_route(so, ao, gs, ga, x2d, wps, wpa, wo, gain2, w_router, b_router, region):
    n_tok = x2d.shape[0]
    tm = min(ROW_TILE, n_tok)
    wr = jnp.zeros((D_MODEL, LANES), F32).at[:, :N_EXPERTS].set(w_router.astype(F32))
    wr_hi = wr.astype(BF16)
    wr_lo = (wr - wr_hi.astype(F32)).astype(BF16)
    br = jnp.full((1, LANES), -jnp.inf, F32).at[0, :N_EXPERTS].set(b_router.astype(F32))
    tri = (jnp.arange(tm)[:, None] > jnp.arange(tm)[None, :]).astype(BF16)
    row = lambda w: pl.BlockSpec((tm, w), lambda i: (i, 0))
    out_shapes = [jax.ShapeDtypeStruct((n_tok, D_MODEL), F32),
                  jax.ShapeDtypeStruct((n_tok, PACK_HALF), jnp.uint32),
                  jax.ShapeDtypeStruct((n_tok, PACK_HALF), jnp.uint32),
                  jax.ShapeDtypeStruct((n_tok, LANES), jnp.int32),
                  jax.ShapeDtypeStruct((n_tok, LANES), F32),
                  jax.ShapeDtypeStruct((1, LANES), jnp.int32)]
    return pl.pallas_call(
        functools.partial(_merge_kernel, region=region),
        grid=(n_tok // tm,),
        in_specs=[row(SSM_WIDTH), row(D_MODEL), row(D_MODEL), row(D_MODEL), row(D_MODEL),
                  _full(wps.shape), _full(wpa.shape), _full(wo.shape), _full((1, D_MODEL)),
                  _full(wr_hi.shape), _full(wr_lo.shape), _full((1, LANES)), _full((tm, tm))],
        out_specs=[row(D_MODEL), row(PACK_HALF), row(PACK_HALF), row(LANES), row(LANES), _full((1, LANES))],
        out_shape=out_shapes,
        scratch_shapes=[pltpu.VMEM((1, LANES), F32)],
        compiler_params=_cparams(("arbitrary",)),
        name="merge_route",
    )(so, ao, gs, ga, x2d, wps, wpa, wo, gain2, wr_hi, wr_lo, br, tri)


def _sc_scatter_rows(rows, dest, n_slots):
    n_tok, width = rows.shape
    mesh = plsc.VectorSubcoreMesh(core_axis_name="core", subcore_axis_name="subcore")

    @pl.kernel(out_type=jax.ShapeDtypeStruct((n_slots, width), rows.dtype), mesh=mesh, scratch_types=[])
    def scatter(rows_hbm, dest_hbm, out_hbm):
        def body(rows_vmem, dest_vmem):
            for k in range(TOP_K):
                pltpu.sync_copy(rows_vmem, out_hbm.at[dest_vmem.at[k]])

        pltpu.emit_pipeline(
            body,
            grid=(n_tok // SC_WINDOW,),
            in_specs=[pl.BlockSpec((SC_WINDOW, width), lambda i: (i, 0)),
                      pl.BlockSpec((TOP_K, SC_WINDOW), lambda i: (0, i))],
            out_specs=[],
            core_axis_name=("core", "subcore"),
            dimension_semantics=(pltpu.PARALLEL,),
        )(rows_hbm, dest_hbm)

    return scatter(rows, dest)


def _sc_gather_rows(table, idx):
    n = idx.shape[1]
    width = table.shape[1]
    mesh = plsc.VectorSubcoreMesh(core_axis_name="core", subcore_axis_name="subcore")

    @pl.kernel(out_type=jax.ShapeDtypeStruct((n, width), table.dtype), mesh=mesh, scratch_types=[])
    def gather(table_hbm, idx_hbm, out_hbm):
        def body(idx_vmem, out_vmem):
            pltpu.sync_copy(table_hbm.at[idx_vmem.at[0]], out_vmem)

        pltpu.emit_pipeline(
            body,
            grid=(n // SC_WINDOW,),
            in_specs=[pl.BlockSpec((1, SC_WINDOW), lambda i: (0, i))],
            out_specs=[pl.BlockSpec((SC_WINDOW, width), lambda i: (i, 0))],
            core_axis_name=("core", "subcore"),
            dimension_semantics=(pltpu.PARALLEL,),
        )(idx_hbm, out_hbm)

    return gather(table, idx)


def _expert_kernel(cnt_ref, xlo_hbm, xhi_hbm, w1_ref, b1_ref, w2_ref, b2_ref, perm_ref, ylo_hbm, yhi_hbm,
                   w1p_ref, w2b_ref, x_ref, act_ref, xin_ref, yout_ref, in_sem, out_sem, *, region):
    e = pl.program_id(0)
    n_blk = (cnt_ref[e] + EXPERT_BLOCK - 1) // EXPERT_BLOCK
    base = e * region
    n_chunks = (2 * D_FF) // MXU_DIM
    x_hbm = (xlo_hbm, xhi_hbm)
    y_hbm = (ylo_hbm, yhi_hbm)

    def rows_of(j):
        return pl.ds(pl.multiple_of(base + j * EXPERT_BLOCK, EXPERT_BLOCK), EXPERT_BLOCK)

    def in_copy(j, slot, half):
        return pltpu.make_async_copy(x_hbm[half].at[rows_of(j), :], xin_ref.at[slot, half], in_sem.at[slot, half])

    def out_copy(j, slot, half):
        return pltpu.make_async_copy(yout_ref.at[slot, half], y_hbm[half].at[rows_of(j), :], out_sem.at[slot, half])

    @pl.when(n_blk > 0)
    def _():
        for half in range(2):
            in_copy(0, 0, half).start(priority=ROW_DMA_PRIORITY)

    perm = perm_ref[...]
    for c in range(n_chunks):
        cols = slice(c * MXU_DIM, (c + 1) * MXU_DIM)
        w1p_ref[c] = jnp.dot(w1_ref[0, :, cols].astype(BF16), perm, preferred_element_type=F32).astype(BF16)
    for c in range(D_MODEL // MXU_DIM):
        w2b_ref[c] = w2_ref[0, :, c * MXU_DIM:(c + 1) * MXU_DIM].astype(BF16)

    def mlp_rows(n_rows, xin, yout):
        words = jnp.concatenate([xin[0], xin[1]], axis=1)
        x_ref[:n_rows, :] = _unpack_rows(words).astype(BF16)

        def up_task(c0):
            chunks = range(c0, c0 + UP_CHUNKS_PER_TASK)

            def matmul():
                return [jnp.dot(x_ref[:n_rows, :], w1p_ref[c], preferred_element_type=F32) for c in chunks]

            def epilogue(results):
                gates, ups = [], []
                for c, r in zip(chunks, results):
                    h = r + b1_ref[0, :, c * MXU_DIM:(c + 1) * MXU_DIM]
                    gates.append(jnp.minimum(h[:, :LANES], SWIGLU_LIMIT))
                    ups.append(jnp.clip(h[:, LANES:], -SWIGLU_LIMIT, SWIGLU_LIMIT))
                gate = jnp.concatenate(gates, axis=1)
                up = jnp.concatenate(ups, axis=1)
                glu = gate * _sigmoid(SWIGLU_ALPHA * gate)
                act_ref[:n_rows, c0 * LANES:(c0 + UP_CHUNKS_PER_TASK) * LANES] = ((up + 1.0) * glu).astype(BF16)
            return matmul, epilogue

        _run_lookahead([up_task(c0) for c0 in range(0, n_chunks, UP_CHUNKS_PER_TASK)])

        def down_task(c):
            cols = slice(c * MXU_DIM, (c + 1) * MXU_DIM)
            half, high = c % 2, c // 2

            def epilogue(r):
                bits = lax.bitcast_convert_type((r + b2_ref[0, :, cols]).astype(BF16).astype(F32), jnp.uint32)
                if high:
                    yout[half] = yout[half] | (bits & jnp.uint32(0xFFFF0000))
                else:
                    yout[half] = bits >> 16
            return (lambda: jnp.dot(act_ref[:n_rows, :], w2b_ref[c], preferred_element_type=F32)), epilogue

        _run_lookahead([down_task(c) for c in range(D_MODEL // MXU_DIM)])

    def block(j, _):
        slot = j % 2
        for half in range(2):
            in_copy(j, slot, half).wait()

        @pl.when(j + 1 < n_blk)
        def _():
            for half in range(2):
                in_copy(j + 1, 1 - slot, half).start(priority=ROW_DMA_PRIORITY)

        @pl.when(j >= 2)
        def _():
            for half in range(2):
                out_copy(j - 2, slot, half).wait()

        mlp_rows(EXPERT_BLOCK, xin_ref.at[slot], yout_ref.at[slot])
        for half in range(2):
            out_copy(j, slot, half).start(priority=ROW_DMA_PRIORITY)
        return 0

    lax.fori_loop(0, n_blk, block, 0)

    for back in (2, 1):
        @pl.when(n_blk >= back)
        def _(back=back):
            j = n_blk - back
            for half in range(2):
                out_copy(j, j % 2, half).wait()


def _gate_up_order():
    j = jnp.arange(MXU_DIM)
    within = jnp.where(j < LANES, 2 * j, 2 * (j - LANES) + 1)
    return within


def _experts(xs_lo, xs_hi, w1, b1, w2, b2, counts, region):
    within = _gate_up_order()
    perm = (jnp.arange(MXU_DIM)[:, None] == within[None, :]).astype(BF16)
    order = (jnp.arange(0, 2 * D_FF, MXU_DIM)[:, None] + within[None, :]).reshape(-1)
    b1p = b1.astype(F32)[:, order][:, None, :]
    b2r = b2.astype(F32)[:, None, :]
    tb = EXPERT_BLOCK
    hbm = pl.BlockSpec(memory_space=pl.ANY)
    wspec = lambda shape: pl.BlockSpec((1,) + shape, lambda e, cnt: (e, 0, 0))
    grid_spec = pltpu.PrefetchScalarGridSpec(
        num_scalar_prefetch=1,
        grid=(N_EXPERTS,),
        in_specs=[hbm, hbm, wspec((D_MODEL, 2 * D_FF)), wspec((1, 2 * D_FF)),
                  wspec((D_FF, D_MODEL)), wspec((1, D_MODEL)),
                  pl.BlockSpec((MXU_DIM, MXU_DIM), lambda e, cnt: (0, 0))],
        out_specs=[hbm, hbm],
        scratch_shapes=[pltpu.VMEM((2 * D_FF // MXU_DIM, D_MODEL, MXU_DIM), BF16),
                        pltpu.VMEM((D_MODEL // MXU_DIM, D_FF, MXU_DIM), BF16),
                        pltpu.VMEM((tb, D_MODEL), BF16), pltpu.VMEM((tb, D_FF), BF16),
                        pltpu.VMEM((2, 2, tb, PACK_HALF), jnp.uint32), pltpu.VMEM((2, 2, tb, PACK_HALF), jnp.uint32),
                        pltpu.SemaphoreType.DMA((2, 2)), pltpu.SemaphoreType.DMA((2, 2))],
    )
    return pl.pallas_call(
        functools.partial(_expert_kernel, region=region),
        grid_spec=grid_spec,
        out_shape=[jax.ShapeDtypeStruct(xs_lo.shape, jnp.uint32)] * 2,
        compiler_params=_cparams(("arbitrary",)),
        name="experts",
    )(counts, xs_lo, xs_hi, w1, b1p, w2, b2r, perm)


def _combine_kernel(x1_ref, gate_ref, *refs):
    lo_refs, hi_refs, o_ref = refs[:TOP_K], refs[TOP_K:2 * TOP_K], refs[-1]
    acc = x1_ref[...]
    gates = gate_ref[...]
    for kk in range(TOP_K):
        words = jnp.concatenate([lo_refs[kk][...], hi_refs[kk][...]], axis=1)
        acc = acc + gates[:, kk:kk + 1] * _unpack_rows(words)
    o_ref[...] = acc


def _combine_part(x1, gates, yg_lo, yg_hi, part, n_parts, out_so_far):
    n_tok = x1.shape[0]
    part_tok = n_tok // n_parts
    tm = min(ROW_TILE, part_tok)
    nblk = part_tok // tm
    row = lambda w: pl.BlockSpec((tm, w), lambda i: (part * nblk + i, 0))
    plane = lambda kk: pl.BlockSpec((tm, PACK_HALF), lambda i, kk=kk: (kk * nblk + i, 0))
    planes = [plane(kk) for kk in range(TOP_K)]
    operands = [x1, gates, *([yg_lo] * TOP_K), *([yg_hi] * TOP_K)]
    in_specs = [row(D_MODEL), row(LANES)] + planes + planes
    aliases = {}
    if out_so_far is not None:
        aliases = {len(operands): 0}
        operands.append(out_so_far)
        in_specs.append(pl.BlockSpec(memory_space=pl.ANY))
    return pl.pallas_call(
        _combine_kernel,
        grid=(nblk,),
        in_specs=in_specs,
        out_specs=row(D_MODEL),
        out_shape=jax.ShapeDtypeStruct((n_tok, D_MODEL), F32),
        input_output_aliases=aliases,
        compiler_params=_cparams(("parallel",)),
        name="combine",
    )(*operands)


def kernel(x, norm1_gain, w_in, lambda_re, lambda_im, log_dt, ssm_b_re, ssm_b_im, ssm_c_re, ssm_c_im, ssm_d, w_glu, b_glu, q_norm_gain, k_norm_gain, lambda_q1, lambda_k1, lambda_q2, lambda_k2, subln_gain, w_proj_ssm, w_proj_attn, w_out, norm2_gain, w_router, b_router, w_exp1, b_exp1, w_exp2, b_exp2):
    bsz, seq, d = x.shape
    n_tok = bsz * seq
    depth = norm1_gain.shape[0]
    row1 = lambda a: a.astype(F32).reshape(1, -1)
    for l in range(depth):
        lambda_init = 0.8 - 0.6 * math.exp(-0.3 * l)
        x2d = x.reshape(n_tok, d)

        u, q, k, vt, gs, ga = _in_proj(x2d, row1(norm1_gain[l]), w_in[l], q_norm_gain[l], k_norm_gain[l])

        bblk, cblk, a_tile = _ssm_params(lambda_re[l], lambda_im[l], log_dt[l], ssm_b_re[l], ssm_b_im[l],
                                         ssm_c_re[l], ssm_c_im[l])
        so = _ssm(u.reshape(bsz, seq, SSM_WIDTH), bblk, cblk, a_tile, row1(ssm_d[l]),
                  w_glu[l].astype(BF16), row1(b_glu[l])).reshape(n_tok, SSM_WIDTH)

        lam = (jnp.exp(jnp.sum(lambda_q1[l].astype(F32) * lambda_k1[l].astype(F32)))
               - jnp.exp(jnp.sum(lambda_q2[l].astype(F32) * lambda_k2[l].astype(F32)))
               + lambda_init).reshape(1)
        ao = _attention(q, k, vt, lam, subln_gain[l], lambda_init, bsz, seq)

        region = n_tok
        x1, h_lo, h_hi, route, gates, counts = _merge_route(
            so, ao, gs, ga, x2d, w_proj_ssm[l].astype(BF16), w_proj_attn[l].astype(BF16),
            w_out[l].astype(BF16), row1(norm2_gain[l]), w_router[l], b_router[l], region)

        dest = route[:, :TOP_K].T
        n_slots = N_EXPERTS * region
        xs_lo = _sc_scatter_rows(h_lo, dest, n_slots)
        xs_hi = _sc_scatter_rows(h_hi, dest, n_slots)

        ys_lo, ys_hi = _experts(xs_lo, xs_hi, w_exp1[l], b_exp1[l], w_exp2[l], b_exp2[l],
                                counts[0, :N_EXPERTS], region)

        part_tok = n_tok // COLLECT_PARTS
        out = None
        for part in range(COLLECT_PARTS):
            flat = dest[:, part * part_tok:(part + 1) * part_tok].reshape(1, TOP_K * part_tok)
            out = _combine_part(x1, gates, _sc_gather_rows(ys_lo, flat), _sc_gather_rows(ys_hi, flat),
                                part, COLLECT_PARTS, out)
        x = out.reshape(bsz, seq, d)
    return x
```

```python
import functools
import math

import jax
import jax.numpy as jnp
from jax import lax
from jax.experimental import pallas as pl
from jax.experimental.pallas import tpu as pltpu
from jax.experimental.pallas import tpu_sc as plsc

F32 = jnp.float32
BF16 = jnp.bfloat16

D_MODEL = 1024
NORM_EPS = 1e-5
SSM_WIDTH = 512
SSM_GROUP = 16
SSM_GROUPS = 32
SSM_STATE = 64
N_STATE = SSM_GROUPS * SSM_STATE
HEADS = 8
HEAD_DIM = 64
N_EXPERTS = 32
TOP_K = 4
D_FF = 1024
SWIGLU_ALPHA = 1.702
SWIGLU_LIMIT = 7.0

LANES = 128
SUBLANES = 8
MXU_DIM = 256
VMEM_LIMIT = 56 * 1024 * 1024

ROW_TILE = 512
SSM_CHUNK = 128
SCAN_LANES = 1024
ATTN_BLOCK = 256
ATTN_HEADS_PER_STEP = 8
ROWSUM_ROWS = 16
EXPERT_BLOCK = 512
UP_CHUNKS_PER_TASK = 2
ROW_DMA_PRIORITY = 1
COLLECT_PARTS = 1
SC_WINDOW = 128
PACK_W = D_MODEL // 2
PACK_HALF = PACK_W // 2

_NEG = -1e30
Q_SCALE = math.log2(math.e) / math.sqrt(HEAD_DIM)


def _cparams(sem):
    return pltpu.CompilerParams(dimension_semantics=sem, vmem_limit_bytes=VMEM_LIMIT)


def _full(shape):
    nd = len(shape)
    return pl.BlockSpec(shape, lambda *_: (0,) * nd)


def _sigmoid(x):
    return 0.5 * jnp.tanh(0.5 * x) + 0.5


def _run_lookahead(tasks):
    pending = tasks[0][0]()
    for i, (_, epilogue) in enumerate(tasks):
        result = pending
        if i + 1 < len(tasks):
            pending = tasks[i + 1][0]()
        epilogue(result)


_IN_CHUNKS = {"u": (0, SSM_WIDTH // MXU_DIM)}
for _name in ("q", "k", "v", "gs", "ga"):
    _start = max(first + count for first, count in _IN_CHUNKS.values())
    _IN_CHUNKS[_name] = (_start, D_MODEL // MXU_DIM)
N_IN_CHUNKS = max(first + count for first, count in _IN_CHUNKS.values())


def _prep_w_in_kernel(w_ref, wc_ref, wvt_ref):
    c = pl.program_id(0)
    w = w_ref[...]
    wc_ref[0] = w.astype(BF16)
    v_first, v_count = _IN_CHUNKS["v"]

    @pl.when((c >= v_first) & (c < v_first + v_count))
    def _():
        wvt_ref[0] = w.T.astype(BF16)


def _prep_w_in(w_in):
    v_first, v_count = _IN_CHUNKS["v"]
    return pl.pallas_call(
        _prep_w_in_kernel,
        grid=(N_IN_CHUNKS,),
        in_specs=[pl.BlockSpec((D_MODEL, MXU_DIM), lambda c: (0, c))],
        out_specs=[pl.BlockSpec((1, D_MODEL, MXU_DIM), lambda c: (c, 0, 0)),
                   pl.BlockSpec((1, MXU_DIM, D_MODEL), lambda c: (jnp.clip(c - v_first, 0, v_count - 1), 0, 0))],
        out_shape=[jax.ShapeDtypeStruct((N_IN_CHUNKS, D_MODEL, MXU_DIM), BF16),
                   jax.ShapeDtypeStruct((v_count, MXU_DIM, D_MODEL), BF16)],
        compiler_params=_cparams(("arbitrary",)),
        name="prep_w_in",
    )(w_in)


def _inproj_kernel(x_ref, g1_ref, w_ref, wvt_ref, qg_ref, kg_ref, seg_ref,
                   u_ref, q_ref, k_ref, vt_ref, gs_ref, ga_ref):
    x = x_ref[...]
    ms = jnp.mean(x * x, axis=-1, keepdims=True)
    h = (x * lax.rsqrt(ms + NORM_EPS) * g1_ref[...]).astype(BF16)

    def proj(name, c):
        chunk = _IN_CHUNKS[name][0] + c
        return lambda: jnp.dot(h, w_ref[chunk], preferred_element_type=F32)

    seg = seg_ref[...]
    tasks = []

    def plain_task(name, out_ref, c, fn):
        cols = slice(c * MXU_DIM, (c + 1) * MXU_DIM)

        def epilogue(r):
            out_ref[:, cols] = fn(r)
        return proj(name, c), epilogue

    tasks += [plain_task("u", u_ref, c, lambda r: r) for c in range(_IN_CHUNKS["u"][1])]

    def head_norm_tasks(name, gain_ref, out_ref, scale, c):
        cols = slice(c * MXU_DIM, (c + 1) * MXU_DIM)
        kept = {}

        def after_proj(y):
            kept["y"] = y
            kept["sq"] = (y * y).astype(BF16)

        def after_sum(ss):
            yn = kept["y"] * lax.rsqrt(ss * (1.0 / HEAD_DIM) + NORM_EPS) * gain_ref[:, cols]
            out_ref[:, cols] = (yn * scale).astype(BF16)

        return ((proj(name, c), after_proj),
                (lambda: jnp.dot(kept["sq"], seg, preferred_element_type=F32), after_sum))

    pairs = [head_norm_tasks(name, gain_ref, out_ref, scale, c)
             for name, gain_ref, out_ref, scale in (("q", qg_ref, q_ref, Q_SCALE), ("k", kg_ref, k_ref, 1.0))
             for c in range(_IN_CHUNKS[name][1])]
    tasks.append(pairs[0][0])
    for prev, cur in zip(pairs, pairs[1:]):
        tasks += [cur[0], prev[1]]
    tasks.append(pairs[-1][1])

    to_gate = lambda r: _sigmoid(r).astype(BF16)
    for c in range(_IN_CHUNKS["gs"][1]):
        tasks.append(plain_task("gs", gs_ref, c, to_gate))
        tasks.append(plain_task("ga", ga_ref, c, to_gate))

    def vt_task(c):
        def matmul():
            return lax.dot_general(wvt_ref[c], h, (((1,), (1,)), ((), ())), preferred_element_type=F32)

        def epilogue(r):
            vt_ref[c * MXU_DIM:(c + 1) * MXU_DIM, :] = r.astype(BF16)
        return matmul, epilogue

    tasks += [vt_task(c) for c in range(_IN_CHUNKS["v"][1])]
    _run_lookahead(tasks)


def _in_proj(x2d, gain1, w_in, q_gain, k_gain):
    n_tok = x2d.shape[0]
    tm = min(ROW_TILE, n_tok)
    w_chunks, w_vt = _prep_w_in(w_in)
    seg = (jnp.arange(MXU_DIM)[:, None] // HEAD_DIM == jnp.arange(MXU_DIM)[None, :] // HEAD_DIM).astype(BF16)
    reps = D_MODEL // HEAD_DIM
    qg = jnp.tile(q_gain.astype(F32), reps)[None, :]
    kg = jnp.tile(k_gain.astype(F32), reps)[None, :]
    row = lambda w: pl.BlockSpec((tm, w), lambda i: (i, 0))
    tok = jax.ShapeDtypeStruct((n_tok, D_MODEL), BF16)
    out_shapes = [jax.ShapeDtypeStruct((n_tok, SSM_WIDTH), F32), tok, tok,
                  jax.ShapeDtypeStruct((D_MODEL, n_tok), BF16), tok, tok]
    vt_spec = pl.BlockSpec((D_MODEL, tm), lambda i: (0, i))
    return pl.pallas_call(
        _inproj_kernel,
        grid=(n_tok // tm,),
        in_specs=[row(D_MODEL), _full((1, D_MODEL)), _full(w_chunks.shape), _full(w_vt.shape),
                  _full((1, D_MODEL)), _full((1, D_MODEL)), _full((MXU_DIM, MXU_DIM))],
        out_specs=[row(SSM_WIDTH), row(D_MODEL), row(D_MODEL), vt_spec, row(D_MODEL), row(D_MODEL)],
        out_shape=out_shapes,
        compiler_params=_cparams(("parallel",)),
        name="in_proj",
    )(x2d, gain1, w_chunks, w_vt, qg, kg, seg)


def _ssm_kernel(u_ref, bblk_ref, a_ref, cblk_ref, d_ref, wglu_ref, bglu_ref, o_ref, bu_ref, st_ref, tb_ref):
    n_batch, chunk = u_ref.shape[0], u_ref.shape[1]

    @pl.when(pl.program_id(0) == 0)
    def _():
        st_ref[...] = jnp.zeros_like(st_ref)

    n_planes = SSM_WIDTH // LANES
    for b in range(n_batch):
        for j in range(n_planes):
            tb_ref[j, pl.ds(b, chunk, stride=n_batch), :] = u_ref[b, :, j * LANES:(j + 1) * LANES]
    u = jnp.concatenate([tb_ref[j] for j in range(n_planes)], axis=1)
    u_bf = u.astype(BF16)
    tiles_per_part = N_STATE // MXU_DIM
    ch_per_tile = SSM_WIDTH // tiles_per_part
    n_groups = N_STATE // SCAN_LANES
    tiles_per_group = SCAN_LANES // MXU_DIM
    assert SCAN_LANES * SSM_WIDTH == N_STATE * MXU_DIM

    def input_tiles(g):
        for part in range(2):
            for t in range(tiles_per_group):
                tile = g * tiles_per_group + t
                ch0 = (tile * ch_per_tile) // LANES * LANES
                lanes = slice(part * N_STATE + tile * MXU_DIM, part * N_STATE + (tile + 1) * MXU_DIM)
                bu_ref[:, lanes] = jnp.dot(u_bf[:, ch0:ch0 + LANES], bblk_ref[ch0:ch0 + LANES, lanes],
                                           preferred_element_type=F32)

    def scan(g):
        re = slice(g * SCAN_LANES, (g + 1) * SCAN_LANES)
        im = slice(N_STATE + g * SCAN_LANES, N_STATE + (g + 1) * SCAN_LANES)
        ar, ai = a_ref[:, re], a_ref[:, im]
        xr, xi = st_ref[:, re], st_ref[:, im]
        for t in range(chunk):
            rows = slice(t * SUBLANES, (t + 1) * SUBLANES)
            xr, xi = ar * xr - ai * xi + bu_ref[rows, re], ar * xi + ai * xr + bu_ref[rows, im]
            bu_ref[rows, re] = xr
            bu_ref[rows, im] = xi
        st_ref[:, re] = xr
        st_ref[:, im] = xi

    def output_tile(g):
        cols = slice(g * MXU_DIM, (g + 1) * MXU_DIM)
        acc = None
        for part in range(2):
            lanes = slice(part * N_STATE + g * SCAN_LANES, part * N_STATE + (g + 1) * SCAN_LANES)
            term = jnp.dot(bu_ref[:, lanes].astype(BF16), cblk_ref[lanes, cols], preferred_element_type=F32)
            acc = term if acc is None else acc + term
        return acc

    input_tiles(0)
    ys = []
    for g in range(n_groups):
        if g + 1 < n_groups:
            input_tiles(g + 1)
        scan(g)
        ys.append(output_tile(g))
    y = jnp.concatenate(ys, axis=1) + d_ref[...] * u
    z = jax.nn.gelu(y)
    gate = _sigmoid(jnp.dot(z.astype(BF16), wglu_ref[...], preferred_element_type=F32) + bglu_ref[...])
    out = z * gate
    for j in range(n_planes):
        tb_ref[j] = out[:, j * LANES:(j + 1) * LANES]
    for b in range(n_batch):
        for j in range(n_planes):
            o_ref[b, :, j * LANES:(j + 1) * LANES] = tb_ref[j, pl.ds(b, chunk, stride=n_batch), :].astype(BF16)


def _ssm_params(lambda_re, lambda_im, log_dt, b_re, b_im, c_re, c_im):
    dt = jnp.exp(log_dt.astype(F32))[:, None]
    lr = jnp.minimum(lambda_re.astype(F32), -1e-4)
    li = lambda_im.astype(F32)
    mag = jnp.exp(lr * dt)
    abar_re = mag * jnp.cos(li * dt)
    abar_im = mag * jnp.sin(li * dt)
    den = lr * lr + li * li
    nr = abar_re - 1.0
    coef_re = (nr * lr + abar_im * li) / den
    coef_im = (abar_im * lr - nr * li) / den
    br = b_re.astype(F32)
    bi = b_im.astype(F32)
    bbar_re = coef_re[..., None] * br - coef_im[..., None] * bi
    bbar_im = coef_re[..., None] * bi + coef_im[..., None] * br
    ch_group = jnp.arange(SSM_WIDTH) // SSM_GROUP
    lane_group = jnp.arange(N_STATE) // SSM_STATE

    def expand_b(b):
        rows = b.transpose(0, 2, 1).reshape(SSM_WIDTH, SSM_STATE)
        return jnp.where(ch_group[:, None] == lane_group[None, :], jnp.tile(rows, (1, SSM_GROUPS)), 0.0)

    def expand_c(c):
        rows = c.transpose(0, 2, 1).reshape(N_STATE, SSM_GROUP)
        return jnp.where(lane_group[:, None] == ch_group[None, :], jnp.tile(rows, (1, SSM_GROUPS)), 0.0)

    bblk = jnp.concatenate([expand_b(bbar_re), expand_b(bbar_im)], axis=1).astype(BF16)
    cblk = jnp.concatenate([expand_c(c_re.astype(F32)), -expand_c(c_im.astype(F32))], axis=0).astype(BF16)
    a_row = jnp.concatenate([abar_re.reshape(-1), abar_im.reshape(-1)])[None, :]
    return bblk, cblk, jnp.broadcast_to(a_row, (SUBLANES, 2 * N_STATE))


def _ssm(u, bblk, cblk, a_tile, d_skip, w_glu_bf, b_glu):
    bsz, seq, _ = u.shape
    assert bsz == SUBLANES
    chunk = min(SSM_CHUNK, seq)
    rows = chunk * SUBLANES
    tok_spec = pl.BlockSpec((bsz, chunk, SSM_WIDTH), lambda c: (0, c, 0))
    return pl.pallas_call(
        _ssm_kernel,
        grid=(seq // chunk,),
        in_specs=[tok_spec, _full(bblk.shape), _full(a_tile.shape), _full(cblk.shape),
                  _full((1, SSM_WIDTH)), _full(w_glu_bf.shape), _full((1, SSM_WIDTH))],
        out_specs=tok_spec,
        out_shape=jax.ShapeDtypeStruct(u.shape, BF16),
        scratch_shapes=[pltpu.VMEM((rows, 2 * N_STATE), F32), pltpu.VMEM((SUBLANES, 2 * N_STATE), F32),
                        pltpu.VMEM((SSM_WIDTH // LANES, rows, LANES), F32)],
        compiler_params=_cparams(("arbitrary",)),
        name="ssm",
    )(u, bblk, a_tile, cblk, d_skip, w_glu_bf, b_glu)


def _attn_kernel(lam_ref, sg_ref, q_ref, k_ref, vt_ref, o_ref, acc_ref, vta_ref, *, out_scale, blk, nh):
    seq = q_ref.shape[1]
    hw = 2 * HEAD_DIM
    lane = lax.broadcasted_iota(jnp.int32, (blk, hw), 1)
    key_i = lax.broadcasted_iota(jnp.int32, (blk, blk), 0)
    qry_i = lax.broadcasted_iota(jnp.int32, (blk, blk), 1)
    keep = key_i <= qry_i
    contract_last = (((1,), (1,)), ((), ()))
    lam = lam_ref[0]
    n_chain = 2 * nh

    for hh in range(nh):
        vta_ref[hh, :hw, :] = vt_ref[hh * hw:(hh + 1) * hw, :]
        vta_ref[hh, hw:, :] = jnp.ones((ROWSUM_ROWS, seq), BF16)

    def q_block(qi, _):
        qrows = pl.ds(pl.multiple_of(qi * blk, blk), blk)
        qs = []
        for hh in range(nh):
            q = q_ref[0, qrows, hh * hw:(hh + 1) * hw]
            zero = jnp.zeros_like(q)
            qs += [jnp.where(lane < HEAD_DIM, q, zero), jnp.where(lane >= HEAD_DIM, q, zero)]
        acc_ref[...] = jnp.zeros_like(acc_ref)

        def kv_block(kb, carry, masked):
            krows = pl.ds(pl.multiple_of(kb * blk, blk), blk)
            scores = [lax.dot_general(k_ref[0, krows, (c // 2) * hw:(c // 2 + 1) * hw], qs[c], contract_last,
                                      preferred_element_type=F32) for c in range(n_chain)]
            out = []
            for c in range(n_chain):
                m = carry[c]
                s = jnp.where(keep, scores[c], _NEG) if masked else scores[c]
                m_new = jnp.maximum(m, jnp.max(s, axis=0, keepdims=True))
                p = jnp.exp2(s - m_new).astype(BF16)
                alpha = jnp.exp2(m - m_new)
                out.append(m_new)
                vt = vta_ref[c // 2, :, krows]
                acc_ref[c] = alpha * acc_ref[c] + jnp.dot(vt, p, preferred_element_type=F32)
            return tuple(out)

        carry = (jnp.full((1, blk), _NEG, F32),) * n_chain
        carry = lax.fori_loop(0, qi, lambda kb, c: kv_block(kb, c, False), carry)
        kv_block(qi, carry, True)
        for hh in range(nh):
            a1, a2 = acc_ref[2 * hh], acc_ref[2 * hh + 1]
            l1, l2 = a1[hw:hw + 1, :], a2[hw:hw + 1, :]
            ot = a1[:hw, :] * (1.0 / l1) - a2[:hw, :] * (lam / l2)
            ot = ot * lax.rsqrt(jnp.mean(ot * ot, axis=0, keepdims=True) + NORM_EPS)
            o_ref[0, qrows, hh * hw:(hh + 1) * hw] = (ot.T * sg_ref[...] * out_scale).astype(BF16)
        return 0

    lax.fori_loop(0, seq // blk, q_block, 0)


def _attention(q, k, vt, lam, subln_gain, lambda_init, bsz, seq):
    blk = min(ATTN_BLOCK, seq)
    nh = ATTN_HEADS_PER_STEP
    hw = 2 * HEAD_DIM
    q3, k3 = (a.reshape(bsz, seq, D_MODEL) for a in (q, k))
    tok_spec = pl.BlockSpec((1, seq, nh * hw), lambda b, h: (b, 0, h))
    out = pl.pallas_call(
        functools.partial(_attn_kernel, out_scale=1.0 - lambda_init, blk=blk, nh=nh),
        grid=(bsz, HEADS // nh),
        in_specs=[pl.BlockSpec(memory_space=pltpu.SMEM), _full((1, hw)), tok_spec, tok_spec,
                  pl.BlockSpec((nh * hw, seq), lambda b, h: (h, b))],
        out_specs=tok_spec,
        out_shape=jax.ShapeDtypeStruct((bsz, seq, D_MODEL), BF16),
        scratch_shapes=[pltpu.VMEM((2 * nh, hw + ROWSUM_ROWS, blk), F32),
                        pltpu.VMEM((nh, hw + ROWSUM_ROWS, seq), BF16)],
        compiler_params=_cparams(("parallel", "parallel")),
        name="diff_attn",
    )(lam, subln_gain.astype(F32)[None, :], q3, k3, vt)
    return out.reshape(bsz * seq, D_MODEL)


def _pack_rows(y):
    bits = lax.bitcast_convert_type(y.astype(BF16).astype(F32), jnp.uint32)
    return (bits[:, :PACK_W] >> 16) | (bits[:, PACK_W:] & jnp.uint32(0xFFFF0000))


def _unpack_rows(w):
    lo = lax.bitcast_convert_type(w << 16, F32)
    hi = lax.bitcast_convert_type(w & jnp.uint32(0xFFFF0000), F32)
    return jnp.concatenate([lo, hi], axis=1)


def _merge_kernel(so_ref, ao_ref, gs_ref, ga_ref, x_ref, wps_ref, wpa_ref, wo_ref, g2_ref,
                  wrh_ref, wrl_ref, br_ref, tri_ref,
                  x1_ref, hlo_ref, hhi_ref, route_ref, gate_ref, cnt_ref, run_ref, *, region):
    @pl.when(pl.program_id(0) == 0)
    def _():
        run_ref[...] = jnp.zeros_like(run_ref)

    tm = x_ref.shape[0]
    n_part = 2 if tm % (2 * MXU_DIM) == 0 else 1
    rows_per = tm // n_part
    expert_f = lax.broadcasted_iota(jnp.int32, (N_EXPERTS, rows_per), 0).astype(F32)
    slot_row = lax.broadcasted_iota(jnp.int32, (SUBLANES, rows_per), 0)
    contract_last = (((1,), (1,)), ((), ()))
    parts = [dict(rows=slice(p * rows_per, (p + 1) * rows_per)) for p in range(n_part)]

    def stage_proj(st):
        def matmul():
            return (jnp.dot(so_ref[st["rows"], :], wps_ref[...], preferred_element_type=F32),
                    jnp.dot(ao_ref[st["rows"], :], wpa_ref[...], preferred_element_type=F32))

        def epilogue(r):
            ps, pa = r
            merged = gs_ref[st["rows"], :].astype(F32) * ps + ga_ref[st["rows"], :].astype(F32) * pa
            st["merged"] = merged.astype(BF16)
        return matmul, epilogue

    def stage_out(st):
        def matmul():
            return jnp.dot(st["merged"], wo_ref[...], preferred_element_type=F32)

        def epilogue(r):
            x1 = x_ref[st["rows"], :] + r
            x1_ref[st["rows"], :] = x1
            h2 = x1 * lax.rsqrt(jnp.mean(x1 * x1, axis=-1, keepdims=True) + NORM_EPS) * g2_ref[...]
            words = _pack_rows(h2)
            hlo_ref[st["rows"], :] = words[:, :PACK_HALF]
            hhi_ref[st["rows"], :] = words[:, PACK_HALF:]
            st["h_hi"] = h2.astype(BF16)
            st["h_lo"] = (h2 - st["h_hi"].astype(F32)).astype(BF16)
        return matmul, epilogue

    def stage_router(st):
        def matmul():
            return (lax.dot_general(wrh_ref[...], st["h_hi"], contract_last, preferred_element_type=F32)
                    + lax.dot_general(wrh_ref[...], st["h_lo"], contract_last, preferred_element_type=F32)
                    + lax.dot_general(wrl_ref[...], st["h_hi"], contract_last, preferred_element_type=F32))

        def epilogue(r):
            work = r + br_ref[...]
            onehots, vals, ids = [], [], []
            for _ in range(TOP_K):
                m = jnp.max(work, axis=0, keepdims=True)
                idx = jnp.min(jnp.where(work == m, expert_f, float(N_EXPERTS)), axis=0, keepdims=True)
                oh = expert_f == idx
                onehots.append(oh)
                vals.append(m)
                ids.append(idx.astype(jnp.int32))
                work = jnp.where(oh, -jnp.inf, work)
            exps = [jnp.exp(v - vals[0]) for v in vals]
            den = exps[0] + exps[1] + exps[2] + exps[3]
            st.update(onehots=onehots, ids=ids, gates=[e / den for e in exps],
                      multi=(onehots[0] | onehots[1] | onehots[2] | onehots[3]).astype(F32))
        return matmul, epilogue

    _run_lookahead([stage(st) for stage in (stage_proj, stage_out, stage_router) for st in parts])

    multi = jnp.concatenate([st["multi"] for st in parts], axis=1)
    before = jnp.dot(multi.astype(BF16), tri_ref[...], preferred_element_type=F32) + run_ref[...]
    for st in parts:
        route = jnp.zeros((SUBLANES, rows_per), jnp.int32)
        gates = jnp.zeros((SUBLANES, rows_per), F32)
        for kk in range(TOP_K):
            rank = jnp.sum(jnp.where(st["onehots"][kk], before[:, st["rows"]], 0.0), axis=0, keepdims=True)
            route = jnp.where(slot_row == kk, st["ids"][kk] * region + rank.astype(jnp.int32), route)
            gates = jnp.where(slot_row == kk, st["gates"][kk], gates)
        route_ref[:, st["rows"]] = route
        gate_ref[:, st["rows"]] = gates
    run = run_ref[...] + jnp.sum(multi, axis=1, keepdims=True)
    run_ref[...] = run
    cnt_ref[...] = jnp.broadcast_to(run, cnt_ref.shape).astype(jnp.int32)


def _merge_route(so, ao, gs, ga, x2d, wps, wpa, wo, gain2, w_router, b_router, region):
    n_tok = x2d.shape[0]
    tm = min(ROW_TILE, n_tok)
    wr = w_router.astype(F32).T
    wr_hi = wr.astype(BF16)
    wr_lo = (wr - wr_hi.astype(F32)).astype(BF16)
    br = b_router.astype(F32)[:, None]
    tri = (jnp.arange(tm)[:, None] < jnp.arange(tm)[None, :]).astype(BF16)
    row = lambda w: pl.BlockSpec((tm, w), lambda i: (i, 0))
    slots = pl.BlockSpec((SUBLANES, tm), lambda i: (0, i))
    out_shapes = [jax.ShapeDtypeStruct((n_tok, D_MODEL), F32),
                  jax.ShapeDtypeStruct((n_tok, PACK_HALF), jnp.uint32),
                  jax.ShapeDtypeStruct((n_tok, PACK_HALF), jnp.uint32),
                  jax.ShapeDtypeStruct((SUBLANES, n_tok), jnp.int32),
                  jax.ShapeDtypeStruct((SUBLANES, n_tok), F32),
                  jax.ShapeDtypeStruct((N_EXPERTS, LANES), jnp.int32)]
    return pl.pallas_call(
        functools.partial(_merge_kernel, region=region),
        grid=(n_tok // tm,),
        in_specs=[row(SSM_WIDTH), row(D_MODEL), row(D_MODEL), row(D_MODEL), row(D_MODEL),
                  _full(wps.shape), _full(wpa.shape), _full(wo.shape), _full((1, D_MODEL)),
                  _full(wr_hi.shape), _full(wr_lo.shape), _full((N_EXPERTS, 1)), _full((tm, tm))],
        out_specs=[row(D_MODEL), row(PACK_HALF), row(PACK_HALF), slots, slots, _full((N_EXPERTS, LANES))],
        out_shape=out_shapes,
        scratch_shapes=[pltpu.VMEM((N_EXPERTS, 1), F32)],
        compiler_params=_cparams(("arbitrary",)),
        name="merge_route",
    )(so, ao, gs, ga, x2d, wps, wpa, wo, gain2, wr_hi, wr_lo, br, tri)


def _sc_scatter_rows(rows, dest, n_slots):
    n_tok, width = rows.shape
    mesh = plsc.VectorSubcoreMesh(core_axis_name="core", subcore_axis_name="subcore")

    @pl.kernel(out_type=jax.ShapeDtypeStruct((n_slots, width), rows.dtype), mesh=mesh, scratch_types=[])
    def scatter(rows_hbm, dest_hbm, out_hbm):
        def body(rows_vmem, dest_vmem):
            for k in range(TOP_K):
                pltpu.sync_copy(rows_vmem, out_hbm.at[dest_vmem.at[k]])

        pltpu.emit_pipeline(
            body,
            grid=(n_tok // SC_WINDOW,),
            in_specs=[pl.BlockSpec((SC_WINDOW, width), lambda i: (i, 0)),
                      pl.BlockSpec((TOP_K, SC_WINDOW), lambda i: (0, i))],
            out_specs=[],
            core_axis_name=("core", "subcore"),
            dimension_semantics=(pltpu.PARALLEL,),
        )(rows_hbm, dest_hbm)

    return scatter(rows, dest)


def _sc_gather_rows(table, idx):
    n = idx.shape[1]
    width = table.shape[1]
    mesh = plsc.VectorSubcoreMesh(core_axis_name="core", subcore_axis_name="subcore")

    @pl.kernel(out_type=jax.ShapeDtypeStruct((n, width), table.dtype), mesh=mesh, scratch_types=[])
    def gather(table_hbm, idx_hbm, out_hbm):
        def body(idx_vmem, out_vmem):
            pltpu.sync_copy(table_hbm.at[idx_vmem.at[0]], out_vmem)

        pltpu.emit_pipeline(
            body,
            grid=(n // SC_WINDOW,),
            in_specs=[pl.BlockSpec((1, SC_WINDOW), lambda i: (0, i))],
            out_specs=[pl.BlockSpec((SC_WINDOW, width), lambda i: (i, 0))],
            core_axis_name=("core", "subcore"),
            dimension_semantics=(pltpu.PARALLEL,),
        )(idx_hbm, out_hbm)

    return gather(table, idx)


def _expert_kernel(cnt_ref, xlo_hbm, xhi_hbm, w1_ref, b1_ref, w2_ref, b2_ref, perm_ref, ylo_hbm, yhi_hbm,
                   w1p_ref, w2b_ref, x_ref, act_ref, xin_ref, yout_ref, in_sem, out_sem, *, region):
    e = pl.program_id(0)
    n_blk = (cnt_ref[e] + EXPERT_BLOCK - 1) // EXPERT_BLOCK
    base = e * region
    n_chunks = (2 * D_FF) // MXU_DIM
    x_hbm = (xlo_hbm, xhi_hbm)
    y_hbm = (ylo_hbm, yhi_hbm)

    def rows_of(j):
        return pl.ds(pl.multiple_of(base + j * EXPERT_BLOCK, EXPERT_BLOCK), EXPERT_BLOCK)

    def in_copy(j, slot, half):
        return pltpu.make_async_copy(x_hbm[half].at[rows_of(j), :], xin_ref.at[slot, half], in_sem.at[slot, half])

    def out_copy(j, slot, half):
        return pltpu.make_async_copy(yout_ref.at[slot, half], y_hbm[half].at[rows_of(j), :], out_sem.at[slot, half])

    @pl.when(n_blk > 0)
    def _():
        for half in range(2):
            in_copy(0, 0, half).start(priority=ROW_DMA_PRIORITY)

    perm = perm_ref[...]
    for c in range(n_chunks):
        cols = slice(c * MXU_DIM, (c + 1) * MXU_DIM)
        w1p_ref[c] = jnp.dot(w1_ref[0, :, cols].astype(BF16), perm, preferred_element_type=F32).astype(BF16)
    for c in range(D_MODEL // MXU_DIM):
        w2b_ref[c] = w2_ref[0, :, c * MXU_DIM:(c + 1) * MXU_DIM].astype(BF16)

    def mlp_rows(n_rows, xin, yout):
        words = jnp.concatenate([xin[0], xin[1]], axis=1)
        x_ref[:n_rows, :] = _unpack_rows(words).astype(BF16)

        def up_task(c0):
            chunks = range(c0, c0 + UP_CHUNKS_PER_TASK)

            def matmul():
                return [jnp.dot(x_ref[:n_rows, :], w1p_ref[c], preferred_element_type=F32) for c in chunks]

            def epilogue(results):
                gates, ups = [], []
                for c, r in zip(chunks, results):
                    h = r + b1_ref[0, :, c * MXU_DIM:(c + 1) * MXU_DIM]
                    gates.append(jnp.minimum(h[:, :LANES], SWIGLU_LIMIT))
                    ups.append(jnp.clip(h[:, LANES:], -SWIGLU_LIMIT, SWIGLU_LIMIT))
                gate = jnp.concatenate(gates, axis=1)
                up = jnp.concatenate(ups, axis=1)
                glu = gate * _sigmoid(SWIGLU_ALPHA * gate)
                act_ref[:n_rows, c0 * LANES:(c0 + UP_CHUNKS_PER_TASK) * LANES] = ((up + 1.0) * glu).astype(BF16)
            return matmul, epilogue

        _run_lookahead([up_task(c0) for c0 in range(0, n_chunks, UP_CHUNKS_PER_TASK)])

        def down_task(c):
            cols = slice(c * MXU_DIM, (c + 1) * MXU_DIM)
            half, high = c % 2, c // 2

            def epilogue(r):
                bits = lax.bitcast_convert_type((r + b2_ref[0, :, cols]).astype(BF16).astype(F32), jnp.uint32)
                if high:
                    yout[half] = yout[half] | (bits & jnp.uint32(0xFFFF0000))
                else:
                    yout[half] = bits >> 16
            return (lambda: jnp.dot(act_ref[:n_rows, :], w2b_ref[c], preferred_element_type=F32)), epilogue

        _run_lookahead([down_task(c) for c in range(D_MODEL // MXU_DIM)])

    def block(j, _):
        slot = j % 2
        for half in range(2):
            in_copy(j, slot, half).wait()

        @pl.when(j + 1 < n_blk)
        def _():
            for half in range(2):
                in_copy(j + 1, 1 - slot, half).start(priority=ROW_DMA_PRIORITY)

        @pl.when(j >= 2)
        def _():
            for half in range(2):
                out_copy(j - 2, slot, half).wait()

        mlp_rows(EXPERT_BLOCK, xin_ref.at[slot], yout_ref.at[slot])
        for half in range(2):
            out_copy(j, slot, half).start(priority=ROW_DMA_PRIORITY)
        return 0

    lax.fori_loop(0, n_blk, block, 0)

    for back in (2, 1):
        @pl.when(n_blk >= back)
        def _(back=back):
            j = n_blk - back
            for half in range(2):
                out_copy(j, j % 2, half).wait()


def _gate_up_order():
    j = jnp.arange(MXU_DIM)
    within = jnp.where(j < LANES, 2 * j, 2 * (j - LANES) + 1)
    return within


def _experts(xs_lo, xs_hi, w1, b1, w2, b2, counts, region):
    within = _gate_up_order()
    perm = (jnp.arange(MXU_DIM)[:, None] == within[None, :]).astype(BF16)
    order = (jnp.arange(0, 2 * D_FF, MXU_DIM)[:, None] + within[None, :]).reshape(-1)
    b1p = b1.astype(F32)[:, order][:, None, :]
    b2r = b2.astype(F32)[:, None, :]
    tb = EXPERT_BLOCK
    hbm = pl.BlockSpec(memory_space=pl.ANY)
    wspec = lambda shape: pl.BlockSpec((1,) + shape, lambda e, cnt: (e, 0, 0))
    grid_spec = pltpu.PrefetchScalarGridSpec(
        num_scalar_prefetch=1,
        grid=(N_EXPERTS,),
        in_specs=[hbm, hbm, wspec((D_MODEL, 2 * D_FF)), wspec((1, 2 * D_FF)),
                  wspec((D_FF, D_MODEL)), wspec((1, D_MODEL)),
                  pl.BlockSpec((MXU_DIM, MXU_DIM), lambda e, cnt: (0, 0))],
        out_specs=[hbm, hbm],
        scratch_shapes=[pltpu.VMEM((2 * D_FF // MXU_DIM, D_MODEL, MXU_DIM), BF16),
                        pltpu.VMEM((D_MODEL // MXU_DIM, D_FF, MXU_DIM), BF16),
                        pltpu.VMEM((tb, D_MODEL), BF16), pltpu.VMEM((tb, D_FF), BF16),
                        pltpu.VMEM((2, 2, tb, PACK_HALF), jnp.uint32), pltpu.VMEM((2, 2, tb, PACK_HALF), jnp.uint32),
                        pltpu.SemaphoreType.DMA((2, 2)), pltpu.SemaphoreType.DMA((2, 2))],
    )
    return pl.pallas_call(
        functools.partial(_expert_kernel, region=region),
        grid_spec=grid_spec,
        out_shape=[jax.ShapeDtypeStruct(xs_lo.shape, jnp.uint32)] * 2,
        compiler_params=_cparams(("arbitrary",)),
        name="experts",
    )(counts, xs_lo, xs_hi, w1, b1p, w2, b2r, perm)


def _combine_kernel(x1_ref, gate_ref, *refs):
    lo_refs, hi_refs, o_ref = refs[:TOP_K], refs[TOP_K:2 * TOP_K], refs[-1]
    acc = x1_ref[...]
    gates = gate_ref[...]
    for kk in range(TOP_K):
        words = jnp.concatenate([lo_refs[kk][...], hi_refs[kk][...]], axis=1)
        acc = acc + gates[:, kk:kk + 1] * _unpack_rows(words)
    o_ref[...] = acc


def _combine_part(x1, gates, yg_lo, yg_hi, part, n_parts, out_so_far):
    n_tok = x1.shape[0]
    part_tok = n_tok // n_parts
    tm = min(ROW_TILE, part_tok)
    nblk = part_tok // tm
    row = lambda w: pl.BlockSpec((tm, w), lambda i: (part * nblk + i, 0))
    plane = lambda kk: pl.BlockSpec((tm, PACK_HALF), lambda i, kk=kk: (kk * nblk + i, 0))
    planes = [plane(kk) for kk in range(TOP_K)]
    operands = [x1, gates, *([yg_lo] * TOP_K), *([yg_hi] * TOP_K)]
    in_specs = [row(D_MODEL), row(LANES)] + planes + planes
    aliases = {}
    if out_so_far is not None:
        aliases = {len(operands): 0}
        operands.append(out_so_far)
        in_specs.append(pl.BlockSpec(memory_space=pl.ANY))
    return pl.pallas_call(
        _combine_kernel,
        grid=(nblk,),
        in_specs=in_specs,
        out_specs=row(D_MODEL),
        out_shape=jax.ShapeDtypeStruct((n_tok, D_MODEL), F32),
        input_output_aliases=aliases,
        compiler_params=_cparams(("parallel",)),
        name="combine",
    )(*operands)


def kernel(x, norm1_gain, w_in, lambda_re, lambda_im, log_dt, ssm_b_re, ssm_b_im, ssm_c_re, ssm_c_im, ssm_d, w_glu, b_glu, q_norm_gain, k_norm_gain, lambda_q1, lambda_k1, lambda_q2, lambda_k2, subln_gain, w_proj_ssm, w_proj_attn, w_out, norm2_gain, w_router, b_router, w_exp1, b_exp1, w_exp2, b_exp2):
    bsz, seq, d = x.shape
    n_tok = bsz * seq
    depth = norm1_gain.shape[0]
    row1 = lambda a: a.astype(F32).reshape(1, -1)
    for l in range(depth):
        lambda_init = 0.8 - 0.6 * math.exp(-0.3 * l)
        x2d = x.reshape(n_tok, d)

        u, q, k, vt, gs, ga = _in_proj(x2d, row1(norm1_gain[l]), w_in[l], q_norm_gain[l], k_norm_gain[l])

        bblk, cblk, a_tile = _ssm_params(lambda_re[l], lambda_im[l], log_dt[l], ssm_b_re[l], ssm_b_im[l],
                                         ssm_c_re[l], ssm_c_im[l])
        so = _ssm(u.reshape(bsz, seq, SSM_WIDTH), bblk, cblk, a_tile, row1(ssm_d[l]),
                  w_glu[l].astype(BF16), row1(b_glu[l])).reshape(n_tok, SSM_WIDTH)

        lam = (jnp.exp(jnp.sum(lambda_q1[l].astype(F32) * lambda_k1[l].astype(F32)))
               - jnp.exp(jnp.sum(lambda_q2[l].astype(F32) * lambda_k2[l].astype(F32)))
               + lambda_init).reshape(1)
        ao = _attention(q, k, vt, lam, subln_gain[l], lambda_init, bsz, seq)

        region = n_tok
        x1, h_lo, h_hi, route, gate_rows, counts = _merge_route(
            so, ao, gs, ga, x2d, w_proj_ssm[l].astype(BF16), w_proj_attn[l].astype(BF16),
            w_out[l].astype(BF16), row1(norm2_gain[l]), w_router[l], b_router[l], region)

        dest = route[:TOP_K]
        gates = jnp.pad(gate_rows[:TOP_K].T, ((0, 0), (0, LANES - TOP_K)))
        n_slots = N_EXPERTS * region
        xs_lo = _sc_scatter_rows(h_lo, dest, n_slots)
        xs_hi = _sc_scatter_rows(h_hi, dest, n_slots)

        ys_lo, ys_hi = _experts(xs_lo, xs_hi, w_exp1[l], b_exp1[l], w_exp2[l], b_exp2[l], counts[:, 0], region)

        part_tok = n_tok // COLLECT_PARTS
        out = None
        for part in range(COLLECT_PARTS):
            flat = dest[:, part * part_tok:(part + 1) * part_tok].reshape(1, TOP_K * part_tok)
            out = _combine_part(x1, gates, _sc_gather_rows(ys_lo, flat), _sc_gather_rows(ys_hi, flat),
                                part, COLLECT_PARTS, out)
        x = out.reshape(bsz, seq, d)
    return x
```

```python
import functools
import math

import jax
import jax.numpy as jnp
from jax import lax
from jax.experimental import pallas as pl
from jax.experimental.pallas import tpu as pltpu
from jax.experimental.pallas import tpu_sc as plsc

F32 = jnp.float32
BF16 = jnp.bfloat16

D_MODEL = 1024
NORM_EPS = 1e-5
SSM_WIDTH = 512
SSM_GROUP = 16
SSM_GROUPS = 32
SSM_STATE = 64
N_STATE = SSM_GROUPS * SSM_STATE
HEADS = 8
HEAD_DIM = 64
N_EXPERTS = 32
TOP_K = 4
D_FF = 1024
SWIGLU_ALPHA = 1.702
SWIGLU_LIMIT = 7.0

LANES = 128
SUBLANES = 8
MXU_DIM = 256
VMEM_LIMIT = 56 * 1024 * 1024

ROW_TILE = 512
SSM_CHUNK = 128
SCAN_LANES = 1024
ATTN_BLOCK = 256
ATTN_HEADS_PER_STEP = 8
ROWSUM_ROWS = 16
EXPERT_BLOCK = 512
UP_CHUNKS_PER_TASK = 2
ROW_DMA_PRIORITY = 1
COLLECT_PARTS = 1
SC_WINDOW = 128
PACK_W = D_MODEL // 2
PACK_HALF = PACK_W // 2

_NEG = -1e30
Q_SCALE = math.log2(math.e) / math.sqrt(HEAD_DIM)


def _cparams(sem):
    return pltpu.CompilerParams(dimension_semantics=sem, vmem_limit_bytes=VMEM_LIMIT)


def _full(shape):
    nd = len(shape)
    return pl.BlockSpec(shape, lambda *_: (0,) * nd)


def _sigmoid(x):
    return 0.5 * jnp.tanh(0.5 * x) + 0.5


def _run_lookahead(tasks):
    pending = tasks[0][0]()
    for i, (_, epilogue) in enumerate(tasks):
        result = pending
        if i + 1 < len(tasks):
            pending = tasks[i + 1][0]()
        epilogue(result)


_IN_CHUNKS = {"u": (0, SSM_WIDTH // MXU_DIM)}
for _name in ("q", "k", "v", "gs", "ga"):
    _start = max(first + count for first, count in _IN_CHUNKS.values())
    _IN_CHUNKS[_name] = (_start, D_MODEL // MXU_DIM)
N_IN_CHUNKS = max(first + count for first, count in _IN_CHUNKS.values())


def _prep_w_in_kernel(w_ref, wc_ref, wvt_ref):
    c = pl.program_id(0)
    w = w_ref[...]
    wc_ref[0] = w.astype(BF16)
    v_first, v_count = _IN_CHUNKS["v"]

    @pl.when((c >= v_first) & (c < v_first + v_count))
    def _():
        wvt_ref[0] = w.T.astype(BF16)


def _prep_w_in(w_in):
    v_first, v_count = _IN_CHUNKS["v"]
    return pl.pallas_call(
        _prep_w_in_kernel,
        grid=(N_IN_CHUNKS,),
        in_specs=[pl.BlockSpec((D_MODEL, MXU_DIM), lambda c: (0, c))],
        out_specs=[pl.BlockSpec((1, D_MODEL, MXU_DIM), lambda c: (c, 0, 0)),
                   pl.BlockSpec((1, MXU_DIM, D_MODEL), lambda c: (jnp.clip(c - v_first, 0, v_count - 1), 0, 0))],
        out_shape=[jax.ShapeDtypeStruct((N_IN_CHUNKS, D_MODEL, MXU_DIM), BF16),
                   jax.ShapeDtypeStruct((v_count, MXU_DIM, D_MODEL), BF16)],
        compiler_params=_cparams(("arbitrary",)),
        name="prep_w_in",
    )(w_in)


def _inproj_kernel(x_ref, g1_ref, w_ref, wvt_ref, qg_ref, kg_ref, seg_ref,
                   u_ref, q_ref, k_ref, vt_ref, gs_ref, ga_ref):
    x = x_ref[...]
    ms = jnp.mean(x * x, axis=-1, keepdims=True)
    h = (x * lax.rsqrt(ms + NORM_EPS) * g1_ref[...]).astype(BF16)

    def proj(name, c):
        chunk = _IN_CHUNKS[name][0] + c
        return lambda: jnp.dot(h, w_ref[chunk], preferred_element_type=F32)

    seg = seg_ref[...]
    tasks = []

    def plain_task(name, out_ref, c, fn):
        cols = slice(c * MXU_DIM, (c + 1) * MXU_DIM)

        def epilogue(r):
            out_ref[:, cols] = fn(r)
        return proj(name, c), epilogue

    tasks += [plain_task("u", u_ref, c, lambda r: r) for c in range(_IN_CHUNKS["u"][1])]

    def head_norm_tasks(name, gain_ref, out_ref, scale, c):
        cols = slice(c * MXU_DIM, (c + 1) * MXU_DIM)
        kept = {}

        def after_proj(y):
            kept["y"] = y
            kept["sq"] = (y * y).astype(BF16)

        def after_sum(ss):
            yn = kept["y"] * lax.rsqrt(ss * (1.0 / HEAD_DIM) + NORM_EPS) * gain_ref[:, cols]
            out_ref[:, cols] = (yn * scale).astype(BF16)

        return ((proj(name, c), after_proj),
                (lambda: jnp.dot(kept["sq"], seg, preferred_element_type=F32), after_sum))

    pairs = [head_norm_tasks(name, gain_ref, out_ref, scale, c)
             for name, gain_ref, out_ref, scale in (("q", qg_ref, q_ref, Q_SCALE), ("k", kg_ref, k_ref, 1.0))
             for c in range(_IN_CHUNKS[name][1])]
    tasks.append(pairs[0][0])
    for prev, cur in zip(pairs, pairs[1:]):
        tasks += [cur[0], prev[1]]
    tasks.append(pairs[-1][1])

    to_gate = lambda r: _sigmoid(r).astype(BF16)
    for c in range(_IN_CHUNKS["gs"][1]):
        tasks.append(plain_task("gs", gs_ref, c, to_gate))
        tasks.append(plain_task("ga", ga_ref, c, to_gate))

    def vt_task(c):
        def matmul():
            return lax.dot_general(wvt_ref[c], h, (((1,), (1,)), ((), ())), preferred_element_type=F32)

        def epilogue(r):
            vt_ref[c * MXU_DIM:(c + 1) * MXU_DIM, :] = r.astype(BF16)
        return matmul, epilogue

    tasks += [vt_task(c) for c in range(_IN_CHUNKS["v"][1])]
    _run_lookahead(tasks)


def _in_proj(x2d, gain1, w_in, q_gain, k_gain):
    n_tok = x2d.shape[0]
    tm = min(ROW_TILE, n_tok)
    w_chunks, w_vt = _prep_w_in(w_in)
    seg = (jnp.arange(MXU_DIM)[:, None] // HEAD_DIM == jnp.arange(MXU_DIM)[None, :] // HEAD_DIM).astype(BF16)
    reps = D_MODEL // HEAD_DIM
    qg = jnp.tile(q_gain.astype(F32), reps)[None, :]
    kg = jnp.tile(k_gain.astype(F32), reps)[None, :]
    row = lambda w: pl.BlockSpec((tm, w), lambda i: (i, 0))
    tok = jax.ShapeDtypeStruct((n_tok, D_MODEL), BF16)
    out_shapes = [jax.ShapeDtypeStruct((n_tok, SSM_WIDTH), F32), tok, tok,
                  jax.ShapeDtypeStruct((D_MODEL, n_tok), BF16), tok, tok]
    vt_spec = pl.BlockSpec((D_MODEL, tm), lambda i: (0, i))
    return pl.pallas_call(
        _inproj_kernel,
        grid=(n_tok // tm,),
        in_specs=[row(D_MODEL), _full((1, D_MODEL)), _full(w_chunks.shape), _full(w_vt.shape),
                  _full((1, D_MODEL)), _full((1, D_MODEL)), _full((MXU_DIM, MXU_DIM))],
        out_specs=[row(SSM_WIDTH), row(D_MODEL), row(D_MODEL), vt_spec, row(D_MODEL), row(D_MODEL)],
        out_shape=out_shapes,
        compiler_params=_cparams(("parallel",)),
        name="in_proj",
    )(x2d, gain1, w_chunks, w_vt, qg, kg, seg)


def _ssm_kernel(u_ref, bblk_ref, a_ref, cblk_ref, d_ref, wglu_ref, bglu_ref, o_ref, bu_ref, st_ref, tb_ref):
    n_batch, chunk = u_ref.shape[0], u_ref.shape[1]

    @pl.when(pl.program_id(0) == 0)
    def _():
        st_ref[...] = jnp.zeros_like(st_ref)

    n_planes = SSM_WIDTH // LANES
    for b in range(n_batch):
        for j in range(n_planes):
            tb_ref[j, pl.ds(b, chunk, stride=n_batch), :] = u_ref[b, :, j * LANES:(j + 1) * LANES]
    u = jnp.concatenate([tb_ref[j] for j in range(n_planes)], axis=1)
    u_bf = u.astype(BF16)
    tiles_per_part = N_STATE // MXU_DIM
    ch_per_tile = SSM_WIDTH // tiles_per_part
    n_groups = N_STATE // SCAN_LANES
    tiles_per_group = SCAN_LANES // MXU_DIM
    assert SCAN_LANES * SSM_WIDTH == N_STATE * MXU_DIM

    def input_tiles(g):
        for part in range(2):
            for t in range(tiles_per_group):
                tile = g * tiles_per_group + t
                ch0 = (tile * ch_per_tile) // LANES * LANES
                lanes = slice(part * N_STATE + tile * MXU_DIM, part * N_STATE + (tile + 1) * MXU_DIM)
                bu_ref[:, lanes] = jnp.dot(u_bf[:, ch0:ch0 + LANES], bblk_ref[ch0:ch0 + LANES, lanes],
                                           preferred_element_type=F32)

    def scan(g):
        re = slice(g * SCAN_LANES, (g + 1) * SCAN_LANES)
        im = slice(N_STATE + g * SCAN_LANES, N_STATE + (g + 1) * SCAN_LANES)
        ar, ai = a_ref[:, re], a_ref[:, im]
        xr, xi = st_ref[:, re], st_ref[:, im]
        for t in range(chunk):
            rows = slice(t * SUBLANES, (t + 1) * SUBLANES)
            xr, xi = ar * xr - ai * xi + bu_ref[rows, re], ar * xi + ai * xr + bu_ref[rows, im]
            bu_ref[rows, re] = xr
            bu_ref[rows, im] = xi
        st_ref[:, re] = xr
        st_ref[:, im] = xi

    def output_tile(g):
        cols = slice(g * MXU_DIM, (g + 1) * MXU_DIM)
        acc = None
        for part in range(2):
            lanes = slice(part * N_STATE + g * SCAN_LANES, part * N_STATE + (g + 1) * SCAN_LANES)
            term = jnp.dot(bu_ref[:, lanes].astype(BF16), cblk_ref[lanes, cols], preferred_element_type=F32)
            acc = term if acc is None else acc + term
        return acc

    input_tiles(0)
    ys = []
    for g in range(n_groups):
        if g + 1 < n_groups:
            input_tiles(g + 1)
        scan(g)
        ys.append(output_tile(g))
    y = jnp.concatenate(ys, axis=1) + d_ref[...] * u
    z = jax.nn.gelu(y)
    gate = _sigmoid(jnp.dot(z.astype(BF16), wglu_ref[...], preferred_element_type=F32) + bglu_ref[...])
    out = z * gate
    for j in range(n_planes):
        tb_ref[j] = out[:, j * LANES:(j + 1) * LANES]
    for b in range(n_batch):
        for j in range(n_planes):
            o_ref[b, :, j * LANES:(j + 1) * LANES] = tb_ref[j, pl.ds(b, chunk, stride=n_batch), :].astype(BF16)


def _ssm_params(lambda_re, lambda_im, log_dt, b_re, b_im, c_re, c_im):
    dt = jnp.exp(log_dt.astype(F32))[:, None]
    lr = jnp.minimum(lambda_re.astype(F32), -1e-4)
    li = lambda_im.astype(F32)
    mag = jnp.exp(lr * dt)
    abar_re = mag * jnp.cos(li * dt)
    abar_im = mag * jnp.sin(li * dt)
    den = lr * lr + li * li
    nr = abar_re - 1.0
    coef_re = (nr * lr + abar_im * li) / den
    coef_im = (abar_im * lr - nr * li) / den
    br = b_re.astype(F32)
    bi = b_im.astype(F32)
    bbar_re = coef_re[..., None] * br - coef_im[..., None] * bi
    bbar_im = coef_re[..., None] * bi + coef_im[..., None] * br
    ch_group = jnp.arange(SSM_WIDTH) // SSM_GROUP
    lane_group = jnp.arange(N_STATE) // SSM_STATE

    def expand_b(b):
        rows = b.transpose(0, 2, 1).reshape(SSM_WIDTH, SSM_STATE)
        return jnp.where(ch_group[:, None] == lane_group[None, :], jnp.tile(rows, (1, SSM_GROUPS)), 0.0)

    def expand_c(c):
        rows = c.transpose(0, 2, 1).reshape(N_STATE, SSM_GROUP)
        return jnp.where(lane_group[:, None] == ch_group[None, :], jnp.tile(rows, (1, SSM_GROUPS)), 0.0)

    bblk = jnp.concatenate([expand_b(bbar_re), expand_b(bbar_im)], axis=1).astype(BF16)
    cblk = jnp.concatenate([expand_c(c_re.astype(F32)), -expand_c(c_im.astype(F32))], axis=0).astype(BF16)
    a_row = jnp.concatenate([abar_re.reshape(-1), abar_im.reshape(-1)])[None, :]
    return bblk, cblk, jnp.broadcast_to(a_row, (SUBLANES, 2 * N_STATE))


def _ssm(u, bblk, cblk, a_tile, d_skip, w_glu_bf, b_glu):
    bsz, seq, _ = u.shape
    assert bsz == SUBLANES
    chunk = min(SSM_CHUNK, seq)
    rows = chunk * SUBLANES
    tok_spec = pl.BlockSpec((bsz, chunk, SSM_WIDTH), lambda c: (0, c, 0))
    return pl.pallas_call(
        _ssm_kernel,
        grid=(seq // chunk,),
        in_specs=[tok_spec, _full(bblk.shape), _full(a_tile.shape), _full(cblk.shape),
                  _full((1, SSM_WIDTH)), _full(w_glu_bf.shape), _full((1, SSM_WIDTH))],
        out_specs=tok_spec,
        out_shape=jax.ShapeDtypeStruct(u.shape, BF16),
        scratch_shapes=[pltpu.VMEM((rows, 2 * N_STATE), F32), pltpu.VMEM((SUBLANES, 2 * N_STATE), F32),
                        pltpu.VMEM((SSM_WIDTH // LANES, rows, LANES), F32)],
        compiler_params=_cparams(("arbitrary",)),
        name="ssm",
    )(u, bblk, a_tile, cblk, d_skip, w_glu_bf, b_glu)


def _attn_kernel(lam_ref, sg_ref, q_ref, k_ref, vt_ref, o_ref, acc_ref, vta_ref, *, out_scale, blk, nh):
    seq = q_ref.shape[1]
    hw = 2 * HEAD_DIM
    lane = lax.broadcasted_iota(jnp.int32, (blk, hw), 1)
    key_i = lax.broadcasted_iota(jnp.int32, (blk, blk), 0)
    qry_i = lax.broadcasted_iota(jnp.int32, (blk, blk), 1)
    keep = key_i <= qry_i
    contract_last = (((1,), (1,)), ((), ()))
    lam = lam_ref[0]
    n_chain = 2 * nh

    for hh in range(nh):
        vta_ref[hh, :hw, :] = vt_ref[hh * hw:(hh + 1) * hw, :]
        vta_ref[hh, hw:, :] = jnp.ones((ROWSUM_ROWS, seq), BF16)

    def q_block(qi, _):
        qrows = pl.ds(pl.multiple_of(qi * blk, blk), blk)
        qs = []
        for hh in range(nh):
            q = q_ref[0, qrows, hh * hw:(hh + 1) * hw]
            zero = jnp.zeros_like(q)
            qs += [jnp.where(lane < HEAD_DIM, q, zero), jnp.where(lane >= HEAD_DIM, q, zero)]
        acc_ref[...] = jnp.zeros_like(acc_ref)

        def kv_block(kb, carry, masked):
            krows = pl.ds(pl.multiple_of(kb * blk, blk), blk)
            scores = [lax.dot_general(k_ref[0, krows, (c // 2) * hw:(c // 2 + 1) * hw], qs[c], contract_last,
                                      preferred_element_type=F32) for c in range(n_chain)]
            out = []
            for c in range(n_chain):
                m = carry[c]
                s = jnp.where(keep, scores[c], _NEG) if masked else scores[c]
                m_new = jnp.maximum(m, jnp.max(s, axis=0, keepdims=True))
                p = jnp.exp2(s - m_new).astype(BF16)
                alpha = jnp.exp2(m - m_new)
                out.append(m_new)
                vt = vta_ref[c // 2, :, krows]
                acc_ref[c] = alpha * acc_ref[c] + jnp.dot(vt, p, preferred_element_type=F32)
            return tuple(out)

        carry = (jnp.full((1, blk), _NEG, F32),) * n_chain
        carry = lax.fori_loop(0, qi, lambda kb, c: kv_block(kb, c, False), carry)
        kv_block(qi, carry, True)
        for hh in range(nh):
            a1, a2 = acc_ref[2 * hh], acc_ref[2 * hh + 1]
            l1, l2 = a1[hw:hw + 1, :], a2[hw:hw + 1, :]
            ot = a1[:hw, :] * (1.0 / l1) - a2[:hw, :] * (lam / l2)
            ot = ot * lax.rsqrt(jnp.mean(ot * ot, axis=0, keepdims=True) + NORM_EPS)
            o_ref[0, qrows, hh * hw:(hh + 1) * hw] = (ot.T * sg_ref[...] * out_scale).astype(BF16)
        return 0

    lax.fori_loop(0, seq // blk, q_block, 0)


def _attention(q, k, vt, lam, subln_gain, lambda_init, bsz, seq):
    blk = min(ATTN_BLOCK, seq)
    nh = ATTN_HEADS_PER_STEP
    hw = 2 * HEAD_DIM
    q3, k3 = (a.reshape(bsz, seq, D_MODEL) for a in (q, k))
    tok_spec = pl.BlockSpec((1, seq, nh * hw), lambda b, h: (b, 0, h))
    out = pl.pallas_call(
        functools.partial(_attn_kernel, out_scale=1.0 - lambda_init, blk=blk, nh=nh),
        grid=(bsz, HEADS // nh),
        in_specs=[pl.BlockSpec(memory_space=pltpu.SMEM), _full((1, hw)), tok_spec, tok_spec,
                  pl.BlockSpec((nh * hw, seq), lambda b, h: (h, b))],
        out_specs=tok_spec,
        out_shape=jax.ShapeDtypeStruct((bsz, seq, D_MODEL), BF16),
        scratch_shapes=[pltpu.VMEM((2 * nh, hw + ROWSUM_ROWS, blk), F32),
                        pltpu.VMEM((nh, hw + ROWSUM_ROWS, seq), BF16)],
        compiler_params=_cparams(("parallel", "parallel")),
        name="diff_attn",
    )(lam, subln_gain.astype(F32)[None, :], q3, k3, vt)
    return out.reshape(bsz * seq, D_MODEL)


def _pack_rows(y):
    bits = lax.bitcast_convert_type(y.astype(BF16).astype(F32), jnp.uint32)
    return (bits[:, :PACK_W] >> 16) | (bits[:, PACK_W:] & jnp.uint32(0xFFFF0000))


def _unpack_rows(w):
    lo = lax.bitcast_convert_type(w << 16, F32)
    hi = lax.bitcast_convert_type(w & jnp.uint32(0xFFFF0000), F32)
    return jnp.concatenate([lo, hi], axis=1)


def _merge_kernel(so_ref, ao_ref, gs_ref, ga_ref, x_ref, wps_ref, wpa_ref, wo_ref, g2_ref,
                  wrh_ref, wrl_ref, br_ref, tri_ref,
                  x1_ref, hlo_ref, hhi_ref, route_ref, gate_ref, cnt_ref, run_ref, *, region):
    @pl.when(pl.program_id(0) == 0)
    def _():
        run_ref[...] = jnp.zeros_like(run_ref)

    tm = x_ref.shape[0]
    n_part = 2 if tm % (2 * MXU_DIM) == 0 else 1
    rows_per = tm // n_part
    expert_f = lax.broadcasted_iota(jnp.int32, (N_EXPERTS, rows_per), 0).astype(F32)
    slot_row = lax.broadcasted_iota(jnp.int32, (SUBLANES, rows_per), 0)
    contract_last = (((1,), (1,)), ((), ()))
    parts = [dict(rows=slice(p * rows_per, (p + 1) * rows_per)) for p in range(n_part)]

    def stage_proj(st):
        def matmul():
            return (jnp.dot(so_ref[st["rows"], :], wps_ref[...], preferred_element_type=F32),
                    jnp.dot(ao_ref[st["rows"], :], wpa_ref[...], preferred_element_type=F32))

        def epilogue(r):
            ps, pa = r
            merged = gs_ref[st["rows"], :].astype(F32) * ps + ga_ref[st["rows"], :].astype(F32) * pa
            st["merged"] = merged.astype(BF16)
        return matmul, epilogue

    def stage_out(st):
        def matmul():
            return jnp.dot(st["merged"], wo_ref[...], preferred_element_type=F32)

        def epilogue(r):
            x1 = x_ref[st["rows"], :] + r
            x1_ref[st["rows"], :] = x1
            h2 = x1 * lax.rsqrt(jnp.mean(x1 * x1, axis=-1, keepdims=True) + NORM_EPS) * g2_ref[...]
            words = _pack_rows(h2)
            hlo_ref[st["rows"], :] = words[:, :PACK_HALF]
            hhi_ref[st["rows"], :] = words[:, PACK_HALF:]
            st["h_hi"] = h2.astype(BF16)
            st["h_lo"] = (h2 - st["h_hi"].astype(F32)).astype(BF16)
        return matmul, epilogue

    def stage_router(st):
        def matmul():
            return (lax.dot_general(wrh_ref[...], st["h_hi"], contract_last, preferred_element_type=F32)
                    + lax.dot_general(wrh_ref[...], st["h_lo"], contract_last, preferred_element_type=F32)
                    + lax.dot_general(wrl_ref[...], st["h_hi"], contract_last, preferred_element_type=F32))

        def epilogue(r):
            work = r + br_ref[...]
            onehots, vals, ids = [], [], []
            for _ in range(TOP_K):
                m = jnp.max(work, axis=0, keepdims=True)
                idx = jnp.min(jnp.where(work == m, expert_f, float(N_EXPERTS)), axis=0, keepdims=True)
                oh = expert_f == idx
                onehots.append(oh)
                vals.append(m)
                ids.append(idx.astype(jnp.int32))
                work = jnp.where(oh, -jnp.inf, work)
            exps = [jnp.exp(v - vals[0]) for v in vals]
            den = exps[0] + exps[1] + exps[2] + exps[3]
            st.update(onehots=onehots, ids=ids, gates=[e / den for e in exps],
                      multi=(onehots[0] | onehots[1] | onehots[2] | onehots[3]).astype(F32))
        return matmul, epilogue

    _run_lookahead([stage(st) for stage in (stage_proj, stage_out, stage_router) for st in parts])

    multi = jnp.concatenate([st["multi"] for st in parts], axis=1)
    before = jnp.dot(multi.astype(BF16), tri_ref[...], preferred_element_type=F32) + run_ref[...]
    gate_row = lax.broadcasted_iota(jnp.int32, (LANES, rows_per), 0)
    for st in parts:
        route = jnp.zeros((SUBLANES, rows_per), jnp.int32)
        gates = jnp.zeros((LANES, rows_per), F32)
        for kk in range(TOP_K):
            rank = jnp.sum(jnp.where(st["onehots"][kk], before[:, st["rows"]], 0.0), axis=0, keepdims=True)
            route = jnp.where(slot_row == kk, st["ids"][kk] * region + rank.astype(jnp.int32), route)
            gates = jnp.where(gate_row == kk, st["gates"][kk], gates)
        route_ref[:, st["rows"]] = route
        gate_ref[st["rows"], :] = gates.T
    run = run_ref[...] + jnp.sum(multi, axis=1, keepdims=True)
    run_ref[...] = run
    cnt_ref[...] = jnp.broadcast_to(run, cnt_ref.shape).astype(jnp.int32)


def _merge_route(so, ao, gs, ga, x2d, wps, wpa, wo, gain2, w_router, b_router, region):
    n_tok = x2d.shape[0]
    tm = min(ROW_TILE, n_tok)
    wr = w_router.astype(F32).T
    wr_hi = wr.astype(BF16)
    wr_lo = (wr - wr_hi.astype(F32)).astype(BF16)
    br = b_router.astype(F32)[:, None]
    tri = (jnp.arange(tm)[:, None] < jnp.arange(tm)[None, :]).astype(BF16)
    row = lambda w: pl.BlockSpec((tm, w), lambda i: (i, 0))
    slots = pl.BlockSpec((SUBLANES, tm), lambda i: (0, i))
    out_shapes = [jax.ShapeDtypeStruct((n_tok, D_MODEL), F32),
                  jax.ShapeDtypeStruct((n_tok, PACK_HALF), jnp.uint32),
                  jax.ShapeDtypeStruct((n_tok, PACK_HALF), jnp.uint32),
                  jax.ShapeDtypeStruct((SUBLANES, n_tok), jnp.int32),
                  jax.ShapeDtypeStruct((n_tok, LANES), F32),
                  jax.ShapeDtypeStruct((N_EXPERTS, LANES), jnp.int32)]
    return pl.pallas_call(
        functools.partial(_merge_kernel, region=region),
        grid=(n_tok // tm,),
        in_specs=[row(SSM_WIDTH), row(D_MODEL), row(D_MODEL), row(D_MODEL), row(D_MODEL),
                  _full(wps.shape), _full(wpa.shape), _full(wo.shape), _full((1, D_MODEL)),
                  _full(wr_hi.shape), _full(wr_lo.shape), _full((N_EXPERTS, 1)), _full((tm, tm))],
        out_specs=[row(D_MODEL), row(PACK_HALF), row(PACK_HALF), slots, row(LANES), _full((N_EXPERTS, LANES))],
        out_shape=out_shapes,
        scratch_shapes=[pltpu.VMEM((N_EXPERTS, 1), F32)],
        compiler_params=_cparams(("arbitrary",)),
        name="merge_route",
    )(so, ao, gs, ga, x2d, wps, wpa, wo, gain2, wr_hi, wr_lo, br, tri)


def _sc_scatter_rows(rows, dest, n_slots):
    n_tok, width = rows.shape
    mesh = plsc.VectorSubcoreMesh(core_axis_name="core", subcore_axis_name="subcore")

    @pl.kernel(out_type=jax.ShapeDtypeStruct((n_slots, width), rows.dtype), mesh=mesh, scratch_types=[])
    def scatter(rows_hbm, dest_hbm, out_hbm):
        def body(rows_vmem, dest_vmem):
            for k in range(TOP_K):
                pltpu.sync_copy(rows_vmem, out_hbm.at[dest_vmem.at[k]])

        pltpu.emit_pipeline(
            body,
            grid=(n_tok // SC_WINDOW,),
            in_specs=[pl.BlockSpec((SC_WINDOW, width), lambda i: (i, 0)),
                      pl.BlockSpec((TOP_K, SC_WINDOW), lambda i: (0, i))],
            out_specs=[],
            core_axis_name=("core", "subcore"),
            dimension_semantics=(pltpu.PARALLEL,),
        )(rows_hbm, dest_hbm)

    return scatter(rows, dest)


def _sc_gather_rows(table, idx):
    n = idx.shape[1]
    width = table.shape[1]
    mesh = plsc.VectorSubcoreMesh(core_axis_name="core", subcore_axis_name="subcore")

    @pl.kernel(out_type=jax.ShapeDtypeStruct((n, width), table.dtype), mesh=mesh, scratch_types=[])
    def gather(table_hbm, idx_hbm, out_hbm):
        def body(idx_vmem, out_vmem):
            pltpu.sync_copy(table_hbm.at[idx_vmem.at[0]], out_vmem)

        pltpu.emit_pipeline(
            body,
            grid=(n // SC_WINDOW,),
            in_specs=[pl.BlockSpec((1, SC_WINDOW), lambda i: (0, i))],
            out_specs=[pl.BlockSpec((SC_WINDOW, width), lambda i: (i, 0))],
            core_axis_name=("core", "subcore"),
            dimension_semantics=(pltpu.PARALLEL,),
        )(idx_hbm, out_hbm)

    return gather(table, idx)


def _expert_kernel(cnt_ref, xlo_hbm, xhi_hbm, w1_ref, b1_ref, w2_ref, b2_ref, perm_ref, ylo_hbm, yhi_hbm,
                   w1p_ref, w2b_ref, x_ref, act_ref, xin_ref, yout_ref, in_sem, out_sem, *, region):
    e = pl.program_id(0)
    n_blk = (cnt_ref[e] + EXPERT_BLOCK - 1) // EXPERT_BLOCK
    base = e * region
    n_chunks = (2 * D_FF) // MXU_DIM
    x_hbm = (xlo_hbm, xhi_hbm)
    y_hbm = (ylo_hbm, yhi_hbm)

    def rows_of(j):
        return pl.ds(pl.multiple_of(base + j * EXPERT_BLOCK, EXPERT_BLOCK), EXPERT_BLOCK)

    def in_copy(j, slot, half):
        return pltpu.make_async_copy(x_hbm[half].at[rows_of(j), :], xin_ref.at[slot, half], in_sem.at[slot, half])

    def out_copy(j, slot, half):
        return pltpu.make_async_copy(yout_ref.at[slot, half], y_hbm[half].at[rows_of(j), :], out_sem.at[slot, half])

    @pl.when(n_blk > 0)
    def _():
        for half in range(2):
            in_copy(0, 0, half).start(priority=ROW_DMA_PRIORITY)

    perm = perm_ref[...]
    for c in range(n_chunks):
        cols = slice(c * MXU_DIM, (c + 1) * MXU_DIM)
        w1p_ref[c] = jnp.dot(w1_ref[0, :, cols].astype(BF16), perm, preferred_element_type=F32).astype(BF16)
    for c in range(D_MODEL // MXU_DIM):
        w2b_ref[c] = w2_ref[0, :, c * MXU_DIM:(c + 1) * MXU_DIM].astype(BF16)

    def mlp_rows(n_rows, xin, yout):
        words = jnp.concatenate([xin[0], xin[1]], axis=1)
        x_ref[:n_rows, :] = _unpack_rows(words).astype(BF16)

        def up_task(c0):
            chunks = range(c0, c0 + UP_CHUNKS_PER_TASK)

            def matmul():
                return [jnp.dot(x_ref[:n_rows, :], w1p_ref[c], preferred_element_type=F32) for c in chunks]

            def epilogue(results):
                gates, ups = [], []
                for c, r in zip(chunks, results):
                    h = r + b1_ref[0, :, c * MXU_DIM:(c + 1) * MXU_DIM]
                    gates.append(jnp.minimum(h[:, :LANES], SWIGLU_LIMIT))
                    ups.append(jnp.clip(h[:, LANES:], -SWIGLU_LIMIT, SWIGLU_LIMIT))
                gate = jnp.concatenate(gates, axis=1)
                up = jnp.concatenate(ups, axis=1)
                glu = gate * _sigmoid(SWIGLU_ALPHA * gate)
                act_ref[:n_rows, c0 * LANES:(c0 + UP_CHUNKS_PER_TASK) * LANES] = ((up + 1.0) * glu).astype(BF16)
            return matmul, epilogue

        _run_lookahead([up_task(c0) for c0 in range(0, n_chunks, UP_CHUNKS_PER_TASK)])

        def down_task(c):
            cols = slice(c * MXU_DIM, (c + 1) * MXU_DIM)
            half, high = c % 2, c // 2

            def epilogue(r):
                bits = lax.bitcast_convert_type((r + b2_ref[0, :, cols]).astype(BF16).astype(F32), jnp.uint32)
                if high:
                    yout[half] = yout[half] | (bits & jnp.uint32(0xFFFF0000))
                else:
                    yout[half] = bits >> 16
            return (lambda: jnp.dot(act_ref[:n_rows, :], w2b_ref[c], preferred_element_type=F32)), epilogue

        _run_lookahead([down_task(c) for c in range(D_MODEL // MXU_DIM)])

    def block(j, _):
        slot = j % 2
        for half in range(2):
            in_copy(j, slot, half).wait()

        @pl.when(j + 1 < n_blk)
        def _():
            for half in range(2):
                in_copy(j + 1, 1 - slot, half).start(priority=ROW_DMA_PRIORITY)

        @pl.when(j >= 2)
        def _():
            for half in range(2):
                out_copy(j - 2, slot, half).wait()

        mlp_rows(EXPERT_BLOCK, xin_ref.at[slot], yout_ref.at[slot])
        for half in range(2):
            out_copy(j, slot, half).start(priority=ROW_DMA_PRIORITY)
        return 0

    lax.fori_loop(0, n_blk, block, 0)

    for back in (2, 1):
        @pl.when(n_blk >= back)
        def _(back=back):
            j = n_blk - back
            for half in range(2):
                out_copy(j, j % 2, half).wait()


def _gate_up_order():
    j = jnp.arange(MXU_DIM)
    within = jnp.where(j < LANES, 2 * j, 2 * (j - LANES) + 1)
    return within


def _experts(xs_lo, xs_hi, w1, b1, w2, b2, counts, region):
    within = _gate_up_order()
    perm = (jnp.arange(MXU_DIM)[:, None] == within[None, :]).astype(BF16)
    order = (jnp.arange(0, 2 * D_FF, MXU_DIM)[:, None] + within[None, :]).reshape(-1)
    b1p = b1.astype(F32)[:, order][:, None, :]
    b2r = b2.astype(F32)[:, None, :]
    tb = EXPERT_BLOCK
    hbm = pl.BlockSpec(memory_space=pl.ANY)
    wspec = lambda shape: pl.BlockSpec((1,) + shape, lambda e, cnt: (e, 0, 0))
    grid_spec = pltpu.PrefetchScalarGridSpec(
        num_scalar_prefetch=1,
        grid=(N_EXPERTS,),
        in_specs=[hbm, hbm, wspec((D_MODEL, 2 * D_FF)), wspec((1, 2 * D_FF)),
                  wspec((D_FF, D_MODEL)), wspec((1, D_MODEL)),
                  pl.BlockSpec((MXU_DIM, MXU_DIM), lambda e, cnt: (0, 0))],
        out_specs=[hbm, hbm],
        scratch_shapes=[pltpu.VMEM((2 * D_FF // MXU_DIM, D_MODEL, MXU_DIM), BF16),
                        pltpu.VMEM((D_MODEL // MXU_DIM, D_FF, MXU_DIM), BF16),
                        pltpu.VMEM((tb, D_MODEL), BF16), pltpu.VMEM((tb, D_FF), BF16),
                        pltpu.VMEM((2, 2, tb, PACK_HALF), jnp.uint32), pltpu.VMEM((2, 2, tb, PACK_HALF), jnp.uint32),
                        pltpu.SemaphoreType.DMA((2, 2)), pltpu.SemaphoreType.DMA((2, 2))],
    )
    return pl.pallas_call(
        functools.partial(_expert_kernel, region=region),
        grid_spec=grid_spec,
        out_shape=[jax.ShapeDtypeStruct(xs_lo.shape, jnp.uint32)] * 2,
        compiler_params=_cparams(("arbitrary",)),
        name="experts",
    )(counts, xs_lo, xs_hi, w1, b1p, w2, b2r, perm)


def _combine_kernel(x1_ref, gate_ref, *refs):
    lo_refs, hi_refs, o_ref = refs[:TOP_K], refs[TOP_K:2 * TOP_K], refs[-1]
    acc = x1_ref[...]
    gates = gate_ref[...]
    for kk in range(TOP_K):
        words = jnp.concatenate([lo_refs[kk][...], hi_refs[kk][...]], axis=1)
        acc = acc + gates[:, kk:kk + 1] * _unpack_rows(words)
    o_ref[...] = acc


def _combine_part(x1, gates, yg_lo, yg_hi, part, n_parts, out_so_far):
    n_tok = x1.shape[0]
    part_tok = n_tok // n_parts
    tm = min(ROW_TILE, part_tok)
    nblk = part_tok // tm
    row = lambda w: pl.BlockSpec((tm, w), lambda i: (part * nblk + i, 0))
    plane = lambda kk: pl.BlockSpec((tm, PACK_HALF), lambda i, kk=kk: (kk * nblk + i, 0))
    planes = [plane(kk) for kk in range(TOP_K)]
    operands = [x1, gates, *([yg_lo] * TOP_K), *([yg_hi] * TOP_K)]
    in_specs = [row(D_MODEL), row(LANES)] + planes + planes
    aliases = {}
    if out_so_far is not None:
        aliases = {len(operands): 0}
        operands.append(out_so_far)
        in_specs.append(pl.BlockSpec(memory_space=pl.ANY))
    return pl.pallas_call(
        _combine_kernel,
        grid=(nblk,),
        in_specs=in_specs,
        out_specs=row(D_MODEL),
        out_shape=jax.ShapeDtypeStruct((n_tok, D_MODEL), F32),
        input_output_aliases=aliases,
        compiler_params=_cparams(("parallel",)),
        name="combine",
    )(*operands)


def kernel(x, norm1_gain, w_in, lambda_re, lambda_im, log_dt, ssm_b_re, ssm_b_im, ssm_c_re, ssm_c_im, ssm_d, w_glu, b_glu, q_norm_gain, k_norm_gain, lambda_q1, lambda_k1, lambda_q2, lambda_k2, subln_gain, w_proj_ssm, w_proj_attn, w_out, norm2_gain, w_router, b_router, w_exp1, b_exp1, w_exp2, b_exp2):
    bsz, seq, d = x.shape
    n_tok = bsz * seq
    depth = norm1_gain.shape[0]
    row1 = lambda a: a.astype(F32).reshape(1, -1)
    for l in range(depth):
        lambda_init = 0.8 - 0.6 * math.exp(-0.3 * l)
        x2d = x.reshape(n_tok, d)

        u, q, k, vt, gs, ga = _in_proj(x2d, row1(norm1_gain[l]), w_in[l], q_norm_gain[l], k_norm_gain[l])

        bblk, cblk, a_tile = _ssm_params(lambda_re[l], lambda_im[l], log_dt[l], ssm_b_re[l], ssm_b_im[l],
                                         ssm_c_re[l], ssm_c_im[l])
        so = _ssm(u.reshape(bsz, seq, SSM_WIDTH), bblk, cblk, a_tile, row1(ssm_d[l]),
                  w_glu[l].astype(BF16), row1(b_glu[l])).reshape(n_tok, SSM_WIDTH)

        lam = (jnp.exp(jnp.sum(lambda_q1[l].astype(F32) * lambda_k1[l].astype(F32)))
               - jnp.exp(jnp.sum(lambda_q2[l].astype(F32) * lambda_k2[l].astype(F32)))
               + lambda_init).reshape(1)
        ao = _attention(q, k, vt, lam, subln_gain[l], lambda_init, bsz, seq)

        region = n_tok
        x1, h_lo, h_hi, route, gates, counts = _merge_route(
            so, ao, gs, ga, x2d, w_proj_ssm[l].astype(BF16), w_proj_attn[l].astype(BF16),
            w_out[l].astype(BF16), row1(norm2_gain[l]), w_router[l], b_router[l], region)

        dest = route[:TOP_K]
        n_slots = N_EXPERTS * region
        xs_lo = _sc_scatter_rows(h_lo, dest, n_slots)
        xs_hi = _sc_scatter_rows(h_hi, dest, n_slots)

        ys_lo, ys_hi = _experts(xs_lo, xs_hi, w_exp1[l], b_exp1[l], w_exp2[l], b_exp2[l], counts[:, 0], region)

        part_tok = n_tok // COLLECT_PARTS
        out = None
        for part in range(COLLECT_PARTS):
            flat = dest[:, part * part_tok:(part + 1) * part_tok].reshape(1, TOP_K * part_tok)
            out = _combine_part(x1, gates, _sc_gather_rows(ys_lo, flat), _sc_gather_rows(ys_hi, flat),
                                part, COLLECT_PARTS, out)
        x = out.reshape(bsz, seq, d)
    return x
```

```python
import functools
import math

import jax
import jax.numpy as jnp
from jax import lax
from jax.experimental import pallas as pl
from jax.experimental.pallas import tpu as pltpu
from jax.experimental.pallas import tpu_sc as plsc

F32 = jnp.float32
BF16 = jnp.bfloat16

D_MODEL = 1024
NORM_EPS = 1e-5
SSM_WIDTH = 512
SSM_GROUP = 16
SSM_GROUPS = 32
SSM_STATE = 64
N_STATE = SSM_GROUPS * SSM_STATE
HEADS = 8
HEAD_DIM = 64
N_EXPERTS = 32
TOP_K = 4
D_FF = 1024
SWIGLU_ALPHA = 1.702
SWIGLU_LIMIT = 7.0

LANES = 128
SUBLANES = 8
MXU_DIM = 256
VMEM_LIMIT = 56 * 1024 * 1024

ROW_TILE = 512
SSM_CHUNK = 128
SCAN_LANES = 1024
ATTN_BLOCK = 256
ATTN_HEADS_PER_STEP = 8
ROWSUM_ROWS = 16
EXPERT_BLOCK = 512
UP_CHUNKS_PER_TASK = 2
ROW_DMA_PRIORITY = 1
COLLECT_PARTS = 1
SC_WINDOW = 128
PACK_W = D_MODEL // 2
PACK_HALF = PACK_W // 2

_NEG = -1e30
Q_SCALE = math.log2(math.e) / math.sqrt(HEAD_DIM)


def _cparams(sem):
    return pltpu.CompilerParams(dimension_semantics=sem, vmem_limit_bytes=VMEM_LIMIT)


def _full(shape):
    nd = len(shape)
    return pl.BlockSpec(shape, lambda *_: (0,) * nd)


def _sigmoid(x):
    return 0.5 * jnp.tanh(0.5 * x) + 0.5


def _run_lookahead(tasks):
    pending = tasks[0][0]()
    for i, (_, epilogue) in enumerate(tasks):
        result = pending
        if i + 1 < len(tasks):
            pending = tasks[i + 1][0]()
        epilogue(result)


_IN_CHUNKS = {"u": (0, SSM_WIDTH // MXU_DIM)}
for _name in ("q", "k", "v", "gs", "ga"):
    _start = max(first + count for first, count in _IN_CHUNKS.values())
    _IN_CHUNKS[_name] = (_start, D_MODEL // MXU_DIM)
N_IN_CHUNKS = max(first + count for first, count in _IN_CHUNKS.values())


def _prep_w_in_kernel(w_ref, wc_ref, wvt_ref):
    c = pl.program_id(0)
    w = w_ref[...]
    wc_ref[0] = w.astype(BF16)
    v_first, v_count = _IN_CHUNKS["v"]

    @pl.when((c >= v_first) & (c < v_first + v_count))
    def _():
        wvt_ref[0] = w.T.astype(BF16)


def _prep_w_in(w_in):
    v_first, v_count = _IN_CHUNKS["v"]
    return pl.pallas_call(
        _prep_w_in_kernel,
        grid=(N_IN_CHUNKS,),
        in_specs=[pl.BlockSpec((D_MODEL, MXU_DIM), lambda c: (0, c))],
        out_specs=[pl.BlockSpec((1, D_MODEL, MXU_DIM), lambda c: (c, 0, 0)),
                   pl.BlockSpec((1, MXU_DIM, D_MODEL), lambda c: (jnp.clip(c - v_first, 0, v_count - 1), 0, 0))],
        out_shape=[jax.ShapeDtypeStruct((N_IN_CHUNKS, D_MODEL, MXU_DIM), BF16),
                   jax.ShapeDtypeStruct((v_count, MXU_DIM, D_MODEL), BF16)],
        compiler_params=_cparams(("arbitrary",)),
        name="prep_w_in",
    )(w_in)


def _inproj_kernel(x_ref, g1_ref, w_ref, wvt_ref, qg_ref, kg_ref, seg_ref,
                   u_ref, q_ref, k_ref, vt_ref, gs_ref, ga_ref):
    x = x_ref[...]
    ms = jnp.mean(x * x, axis=-1, keepdims=True)
    h = (x * lax.rsqrt(ms + NORM_EPS) * g1_ref[...]).astype(BF16)

    def proj(name, c):
        chunk = _IN_CHUNKS[name][0] + c
        return lambda: jnp.dot(h, w_ref[chunk], preferred_element_type=F32)

    seg = seg_ref[...]
    tasks = []

    def plain_task(name, out_ref, c, fn):
        cols = slice(c * MXU_DIM, (c + 1) * MXU_DIM)

        def epilogue(r):
            out_ref[:, cols] = fn(r)
        return proj(name, c), epilogue

    tasks += [plain_task("u", u_ref, c, lambda r: r) for c in range(_IN_CHUNKS["u"][1])]

    def head_norm_tasks(name, gain_ref, out_ref, scale, c):
        cols = slice(c * MXU_DIM, (c + 1) * MXU_DIM)
        kept = {}

        def after_proj(y):
            kept["y"] = y
            kept["sq"] = (y * y).astype(BF16)

        def after_sum(ss):
            yn = kept["y"] * lax.rsqrt(ss * (1.0 / HEAD_DIM) + NORM_EPS) * gain_ref[:, cols]
            out_ref[:, cols] = (yn * scale).astype(BF16)

        return ((proj(name, c), after_proj),
                (lambda: jnp.dot(kept["sq"], seg, preferred_element_type=F32), after_sum))

    pairs = [head_norm_tasks(name, gain_ref, out_ref, scale, c)
             for name, gain_ref, out_ref, scale in (("q", qg_ref, q_ref, Q_SCALE), ("k", kg_ref, k_ref, 1.0))
             for c in range(_IN_CHUNKS[name][1])]
    tasks.append(pairs[0][0])
    for prev, cur in zip(pairs, pairs[1:]):
        tasks += [cur[0], prev[1]]
    tasks.append(pairs[-1][1])

    to_gate = lambda r: _sigmoid(r).astype(BF16)
    for c in range(_IN_CHUNKS["gs"][1]):
        tasks.append(plain_task("gs", gs_ref, c, to_gate))
        tasks.append(plain_task("ga", ga_ref, c, to_gate))

    def vt_task(c):
        def matmul():
            return lax.dot_general(wvt_ref[c], h, (((1,), (1,)), ((), ())), preferred_element_type=F32)

        def epilogue(r):
            vt_ref[c * MXU_DIM:(c + 1) * MXU_DIM, :] = r.astype(BF16)
        return matmul, epilogue

    tasks += [vt_task(c) for c in range(_IN_CHUNKS["v"][1])]
    _run_lookahead(tasks)


def _in_proj(x2d, gain1, w_in, q_gain, k_gain):
    n_tok = x2d.shape[0]
    tm = min(ROW_TILE, n_tok)
    w_chunks, w_vt = _prep_w_in(w_in)
    seg = (jnp.arange(MXU_DIM)[:, None] // HEAD_DIM == jnp.arange(MXU_DIM)[None, :] // HEAD_DIM).astype(BF16)
    reps = D_MODEL // HEAD_DIM
    qg = jnp.tile(q_gain.astype(F32), reps)[None, :]
    kg = jnp.tile(k_gain.astype(F32), reps)[None, :]
    row = lambda w: pl.BlockSpec((tm, w), lambda i: (i, 0))
    tok = jax.ShapeDtypeStruct((n_tok, D_MODEL), BF16)
    out_shapes = [jax.ShapeDtypeStruct((n_tok, SSM_WIDTH), F32), tok, tok,
                  jax.ShapeDtypeStruct((D_MODEL, n_tok), BF16), tok, tok]
    vt_spec = pl.BlockSpec((D_MODEL, tm), lambda i: (0, i))
    return pl.pallas_call(
        _inproj_kernel,
        grid=(n_tok // tm,),
        in_specs=[row(D_MODEL), _full((1, D_MODEL)), _full(w_chunks.shape), _full(w_vt.shape),
                  _full((1, D_MODEL)), _full((1, D_MODEL)), _full((MXU_DIM, MXU_DIM))],
        out_specs=[row(SSM_WIDTH), row(D_MODEL), row(D_MODEL), vt_spec, row(D_MODEL), row(D_MODEL)],
        out_shape=out_shapes,
        compiler_params=_cparams(("parallel",)),
        name="in_proj",
    )(x2d, gain1, w_chunks, w_vt, qg, kg, seg)


def _ssm_kernel(u_ref, bblk_ref, a_ref, cblk_ref, d_ref, wglu_ref, bglu_ref, o_ref, bu_ref, st_ref, tb_ref):
    n_batch, chunk = u_ref.shape[0], u_ref.shape[1]

    @pl.when(pl.program_id(0) == 0)
    def _():
        st_ref[...] = jnp.zeros_like(st_ref)

    n_planes = SSM_WIDTH // LANES
    for b in range(n_batch):
        for j in range(n_planes):
            tb_ref[j, pl.ds(b, chunk, stride=n_batch), :] = u_ref[b, :, j * LANES:(j + 1) * LANES]
    u = jnp.concatenate([tb_ref[j] for j in range(n_planes)], axis=1)
    u_bf = u.astype(BF16)
    tiles_per_part = N_STATE // MXU_DIM
    ch_per_tile = SSM_WIDTH // tiles_per_part
    n_groups = N_STATE // SCAN_LANES
    tiles_per_group = SCAN_LANES // MXU_DIM
    assert SCAN_LANES * SSM_WIDTH == N_STATE * MXU_DIM

    def input_tiles(g):
        for part in range(2):
            for t in range(tiles_per_group):
                tile = g * tiles_per_group + t
                ch0 = (tile * ch_per_tile) // LANES * LANES
                lanes = slice(part * N_STATE + tile * MXU_DIM, part * N_STATE + (tile + 1) * MXU_DIM)
                bu_ref[:, lanes] = jnp.dot(u_bf[:, ch0:ch0 + LANES], bblk_ref[ch0:ch0 + LANES, lanes],
                                           preferred_element_type=F32)

    def scan(g):
        re = slice(g * SCAN_LANES, (g + 1) * SCAN_LANES)
        im = slice(N_STATE + g * SCAN_LANES, N_STATE + (g + 1) * SCAN_LANES)
        ar, ai = a_ref[:, re], a_ref[:, im]
        xr, xi = st_ref[:, re], st_ref[:, im]
        for t in range(chunk):
            rows = slice(t * SUBLANES, (t + 1) * SUBLANES)
            xr, xi = ar * xr - ai * xi + bu_ref[rows, re], ar * xi + ai * xr + bu_ref[rows, im]
            bu_ref[rows, re] = xr
            bu_ref[rows, im] = xi
        st_ref[:, re] = xr
        st_ref[:, im] = xi

    def output_tile(g):
        cols = slice(g * MXU_DIM, (g + 1) * MXU_DIM)
        acc = None
        for part in range(2):
            lanes = slice(part * N_STATE + g * SCAN_LANES, part * N_STATE + (g + 1) * SCAN_LANES)
            term = jnp.dot(bu_ref[:, lanes].astype(BF16), cblk_ref[lanes, cols], preferred_element_type=F32)
            acc = term if acc is None else acc + term
        return acc

    input_tiles(0)
    ys = []
    for g in range(n_groups):
        if g + 1 < n_groups:
            input_tiles(g + 1)
        scan(g)
        ys.append(output_tile(g))
    y = jnp.concatenate(ys, axis=1) + d_ref[...] * u
    z = jax.nn.gelu(y)
    gate = _sigmoid(jnp.dot(z.astype(BF16), wglu_ref[...], preferred_element_type=F32) + bglu_ref[...])
    out = z * gate
    for j in range(n_planes):
        tb_ref[j] = out[:, j * LANES:(j + 1) * LANES]
    for b in range(n_batch):
        for j in range(n_planes):
            o_ref[b, :, j * LANES:(j + 1) * LANES] = tb_ref[j, pl.ds(b, chunk, stride=n_batch), :].astype(BF16)


def _ssm_params(lambda_re, lambda_im, log_dt, b_re, b_im, c_re, c_im):
    dt = jnp.exp(log_dt.astype(F32))[:, None]
    lr = jnp.minimum(lambda_re.astype(F32), -1e-4)
    li = lambda_im.astype(F32)
    mag = jnp.exp(lr * dt)
    abar_re = mag * jnp.cos(li * dt)
    abar_im = mag * jnp.sin(li * dt)
    den = lr * lr + li * li
    nr = abar_re - 1.0
    coef_re = (nr * lr + abar_im * li) / den
    coef_im = (abar_im * lr - nr * li) / den
    br = b_re.astype(F32)
    bi = b_im.astype(F32)
    bbar_re = coef_re[..., None] * br - coef_im[..., None] * bi
    bbar_im = coef_re[..., None] * bi + coef_im[..., None] * br
    ch_group = jnp.arange(SSM_WIDTH) // SSM_GROUP
    lane_group = jnp.arange(N_STATE) // SSM_STATE

    def expand_b(b):
        rows = b.transpose(0, 2, 1).reshape(SSM_WIDTH, SSM_STATE)
        return jnp.where(ch_group[:, None] == lane_group[None, :], jnp.tile(rows, (1, SSM_GROUPS)), 0.0)

    def expand_c(c):
        rows = c.transpose(0, 2, 1).reshape(N_STATE, SSM_GROUP)
        return jnp.where(lane_group[:, None] == ch_group[None, :], jnp.tile(rows, (1, SSM_GROUPS)), 0.0)

    bblk = jnp.concatenate([expand_b(bbar_re), expand_b(bbar_im)], axis=1).astype(BF16)
    cblk = jnp.concatenate([expand_c(c_re.astype(F32)), -expand_c(c_im.astype(F32))], axis=0).astype(BF16)
    a_row = jnp.concatenate([abar_re.reshape(-1), abar_im.reshape(-1)])[None, :]
    return bblk, cblk, jnp.broadcast_to(a_row, (SUBLANES, 2 * N_STATE))


def _ssm(u, bblk, cblk, a_tile, d_skip, w_glu_bf, b_glu):
    bsz, seq, _ = u.shape
    assert bsz == SUBLANES
    chunk = min(SSM_CHUNK, seq)
    rows = chunk * SUBLANES
    tok_spec = pl.BlockSpec((bsz, chunk, SSM_WIDTH), lambda c: (0, c, 0))
    return pl.pallas_call(
        _ssm_kernel,
        grid=(seq // chunk,),
        in_specs=[tok_spec, _full(bblk.shape), _full(a_tile.shape), _full(cblk.shape),
                  _full((1, SSM_WIDTH)), _full(w_glu_bf.shape), _full((1, SSM_WIDTH))],
        out_specs=tok_spec,
        out_shape=jax.ShapeDtypeStruct(u.shape, BF16),
        scratch_shapes=[pltpu.VMEM((rows, 2 * N_STATE), F32), pltpu.VMEM((SUBLANES, 2 * N_STATE), F32),
                        pltpu.VMEM((SSM_WIDTH // LANES, rows, LANES), F32)],
        compiler_params=_cparams(("arbitrary",)),
        name="ssm",
    )(u, bblk, a_tile, cblk, d_skip, w_glu_bf, b_glu)


def _attn_kernel(lam_ref, sg_ref, q_ref, k_ref, vt_ref, o_ref, acc_ref, vta_ref, *, out_scale, blk, nh):
    seq = q_ref.shape[1]
    hw = 2 * HEAD_DIM
    lane = lax.broadcasted_iota(jnp.int32, (blk, hw), 1)
    key_i = lax.broadcasted_iota(jnp.int32, (blk, blk), 0)
    qry_i = lax.broadcasted_iota(jnp.int32, (blk, blk), 1)
    keep = key_i <= qry_i
    contract_last = (((1,), (1,)), ((), ()))
    lam = lam_ref[0]
    n_chain = 2 * nh

    for hh in range(nh):
        vta_ref[hh, :hw, :] = vt_ref[hh * hw:(hh + 1) * hw, :]
        vta_ref[hh, hw:, :] = jnp.ones((ROWSUM_ROWS, seq), BF16)

    def q_block(qi, _):
        qrows = pl.ds(pl.multiple_of(qi * blk, blk), blk)
        qs = []
        for hh in range(nh):
            q = q_ref[0, qrows, hh * hw:(hh + 1) * hw]
            zero = jnp.zeros_like(q)
            qs += [jnp.where(lane < HEAD_DIM, q, zero), jnp.where(lane >= HEAD_DIM, q, zero)]
        acc_ref[...] = jnp.zeros_like(acc_ref)

        def kv_block(kb, carry, masked):
            krows = pl.ds(pl.multiple_of(kb * blk, blk), blk)
            scores = [lax.dot_general(k_ref[0, krows, (c // 2) * hw:(c // 2 + 1) * hw], qs[c], contract_last,
                                      preferred_element_type=F32) for c in range(n_chain)]
            out = []
            for c in range(n_chain):
                m = carry[c]
                s = jnp.where(keep, scores[c], _NEG) if masked else scores[c]
                m_new = jnp.maximum(m, jnp.max(s, axis=0, keepdims=True))
                p = jnp.exp2(s - m_new).astype(BF16)
                alpha = jnp.exp2(m - m_new)
                out.append(m_new)
                vt = vta_ref[c // 2, :, krows]
                acc_ref[c] = alpha * acc_ref[c] + jnp.dot(vt, p, preferred_element_type=F32)
            return tuple(out)

        carry = (jnp.full((1, blk), _NEG, F32),) * n_chain
        carry = lax.fori_loop(0, qi, lambda kb, c: kv_block(kb, c, False), carry)
        kv_block(qi, carry, True)
        for hh in range(nh):
            a1, a2 = acc_ref[2 * hh], acc_ref[2 * hh + 1]
            l1, l2 = a1[hw:hw + 1, :], a2[hw:hw + 1, :]
            ot = a1[:hw, :] * (1.0 / l1) - a2[:hw, :] * (lam / l2)
            ot = ot * lax.rsqrt(jnp.mean(ot * ot, axis=0, keepdims=True) + NORM_EPS)
            o_ref[0, qrows, hh * hw:(hh + 1) * hw] = (ot.T * sg_ref[...] * out_scale).astype(BF16)
        return 0

    lax.fori_loop(0, seq // blk, q_block, 0)


def _attention(q, k, vt, lam, subln_gain, lambda_init, bsz, seq):
    blk = min(ATTN_BLOCK, seq)
    nh = ATTN_HEADS_PER_STEP
    hw = 2 * HEAD_DIM
    q3, k3 = (a.reshape(bsz, seq, D_MODEL) for a in (q, k))
    tok_spec = pl.BlockSpec((1, seq, nh * hw), lambda b, h: (b, 0, h))
    out = pl.pallas_call(
        functools.partial(_attn_kernel, out_scale=1.0 - lambda_init, blk=blk, nh=nh),
        grid=(bsz, HEADS // nh),
        in_specs=[pl.BlockSpec(memory_space=pltpu.SMEM), _full((1, hw)), tok_spec, tok_spec,
                  pl.BlockSpec((nh * hw, seq), lambda b, h: (h, b))],
        out_specs=tok_spec,
        out_shape=jax.ShapeDtypeStruct((bsz, seq, D_MODEL), BF16),
        scratch_shapes=[pltpu.VMEM((2 * nh, hw + ROWSUM_ROWS, blk), F32),
                        pltpu.VMEM((nh, hw + ROWSUM_ROWS, seq), BF16)],
        compiler_params=_cparams(("parallel", "parallel")),
        name="diff_attn",
    )(lam, subln_gain.astype(F32)[None, :], q3, k3, vt)
    return out.reshape(bsz * seq, D_MODEL)


def _pack_rows(y):
    bits = lax.bitcast_convert_type(y.astype(BF16).astype(F32), jnp.uint32)
    return (bits[:, :PACK_W] >> 16) | (bits[:, PACK_W:] & jnp.uint32(0xFFFF0000))


def _unpack_rows(w):
    lo = lax.bitcast_convert_type(w << 16, F32)
    hi = lax.bitcast_convert_type(w & jnp.uint32(0xFFFF0000), F32)
    return jnp.concatenate([lo, hi], axis=1)


def _merge_kernel(so_ref, ao_ref, gs_ref, ga_ref, x_ref, wps_ref, wpa_ref, wo_ref, g2_ref,
                  wrh_ref, wrl_ref, br_ref, tri_ref,
                  x1_ref, hlo_ref, hhi_ref, route_ref, gate_ref, cnt_ref, run_ref, *, region):
    @pl.when(pl.program_id(0) == 0)
    def _():
        run_ref[...] = jnp.zeros_like(run_ref)

    tm = x_ref.shape[0]
    n_part = 2 if tm % (2 * MXU_DIM) == 0 else 1
    rows_per = tm // n_part
    expert_f = lax.broadcasted_iota(jnp.int32, (N_EXPERTS, rows_per), 0).astype(F32)
    slot_row = lax.broadcasted_iota(jnp.int32, (SUBLANES, rows_per), 0)
    contract_last = (((1,), (1,)), ((), ()))
    parts = [dict(rows=slice(p * rows_per, (p + 1) * rows_per)) for p in range(n_part)]

    def stage_proj(st):
        def matmul():
            return (jnp.dot(so_ref[st["rows"], :], wps_ref[...], preferred_element_type=F32),
                    jnp.dot(ao_ref[st["rows"], :], wpa_ref[...], preferred_element_type=F32))

        def epilogue(r):
            ps, pa = r
            merged = gs_ref[st["rows"], :].astype(F32) * ps + ga_ref[st["rows"], :].astype(F32) * pa
            st["merged"] = merged.astype(BF16)
        return matmul, epilogue

    def stage_out(st):
        def matmul():
            return jnp.dot(st["merged"], wo_ref[...], preferred_element_type=F32)

        def epilogue(r):
            x1 = x_ref[st["rows"], :] + r
            x1_ref[st["rows"], :] = x1
            h2 = x1 * lax.rsqrt(jnp.mean(x1 * x1, axis=-1, keepdims=True) + NORM_EPS) * g2_ref[...]
            words = _pack_rows(h2)
            hlo_ref[st["rows"], :] = words[:, :PACK_HALF]
            hhi_ref[st["rows"], :] = words[:, PACK_HALF:]
            st["h_hi"] = h2.astype(BF16)
            st["h_lo"] = (h2 - st["h_hi"].astype(F32)).astype(BF16)
        return matmul, epilogue

    def stage_router(st):
        def matmul():
            return (lax.dot_general(wrh_ref[...], st["h_hi"], contract_last, preferred_element_type=F32)
                    + lax.dot_general(wrh_ref[...], st["h_lo"], contract_last, preferred_element_type=F32)
                    + lax.dot_general(wrl_ref[...], st["h_hi"], contract_last, preferred_element_type=F32))

        def epilogue(r):
            work = r + br_ref[...]
            onehots, vals, ids = [], [], []
            for _ in range(TOP_K):
                m = jnp.max(work, axis=0, keepdims=True)
                idx = jnp.min(jnp.where(work == m, expert_f, float(N_EXPERTS)), axis=0, keepdims=True)
                oh = expert_f == idx
                onehots.append(oh)
                vals.append(m)
                ids.append(idx.astype(jnp.int32))
                work = jnp.where(oh, -jnp.inf, work)
            exps = [jnp.exp(v - vals[0]) for v in vals]
            den = exps[0] + exps[1] + exps[2] + exps[3]
            st.update(onehots=onehots, ids=ids, gates=[e / den for e in exps],
                      multi=(onehots[0] | onehots[1] | onehots[2] | onehots[3]).astype(F32))
        return matmul, epilogue

    _run_lookahead([stage(st) for stage in (stage_proj, stage_out, stage_router) for st in parts])

    multi = jnp.concatenate([st["multi"] for st in parts], axis=1)
    before = jnp.dot(multi.astype(BF16), tri_ref[...], preferred_element_type=F32) + run_ref[...]
    gate_row = lax.broadcasted_iota(jnp.int32, (LANES, rows_per), 0)
    for st in parts:
        route = jnp.zeros((SUBLANES, rows_per), jnp.int32)
        gates = jnp.zeros((LANES, rows_per), F32)
        for kk in range(TOP_K):
            rank = jnp.sum(jnp.where(st["onehots"][kk], before[:, st["rows"]], 0.0), axis=0, keepdims=True)
            route = jnp.where(slot_row == kk, st["ids"][kk] * region + rank.astype(jnp.int32), route)
            gates = jnp.where(gate_row == kk, st["gates"][kk], gates)
        route_ref[:, st["rows"]] = route
        gate_ref[st["rows"], :] = gates.T
    run = run_ref[...] + jnp.sum(multi, axis=1, keepdims=True)
    run_ref[...] = run
    cnt_ref[...] = jnp.broadcast_to(run, cnt_ref.shape).astype(jnp.int32)


def _merge_route(so, ao, gs, ga, x2d, wps, wpa, wo, gain2, w_router, b_router, region):
    n_tok = x2d.shape[0]
    tm = min(ROW_TILE, n_tok)
    wr = w_router.astype(F32).T
    wr_hi = wr.astype(BF16)
    wr_lo = (wr - wr_hi.astype(F32)).astype(BF16)
    br = b_router.astype(F32)[:, None]
    tri = (jnp.arange(tm)[:, None] < jnp.arange(tm)[None, :]).astype(BF16)
    row = lambda w: pl.BlockSpec((tm, w), lambda i: (i, 0))
    slots = pl.BlockSpec((SUBLANES, tm), lambda i: (0, i))
    out_shapes = [jax.ShapeDtypeStruct((n_tok, D_MODEL), F32),
                  jax.ShapeDtypeStruct((n_tok, PACK_HALF), jnp.uint32),
                  jax.ShapeDtypeStruct((n_tok, PACK_HALF), jnp.uint32),
                  jax.ShapeDtypeStruct((SUBLANES, n_tok), jnp.int32),
                  jax.ShapeDtypeStruct((n_tok, LANES), F32),
                  jax.ShapeDtypeStruct((N_EXPERTS, LANES), jnp.int32)]
    return pl.pallas_call(
        functools.partial(_merge_kernel, region=region),
        grid=(n_tok // tm,),
        in_specs=[row(SSM_WIDTH), row(D_MODEL), row(D_MODEL), row(D_MODEL), row(D_MODEL),
                  _full(wps.shape), _full(wpa.shape), _full(wo.shape), _full((1, D_MODEL)),
                  _full(wr_hi.shape), _full(wr_lo.shape), _full((N_EXPERTS, 1)), _full((tm, tm))],
        out_specs=[row(D_MODEL), row(PACK_HALF), row(PACK_HALF), slots, row(LANES), _full((N_EXPERTS, LANES))],
        out_shape=out_shapes,
        scratch_shapes=[pltpu.VMEM((N_EXPERTS, 1), F32)],
        compiler_params=_cparams(("arbitrary",)),
        name="merge_route",
    )(so, ao, gs, ga, x2d, wps, wpa, wo, gain2, wr_hi, wr_lo, br, tri)


def _sc_scatter_rows(rows_lo, rows_hi, dest, n_slots):
    n_tok, width = rows_lo.shape
    mesh = plsc.VectorSubcoreMesh(core_axis_name="core", subcore_axis_name="subcore")
    out_type = jax.ShapeDtypeStruct((n_slots, width), rows_lo.dtype)

    @pl.kernel(out_type=(out_type, out_type), mesh=mesh, scratch_types=[])
    def scatter(lo_hbm, hi_hbm, dest_hbm, out_lo_hbm, out_hi_hbm):
        for rows_hbm, out_hbm in ((lo_hbm, out_lo_hbm), (hi_hbm, out_hi_hbm)):
            def body(rows_vmem, dest_vmem, out_hbm=out_hbm):
                for k in range(TOP_K):
                    pltpu.sync_copy(rows_vmem, out_hbm.at[dest_vmem.at[k]])

            pltpu.emit_pipeline(
                body,
                grid=(n_tok // SC_WINDOW,),
                in_specs=[pl.BlockSpec((SC_WINDOW, width), lambda i: (i, 0)),
                          pl.BlockSpec((TOP_K, SC_WINDOW), lambda i: (0, i))],
                out_specs=[],
                core_axis_name=("core", "subcore"),
                dimension_semantics=(pltpu.PARALLEL,),
            )(rows_hbm, dest_hbm)

    return scatter(rows_lo, rows_hi, dest)


def _sc_gather_rows(table_lo, table_hi, idx):
    n = idx.shape[1]
    width = table_lo.shape[1]
    mesh = plsc.VectorSubcoreMesh(core_axis_name="core", subcore_axis_name="subcore")
    out_type = jax.ShapeDtypeStruct((n, width), table_lo.dtype)

    @pl.kernel(out_type=(out_type, out_type), mesh=mesh, scratch_types=[])
    def gather(lo_hbm, hi_hbm, idx_hbm, out_lo_hbm, out_hi_hbm):
        for table_hbm, out_hbm in ((lo_hbm, out_lo_hbm), (hi_hbm, out_hi_hbm)):
            def body(idx_vmem, out_vmem, table_hbm=table_hbm):
                pltpu.sync_copy(table_hbm.at[idx_vmem.at[0]], out_vmem)

            pltpu.emit_pipeline(
                body,
                grid=(n // SC_WINDOW,),
                in_specs=[pl.BlockSpec((1, SC_WINDOW), lambda i: (0, i))],
                out_specs=[pl.BlockSpec((SC_WINDOW, width), lambda i: (i, 0))],
                core_axis_name=("core", "subcore"),
                dimension_semantics=(pltpu.PARALLEL,),
            )(idx_hbm, out_hbm)

    return gather(table_lo, table_hi, idx)


def _expert_kernel(cnt_ref, xlo_hbm, xhi_hbm, w1_ref, b1_ref, w2_ref, b2_ref, perm_ref, ylo_hbm, yhi_hbm,
                   w1p_ref, w2b_ref, x_ref, act_ref, xin_ref, yout_ref, in_sem, out_sem, *, region):
    e = pl.program_id(0)
    n_blk = (cnt_ref[e] + EXPERT_BLOCK - 1) // EXPERT_BLOCK
    base = e * region
    n_chunks = (2 * D_FF) // MXU_DIM
    x_hbm = (xlo_hbm, xhi_hbm)
    y_hbm = (ylo_hbm, yhi_hbm)

    def rows_of(j):
        return pl.ds(pl.multiple_of(base + j * EXPERT_BLOCK, EXPERT_BLOCK), EXPERT_BLOCK)

    def in_copy(j, slot, half):
        return pltpu.make_async_copy(x_hbm[half].at[rows_of(j), :], xin_ref.at[slot, half], in_sem.at[slot, half])

    def out_copy(j, slot, half):
        return pltpu.make_async_copy(yout_ref.at[slot, half], y_hbm[half].at[rows_of(j), :], out_sem.at[slot, half])

    @pl.when(n_blk > 0)
    def _():
        for half in range(2):
            in_copy(0, 0, half).start(priority=ROW_DMA_PRIORITY)

    perm = perm_ref[...]
    for c in range(n_chunks):
        cols = slice(c * MXU_DIM, (c + 1) * MXU_DIM)
        w1p_ref[c] = jnp.dot(w1_ref[0, :, cols].astype(BF16), perm, preferred_element_type=F32).astype(BF16)
    for c in range(D_MODEL // MXU_DIM):
        w2b_ref[c] = w2_ref[0, :, c * MXU_DIM:(c + 1) * MXU_DIM].astype(BF16)

    def mlp_rows(n_rows, xin, yout):
        words = jnp.concatenate([xin[0], xin[1]], axis=1)
        x_ref[:n_rows, :] = _unpack_rows(words).astype(BF16)

        def up_task(c0):
            chunks = range(c0, c0 + UP_CHUNKS_PER_TASK)

            def matmul():
                return [jnp.dot(x_ref[:n_rows, :], w1p_ref[c], preferred_element_type=F32) for c in chunks]

            def epilogue(results):
                gates, ups = [], []
                for c, r in zip(chunks, results):
                    h = r + b1_ref[0, :, c * MXU_DIM:(c + 1) * MXU_DIM]
                    gates.append(jnp.minimum(h[:, :LANES], SWIGLU_LIMIT))
                    ups.append(jnp.clip(h[:, LANES:], -SWIGLU_LIMIT, SWIGLU_LIMIT))
                gate = jnp.concatenate(gates, axis=1)
                up = jnp.concatenate(ups, axis=1)
                glu = gate * _sigmoid(SWIGLU_ALPHA * gate)
                act_ref[:n_rows, c0 * LANES:(c0 + UP_CHUNKS_PER_TASK) * LANES] = ((up + 1.0) * glu).astype(BF16)
            return matmul, epilogue

        _run_lookahead([up_task(c0) for c0 in range(0, n_chunks, UP_CHUNKS_PER_TASK)])

        def down_task(c):
            cols = slice(c * MXU_DIM, (c + 1) * MXU_DIM)
            half, high = c % 2, c // 2

            def epilogue(r):
                bits = lax.bitcast_convert_type((r + b2_ref[0, :, cols]).astype(BF16).astype(F32), jnp.uint32)
                if high:
                    yout[half] = yout[half] | (bits & jnp.uint32(0xFFFF0000))
                else:
                    yout[half] = bits >> 16
            return (lambda: jnp.dot(act_ref[:n_rows, :], w2b_ref[c], preferred_element_type=F32)), epilogue

        _run_lookahead([down_task(c) for c in range(D_MODEL // MXU_DIM)])

    def block(j, _):
        slot = j % 2
        for half in range(2):
            in_copy(j, slot, half).wait()

        @pl.when(j + 1 < n_blk)
        def _():
            for half in range(2):
                in_copy(j + 1, 1 - slot, half).start(priority=ROW_DMA_PRIORITY)

        @pl.when(j >= 2)
        def _():
            for half in range(2):
                out_copy(j - 2, slot, half).wait()

        mlp_rows(EXPERT_BLOCK, xin_ref.at[slot], yout_ref.at[slot])
        for half in range(2):
            out_copy(j, slot, half).start(priority=ROW_DMA_PRIORITY)
        return 0

    lax.fori_loop(0, n_blk, block, 0)

    for back in (2, 1):
        @pl.when(n_blk >= back)
        def _(back=back):
            j = n_blk - back
            for half in range(2):
                out_copy(j, j % 2, half).wait()


def _gate_up_order():
    j = jnp.arange(MXU_DIM)
    within = jnp.where(j < LANES, 2 * j, 2 * (j - LANES) + 1)
    return within


def _experts(xs_lo, xs_hi, w1, b1, w2, b2, counts, region):
    within = _gate_up_order()
    perm = (jnp.arange(MXU_DIM)[:, None] == within[None, :]).astype(BF16)
    order = (jnp.arange(0, 2 * D_FF, MXU_DIM)[:, None] + within[None, :]).reshape(-1)
    b1p = b1.astype(F32)[:, order][:, None, :]
    b2r = b2.astype(F32)[:, None, :]
    tb = EXPERT_BLOCK
    hbm = pl.BlockSpec(memory_space=pl.ANY)
    wspec = lambda shape: pl.BlockSpec((1,) + shape, lambda e, cnt: (e, 0, 0))
    grid_spec = pltpu.PrefetchScalarGridSpec(
        num_scalar_prefetch=1,
        grid=(N_EXPERTS,),
        in_specs=[hbm, hbm, wspec((D_MODEL, 2 * D_FF)), wspec((1, 2 * D_FF)),
                  wspec((D_FF, D_MODEL)), wspec((1, D_MODEL)),
                  pl.BlockSpec((MXU_DIM, MXU_DIM), lambda e, cnt: (0, 0))],
        out_specs=[hbm, hbm],
        scratch_shapes=[pltpu.VMEM((2 * D_FF // MXU_DIM, D_MODEL, MXU_DIM), BF16),
                        pltpu.VMEM((D_MODEL // MXU_DIM, D_FF, MXU_DIM), BF16),
                        pltpu.VMEM((tb, D_MODEL), BF16), pltpu.VMEM((tb, D_FF), BF16),
                        pltpu.VMEM((2, 2, tb, PACK_HALF), jnp.uint32), pltpu.VMEM((2, 2, tb, PACK_HALF), jnp.uint32),
                        pltpu.SemaphoreType.DMA((2, 2)), pltpu.SemaphoreType.DMA((2, 2))],
    )
    return pl.pallas_call(
        functools.partial(_expert_kernel, region=region),
        grid_spec=grid_spec,
        out_shape=[jax.ShapeDtypeStruct(xs_lo.shape, jnp.uint32)] * 2,
        compiler_params=_cparams(("arbitrary",)),
        name="experts",
    )(counts, xs_lo, xs_hi, w1, b1p, w2, b2r, perm)


def _combine_kernel(x1_ref, gate_ref, *refs):
    lo_refs, hi_refs, o_ref = refs[:TOP_K], refs[TOP_K:2 * TOP_K], refs[-1]
    acc = x1_ref[...]
    gates = gate_ref[...]
    for kk in range(TOP_K):
        words = jnp.concatenate([lo_refs[kk][...], hi_refs[kk][...]], axis=1)
        acc = acc + gates[:, kk:kk + 1] * _unpack_rows(words)
    o_ref[...] = acc


def _combine_part(x1, gates, yg_lo, yg_hi, part, n_parts, out_so_far):
    n_tok = x1.shape[0]
    part_tok = n_tok // n_parts
    tm = min(ROW_TILE, part_tok)
    nblk = part_tok // tm
    row = lambda w: pl.BlockSpec((tm, w), lambda i: (part * nblk + i, 0))
    plane = lambda kk: pl.BlockSpec((tm, PACK_HALF), lambda i, kk=kk: (kk * nblk + i, 0))
    planes = [plane(kk) for kk in range(TOP_K)]
    operands = [x1, gates, *([yg_lo] * TOP_K), *([yg_hi] * TOP_K)]
    in_specs = [row(D_MODEL), row(LANES)] + planes + planes
    aliases = {}
    if out_so_far is not None:
        aliases = {len(operands): 0}
        operands.append(out_so_far)
        in_specs.append(pl.BlockSpec(memory_space=pl.ANY))
    return pl.pallas_call(
        _combine_kernel,
        grid=(nblk,),
        in_specs=in_specs,
        out_specs=row(D_MODEL),
        out_shape=jax.ShapeDtypeStruct((n_tok, D_MODEL), F32),
        input_output_aliases=aliases,
        compiler_params=_cparams(("parallel",)),
        name="combine",
    )(*operands)


def kernel(x, norm1_gain, w_in, lambda_re, lambda_im, log_dt, ssm_b_re, ssm_b_im, ssm_c_re, ssm_c_im, ssm_d, w_glu, b_glu, q_norm_gain, k_norm_gain, lambda_q1, lambda_k1, lambda_q2, lambda_k2, subln_gain, w_proj_ssm, w_proj_attn, w_out, norm2_gain, w_router, b_router, w_exp1, b_exp1, w_exp2, b_exp2):
    bsz, seq, d = x.shape
    n_tok = bsz * seq
    depth = norm1_gain.shape[0]
    row1 = lambda a: a.astype(F32).reshape(1, -1)
    for l in range(depth):
        lambda_init = 0.8 - 0.6 * math.exp(-0.3 * l)
        x2d = x.reshape(n_tok, d)

        u, q, k, vt, gs, ga = _in_proj(x2d, row1(norm1_gain[l]), w_in[l], q_norm_gain[l], k_norm_gain[l])

        bblk, cblk, a_tile = _ssm_params(lambda_re[l], lambda_im[l], log_dt[l], ssm_b_re[l], ssm_b_im[l],
                                         ssm_c_re[l], ssm_c_im[l])
        so = _ssm(u.reshape(bsz, seq, SSM_WIDTH), bblk, cblk, a_tile, row1(ssm_d[l]),
                  w_glu[l].astype(BF16), row1(b_glu[l])).reshape(n_tok, SSM_WIDTH)

        lam = (jnp.exp(jnp.sum(lambda_q1[l].astype(F32) * lambda_k1[l].astype(F32)))
               - jnp.exp(jnp.sum(lambda_q2[l].astype(F32) * lambda_k2[l].astype(F32)))
               + lambda_init).reshape(1)
        ao = _attention(q, k, vt, lam, subln_gain[l], lambda_init, bsz, seq)

        region = n_tok
        x1, h_lo, h_hi, route, gates, counts = _merge_route(
            so, ao, gs, ga, x2d, w_proj_ssm[l].astype(BF16), w_proj_attn[l].astype(BF16),
            w_out[l].astype(BF16), row1(norm2_gain[l]), w_router[l], b_router[l], region)

        dest = route[:TOP_K]
        n_slots = N_EXPERTS * region
        xs_lo, xs_hi = _sc_scatter_rows(h_lo, h_hi, dest, n_slots)

        ys_lo, ys_hi = _experts(xs_lo, xs_hi, w_exp1[l], b_exp1[l], w_exp2[l], b_exp2[l], counts[:, 0], region)

        part_tok = n_tok // COLLECT_PARTS
        out = None
        for part in range(COLLECT_PARTS):
            flat = dest[:, part * part_tok:(part + 1) * part_tok].reshape(1, TOP_K * part_tok)
            yg_lo, yg_hi = _sc_gather_rows(ys_lo, ys_hi, flat)
            out = _combine_part(x1, gates, yg_lo, yg_hi, part, COLLECT_PARTS, out)
        x = out.reshape(bsz, seq, d)
    return x
```

```python
import functools
import math

import jax
import jax.numpy as jnp
from jax import lax
from jax.experimental import pallas as pl
from jax.experimental.pallas import tpu as pltpu
from jax.experimental.pallas import tpu_sc as plsc

F32 = jnp.float32
BF16 = jnp.bfloat16

D_MODEL = 1024
NORM_EPS = 1e-5
SSM_WIDTH = 512
SSM_GROUP = 16
SSM_GROUPS = 32
SSM_STATE = 64
N_STATE = SSM_GROUPS * SSM_STATE
HEADS = 8
HEAD_DIM = 64
N_EXPERTS = 32
TOP_K = 4
D_FF = 1024
SWIGLU_ALPHA = 1.702
SWIGLU_LIMIT = 7.0

LANES = 128
SUBLANES = 8
MXU_DIM = 256
VMEM_LIMIT = 56 * 1024 * 1024

ROW_TILE = 512
SSM_CHUNK = 128
SCAN_LANES = 1024
ATTN_BLOCK = 256
ATTN_HEADS_PER_STEP = 8
ROWSUM_ROWS = 16
EXPERT_BLOCK = 512
UP_CHUNKS_PER_TASK = 2
ROW_DMA_PRIORITY = 1
COLLECT_PARTS = 1
SC_WINDOW = 128
PACK_W = D_MODEL // 2
PACK_HALF = PACK_W // 2

_NEG = -1e30
Q_SCALE = math.log2(math.e) / math.sqrt(HEAD_DIM)
ATTN_SAFE_EXPONENT = 40.0
NORM_BOUND_SLACK = 1.05


def _cparams(sem):
    return pltpu.CompilerParams(dimension_semantics=sem, vmem_limit_bytes=VMEM_LIMIT)


def _full(shape):
    nd = len(shape)
    return pl.BlockSpec(shape, lambda *_: (0,) * nd)


def _sigmoid(x):
    return 0.5 * jnp.tanh(0.5 * x) + 0.5


def _run_lookahead(tasks):
    pending = tasks[0][0]()
    for i, (_, epilogue) in enumerate(tasks):
        result = pending
        if i + 1 < len(tasks):
            pending = tasks[i + 1][0]()
        epilogue(result)


_IN_CHUNKS = {"u": (0, SSM_WIDTH // MXU_DIM)}
for _name in ("q", "k", "v", "gs", "ga"):
    _start = max(first + count for first, count in _IN_CHUNKS.values())
    _IN_CHUNKS[_name] = (_start, D_MODEL // MXU_DIM)
N_IN_CHUNKS = max(first + count for first, count in _IN_CHUNKS.values())


def _prep_w_in_kernel(w_ref, wc_ref, wvt_ref):
    c = pl.program_id(0)
    w = w_ref[...]
    wc_ref[0] = w.astype(BF16)
    v_first, v_count = _IN_CHUNKS["v"]

    @pl.when((c >= v_first) & (c < v_first + v_count))
    def _():
        wvt_ref[0] = w.T.astype(BF16)


def _prep_w_in(w_in):
    v_first, v_count = _IN_CHUNKS["v"]
    return pl.pallas_call(
        _prep_w_in_kernel,
        grid=(N_IN_CHUNKS,),
        in_specs=[pl.BlockSpec((D_MODEL, MXU_DIM), lambda c: (0, c))],
        out_specs=[pl.BlockSpec((1, D_MODEL, MXU_DIM), lambda c: (c, 0, 0)),
                   pl.BlockSpec((1, MXU_DIM, D_MODEL), lambda c: (jnp.clip(c - v_first, 0, v_count - 1), 0, 0))],
        out_shape=[jax.ShapeDtypeStruct((N_IN_CHUNKS, D_MODEL, MXU_DIM), BF16),
                   jax.ShapeDtypeStruct((v_count, MXU_DIM, D_MODEL), BF16)],
        compiler_params=_cparams(("arbitrary",)),
        name="prep_w_in",
    )(w_in)


def _inproj_kernel(x_ref, g1_ref, w_ref, wvt_ref, qg_ref, kg_ref, seg_ref,
                   u_ref, q_ref, k_ref, vt_ref, gs_ref, ga_ref):
    x = x_ref[...]
    ms = jnp.mean(x * x, axis=-1, keepdims=True)
    h = (x * lax.rsqrt(ms + NORM_EPS) * g1_ref[...]).astype(BF16)

    def proj(name, c):
        chunk = _IN_CHUNKS[name][0] + c
        return lambda: jnp.dot(h, w_ref[chunk], preferred_element_type=F32)

    seg = seg_ref[...]
    tasks = []

    def plain_task(name, out_ref, c, fn):
        cols = slice(c * MXU_DIM, (c + 1) * MXU_DIM)

        def epilogue(r):
            out_ref[:, cols] = fn(r)
        return proj(name, c), epilogue

    tasks += [plain_task("u", u_ref, c, lambda r: r) for c in range(_IN_CHUNKS["u"][1])]

    def head_norm_tasks(name, gain_ref, out_ref, scale, c):
        cols = slice(c * MXU_DIM, (c + 1) * MXU_DIM)
        kept = {}

        def after_proj(y):
            kept["y"] = y
            kept["sq"] = (y * y).astype(BF16)

        def after_sum(ss):
            yn = kept["y"] * lax.rsqrt(ss * (1.0 / HEAD_DIM) + NORM_EPS) * gain_ref[:, cols]
            out_ref[:, cols] = (yn * scale).astype(BF16)

        return ((proj(name, c), after_proj),
                (lambda: jnp.dot(kept["sq"], seg, preferred_element_type=F32), after_sum))

    pairs = [head_norm_tasks(name, gain_ref, out_ref, scale, c)
             for name, gain_ref, out_ref, scale in (("q", qg_ref, q_ref, Q_SCALE), ("k", kg_ref, k_ref, 1.0))
             for c in range(_IN_CHUNKS[name][1])]
    tasks.append(pairs[0][0])
    for prev, cur in zip(pairs, pairs[1:]):
        tasks += [cur[0], prev[1]]
    tasks.append(pairs[-1][1])

    to_gate = lambda r: _sigmoid(r).astype(BF16)
    for c in range(_IN_CHUNKS["gs"][1]):
        tasks.append(plain_task("gs", gs_ref, c, to_gate))
        tasks.append(plain_task("ga", ga_ref, c, to_gate))

    def vt_task(c):
        def matmul():
            return lax.dot_general(wvt_ref[c], h, (((1,), (1,)), ((), ())), preferred_element_type=F32)

        def epilogue(r):
            vt_ref[c * MXU_DIM:(c + 1) * MXU_DIM, :] = r.astype(BF16)
        return matmul, epilogue

    tasks += [vt_task(c) for c in range(_IN_CHUNKS["v"][1])]
    _run_lookahead(tasks)


def _in_proj(x2d, gain1, w_in, q_gain, k_gain):
    n_tok = x2d.shape[0]
    tm = min(ROW_TILE, n_tok)
    w_chunks, w_vt = _prep_w_in(w_in)
    seg = (jnp.arange(MXU_DIM)[:, None] // HEAD_DIM == jnp.arange(MXU_DIM)[None, :] // HEAD_DIM).astype(BF16)
    reps = D_MODEL // HEAD_DIM
    qg = jnp.tile(q_gain.astype(F32), reps)[None, :]
    kg = jnp.tile(k_gain.astype(F32), reps)[None, :]
    row = lambda w: pl.BlockSpec((tm, w), lambda i: (i, 0))
    tok = jax.ShapeDtypeStruct((n_tok, D_MODEL), BF16)
    out_shapes = [jax.ShapeDtypeStruct((n_tok, SSM_WIDTH), F32), tok, tok,
                  jax.ShapeDtypeStruct((D_MODEL, n_tok), BF16), tok, tok]
    vt_spec = pl.BlockSpec((D_MODEL, tm), lambda i: (0, i))
    return pl.pallas_call(
        _inproj_kernel,
        grid=(n_tok // tm,),
        in_specs=[row(D_MODEL), _full((1, D_MODEL)), _full(w_chunks.shape), _full(w_vt.shape),
                  _full((1, D_MODEL)), _full((1, D_MODEL)), _full((MXU_DIM, MXU_DIM))],
        out_specs=[row(SSM_WIDTH), row(D_MODEL), row(D_MODEL), vt_spec, row(D_MODEL), row(D_MODEL)],
        out_shape=out_shapes,
        compiler_params=_cparams(("parallel",)),
        name="in_proj",
    )(x2d, gain1, w_chunks, w_vt, qg, kg, seg)


def _ssm_kernel(u_ref, bblk_ref, a_ref, cblk_ref, d_ref, wglu_ref, bglu_ref, o_ref, bu_ref, st_ref, tb_ref):
    n_batch, chunk = u_ref.shape[0], u_ref.shape[1]

    @pl.when(pl.program_id(0) == 0)
    def _():
        st_ref[...] = jnp.zeros_like(st_ref)

    n_planes = SSM_WIDTH // LANES
    for b in range(n_batch):
        for j in range(n_planes):
            tb_ref[j, pl.ds(b, chunk, stride=n_batch), :] = u_ref[b, :, j * LANES:(j + 1) * LANES]
    u = jnp.concatenate([tb_ref[j] for j in range(n_planes)], axis=1)
    u_bf = u.astype(BF16)
    tiles_per_part = N_STATE // MXU_DIM
    ch_per_tile = SSM_WIDTH // tiles_per_part
    n_groups = N_STATE // SCAN_LANES
    tiles_per_group = SCAN_LANES // MXU_DIM
    assert SCAN_LANES * SSM_WIDTH == N_STATE * MXU_DIM

    def input_tiles(g):
        for part in range(2):
            for t in range(tiles_per_group):
                tile = g * tiles_per_group + t
                ch0 = (tile * ch_per_tile) // LANES * LANES
                lanes = slice(part * N_STATE + tile * MXU_DIM, part * N_STATE + (tile + 1) * MXU_DIM)
                bu_ref[:, lanes] = jnp.dot(u_bf[:, ch0:ch0 + LANES], bblk_ref[ch0:ch0 + LANES, lanes],
                                           preferred_element_type=F32)

    def scan(g):
        re = slice(g * SCAN_LANES, (g + 1) * SCAN_LANES)
        im = slice(N_STATE + g * SCAN_LANES, N_STATE + (g + 1) * SCAN_LANES)
        ar, ai = a_ref[:, re], a_ref[:, im]
        xr, xi = st_ref[:, re], st_ref[:, im]
        for t in range(chunk):
            rows = slice(t * SUBLANES, (t + 1) * SUBLANES)
            xr, xi = ar * xr - ai * xi + bu_ref[rows, re], ar * xi + ai * xr + bu_ref[rows, im]
            bu_ref[rows, re] = xr
            bu_ref[rows, im] = xi
        st_ref[:, re] = xr
        st_ref[:, im] = xi

    def output_tile(g):
        cols = slice(g * MXU_DIM, (g + 1) * MXU_DIM)
        acc = None
        for part in range(2):
            lanes = slice(part * N_STATE + g * SCAN_LANES, part * N_STATE + (g + 1) * SCAN_LANES)
            term = jnp.dot(bu_ref[:, lanes].astype(BF16), cblk_ref[lanes, cols], preferred_element_type=F32)
            acc = term if acc is None else acc + term
        return acc

    input_tiles(0)
    ys = []
    for g in range(n_groups):
        if g + 1 < n_groups:
            input_tiles(g + 1)
        scan(g)
        ys.append(output_tile(g))
    y = jnp.concatenate(ys, axis=1) + d_ref[...] * u
    z = jax.nn.gelu(y)
    gate = _sigmoid(jnp.dot(z.astype(BF16), wglu_ref[...], preferred_element_type=F32) + bglu_ref[...])
    out = z * gate
    for j in range(n_planes):
        tb_ref[j] = out[:, j * LANES:(j + 1) * LANES]
    for b in range(n_batch):
        for j in range(n_planes):
            o_ref[b, :, j * LANES:(j + 1) * LANES] = tb_ref[j, pl.ds(b, chunk, stride=n_batch), :].astype(BF16)


def _ssm_params(lambda_re, lambda_im, log_dt, b_re, b_im, c_re, c_im):
    dt = jnp.exp(log_dt.astype(F32))[:, None]
    lr = jnp.minimum(lambda_re.astype(F32), -1e-4)
    li = lambda_im.astype(F32)
    mag = jnp.exp(lr * dt)
    abar_re = mag * jnp.cos(li * dt)
    abar_im = mag * jnp.sin(li * dt)
    den = lr * lr + li * li
    nr = abar_re - 1.0
    coef_re = (nr * lr + abar_im * li) / den
    coef_im = (abar_im * lr - nr * li) / den
    br = b_re.astype(F32)
    bi = b_im.astype(F32)
    bbar_re = coef_re[..., None] * br - coef_im[..., None] * bi
    bbar_im = coef_re[..., None] * bi + coef_im[..., None] * br
    ch_group = jnp.arange(SSM_WIDTH) // SSM_GROUP
    lane_group = jnp.arange(N_STATE) // SSM_STATE

    def expand_b(b):
        rows = b.transpose(0, 2, 1).reshape(SSM_WIDTH, SSM_STATE)
        return jnp.where(ch_group[:, None] == lane_group[None, :], jnp.tile(rows, (1, SSM_GROUPS)), 0.0)

    def expand_c(c):
        rows = c.transpose(0, 2, 1).reshape(N_STATE, SSM_GROUP)
        return jnp.where(lane_group[:, None] == ch_group[None, :], jnp.tile(rows, (1, SSM_GROUPS)), 0.0)

    bblk = jnp.concatenate([expand_b(bbar_re), expand_b(bbar_im)], axis=1).astype(BF16)
    cblk = jnp.concatenate([expand_c(c_re.astype(F32)), -expand_c(c_im.astype(F32))], axis=0).astype(BF16)
    a_row = jnp.concatenate([abar_re.reshape(-1), abar_im.reshape(-1)])[None, :]
    return bblk, cblk, jnp.broadcast_to(a_row, (SUBLANES, 2 * N_STATE))


def _ssm(u, bblk, cblk, a_tile, d_skip, w_glu_bf, b_glu):
    bsz, seq, _ = u.shape
    assert bsz == SUBLANES
    chunk = min(SSM_CHUNK, seq)
    rows = chunk * SUBLANES
    tok_spec = pl.BlockSpec((bsz, chunk, SSM_WIDTH), lambda c: (0, c, 0))
    return pl.pallas_call(
        _ssm_kernel,
        grid=(seq // chunk,),
        in_specs=[tok_spec, _full(bblk.shape), _full(a_tile.shape), _full(cblk.shape),
                  _full((1, SSM_WIDTH)), _full(w_glu_bf.shape), _full((1, SSM_WIDTH))],
        out_specs=tok_spec,
        out_shape=jax.ShapeDtypeStruct(u.shape, BF16),
        scratch_shapes=[pltpu.VMEM((rows, 2 * N_STATE), F32), pltpu.VMEM((SUBLANES, 2 * N_STATE), F32),
                        pltpu.VMEM((SSM_WIDTH // LANES, rows, LANES), F32)],
        compiler_params=_cparams(("arbitrary",)),
        name="ssm",
    )(u, bblk, a_tile, cblk, d_skip, w_glu_bf, b_glu)


def _attn_kernel(lam_ref, sg_ref, q_ref, k_ref, vt_ref, o_ref, acc_ref, vta_ref, *, out_scale, blk, nh, bounded):
    seq = q_ref.shape[1]
    hw = 2 * HEAD_DIM
    lane = lax.broadcasted_iota(jnp.int32, (blk, hw), 1)
    key_i = lax.broadcasted_iota(jnp.int32, (blk, blk), 0)
    qry_i = lax.broadcasted_iota(jnp.int32, (blk, blk), 1)
    keep = key_i <= qry_i
    contract_last = (((1,), (1,)), ((), ()))
    lam = lam_ref[0]
    n_chain = 2 * nh

    for hh in range(nh):
        vta_ref[hh, :hw, :] = vt_ref[hh * hw:(hh + 1) * hw, :]
        vta_ref[hh, hw:, :] = jnp.ones((ROWSUM_ROWS, seq), BF16)

    def q_block(qi, _):
        qrows = pl.ds(pl.multiple_of(qi * blk, blk), blk)
        qs = []
        for hh in range(nh):
            q = q_ref[0, qrows, hh * hw:(hh + 1) * hw]
            zero = jnp.zeros_like(q)
            qs += [jnp.where(lane < HEAD_DIM, q, zero), jnp.where(lane >= HEAD_DIM, q, zero)]
        acc_ref[...] = jnp.zeros_like(acc_ref)

        def kv_block(kb, carry, masked):
            krows = pl.ds(pl.multiple_of(kb * blk, blk), blk)
            scores = [lax.dot_general(k_ref[0, krows, (c // 2) * hw:(c // 2 + 1) * hw], qs[c], contract_last,
                                      preferred_element_type=F32) for c in range(n_chain)]
            out = []
            for c in range(n_chain):
                s = jnp.where(keep, scores[c], _NEG) if masked else scores[c]
                vt = vta_ref[c // 2, :, krows]
                if bounded:
                    acc_ref[c] += jnp.dot(vt, jnp.exp2(s).astype(BF16), preferred_element_type=F32)
                    continue
                m = carry[c]
                m_new = jnp.maximum(m, jnp.max(s, axis=0, keepdims=True))
                p = jnp.exp2(s - m_new).astype(BF16)
                alpha = jnp.exp2(m - m_new)
                out.append(m_new)
                acc_ref[c] = alpha * acc_ref[c] + jnp.dot(vt, p, preferred_element_type=F32)
            return tuple(out)

        carry = (jnp.full((1, blk), _NEG, F32),) * (0 if bounded else n_chain)
        carry = lax.fori_loop(0, qi, lambda kb, c: kv_block(kb, c, False), carry)
        kv_block(qi, carry, True)
        for hh in range(nh):
            a1, a2 = acc_ref[2 * hh], acc_ref[2 * hh + 1]
            l1, l2 = a1[hw:hw + 1, :], a2[hw:hw + 1, :]
            ot = a1[:hw, :] * (1.0 / l1) - a2[:hw, :] * (lam / l2)
            ot = ot * lax.rsqrt(jnp.mean(ot * ot, axis=0, keepdims=True) + NORM_EPS)
            o_ref[0, qrows, hh * hw:(hh + 1) * hw] = (ot.T * sg_ref[...] * out_scale).astype(BF16)
        return 0

    lax.fori_loop(0, seq // blk, q_block, 0)


def _attention(q, k, vt, lam, subln_gain, score_bound, lambda_init, bsz, seq):
    blk = min(ATTN_BLOCK, seq)
    nh = ATTN_HEADS_PER_STEP
    hw = 2 * HEAD_DIM
    q3, k3 = (a.reshape(bsz, seq, D_MODEL) for a in (q, k))
    tok_spec = pl.BlockSpec((1, seq, nh * hw), lambda b, h: (b, 0, h))

    def call(bounded):
        return pl.pallas_call(
            functools.partial(_attn_kernel, out_scale=1.0 - lambda_init, blk=blk, nh=nh, bounded=bounded),
            grid=(bsz, HEADS // nh),
            in_specs=[pl.BlockSpec(memory_space=pltpu.SMEM), _full((1, hw)), tok_spec, tok_spec,
                      pl.BlockSpec((nh * hw, seq), lambda b, h: (h, b))],
            out_specs=tok_spec,
            out_shape=jax.ShapeDtypeStruct((bsz, seq, D_MODEL), BF16),
            scratch_shapes=[pltpu.VMEM((2 * nh, hw + ROWSUM_ROWS, blk), F32),
                            pltpu.VMEM((nh, hw + ROWSUM_ROWS, seq), BF16)],
            compiler_params=_cparams(("parallel", "parallel")),
            name="diff_attn_bounded" if bounded else "diff_attn",
        )

    operands = (lam, subln_gain.astype(F32)[None, :], q3, k3, vt)
    out = lax.cond(score_bound <= ATTN_SAFE_EXPONENT, lambda ops: call(True)(*ops), lambda ops: call(False)(*ops),
                   operands)
    return out.reshape(bsz * seq, D_MODEL)


def _pack_rows(y):
    bits = lax.bitcast_convert_type(y.astype(BF16).astype(F32), jnp.uint32)
    return (bits[:, :PACK_W] >> 16) | (bits[:, PACK_W:] & jnp.uint32(0xFFFF0000))


def _unpack_rows(w):
    lo = lax.bitcast_convert_type(w << 16, F32)
    hi = lax.bitcast_convert_type(w & jnp.uint32(0xFFFF0000), F32)
    return jnp.concatenate([lo, hi], axis=1)


def _merge_kernel(so_ref, ao_ref, gs_ref, ga_ref, x_ref, wps_ref, wpa_ref, wo_ref, g2_ref,
                  wrh_ref, wrl_ref, br_ref, tri_ref,
                  x1_ref, hlo_ref, hhi_ref, route_ref, gate_ref, cnt_ref, run_ref, *, region):
    @pl.when(pl.program_id(0) == 0)
    def _():
        run_ref[...] = jnp.zeros_like(run_ref)

    tm = x_ref.shape[0]
    n_part = 2 if tm % (2 * MXU_DIM) == 0 else 1
    rows_per = tm // n_part
    expert_f = lax.broadcasted_iota(jnp.int32, (N_EXPERTS, rows_per), 0).astype(F32)
    slot_row = lax.broadcasted_iota(jnp.int32, (SUBLANES, rows_per), 0)
    contract_last = (((1,), (1,)), ((), ()))
    parts = [dict(rows=slice(p * rows_per, (p + 1) * rows_per)) for p in range(n_part)]

    def stage_proj(st):
        def matmul():
            return (jnp.dot(so_ref[st["rows"], :], wps_ref[...], preferred_element_type=F32),
                    jnp.dot(ao_ref[st["rows"], :], wpa_ref[...], preferred_element_type=F32))

        def epilogue(r):
            ps, pa = r
            merged = gs_ref[st["rows"], :].astype(F32) * ps + ga_ref[st["rows"], :].astype(F32) * pa
            st["merged"] = merged.astype(BF16)
        return matmul, epilogue

    def stage_out(st):
        def matmul():
            return jnp.dot(st["merged"], wo_ref[...], preferred_element_type=F32)

        def epilogue(r):
            x1 = x_ref[st["rows"], :] + r
            x1_ref[st["rows"], :] = x1
            h2 = x1 * lax.rsqrt(jnp.mean(x1 * x1, axis=-1, keepdims=True) + NORM_EPS) * g2_ref[...]
            words = _pack_rows(h2)
            hlo_ref[st["rows"], :] = words[:, :PACK_HALF]
            hhi_ref[st["rows"], :] = words[:, PACK_HALF:]
            st["h_hi"] = h2.astype(BF16)
            st["h_lo"] = (h2 - st["h_hi"].astype(F32)).astype(BF16)
        return matmul, epilogue

    def stage_router(st):
        def matmul():
            return (lax.dot_general(wrh_ref[...], st["h_hi"], contract_last, preferred_element_type=F32)
                    + lax.dot_general(wrh_ref[...], st["h_lo"], contract_last, preferred_element_type=F32)
                    + lax.dot_general(wrl_ref[...], st["h_hi"], contract_last, preferred_element_type=F32))

        def epilogue(r):
            work = r + br_ref[...]
            onehots, vals, ids = [], [], []
            for _ in range(TOP_K):
                m = jnp.max(work, axis=0, keepdims=True)
                idx = jnp.min(jnp.where(work == m, expert_f, float(N_EXPERTS)), axis=0, keepdims=True)
                oh = expert_f == idx
                onehots.append(oh)
                vals.append(m)
                ids.append(idx.astype(jnp.int32))
                work = jnp.where(oh, -jnp.inf, work)
            exps = [jnp.exp(v - vals[0]) for v in vals]
            den = exps[0] + exps[1] + exps[2] + exps[3]
            st.update(onehots=onehots, ids=ids, gates=[e / den for e in exps],
                      multi=(onehots[0] | onehots[1] | onehots[2] | onehots[3]).astype(F32))
        return matmul, epilogue

    _run_lookahead([stage(st) for stage in (stage_proj, stage_out, stage_router) for st in parts])

    multi = jnp.concatenate([st["multi"] for st in parts], axis=1)
    before = jnp.dot(multi.astype(BF16), tri_ref[...], preferred_element_type=F32) + run_ref[...]
    gate_row = lax.broadcasted_iota(jnp.int32, (LANES, rows_per), 0)
    for st in parts:
        route = jnp.zeros((SUBLANES, rows_per), jnp.int32)
        gates = jnp.zeros((LANES, rows_per), F32)
        for kk in range(TOP_K):
            rank = jnp.sum(jnp.where(st["onehots"][kk], before[:, st["rows"]], 0.0), axis=0, keepdims=True)
            route = jnp.where(slot_row == kk, st["ids"][kk] * region + rank.astype(jnp.int32), route)
            gates = jnp.where(gate_row == kk, st["gates"][kk], gates)
        route_ref[:, st["rows"]] = route
        gate_ref[st["rows"], :] = gates.T
    run = run_ref[...] + jnp.sum(multi, axis=1, keepdims=True)
    run_ref[...] = run
    cnt_ref[...] = jnp.broadcast_to(run, cnt_ref.shape).astype(jnp.int32)


def _merge_route(so, ao, gs, ga, x2d, wps, wpa, wo, gain2, w_router, b_router, region):
    n_tok = x2d.shape[0]
    tm = min(ROW_TILE, n_tok)
    wr = w_router.astype(F32).T
    wr_hi = wr.astype(BF16)
    wr_lo = (wr - wr_hi.astype(F32)).astype(BF16)
    br = b_router.astype(F32)[:, None]
    tri = (jnp.arange(tm)[:, None] < jnp.arange(tm)[None, :]).astype(BF16)
    row = lambda w: pl.BlockSpec((tm, w), lambda i: (i, 0))
    slots = pl.BlockSpec((SUBLANES, tm), lambda i: (0, i))
    out_shapes = [jax.ShapeDtypeStruct((n_tok, D_MODEL), F32),
                  jax.ShapeDtypeStruct((n_tok, PACK_HALF), jnp.uint32),
                  jax.ShapeDtypeStruct((n_tok, PACK_HALF), jnp.uint32),
                  jax.ShapeDtypeStruct((SUBLANES, n_tok), jnp.int32),
                  jax.ShapeDtypeStruct((n_tok, LANES), F32),
                  jax.ShapeDtypeStruct((N_EXPERTS, LANES), jnp.int32)]
    return pl.pallas_call(
        functools.partial(_merge_kernel, region=region),
        grid=(n_tok // tm,),
        in_specs=[row(SSM_WIDTH), row(D_MODEL), row(D_MODEL), row(D_MODEL), row(D_MODEL),
                  _full(wps.shape), _full(wpa.shape), _full(wo.shape), _full((1, D_MODEL)),
                  _full(wr_hi.shape), _full(wr_lo.shape), _full((N_EXPERTS, 1)), _full((tm, tm))],
        out_specs=[row(D_MODEL), row(PACK_HALF), row(PACK_HALF), slots, row(LANES), _full((N_EXPERTS, LANES))],
        out_shape=out_shapes,
        scratch_shapes=[pltpu.VMEM((N_EXPERTS, 1), F32)],
        compiler_params=_cparams(("arbitrary",)),
        name="merge_route",
    )(so, ao, gs, ga, x2d, wps, wpa, wo, gain2, wr_hi, wr_lo, br, tri)


def _sc_scatter_rows(rows_lo, rows_hi, dest, n_slots):
    n_tok, width = rows_lo.shape
    mesh = plsc.VectorSubcoreMesh(core_axis_name="core", subcore_axis_name="subcore")
    out_type = jax.ShapeDtypeStruct((n_slots, width), rows_lo.dtype)

    @pl.kernel(out_type=(out_type, out_type), mesh=mesh, scratch_types=[])
    def scatter(lo_hbm, hi_hbm, dest_hbm, out_lo_hbm, out_hi_hbm):
        for rows_hbm, out_hbm in ((lo_hbm, out_lo_hbm), (hi_hbm, out_hi_hbm)):
            def body(rows_vmem, dest_vmem, out_hbm=out_hbm):
                for k in range(TOP_K):
                    pltpu.sync_copy(rows_vmem, out_hbm.at[dest_vmem.at[k]])

            pltpu.emit_pipeline(
                body,
                grid=(n_tok // SC_WINDOW,),
                in_specs=[pl.BlockSpec((SC_WINDOW, width), lambda i: (i, 0)),
                          pl.BlockSpec((TOP_K, SC_WINDOW), lambda i: (0, i))],
                out_specs=[],
                core_axis_name=("core", "subcore"),
                dimension_semantics=(pltpu.PARALLEL,),
            )(rows_hbm, dest_hbm)

    return scatter(rows_lo, rows_hi, dest)


def _sc_gather_rows(table_lo, table_hi, idx):
    n = idx.shape[1]
    width = table_lo.shape[1]
    mesh = plsc.VectorSubcoreMesh(core_axis_name="core", subcore_axis_name="subcore")
    out_type = jax.ShapeDtypeStruct((n, width), table_lo.dtype)

    @pl.kernel(out_type=(out_type, out_type), mesh=mesh, scratch_types=[])
    def gather(lo_hbm, hi_hbm, idx_hbm, out_lo_hbm, out_hi_hbm):
        for table_hbm, out_hbm in ((lo_hbm, out_lo_hbm), (hi_hbm, out_hi_hbm)):
            def body(idx_vmem, out_vmem, table_hbm=table_hbm):
                pltpu.sync_copy(table_hbm.at[idx_vmem.at[0]], out_vmem)

            pltpu.emit_pipeline(
                body,
                grid=(n // SC_WINDOW,),
                in_specs=[pl.BlockSpec((1, SC_WINDOW), lambda i: (0, i))],
                out_specs=[pl.BlockSpec((SC_WINDOW, width), lambda i: (i, 0))],
                core_axis_name=("core", "subcore"),
                dimension_semantics=(pltpu.PARALLEL,),
            )(idx_hbm, out_hbm)

    return gather(table_lo, table_hi, idx)


def _expert_kernel(cnt_ref, xlo_hbm, xhi_hbm, w1_ref, b1_ref, w2_ref, b2_ref, perm_ref, ylo_hbm, yhi_hbm,
                   w1p_ref, w2b_ref, x_ref, act_ref, xin_ref, yout_ref, in_sem, out_sem, *, region):
    e = pl.program_id(0)
    n_blk = (cnt_ref[e] + EXPERT_BLOCK - 1) // EXPERT_BLOCK
    base = e * region
    n_chunks = (2 * D_FF) // MXU_DIM
    x_hbm = (xlo_hbm, xhi_hbm)
    y_hbm = (ylo_hbm, yhi_hbm)

    def rows_of(j):
        return pl.ds(pl.multiple_of(base + j * EXPERT_BLOCK, EXPERT_BLOCK), EXPERT_BLOCK)

    def in_copy(j, slot, half):
        return pltpu.make_async_copy(x_hbm[half].at[rows_of(j), :], xin_ref.at[slot, half], in_sem.at[slot, half])

    def out_copy(j, slot, half):
        return pltpu.make_async_copy(yout_ref.at[slot, half], y_hbm[half].at[rows_of(j), :], out_sem.at[slot, half])

    @pl.when(n_blk > 0)
    def _():
        for half in range(2):
            in_copy(0, 0, half).start(priority=ROW_DMA_PRIORITY)

    perm = perm_ref[...]
    for c in range(n_chunks):
        cols = slice(c * MXU_DIM, (c + 1) * MXU_DIM)
        w1p_ref[c] = jnp.dot(w1_ref[0, :, cols].astype(BF16), perm, preferred_element_type=F32).astype(BF16)
    for c in range(D_MODEL // MXU_DIM):
        w2b_ref[c] = w2_ref[0, :, c * MXU_DIM:(c + 1) * MXU_DIM].astype(BF16)

    def mlp_rows(n_rows, xin, yout):
        words = jnp.concatenate([xin[0], xin[1]], axis=1)
        x_ref[:n_rows, :] = _unpack_rows(words).astype(BF16)

        def up_task(c0):
            chunks = range(c0, c0 + UP_CHUNKS_PER_TASK)

            def matmul():
                return [jnp.dot(x_ref[:n_rows, :], w1p_ref[c], preferred_element_type=F32) for c in chunks]

            def epilogue(results):
                gates, ups = [], []
                for c, r in zip(chunks, results):
                    h = r + b1_ref[0, :, c * MXU_DIM:(c + 1) * MXU_DIM]
                    gates.append(jnp.minimum(h[:, :LANES], SWIGLU_LIMIT))
                    ups.append(jnp.clip(h[:, LANES:], -SWIGLU_LIMIT, SWIGLU_LIMIT))
                gate = jnp.concatenate(gates, axis=1)
                up = jnp.concatenate(ups, axis=1)
                glu = gate * _sigmoid(SWIGLU_ALPHA * gate)
                act_ref[:n_rows, c0 * LANES:(c0 + UP_CHUNKS_PER_TASK) * LANES] = ((up + 1.0) * glu).astype(BF16)
            return matmul, epilogue

        _run_lookahead([up_task(c0) for c0 in range(0, n_chunks, UP_CHUNKS_PER_TASK)])

        def down_task(c):
            cols = slice(c * MXU_DIM, (c + 1) * MXU_DIM)
            half, high = c % 2, c // 2

            def epilogue(r):
                bits = lax.bitcast_convert_type((r + b2_ref[0, :, cols]).astype(BF16).astype(F32), jnp.uint32)
                if high:
                    yout[half] = yout[half] | (bits & jnp.uint32(0xFFFF0000))
                else:
                    yout[half] = bits >> 16
            return (lambda: jnp.dot(act_ref[:n_rows, :], w2b_ref[c], preferred_element_type=F32)), epilogue

        _run_lookahead([down_task(c) for c in range(D_MODEL // MXU_DIM)])

    def block(j, _):
        slot = j % 2
        for half in range(2):
            in_copy(j, slot, half).wait()

        @pl.when(j + 1 < n_blk)
        def _():
            for half in range(2):
                in_copy(j + 1, 1 - slot, half).start(priority=ROW_DMA_PRIORITY)

        @pl.when(j >= 2)
        def _():
            for half in range(2):
                out_copy(j - 2, slot, half).wait()

        mlp_rows(EXPERT_BLOCK, xin_ref.at[slot], yout_ref.at[slot])
        for half in range(2):
            out_copy(j, slot, half).start(priority=ROW_DMA_PRIORITY)
        return 0

    lax.fori_loop(0, n_blk, block, 0)

    for back in (2, 1):
        @pl.when(n_blk >= back)
        def _(back=back):
            j = n_blk - back
            for half in range(2):
                out_copy(j, j % 2, half).wait()


def _gate_up_order():
    j = jnp.arange(MXU_DIM)
    within = jnp.where(j < LANES, 2 * j, 2 * (j - LANES) + 1)
    return within


def _experts(xs_lo, xs_hi, w1, b1, w2, b2, counts, region):
    within = _gate_up_order()
    perm = (jnp.arange(MXU_DIM)[:, None] == within[None, :]).astype(BF16)
    order = (jnp.arange(0, 2 * D_FF, MXU_DIM)[:, None] + within[None, :]).reshape(-1)
    b1p = b1.astype(F32)[:, order][:, None, :]
    b2r = b2.astype(F32)[:, None, :]
    tb = EXPERT_BLOCK
    hbm = pl.BlockSpec(memory_space=pl.ANY)
    wspec = lambda shape: pl.BlockSpec((1,) + shape, lambda e, cnt: (e, 0, 0))
    grid_spec = pltpu.PrefetchScalarGridSpec(
        num_scalar_prefetch=1,
        grid=(N_EXPERTS,),
        in_specs=[hbm, hbm, wspec((D_MODEL, 2 * D_FF)), wspec((1, 2 * D_FF)),
                  wspec((D_FF, D_MODEL)), wspec((1, D_MODEL)),
                  pl.BlockSpec((MXU_DIM, MXU_DIM), lambda e, cnt: (0, 0))],
        out_specs=[hbm, hbm],
        scratch_shapes=[pltpu.VMEM((2 * D_FF // MXU_DIM, D_MODEL, MXU_DIM), BF16),
                        pltpu.VMEM((D_MODEL // MXU_DIM, D_FF, MXU_DIM), BF16),
                        pltpu.VMEM((tb, D_MODEL), BF16), pltpu.VMEM((tb, D_FF), BF16),
                        pltpu.VMEM((2, 2, tb, PACK_HALF), jnp.uint32), pltpu.VMEM((2, 2, tb, PACK_HALF), jnp.uint32),
                        pltpu.SemaphoreType.DMA((2, 2)), pltpu.SemaphoreType.DMA((2, 2))],
    )
    return pl.pallas_call(
        functools.partial(_expert_kernel, region=region),
        grid_spec=grid_spec,
        out_shape=[jax.ShapeDtypeStruct(xs_lo.shape, jnp.uint32)] * 2,
        compiler_params=_cparams(("arbitrary",)),
        name="experts",
    )(counts, xs_lo, xs_hi, w1, b1p, w2, b2r, perm)


def _combine_kernel(x1_ref, gate_ref, *refs):
    lo_refs, hi_refs, o_ref = refs[:TOP_K], refs[TOP_K:2 * TOP_K], refs[-1]
    acc = x1_ref[...]
    gates = gate_ref[...]
    for kk in range(TOP_K):
        words = jnp.concatenate([lo_refs[kk][...], hi_refs[kk][...]], axis=1)
        acc = acc + gates[:, kk:kk + 1] * _unpack_rows(words)
    o_ref[...] = acc


def _combine_part(x1, gates, yg_lo, yg_hi, part, n_parts, out_so_far):
    n_tok = x1.shape[0]
    part_tok = n_tok // n_parts
    tm = min(ROW_TILE, part_tok)
    nblk = part_tok // tm
    row = lambda w: pl.BlockSpec((tm, w), lambda i: (part * nblk + i, 0))
    plane = lambda kk: pl.BlockSpec((tm, PACK_HALF), lambda i, kk=kk: (kk * nblk + i, 0))
    planes = [plane(kk) for kk in range(TOP_K)]
    operands = [x1, gates, *([yg_lo] * TOP_K), *([yg_hi] * TOP_K)]
    in_specs = [row(D_MODEL), row(LANES)] + planes + planes
    aliases = {}
    if out_so_far is not None:
        aliases = {len(operands): 0}
        operands.append(out_so_far)
        in_specs.append(pl.BlockSpec(memory_space=pl.ANY))
    return pl.pallas_call(
        _combine_kernel,
        grid=(nblk,),
        in_specs=in_specs,
        out_specs=row(D_MODEL),
        out_shape=jax.ShapeDtypeStruct((n_tok, D_MODEL), F32),
        input_output_aliases=aliases,
        compiler_params=_cparams(("parallel",)),
        name="combine",
    )(*operands)


def kernel(x, norm1_gain, w_in, lambda_re, lambda_im, log_dt, ssm_b_re, ssm_b_im, ssm_c_re, ssm_c_im, ssm_d, w_glu, b_glu, q_norm_gain, k_norm_gain, lambda_q1, lambda_k1, lambda_q2, lambda_k2, subln_gain, w_proj_ssm, w_proj_attn, w_out, norm2_gain, w_router, b_router, w_exp1, b_exp1, w_exp2, b_exp2):
    bsz, seq, d = x.shape
    n_tok = bsz * seq
    depth = norm1_gain.shape[0]
    row1 = lambda a: a.astype(F32).reshape(1, -1)
    for l in range(depth):
        lambda_init = 0.8 - 0.6 * math.exp(-0.3 * l)
        x2d = x.reshape(n_tok, d)

        u, q, k, vt, gs, ga = _in_proj(x2d, row1(norm1_gain[l]), w_in[l], q_norm_gain[l], k_norm_gain[l])

        bblk, cblk, a_tile = _ssm_params(lambda_re[l], lambda_im[l], log_dt[l], ssm_b_re[l], ssm_b_im[l],
                                         ssm_c_re[l], ssm_c_im[l])
        so = _ssm(u.reshape(bsz, seq, SSM_WIDTH), bblk, cblk, a_tile, row1(ssm_d[l]),
                  w_glu[l].astype(BF16), row1(b_glu[l])).reshape(n_tok, SSM_WIDTH)

        lam = (jnp.exp(jnp.sum(lambda_q1[l].astype(F32) * lambda_k1[l].astype(F32)))
               - jnp.exp(jnp.sum(lambda_q2[l].astype(F32) * lambda_k2[l].astype(F32)))
               + lambda_init).reshape(1)
        score_bound = (NORM_BOUND_SLACK * HEAD_DIM * Q_SCALE * jnp.max(jnp.abs(q_norm_gain[l].astype(F32)))
                       * jnp.max(jnp.abs(k_norm_gain[l].astype(F32))))
        ao = _attention(q, k, vt, lam, subln_gain[l], score_bound, lambda_init, bsz, seq)

        region = n_tok
        x1, h_lo, h_hi, route, gates, counts = _merge_route(
            so, ao, gs, ga, x2d, w_proj_ssm[l].astype(BF16), w_proj_attn[l].astype(BF16),
            w_out[l].astype(BF16), row1(norm2_gain[l]), w_router[l], b_router[l], region)

        dest = route[:TOP_K]
        n_slots = N_EXPERTS * region
        xs_lo, xs_hi = _sc_scatter_rows(h_lo, h_hi, dest, n_slots)

        ys_lo, ys_hi = _experts(xs_lo, xs_hi, w_exp1[l], b_exp1[l], w_exp2[l], b_exp2[l], counts[:, 0], region)

        part_tok = n_tok // COLLECT_PARTS
        out = None
        for part in range(COLLECT_PARTS):
            flat = dest[:, part * part_tok:(part + 1) * part_tok].reshape(1, TOP_K * part_tok)
            yg_lo, yg_hi = _sc_gather_rows(ys_lo, ys_hi, flat)
            out = _combine_part(x1, gates, yg_lo, yg_hi, part, COLLECT_PARTS, out)
        x = out.reshape(bsz, seq, d)
    return x
```

```python
import functools
import math

import jax
import jax.numpy as jnp
from jax import lax
from jax.experimental import pallas as pl
from jax.experimental.pallas import tpu as pltpu
from jax.experimental.pallas import tpu_sc as plsc

F32 = jnp.float32
BF16 = jnp.bfloat16

D_MODEL = 1024
NORM_EPS = 1e-5
SSM_WIDTH = 512
SSM_GROUP = 16
SSM_GROUPS = 32
SSM_STATE = 64
N_STATE = SSM_GROUPS * SSM_STATE
HEADS = 8
HEAD_DIM = 64
N_EXPERTS = 32
TOP_K = 4
D_FF = 1024
SWIGLU_ALPHA = 1.702
SWIGLU_LIMIT = 7.0

LANES = 128
SUBLANES = 8
MXU_DIM = 256
VMEM_LIMIT = 56 * 1024 * 1024

ROW_TILE = 512
MERGE_TILE = 1024
SSM_CHUNK = 128
SCAN_LANES = 1024
ATTN_BLOCK = 256
ATTN_HEADS_PER_STEP = 8
ROWSUM_ROWS = 16
EXPERT_BLOCK = 512
UP_CHUNKS_PER_TASK = 2
ROW_DMA_PRIORITY = 1
COLLECT_PARTS = 1
SC_WINDOW = 128
PACK_W = D_MODEL // 2
PACK_HALF = PACK_W // 2

_NEG = -1e30
Q_SCALE = math.log2(math.e) / math.sqrt(HEAD_DIM)
ATTN_SAFE_EXPONENT = 40.0
NORM_BOUND_SLACK = 1.05


def _cparams(sem):
    return pltpu.CompilerParams(dimension_semantics=sem, vmem_limit_bytes=VMEM_LIMIT)


def _full(shape):
    nd = len(shape)
    return pl.BlockSpec(shape, lambda *_: (0,) * nd)


def _sigmoid(x):
    return 0.5 * jnp.tanh(0.5 * x) + 0.5


def _run_lookahead(tasks):
    pending = tasks[0][0]()
    for i, (_, epilogue) in enumerate(tasks):
        result = pending
        if i + 1 < len(tasks):
            pending = tasks[i + 1][0]()
        epilogue(result)


_IN_CHUNKS = {"u": (0, SSM_WIDTH // MXU_DIM)}
for _name in ("q", "k", "v", "gs", "ga"):
    _start = max(first + count for first, count in _IN_CHUNKS.values())
    _IN_CHUNKS[_name] = (_start, D_MODEL // MXU_DIM)
N_IN_CHUNKS = max(first + count for first, count in _IN_CHUNKS.values())


def _prep_w_in_kernel(w_ref, wc_ref, wvt_ref):
    c = pl.program_id(0)
    w = w_ref[...]
    wc_ref[0] = w.astype(BF16)
    v_first, v_count = _IN_CHUNKS["v"]

    @pl.when((c >= v_first) & (c < v_first + v_count))
    def _():
        wvt_ref[0] = w.T.astype(BF16)


def _prep_w_in(w_in):
    v_first, v_count = _IN_CHUNKS["v"]
    return pl.pallas_call(
        _prep_w_in_kernel,
        grid=(N_IN_CHUNKS,),
        in_specs=[pl.BlockSpec((D_MODEL, MXU_DIM), lambda c: (0, c))],
        out_specs=[pl.BlockSpec((1, D_MODEL, MXU_DIM), lambda c: (c, 0, 0)),
                   pl.BlockSpec((1, MXU_DIM, D_MODEL), lambda c: (jnp.clip(c - v_first, 0, v_count - 1), 0, 0))],
        out_shape=[jax.ShapeDtypeStruct((N_IN_CHUNKS, D_MODEL, MXU_DIM), BF16),
                   jax.ShapeDtypeStruct((v_count, MXU_DIM, D_MODEL), BF16)],
        compiler_params=_cparams(("arbitrary",)),
        name="prep_w_in",
    )(w_in)


def _inproj_kernel(x_ref, g1_ref, w_ref, wvt_ref, qg_ref, kg_ref, seg_ref,
                   u_ref, q_ref, k_ref, vt_ref, gs_ref, ga_ref):
    x = x_ref[...]
    ms = jnp.mean(x * x, axis=-1, keepdims=True)
    h = (x * lax.rsqrt(ms + NORM_EPS) * g1_ref[...]).astype(BF16)

    def proj(name, c):
        chunk = _IN_CHUNKS[name][0] + c
        return lambda: jnp.dot(h, w_ref[chunk], preferred_element_type=F32)

    seg = seg_ref[...]
    tasks = []

    def plain_task(name, out_ref, c, fn):
        cols = slice(c * MXU_DIM, (c + 1) * MXU_DIM)

        def epilogue(r):
            out_ref[:, cols] = fn(r)
        return proj(name, c), epilogue

    tasks += [plain_task("u", u_ref, c, lambda r: r) for c in range(_IN_CHUNKS["u"][1])]

    def head_norm_tasks(name, gain_ref, out_ref, scale, c):
        cols = slice(c * MXU_DIM, (c + 1) * MXU_DIM)
        kept = {}

        def after_proj(y):
            kept["y"] = y
            kept["sq"] = (y * y).astype(BF16)

        def after_sum(ss):
            yn = kept["y"] * lax.rsqrt(ss * (1.0 / HEAD_DIM) + NORM_EPS) * gain_ref[:, cols]
            out_ref[:, cols] = (yn * scale).astype(BF16)

        return ((proj(name, c), after_proj),
                (lambda: jnp.dot(kept["sq"], seg, preferred_element_type=F32), after_sum))

    pairs = [head_norm_tasks(name, gain_ref, out_ref, scale, c)
             for name, gain_ref, out_ref, scale in (("q", qg_ref, q_ref, Q_SCALE), ("k", kg_ref, k_ref, 1.0))
             for c in range(_IN_CHUNKS[name][1])]
    tasks.append(pairs[0][0])
    for prev, cur in zip(pairs, pairs[1:]):
        tasks += [cur[0], prev[1]]
    tasks.append(pairs[-1][1])

    to_gate = lambda r: _sigmoid(r).astype(BF16)
    for c in range(_IN_CHUNKS["gs"][1]):
        tasks.append(plain_task("gs", gs_ref, c, to_gate))
        tasks.append(plain_task("ga", ga_ref, c, to_gate))

    def vt_task(c):
        def matmul():
            return lax.dot_general(wvt_ref[c], h, (((1,), (1,)), ((), ())), preferred_element_type=F32)

        def epilogue(r):
            vt_ref[c * MXU_DIM:(c + 1) * MXU_DIM, :] = r.astype(BF16)
        return matmul, epilogue

    tasks += [vt_task(c) for c in range(_IN_CHUNKS["v"][1])]
    _run_lookahead(tasks)


def _in_proj(x2d, gain1, w_in, q_gain, k_gain):
    n_tok = x2d.shape[0]
    tm = min(ROW_TILE, n_tok)
    w_chunks, w_vt = _prep_w_in(w_in)
    seg = (jnp.arange(MXU_DIM)[:, None] // HEAD_DIM == jnp.arange(MXU_DIM)[None, :] // HEAD_DIM).astype(BF16)
    reps = D_MODEL // HEAD_DIM
    qg = jnp.tile(q_gain.astype(F32), reps)[None, :]
    kg = jnp.tile(k_gain.astype(F32), reps)[None, :]
    row = lambda w: pl.BlockSpec((tm, w), lambda i: (i, 0))
    tok = jax.ShapeDtypeStruct((n_tok, D_MODEL), BF16)
    out_shapes = [jax.ShapeDtypeStruct((n_tok, SSM_WIDTH), F32), tok, tok,
                  jax.ShapeDtypeStruct((D_MODEL, n_tok), BF16), tok, tok]
    vt_spec = pl.BlockSpec((D_MODEL, tm), lambda i: (0, i))
    return pl.pallas_call(
        _inproj_kernel,
        grid=(n_tok // tm,),
        in_specs=[row(D_MODEL), _full((1, D_MODEL)), _full(w_chunks.shape), _full(w_vt.shape),
                  _full((1, D_MODEL)), _full((1, D_MODEL)), _full((MXU_DIM, MXU_DIM))],
        out_specs=[row(SSM_WIDTH), row(D_MODEL), row(D_MODEL), vt_spec, row(D_MODEL), row(D_MODEL)],
        out_shape=out_shapes,
        compiler_params=_cparams(("parallel",)),
        name="in_proj",
    )(x2d, gain1, w_chunks, w_vt, qg, kg, seg)


def _ssm_kernel(u_ref, bblk_ref, a_ref, cblk_ref, d_ref, wglu_ref, bglu_ref, o_ref, bu_ref, st_ref, tb_ref):
    n_batch, chunk = u_ref.shape[0], u_ref.shape[1]

    @pl.when(pl.program_id(0) == 0)
    def _():
        st_ref[...] = jnp.zeros_like(st_ref)

    n_planes = SSM_WIDTH // LANES
    for b in range(n_batch):
        for j in range(n_planes):
            tb_ref[j, pl.ds(b, chunk, stride=n_batch), :] = u_ref[b, :, j * LANES:(j + 1) * LANES]
    u = jnp.concatenate([tb_ref[j] for j in range(n_planes)], axis=1)
    u_bf = u.astype(BF16)
    tiles_per_part = N_STATE // MXU_DIM
    ch_per_tile = SSM_WIDTH // tiles_per_part
    n_groups = N_STATE // SCAN_LANES
    tiles_per_group = SCAN_LANES // MXU_DIM
    assert SCAN_LANES * SSM_WIDTH == N_STATE * MXU_DIM

    def input_tiles(g):
        for part in range(2):
            for t in range(tiles_per_group):
                tile = g * tiles_per_group + t
                ch0 = (tile * ch_per_tile) // LANES * LANES
                lanes = slice(part * N_STATE + tile * MXU_DIM, part * N_STATE + (tile + 1) * MXU_DIM)
                bu_ref[:, lanes] = jnp.dot(u_bf[:, ch0:ch0 + LANES], bblk_ref[ch0:ch0 + LANES, lanes],
                                           preferred_element_type=F32)

    def scan(g):
        re = slice(g * SCAN_LANES, (g + 1) * SCAN_LANES)
        im = slice(N_STATE + g * SCAN_LANES, N_STATE + (g + 1) * SCAN_LANES)
        ar, ai = a_ref[:, re], a_ref[:, im]
        xr, xi = st_ref[:, re], st_ref[:, im]
        for t in range(chunk):
            rows = slice(t * SUBLANES, (t + 1) * SUBLANES)
            xr, xi = ar * xr - ai * xi + bu_ref[rows, re], ar * xi + ai * xr + bu_ref[rows, im]
            bu_ref[rows, re] = xr
            bu_ref[rows, im] = xi
        st_ref[:, re] = xr
        st_ref[:, im] = xi

    def output_tile(g):
        cols = slice(g * MXU_DIM, (g + 1) * MXU_DIM)
        acc = None
        for part in range(2):
            lanes = slice(part * N_STATE + g * SCAN_LANES, part * N_STATE + (g + 1) * SCAN_LANES)
            term = jnp.dot(bu_ref[:, lanes].astype(BF16), cblk_ref[lanes, cols], preferred_element_type=F32)
            acc = term if acc is None else acc + term
        return acc

    input_tiles(0)
    ys = []
    for g in range(n_groups):
        if g + 1 < n_groups:
            input_tiles(g + 1)
        scan(g)
        ys.append(output_tile(g))
    y = jnp.concatenate(ys, axis=1) + d_ref[...] * u
    z = jax.nn.gelu(y)
    gate = _sigmoid(jnp.dot(z.astype(BF16), wglu_ref[...], preferred_element_type=F32) + bglu_ref[...])
    out = z * gate
    for j in range(n_planes):
        tb_ref[j] = out[:, j * LANES:(j + 1) * LANES]
    for b in range(n_batch):
        for j in range(n_planes):
            o_ref[b, :, j * LANES:(j + 1) * LANES] = tb_ref[j, pl.ds(b, chunk, stride=n_batch), :].astype(BF16)


def _ssm_params(lambda_re, lambda_im, log_dt, b_re, b_im, c_re, c_im):
    dt = jnp.exp(log_dt.astype(F32))[:, None]
    lr = jnp.minimum(lambda_re.astype(F32), -1e-4)
    li = lambda_im.astype(F32)
    mag = jnp.exp(lr * dt)
    abar_re = mag * jnp.cos(li * dt)
    abar_im = mag * jnp.sin(li * dt)
    den = lr * lr + li * li
    nr = abar_re - 1.0
    coef_re = (nr * lr + abar_im * li) / den
    coef_im = (abar_im * lr - nr * li) / den
    br = b_re.astype(F32)
    bi = b_im.astype(F32)
    bbar_re = coef_re[..., None] * br - coef_im[..., None] * bi
    bbar_im = coef_re[..., None] * bi + coef_im[..., None] * br
    ch_group = jnp.arange(SSM_WIDTH) // SSM_GROUP
    lane_group = jnp.arange(N_STATE) // SSM_STATE

    def expand_b(b):
        rows = b.transpose(0, 2, 1).reshape(SSM_WIDTH, SSM_STATE)
        return jnp.where(ch_group[:, None] == lane_group[None, :], jnp.tile(rows, (1, SSM_GROUPS)), 0.0)

    def expand_c(c):
        rows = c.transpose(0, 2, 1).reshape(N_STATE, SSM_GROUP)
        return jnp.where(lane_group[:, None] == ch_group[None, :], jnp.tile(rows, (1, SSM_GROUPS)), 0.0)

    bblk = jnp.concatenate([expand_b(bbar_re), expand_b(bbar_im)], axis=1).astype(BF16)
    cblk = jnp.concatenate([expand_c(c_re.astype(F32)), -expand_c(c_im.astype(F32))], axis=0).astype(BF16)
    a_row = jnp.concatenate([abar_re.reshape(-1), abar_im.reshape(-1)])[None, :]
    return bblk, cblk, jnp.broadcast_to(a_row, (SUBLANES, 2 * N_STATE))


def _ssm(u, bblk, cblk, a_tile, d_skip, w_glu_bf, b_glu):
    bsz, seq, _ = u.shape
    assert bsz == SUBLANES
    chunk = min(SSM_CHUNK, seq)
    rows = chunk * SUBLANES
    tok_spec = pl.BlockSpec((bsz, chunk, SSM_WIDTH), lambda c: (0, c, 0))
    return pl.pallas_call(
        _ssm_kernel,
        grid=(seq // chunk,),
        in_specs=[tok_spec, _full(bblk.shape), _full(a_tile.shape), _full(cblk.shape),
                  _full((1, SSM_WIDTH)), _full(w_glu_bf.shape), _full((1, SSM_WIDTH))],
        out_specs=tok_spec,
        out_shape=jax.ShapeDtypeStruct(u.shape, BF16),
        scratch_shapes=[pltpu.VMEM((rows, 2 * N_STATE), F32), pltpu.VMEM((SUBLANES, 2 * N_STATE), F32),
                        pltpu.VMEM((SSM_WIDTH // LANES, rows, LANES), F32)],
        compiler_params=_cparams(("arbitrary",)),
        name="ssm",
    )(u, bblk, a_tile, cblk, d_skip, w_glu_bf, b_glu)


def _attn_kernel(lam_ref, sg_ref, q_ref, k_ref, vt_ref, o_ref, acc_ref, vta_ref, *, out_scale, blk, nh, bounded):
    seq = q_ref.shape[1]
    hw = 2 * HEAD_DIM
    lane = lax.broadcasted_iota(jnp.int32, (blk, hw), 1)
    key_i = lax.broadcasted_iota(jnp.int32, (blk, blk), 0)
    qry_i = lax.broadcasted_iota(jnp.int32, (blk, blk), 1)
    keep = key_i <= qry_i
    contract_last = (((1,), (1,)), ((), ()))
    lam = lam_ref[0]
    n_chain = 2 * nh

    for hh in range(nh):
        vta_ref[hh, :hw, :] = vt_ref[hh * hw:(hh + 1) * hw, :]
        vta_ref[hh, hw:, :] = jnp.ones((ROWSUM_ROWS, seq), BF16)

    def q_block(qi, _):
        qrows = pl.ds(pl.multiple_of(qi * blk, blk), blk)
        qs = []
        for hh in range(nh):
            q = q_ref[0, qrows, hh * hw:(hh + 1) * hw]
            zero = jnp.zeros_like(q)
            qs += [jnp.where(lane < HEAD_DIM, q, zero), jnp.where(lane >= HEAD_DIM, q, zero)]
        acc_ref[...] = jnp.zeros_like(acc_ref)

        def kv_block(kb, carry, masked, n_key_blocks=1):
            krows = pl.ds(pl.multiple_of(kb * blk, blk), n_key_blocks * blk)
            scores = [lax.dot_general(k_ref[0, krows, (c // 2) * hw:(c // 2 + 1) * hw], qs[c], contract_last,
                                      preferred_element_type=F32) for c in range(n_chain)]
            out = []
            for c in range(n_chain):
                s = jnp.where(keep, scores[c], _NEG) if masked else scores[c]
                vt = vta_ref[c // 2, :, krows]
                if bounded:
                    acc_ref[c] += jnp.dot(vt, jnp.exp2(s).astype(BF16), preferred_element_type=F32)
                    continue
                m = carry[c]
                m_new = jnp.maximum(m, jnp.max(s, axis=0, keepdims=True))
                p = jnp.exp2(s - m_new).astype(BF16)
                alpha = jnp.exp2(m - m_new)
                out.append(m_new)
                acc_ref[c] = alpha * acc_ref[c] + jnp.dot(vt, p, preferred_element_type=F32)
            return tuple(out)

        if bounded:
            lax.fori_loop(0, qi // 2, lambda pair, c: kv_block(2 * pair, c, False, 2), ())

            @pl.when(qi % 2 == 1)
            def _():
                kv_block(qi - 1, (), False)
            carry = ()
        else:
            carry = lax.fori_loop(0, qi, lambda kb, c: kv_block(kb, c, False),
                                  (jnp.full((1, blk), _NEG, F32),) * n_chain)
        kv_block(qi, carry, True)
        for hh in range(nh):
            a1, a2 = acc_ref[2 * hh], acc_ref[2 * hh + 1]
            l1, l2 = a1[hw:hw + 1, :], a2[hw:hw + 1, :]
            ot = a1[:hw, :] * (1.0 / l1) - a2[:hw, :] * (lam / l2)
            ot = ot * lax.rsqrt(jnp.mean(ot * ot, axis=0, keepdims=True) + NORM_EPS)
            o_ref[0, qrows, hh * hw:(hh + 1) * hw] = (ot.T * sg_ref[...] * out_scale).astype(BF16)
        return 0

    lax.fori_loop(0, seq // blk, q_block, 0)


def _attention(q, k, vt, lam, subln_gain, score_bound, lambda_init, bsz, seq):
    blk = min(ATTN_BLOCK, seq)
    nh = ATTN_HEADS_PER_STEP
    hw = 2 * HEAD_DIM
    q3, k3 = (a.reshape(bsz, seq, D_MODEL) for a in (q, k))
    tok_spec = pl.BlockSpec((1, seq, nh * hw), lambda b, h: (b, 0, h))

    def call(bounded):
        return pl.pallas_call(
            functools.partial(_attn_kernel, out_scale=1.0 - lambda_init, blk=blk, nh=nh, bounded=bounded),
            grid=(bsz, HEADS // nh),
            in_specs=[pl.BlockSpec(memory_space=pltpu.SMEM), _full((1, hw)), tok_spec, tok_spec,
                      pl.BlockSpec((nh * hw, seq), lambda b, h: (h, b))],
            out_specs=tok_spec,
            out_shape=jax.ShapeDtypeStruct((bsz, seq, D_MODEL), BF16),
            scratch_shapes=[pltpu.VMEM((2 * nh, hw + ROWSUM_ROWS, blk), F32),
                            pltpu.VMEM((nh, hw + ROWSUM_ROWS, seq), BF16)],
            compiler_params=_cparams(("parallel", "parallel")),
            name="diff_attn_bounded" if bounded else "diff_attn",
        )

    operands = (lam, subln_gain.astype(F32)[None, :], q3, k3, vt)
    out = lax.cond(score_bound <= ATTN_SAFE_EXPONENT, lambda ops: call(True)(*ops), lambda ops: call(False)(*ops),
                   operands)
    return out.reshape(bsz * seq, D_MODEL)


def _pack_rows(y):
    bits = lax.bitcast_convert_type(y.astype(BF16).astype(F32), jnp.uint32)
    return (bits[:, :PACK_W] >> 16) | (bits[:, PACK_W:] & jnp.uint32(0xFFFF0000))


def _unpack_rows(w):
    lo = lax.bitcast_convert_type(w << 16, F32)
    hi = lax.bitcast_convert_type(w & jnp.uint32(0xFFFF0000), F32)
    return jnp.concatenate([lo, hi], axis=1)


def _merge_kernel(so_ref, ao_ref, gs_ref, ga_ref, x_ref, wps_ref, wpa_ref, wo_ref, g2_ref,
                  wrh_ref, wrl_ref, br_ref, tri_ref,
                  x1_ref, hlo_ref, hhi_ref, route_ref, gate_ref, cnt_ref, run_ref, *, region):
    @pl.when(pl.program_id(0) == 0)
    def _():
        run_ref[...] = jnp.zeros_like(run_ref)

    tm = x_ref.shape[0]
    n_part = 2 if tm % (2 * MXU_DIM) == 0 else 1
    rows_per = tm // n_part
    expert_f = lax.broadcasted_iota(jnp.int32, (N_EXPERTS, rows_per), 0).astype(F32)
    slot_row = lax.broadcasted_iota(jnp.int32, (SUBLANES, rows_per), 0)
    contract_last = (((1,), (1,)), ((), ()))
    parts = [dict(rows=slice(p * rows_per, (p + 1) * rows_per)) for p in range(n_part)]

    def stage_proj(st):
        def matmul():
            return (jnp.dot(so_ref[st["rows"], :], wps_ref[...], preferred_element_type=F32),
                    jnp.dot(ao_ref[st["rows"], :], wpa_ref[...], preferred_element_type=F32))

        def epilogue(r):
            ps, pa = r
            merged = gs_ref[st["rows"], :].astype(F32) * ps + ga_ref[st["rows"], :].astype(F32) * pa
            st["merged"] = merged.astype(BF16)
        return matmul, epilogue

    def stage_out(st):
        def matmul():
            return jnp.dot(st["merged"], wo_ref[...], preferred_element_type=F32)

        def epilogue(r):
            x1 = x_ref[st["rows"], :] + r
            x1_ref[st["rows"], :] = x1
            h2 = x1 * lax.rsqrt(jnp.mean(x1 * x1, axis=-1, keepdims=True) + NORM_EPS) * g2_ref[...]
            words = _pack_rows(h2)
            hlo_ref[st["rows"], :] = words[:, :PACK_HALF]
            hhi_ref[st["rows"], :] = words[:, PACK_HALF:]
            st["h_hi"] = h2.astype(BF16)
            st["h_lo"] = (h2 - st["h_hi"].astype(F32)).astype(BF16)
        return matmul, epilogue

    def stage_router(st):
        def matmul():
            return (lax.dot_general(wrh_ref[...], st["h_hi"], contract_last, preferred_element_type=F32)
                    + lax.dot_general(wrh_ref[...], st["h_lo"], contract_last, preferred_element_type=F32)
                    + lax.dot_general(wrl_ref[...], st["h_hi"], contract_last, preferred_element_type=F32))

        def epilogue(r):
            work = r + br_ref[...]
            onehots, vals, ids = [], [], []
            for _ in range(TOP_K):
                m = jnp.max(work, axis=0, keepdims=True)
                idx = jnp.min(jnp.where(work == m, expert_f, float(N_EXPERTS)), axis=0, keepdims=True)
                oh = expert_f == idx
                onehots.append(oh)
                vals.append(m)
                ids.append(idx.astype(jnp.int32))
                work = jnp.where(oh, -jnp.inf, work)
            exps = [jnp.exp(v - vals[0]) for v in vals]
            den = exps[0] + exps[1] + exps[2] + exps[3]
            st.update(onehots=onehots, ids=ids, gates=[e / den for e in exps],
                      multi=(onehots[0] | onehots[1] | onehots[2] | onehots[3]).astype(F32))
        return matmul, epilogue

    _run_lookahead([stage(st) for stage in (stage_proj, stage_out, stage_router) for st in parts])

    multi = jnp.concatenate([st["multi"] for st in parts], axis=1)
    before = jnp.dot(multi.astype(BF16), tri_ref[...], preferred_element_type=F32) + run_ref[...]
    gate_row = lax.broadcasted_iota(jnp.int32, (LANES, rows_per), 0)
    for st in parts:
        route = jnp.zeros((SUBLANES, rows_per), jnp.int32)
        gates = jnp.zeros((LANES, rows_per), F32)
        for kk in range(TOP_K):
            rank = jnp.sum(jnp.where(st["onehots"][kk], before[:, st["rows"]], 0.0), axis=0, keepdims=True)
            route = jnp.where(slot_row == kk, st["ids"][kk] * region + rank.astype(jnp.int32), route)
            gates = jnp.where(gate_row == kk, st["gates"][kk], gates)
        route_ref[:, st["rows"]] = route
        gate_ref[st["rows"], :] = gates.T
    run = run_ref[...] + jnp.sum(multi, axis=1, keepdims=True)
    run_ref[...] = run
    cnt_ref[...] = jnp.broadcast_to(run, cnt_ref.shape).astype(jnp.int32)


def _merge_route(so, ao, gs, ga, x2d, wps, wpa, wo, gain2, w_router, b_router, region):
    n_tok = x2d.shape[0]
    tm = min(MERGE_TILE, n_tok)
    wr = w_router.astype(F32).T
    wr_hi = wr.astype(BF16)
    wr_lo = (wr - wr_hi.astype(F32)).astype(BF16)
    br = b_router.astype(F32)[:, None]
    tri = (jnp.arange(tm)[:, None] < jnp.arange(tm)[None, :]).astype(BF16)
    row = lambda w: pl.BlockSpec((tm, w), lambda i: (i, 0))
    slots = pl.BlockSpec((SUBLANES, tm), lambda i: (0, i))
    out_shapes = [jax.ShapeDtypeStruct((n_tok, D_MODEL), F32),
                  jax.ShapeDtypeStruct((n_tok, PACK_HALF), jnp.uint32),
                  jax.ShapeDtypeStruct((n_tok, PACK_HALF), jnp.uint32),
                  jax.ShapeDtypeStruct((SUBLANES, n_tok), jnp.int32),
                  jax.ShapeDtypeStruct((n_tok, LANES), F32),
                  jax.ShapeDtypeStruct((N_EXPERTS, LANES), jnp.int32)]
    return pl.pallas_call(
        functools.partial(_merge_kernel, region=region),
        grid=(n_tok // tm,),
        in_specs=[row(SSM_WIDTH), row(D_MODEL), row(D_MODEL), row(D_MODEL), row(D_MODEL),
                  _full(wps.shape), _full(wpa.shape), _full(wo.shape), _full((1, D_MODEL)),
                  _full(wr_hi.shape), _full(wr_lo.shape), _full((N_EXPERTS, 1)), _full((tm, tm))],
        out_specs=[row(D_MODEL), row(PACK_HALF), row(PACK_HALF), slots, row(LANES), _full((N_EXPERTS, LANES))],
        out_shape=out_shapes,
        scratch_shapes=[pltpu.VMEM((N_EXPERTS, 1), F32)],
        compiler_params=_cparams(("arbitrary",)),
        name="merge_route",
    )(so, ao, gs, ga, x2d, wps, wpa, wo, gain2, wr_hi, wr_lo, br, tri)


def _sc_scatter_rows(rows_lo, rows_hi, dest, n_slots):
    n_tok, width = rows_lo.shape
    mesh = plsc.VectorSubcoreMesh(core_axis_name="core", subcore_axis_name="subcore")
    out_type = jax.ShapeDtypeStruct((n_slots, width), rows_lo.dtype)

    @pl.kernel(out_type=(out_type, out_type), mesh=mesh, scratch_types=[])
    def scatter(lo_hbm, hi_hbm, dest_hbm, out_lo_hbm, out_hi_hbm):
        for rows_hbm, out_hbm in ((lo_hbm, out_lo_hbm), (hi_hbm, out_hi_hbm)):
            def body(rows_vmem, dest_vmem, out_hbm=out_hbm):
                for k in range(TOP_K):
                    pltpu.sync_copy(rows_vmem, out_hbm.at[dest_vmem.at[k]])

            pltpu.emit_pipeline(
                body,
                grid=(n_tok // SC_WINDOW,),
                in_specs=[pl.BlockSpec((SC_WINDOW, width), lambda i: (i, 0)),
                          pl.BlockSpec((TOP_K, SC_WINDOW), lambda i: (0, i))],
                out_specs=[],
                core_axis_name=("core", "subcore"),
                dimension_semantics=(pltpu.PARALLEL,),
            )(rows_hbm, dest_hbm)

    return scatter(rows_lo, rows_hi, dest)


def _sc_gather_rows(table_lo, table_hi, idx):
    n = idx.shape[1]
    width = table_lo.shape[1]
    mesh = plsc.VectorSubcoreMesh(core_axis_name="core", subcore_axis_name="subcore")
    out_type = jax.ShapeDtypeStruct((n, width), table_lo.dtype)

    @pl.kernel(out_type=(out_type, out_type), mesh=mesh, scratch_types=[])
    def gather(lo_hbm, hi_hbm, idx_hbm, out_lo_hbm, out_hi_hbm):
        for table_hbm, out_hbm in ((lo_hbm, out_lo_hbm), (hi_hbm, out_hi_hbm)):
            def body(idx_vmem, out_vmem, table_hbm=table_hbm):
                pltpu.sync_copy(table_hbm.at[idx_vmem.at[0]], out_vmem)

            pltpu.emit_pipeline(
                body,
                grid=(n // SC_WINDOW,),
                in_specs=[pl.BlockSpec((1, SC_WINDOW), lambda i: (0, i))],
                out_specs=[pl.BlockSpec((SC_WINDOW, width), lambda i: (i, 0))],
                core_axis_name=("core", "subcore"),
                dimension_semantics=(pltpu.PARALLEL,),
            )(idx_hbm, out_hbm)

    return gather(table_lo, table_hi, idx)


def _expert_kernel(cnt_ref, xlo_hbm, xhi_hbm, w1_ref, b1_ref, w2_ref, b2_ref, perm_ref, ylo_hbm, yhi_hbm,
                   w1p_ref, w2b_ref, x_ref, act_ref, xin_ref, yout_ref, in_sem, out_sem, *, region):
    e = pl.program_id(0)
    n_blk = (cnt_ref[e] + EXPERT_BLOCK - 1) // EXPERT_BLOCK
    base = e * region
    n_chunks = (2 * D_FF) // MXU_DIM
    x_hbm = (xlo_hbm, xhi_hbm)
    y_hbm = (ylo_hbm, yhi_hbm)

    def rows_of(j):
        return pl.ds(pl.multiple_of(base + j * EXPERT_BLOCK, EXPERT_BLOCK), EXPERT_BLOCK)

    def in_copy(j, slot, half):
        return pltpu.make_async_copy(x_hbm[half].at[rows_of(j), :], xin_ref.at[slot, half], in_sem.at[slot, half])

    def out_copy(j, slot, half):
        return pltpu.make_async_copy(yout_ref.at[slot, half], y_hbm[half].at[rows_of(j), :], out_sem.at[slot, half])

    @pl.when(n_blk > 0)
    def _():
        for half in range(2):
            in_copy(0, 0, half).start(priority=ROW_DMA_PRIORITY)

    perm = perm_ref[...]
    for c in range(n_chunks):
        cols = slice(c * MXU_DIM, (c + 1) * MXU_DIM)
        w1p_ref[c] = jnp.dot(w1_ref[0, :, cols].astype(BF16), perm, preferred_element_type=F32).astype(BF16)
    for c in range(D_MODEL // MXU_DIM):
        w2b_ref[c] = w2_ref[0, :, c * MXU_DIM:(c + 1) * MXU_DIM].astype(BF16)

    def mlp_rows(n_rows, xin, yout):
        words = jnp.concatenate([xin[0], xin[1]], axis=1)
        x_ref[:n_rows, :] = _unpack_rows(words).astype(BF16)

        def up_task(c0):
            chunks = range(c0, c0 + UP_CHUNKS_PER_TASK)

            def matmul():
                return [jnp.dot(x_ref[:n_rows, :], w1p_ref[c], preferred_element_type=F32) for c in chunks]

            def epilogue(results):
                gates, ups = [], []
                for c, r in zip(chunks, results):
                    h = r + b1_ref[0, :, c * MXU_DIM:(c + 1) * MXU_DIM]
                    gates.append(jnp.minimum(h[:, :LANES], SWIGLU_LIMIT))
                    ups.append(jnp.clip(h[:, LANES:], -SWIGLU_LIMIT, SWIGLU_LIMIT))
                gate = jnp.concatenate(gates, axis=1)
                up = jnp.concatenate(ups, axis=1)
                glu = gate * _sigmoid(SWIGLU_ALPHA * gate)
                act_ref[:n_rows, c0 * LANES:(c0 + UP_CHUNKS_PER_TASK) * LANES] = ((up + 1.0) * glu).astype(BF16)
            return matmul, epilogue

        _run_lookahead([up_task(c0) for c0 in range(0, n_chunks, UP_CHUNKS_PER_TASK)])

        def down_task(c):
            cols = slice(c * MXU_DIM, (c + 1) * MXU_DIM)
            half, high = c % 2, c // 2

            def epilogue(r):
                bits = lax.bitcast_convert_type((r + b2_ref[0, :, cols]).astype(BF16).astype(F32), jnp.uint32)
                if high:
                    yout[half] = yout[half] | (bits & jnp.uint32(0xFFFF0000))
                else:
                    yout[half] = bits >> 16
            return (lambda: jnp.dot(act_ref[:n_rows, :], w2b_ref[c], preferred_element_type=F32)), epilogue

        _run_lookahead([down_task(c) for c in range(D_MODEL // MXU_DIM)])

    def block(j, _):
        slot = j % 2
        for half in range(2):
            in_copy(j, slot, half).wait()

        @pl.when(j + 1 < n_blk)
        def _():
            for half in range(2):
                in_copy(j + 1, 1 - slot, half).start(priority=ROW_DMA_PRIORITY)

        @pl.when(j >= 2)
        def _():
            for half in range(2):
                out_copy(j - 2, slot, half).wait()

        mlp_rows(EXPERT_BLOCK, xin_ref.at[slot], yout_ref.at[slot])
        for half in range(2):
            out_copy(j, slot, half).start(priority=ROW_DMA_PRIORITY)
        return 0

    lax.fori_loop(0, n_blk, block, 0)

    for back in (2, 1):
        @pl.when(n_blk >= back)
        def _(back=back):
            j = n_blk - back
            for half in range(2):
                out_copy(j, j % 2, half).wait()


def _gate_up_order():
    j = jnp.arange(MXU_DIM)
    within = jnp.where(j < LANES, 2 * j, 2 * (j - LANES) + 1)
    return within


def _experts(xs_lo, xs_hi, w1, b1, w2, b2, counts, region):
    within = _gate_up_order()
    perm = (jnp.arange(MXU_DIM)[:, None] == within[None, :]).astype(BF16)
    order = (jnp.arange(0, 2 * D_FF, MXU_DIM)[:, None] + within[None, :]).reshape(-1)
    b1p = b1.astype(F32)[:, order][:, None, :]
    b2r = b2.astype(F32)[:, None, :]
    tb = EXPERT_BLOCK
    hbm = pl.BlockSpec(memory_space=pl.ANY)
    wspec = lambda shape: pl.BlockSpec((1,) + shape, lambda e, cnt: (e, 0, 0))
    grid_spec = pltpu.PrefetchScalarGridSpec(
        num_scalar_prefetch=1,
        grid=(N_EXPERTS,),
        in_specs=[hbm, hbm, wspec((D_MODEL, 2 * D_FF)), wspec((1, 2 * D_FF)),
                  wspec((D_FF, D_MODEL)), wspec((1, D_MODEL)),
                  pl.BlockSpec((MXU_DIM, MXU_DIM), lambda e, cnt: (0, 0))],
        out_specs=[hbm, hbm],
        scratch_shapes=[pltpu.VMEM((2 * D_FF // MXU_DIM, D_MODEL, MXU_DIM), BF16),
                        pltpu.VMEM((D_MODEL // MXU_DIM, D_FF, MXU_DIM), BF16),
                        pltpu.VMEM((tb, D_MODEL), BF16), pltpu.VMEM((tb, D_FF), BF16),
                        pltpu.VMEM((2, 2, tb, PACK_HALF), jnp.uint32), pltpu.VMEM((2, 2, tb, PACK_HALF), jnp.uint32),
                        pltpu.SemaphoreType.DMA((2, 2)), pltpu.SemaphoreType.DMA((2, 2))],
    )
    return pl.pallas_call(
        functools.partial(_expert_kernel, region=region),
        grid_spec=grid_spec,
        out_shape=[jax.ShapeDtypeStruct(xs_lo.shape, jnp.uint32)] * 2,
        compiler_params=_cparams(("arbitrary",)),
        name="experts",
    )(counts, xs_lo, xs_hi, w1, b1p, w2, b2r, perm)


def _combine_kernel(x1_ref, gate_ref, *refs):
    lo_refs, hi_refs, o_ref = refs[:TOP_K], refs[TOP_K:2 * TOP_K], refs[-1]
    acc = x1_ref[...]
    gates = gate_ref[...]
    for kk in range(TOP_K):
        words = jnp.concatenate([lo_refs[kk][...], hi_refs[kk][...]], axis=1)
        acc = acc + gates[:, kk:kk + 1] * _unpack_rows(words)
    o_ref[...] = acc


def _combine_part(x1, gates, yg_lo, yg_hi, part, n_parts, out_so_far):
    n_tok = x1.shape[0]
    part_tok = n_tok // n_parts
    tm = min(ROW_TILE, part_tok)
    nblk = part_tok // tm
    row = lambda w: pl.BlockSpec((tm, w), lambda i: (part * nblk + i, 0))
    plane = lambda kk: pl.BlockSpec((tm, PACK_HALF), lambda i, kk=kk: (kk * nblk + i, 0))
    planes = [plane(kk) for kk in range(TOP_K)]
    operands = [x1, gates, *([yg_lo] * TOP_K), *([yg_hi] * TOP_K)]
    in_specs = [row(D_MODEL), row(LANES)] + planes + planes
    aliases = {}
    if out_so_far is not None:
        aliases = {len(operands): 0}
        operands.append(out_so_far)
        in_specs.append(pl.BlockSpec(memory_space=pl.ANY))
    return pl.pallas_call(
        _combine_kernel,
        grid=(nblk,),
        in_specs=in_specs,
        out_specs=row(D_MODEL),
        out_shape=jax.ShapeDtypeStruct((n_tok, D_MODEL), F32),
        input_output_aliases=aliases,
        compiler_params=_cparams(("parallel",)),
        name="combine",
    )(*operands)


def kernel(x, norm1_gain, w_in, lambda_re, lambda_im, log_dt, ssm_b_re, ssm_b_im, ssm_c_re, ssm_c_im, ssm_d, w_glu, b_glu, q_norm_gain, k_norm_gain, lambda_q1, lambda_k1, lambda_q2, lambda_k2, subln_gain, w_proj_ssm, w_proj_attn, w_out, norm2_gain, w_router, b_router, w_exp1, b_exp1, w_exp2, b_exp2):
    bsz, seq, d = x.shape
    n_tok = bsz * seq
    depth = norm1_gain.shape[0]
    row1 = lambda a: a.astype(F32).reshape(1, -1)
    for l in range(depth):
        lambda_init = 0.8 - 0.6 * math.exp(-0.3 * l)
        x2d = x.reshape(n_tok, d)

        u, q, k, vt, gs, ga = _in_proj(x2d, row1(norm1_gain[l]), w_in[l], q_norm_gain[l], k_norm_gain[l])

        bblk, cblk, a_tile = _ssm_params(lambda_re[l], lambda_im[l], log_dt[l], ssm_b_re[l], ssm_b_im[l],
                                         ssm_c_re[l], ssm_c_im[l])
        so = _ssm(u.reshape(bsz, seq, SSM_WIDTH), bblk, cblk, a_tile, row1(ssm_d[l]),
                  w_glu[l].astype(BF16), row1(b_glu[l])).reshape(n_tok, SSM_WIDTH)

        lam = (jnp.exp(jnp.sum(lambda_q1[l].astype(F32) * lambda_k1[l].astype(F32)))
               - jnp.exp(jnp.sum(lambda_q2[l].astype(F32) * lambda_k2[l].astype(F32)))
               + lambda_init).reshape(1)
        score_bound = (NORM_BOUND_SLACK * HEAD_DIM * Q_SCALE * jnp.max(jnp.abs(q_norm_gain[l].astype(F32)))
                       * jnp.max(jnp.abs(k_norm_gain[l].astype(F32))))
        ao = _attention(q, k, vt, lam, subln_gain[l], score_bound, lambda_init, bsz, seq)

        region = -(-n_tok // EXPERT_BLOCK) * EXPERT_BLOCK
        x1, h_lo, h_hi, route, gates, counts = _merge_route(
            so, ao, gs, ga, x2d, w_proj_ssm[l].astype(BF16), w_proj_attn[l].astype(BF16),
            w_out[l].astype(BF16), row1(norm2_gain[l]), w_router[l], b_router[l], region)

        dest = route[:TOP_K]
        n_slots = N_EXPERTS * region
        xs_lo, xs_hi = _sc_scatter_rows(h_lo, h_hi, dest, n_slots)

        ys_lo, ys_hi = _experts(xs_lo, xs_hi, w_exp1[l], b_exp1[l], w_exp2[l], b_exp2[l], counts[:, 0], region)

        part_tok = n_tok // COLLECT_PARTS
        out = None
        for part in range(COLLECT_PARTS):
            flat = dest[:, part * part_tok:(part + 1) * part_tok].reshape(1, TOP_K * part_tok)
            yg_lo, yg_hi = _sc_gather_rows(ys_lo, ys_hi, flat)
            out = _combine_part(x1, gates, yg_lo, yg_hi, part, COLLECT_PARTS, out)
        x = out.reshape(bsz, seq, d)
    return x
```

```python
import functools
import math

import jax
import jax.numpy as jnp
from jax import lax
from jax.experimental import pallas as pl
from jax.experimental.pallas import tpu as pltpu
from jax.experimental.pallas import tpu_sc as plsc

F32 = jnp.float32
BF16 = jnp.bfloat16

D_MODEL = 1024
NORM_EPS = 1e-5
SSM_WIDTH = 512
SSM_GROUP = 16
SSM_GROUPS = 32
SSM_STATE = 64
N_STATE = SSM_GROUPS * SSM_STATE
HEADS = 8
HEAD_DIM = 64
N_EXPERTS = 32
TOP_K = 4
D_FF = 1024
SWIGLU_ALPHA = 1.702
SWIGLU_LIMIT = 7.0

LANES = 128
SUBLANES = 8
MXU_DIM = 256
VMEM_LIMIT = 56 * 1024 * 1024

ROW_TILE = 512
MERGE_TILE = 1024
SSM_CHUNK = 128
SCAN_LANES = 1024
ATTN_BLOCK = 256
ATTN_HEADS_PER_STEP = 8
ROWSUM_ROWS = 16
EXPERT_BLOCK = 512
UP_CHUNKS_PER_TASK = 2
ROW_DMA_PRIORITY = 1
SC_WINDOW = 128
PACK_W = D_MODEL // 2
PACK_HALF = PACK_W // 2

_NEG = -1e30
Q_SCALE = math.log2(math.e) / math.sqrt(HEAD_DIM)
ATTN_SAFE_EXPONENT = 40.0
NORM_BOUND_SLACK = 1.05


def _cparams(sem):
    return pltpu.CompilerParams(dimension_semantics=sem, vmem_limit_bytes=VMEM_LIMIT)


def _full(shape):
    nd = len(shape)
    return pl.BlockSpec(shape, lambda *_: (0,) * nd)


def _sigmoid(x):
    return 0.5 * jnp.tanh(0.5 * x) + 0.5


def _run_lookahead(tasks):
    pending = tasks[0][0]()
    for i, (_, epilogue) in enumerate(tasks):
        result = pending
        if i + 1 < len(tasks):
            pending = tasks[i + 1][0]()
        epilogue(result)


_IN_CHUNKS = {"u": (0, SSM_WIDTH // MXU_DIM)}
for _name in ("q", "k", "v", "gs", "ga"):
    _start = max(first + count for first, count in _IN_CHUNKS.values())
    _IN_CHUNKS[_name] = (_start, D_MODEL // MXU_DIM)
N_IN_CHUNKS = max(first + count for first, count in _IN_CHUNKS.values())


def _prep_w_in_kernel(w_ref, wc_ref, wvt_ref):
    c = pl.program_id(0)
    w = w_ref[...]
    wc_ref[0] = w.astype(BF16)
    v_first, v_count = _IN_CHUNKS["v"]

    @pl.when((c >= v_first) & (c < v_first + v_count))
    def _():
        wvt_ref[0] = w.T.astype(BF16)


def _prep_w_in(w_in):
    v_first, v_count = _IN_CHUNKS["v"]
    return pl.pallas_call(
        _prep_w_in_kernel,
        grid=(N_IN_CHUNKS,),
        in_specs=[pl.BlockSpec((D_MODEL, MXU_DIM), lambda c: (0, c))],
        out_specs=[pl.BlockSpec((1, D_MODEL, MXU_DIM), lambda c: (c, 0, 0)),
                   pl.BlockSpec((1, MXU_DIM, D_MODEL), lambda c: (jnp.clip(c - v_first, 0, v_count - 1), 0, 0))],
        out_shape=[jax.ShapeDtypeStruct((N_IN_CHUNKS, D_MODEL, MXU_DIM), BF16),
                   jax.ShapeDtypeStruct((v_count, MXU_DIM, D_MODEL), BF16)],
        compiler_params=_cparams(("arbitrary",)),
        name="prep_w_in",
    )(w_in)


def _inproj_kernel(x_ref, g1_ref, w_ref, wvt_ref, qg_ref, kg_ref, seg_ref,
                   u_ref, q_ref, k_ref, vt_ref, gs_ref, ga_ref):
    x = x_ref[...]
    ms = jnp.mean(x * x, axis=-1, keepdims=True)
    h = (x * lax.rsqrt(ms + NORM_EPS) * g1_ref[...]).astype(BF16)

    def proj(name, c):
        chunk = _IN_CHUNKS[name][0] + c
        return lambda: jnp.dot(h, w_ref[chunk], preferred_element_type=F32)

    seg = seg_ref[...]
    tasks = []

    def plain_task(name, out_ref, c, fn):
        cols = slice(c * MXU_DIM, (c + 1) * MXU_DIM)

        def epilogue(r):
            out_ref[:, cols] = fn(r)
        return proj(name, c), epilogue

    tasks += [plain_task("u", u_ref, c, lambda r: r) for c in range(_IN_CHUNKS["u"][1])]

    def head_norm_tasks(name, gain_ref, out_ref, scale, c):
        cols = slice(c * MXU_DIM, (c + 1) * MXU_DIM)
        kept = {}

        def after_proj(y):
            kept["y"] = y
            kept["sq"] = (y * y).astype(BF16)

        def after_sum(ss):
            yn = kept["y"] * lax.rsqrt(ss * (1.0 / HEAD_DIM) + NORM_EPS) * gain_ref[:, cols]
            out_ref[:, cols] = (yn * scale).astype(BF16)

        return ((proj(name, c), after_proj),
                (lambda: jnp.dot(kept["sq"], seg, preferred_element_type=F32), after_sum))

    pairs = [head_norm_tasks(name, gain_ref, out_ref, scale, c)
             for name, gain_ref, out_ref, scale in (("q", qg_ref, q_ref, Q_SCALE), ("k", kg_ref, k_ref, 1.0))
             for c in range(_IN_CHUNKS[name][1])]
    tasks.append(pairs[0][0])
    for prev, cur in zip(pairs, pairs[1:]):
        tasks += [cur[0], prev[1]]
    tasks.append(pairs[-1][1])

    to_gate = lambda r: _sigmoid(r).astype(BF16)
    for c in range(_IN_CHUNKS["gs"][1]):
        tasks.append(plain_task("gs", gs_ref, c, to_gate))
        tasks.append(plain_task("ga", ga_ref, c, to_gate))

    def vt_task(c):
        def matmul():
            return lax.dot_general(wvt_ref[c], h, (((1,), (1,)), ((), ())), preferred_element_type=F32)

        def epilogue(r):
            vt_ref[c * MXU_DIM:(c + 1) * MXU_DIM, :] = r.astype(BF16)
        return matmul, epilogue

    tasks += [vt_task(c) for c in range(_IN_CHUNKS["v"][1])]
    _run_lookahead(tasks)


def _in_proj(x2d, gain1, w_in, q_gain, k_gain):
    n_tok = x2d.shape[0]
    tm = min(ROW_TILE, n_tok)
    w_chunks, w_vt = _prep_w_in(w_in)
    seg = (jnp.arange(MXU_DIM)[:, None] // HEAD_DIM == jnp.arange(MXU_DIM)[None, :] // HEAD_DIM).astype(BF16)
    reps = D_MODEL // HEAD_DIM
    qg = jnp.tile(q_gain.astype(F32), reps)[None, :]
    kg = jnp.tile(k_gain.astype(F32), reps)[None, :]
    row = lambda w: pl.BlockSpec((tm, w), lambda i: (i, 0))
    tok = jax.ShapeDtypeStruct((n_tok, D_MODEL), BF16)
    out_shapes = [jax.ShapeDtypeStruct((n_tok, SSM_WIDTH), F32), tok, tok,
                  jax.ShapeDtypeStruct((D_MODEL, n_tok), BF16), tok, tok]
    vt_spec = pl.BlockSpec((D_MODEL, tm), lambda i: (0, i))
    return pl.pallas_call(
        _inproj_kernel,
        grid=(n_tok // tm,),
        in_specs=[row(D_MODEL), _full((1, D_MODEL)), _full(w_chunks.shape), _full(w_vt.shape),
                  _full((1, D_MODEL)), _full((1, D_MODEL)), _full((MXU_DIM, MXU_DIM))],
        out_specs=[row(SSM_WIDTH), row(D_MODEL), row(D_MODEL), vt_spec, row(D_MODEL), row(D_MODEL)],
        out_shape=out_shapes,
        compiler_params=_cparams(("parallel",)),
        name="in_proj",
    )(x2d, gain1, w_chunks, w_vt, qg, kg, seg)


def _ssm_kernel(u_ref, bblk_ref, a_ref, cblk_ref, d_ref, wglu_ref, bglu_ref, o_ref, bu_ref, st_ref, tb_ref):
    n_batch, chunk = u_ref.shape[0], u_ref.shape[1]

    @pl.when(pl.program_id(0) == 0)
    def _():
        st_ref[...] = jnp.zeros_like(st_ref)

    n_planes = SSM_WIDTH // LANES
    for b in range(n_batch):
        for j in range(n_planes):
            tb_ref[j, pl.ds(b, chunk, stride=n_batch), :] = u_ref[b, :, j * LANES:(j + 1) * LANES]
    u = jnp.concatenate([tb_ref[j] for j in range(n_planes)], axis=1)
    u_bf = u.astype(BF16)
    tiles_per_part = N_STATE // MXU_DIM
    ch_per_tile = SSM_WIDTH // tiles_per_part
    n_groups = N_STATE // SCAN_LANES
    tiles_per_group = SCAN_LANES // MXU_DIM
    assert SCAN_LANES * SSM_WIDTH == N_STATE * MXU_DIM

    def input_tiles(g):
        for part in range(2):
            for t in range(tiles_per_group):
                tile = g * tiles_per_group + t
                ch0 = (tile * ch_per_tile) // LANES * LANES
                lanes = slice(part * N_STATE + tile * MXU_DIM, part * N_STATE + (tile + 1) * MXU_DIM)
                bu_ref[:, lanes] = jnp.dot(u_bf[:, ch0:ch0 + LANES], bblk_ref[ch0:ch0 + LANES, lanes],
                                           preferred_element_type=F32)

    def scan(g):
        re = slice(g * SCAN_LANES, (g + 1) * SCAN_LANES)
        im = slice(N_STATE + g * SCAN_LANES, N_STATE + (g + 1) * SCAN_LANES)
        ar, ai = a_ref[:, re], a_ref[:, im]
        xr, xi = st_ref[:, re], st_ref[:, im]
        for t in range(chunk):
            rows = slice(t * SUBLANES, (t + 1) * SUBLANES)
            xr, xi = ar * xr - ai * xi + bu_ref[rows, re], ar * xi + ai * xr + bu_ref[rows, im]
            bu_ref[rows, re] = xr
            bu_ref[rows, im] = xi
        st_ref[:, re] = xr
        st_ref[:, im] = xi

    def output_tile(g):
        cols = slice(g * MXU_DIM, (g + 1) * MXU_DIM)
        acc = None
        for part in range(2):
            lanes = slice(part * N_STATE + g * SCAN_LANES, part * N_STATE + (g + 1) * SCAN_LANES)
            term = jnp.dot(bu_ref[:, lanes].astype(BF16), cblk_ref[lanes, cols], preferred_element_type=F32)
            acc = term if acc is None else acc + term
        return acc

    input_tiles(0)
    ys = []
    for g in range(n_groups):
        if g + 1 < n_groups:
            input_tiles(g + 1)
        scan(g)
        ys.append(output_tile(g))
    y = jnp.concatenate(ys, axis=1) + d_ref[...] * u
    z = jax.nn.gelu(y)
    gate = _sigmoid(jnp.dot(z.astype(BF16), wglu_ref[...], preferred_element_type=F32) + bglu_ref[...])
    out = z * gate
    for j in range(n_planes):
        tb_ref[j] = out[:, j * LANES:(j + 1) * LANES]
    for b in range(n_batch):
        for j in range(n_planes):
            o_ref[b, :, j * LANES:(j + 1) * LANES] = tb_ref[j, pl.ds(b, chunk, stride=n_batch), :].astype(BF16)


def _ssm_params(lambda_re, lambda_im, log_dt, b_re, b_im, c_re, c_im):
    dt = jnp.exp(log_dt.astype(F32))[:, None]
    lr = jnp.minimum(lambda_re.astype(F32), -1e-4)
    li = lambda_im.astype(F32)
    mag = jnp.exp(lr * dt)
    abar_re = mag * jnp.cos(li * dt)
    abar_im = mag * jnp.sin(li * dt)
    den = lr * lr + li * li
    nr = abar_re - 1.0
    coef_re = (nr * lr + abar_im * li) / den
    coef_im = (abar_im * lr - nr * li) / den
    br = b_re.astype(F32)
    bi = b_im.astype(F32)
    bbar_re = coef_re[..., None] * br - coef_im[..., None] * bi
    bbar_im = coef_re[..., None] * bi + coef_im[..., None] * br
    ch_group = jnp.arange(SSM_WIDTH) // SSM_GROUP
    lane_group = jnp.arange(N_STATE) // SSM_STATE

    def expand_b(b):
        rows = b.transpose(0, 2, 1).reshape(SSM_WIDTH, SSM_STATE)
        return jnp.where(ch_group[:, None] == lane_group[None, :], jnp.tile(rows, (1, SSM_GROUPS)), 0.0)

    def expand_c(c):
        rows = c.transpose(0, 2, 1).reshape(N_STATE, SSM_GROUP)
        return jnp.where(lane_group[:, None] == ch_group[None, :], jnp.tile(rows, (1, SSM_GROUPS)), 0.0)

    bblk = jnp.concatenate([expand_b(bbar_re), expand_b(bbar_im)], axis=1).astype(BF16)
    cblk = jnp.concatenate([expand_c(c_re.astype(F32)), -expand_c(c_im.astype(F32))], axis=0).astype(BF16)
    a_row = jnp.concatenate([abar_re.reshape(-1), abar_im.reshape(-1)])[None, :]
    return bblk, cblk, jnp.broadcast_to(a_row, (SUBLANES, 2 * N_STATE))


def _ssm(u, bblk, cblk, a_tile, d_skip, w_glu_bf, b_glu):
    bsz, seq, _ = u.shape
    assert bsz == SUBLANES
    chunk = min(SSM_CHUNK, seq)
    rows = chunk * SUBLANES
    tok_spec = pl.BlockSpec((bsz, chunk, SSM_WIDTH), lambda c: (0, c, 0))
    return pl.pallas_call(
        _ssm_kernel,
        grid=(seq // chunk,),
        in_specs=[tok_spec, _full(bblk.shape), _full(a_tile.shape), _full(cblk.shape),
                  _full((1, SSM_WIDTH)), _full(w_glu_bf.shape), _full((1, SSM_WIDTH))],
        out_specs=tok_spec,
        out_shape=jax.ShapeDtypeStruct(u.shape, BF16),
        scratch_shapes=[pltpu.VMEM((rows, 2 * N_STATE), F32), pltpu.VMEM((SUBLANES, 2 * N_STATE), F32),
                        pltpu.VMEM((SSM_WIDTH // LANES, rows, LANES), F32)],
        compiler_params=_cparams(("arbitrary",)),
        name="ssm",
    )(u, bblk, a_tile, cblk, d_skip, w_glu_bf, b_glu)


def _attn_kernel(lam_ref, sg_ref, q_ref, k_ref, vt_ref, o_ref, acc_ref, vta_ref, *, out_scale, blk, nh, bounded):
    seq = q_ref.shape[1]
    hw = 2 * HEAD_DIM
    lane = lax.broadcasted_iota(jnp.int32, (blk, hw), 1)
    key_i = lax.broadcasted_iota(jnp.int32, (blk, blk), 0)
    qry_i = lax.broadcasted_iota(jnp.int32, (blk, blk), 1)
    keep = key_i <= qry_i
    contract_last = (((1,), (1,)), ((), ()))
    lam = lam_ref[0]
    n_chain = 2 * nh

    for hh in range(nh):
        vta_ref[hh, :hw, :] = vt_ref[hh * hw:(hh + 1) * hw, :]
        vta_ref[hh, hw:, :] = jnp.ones((ROWSUM_ROWS, seq), BF16)

    def q_block(qi, _):
        qrows = pl.ds(pl.multiple_of(qi * blk, blk), blk)
        qs = []
        for hh in range(nh):
            q = q_ref[0, qrows, hh * hw:(hh + 1) * hw]
            zero = jnp.zeros_like(q)
            qs += [jnp.where(lane < HEAD_DIM, q, zero), jnp.where(lane >= HEAD_DIM, q, zero)]
        acc_ref[...] = jnp.zeros_like(acc_ref)

        def kv_block(kb, carry, masked, n_key_blocks=1):
            krows = pl.ds(pl.multiple_of(kb * blk, blk), n_key_blocks * blk)
            scores = [lax.dot_general(k_ref[0, krows, (c // 2) * hw:(c // 2 + 1) * hw], qs[c], contract_last,
                                      preferred_element_type=F32) for c in range(n_chain)]
            out = []
            for c in range(n_chain):
                s = jnp.where(keep, scores[c], _NEG) if masked else scores[c]
                vt = vta_ref[c // 2, :, krows]
                if bounded:
                    acc_ref[c] += jnp.dot(vt, jnp.exp2(s).astype(BF16), preferred_element_type=F32)
                    continue
                m = carry[c]
                m_new = jnp.maximum(m, jnp.max(s, axis=0, keepdims=True))
                p = jnp.exp2(s - m_new).astype(BF16)
                alpha = jnp.exp2(m - m_new)
                out.append(m_new)
                acc_ref[c] = alpha * acc_ref[c] + jnp.dot(vt, p, preferred_element_type=F32)
            return tuple(out)

        if bounded:
            lax.fori_loop(0, qi // 2, lambda pair, c: kv_block(2 * pair, c, False, 2), ())

            @pl.when(qi % 2 == 1)
            def _():
                kv_block(qi - 1, (), False)
            carry = ()
        else:
            carry = lax.fori_loop(0, qi, lambda kb, c: kv_block(kb, c, False),
                                  (jnp.full((1, blk), _NEG, F32),) * n_chain)
        kv_block(qi, carry, True)
        for hh in range(nh):
            a1, a2 = acc_ref[2 * hh], acc_ref[2 * hh + 1]
            l1, l2 = a1[hw:hw + 1, :], a2[hw:hw + 1, :]
            ot = a1[:hw, :] * (1.0 / l1) - a2[:hw, :] * (lam / l2)
            ot = ot * lax.rsqrt(jnp.mean(ot * ot, axis=0, keepdims=True) + NORM_EPS)
            o_ref[0, qrows, hh * hw:(hh + 1) * hw] = (ot.T * sg_ref[...] * out_scale).astype(BF16)
        return 0

    lax.fori_loop(0, seq // blk, q_block, 0)


def _attention(q, k, vt, lam, subln_gain, score_bound, lambda_init, bsz, seq):
    blk = min(ATTN_BLOCK, seq)
    nh = ATTN_HEADS_PER_STEP
    hw = 2 * HEAD_DIM
    q3, k3 = (a.reshape(bsz, seq, D_MODEL) for a in (q, k))
    tok_spec = pl.BlockSpec((1, seq, nh * hw), lambda b, h: (b, 0, h))

    def call(bounded):
        return pl.pallas_call(
            functools.partial(_attn_kernel, out_scale=1.0 - lambda_init, blk=blk, nh=nh, bounded=bounded),
            grid=(bsz, HEADS // nh),
            in_specs=[pl.BlockSpec(memory_space=pltpu.SMEM), _full((1, hw)), tok_spec, tok_spec,
                      pl.BlockSpec((nh * hw, seq), lambda b, h: (h, b))],
            out_specs=tok_spec,
            out_shape=jax.ShapeDtypeStruct((bsz, seq, D_MODEL), BF16),
            scratch_shapes=[pltpu.VMEM((2 * nh, hw + ROWSUM_ROWS, blk), F32),
                            pltpu.VMEM((nh, hw + ROWSUM_ROWS, seq), BF16)],
            compiler_params=_cparams(("parallel", "parallel")),
            name="diff_attn_bounded" if bounded else "diff_attn",
        )

    operands = (lam, subln_gain.astype(F32)[None, :], q3, k3, vt)
    out = lax.cond(score_bound <= ATTN_SAFE_EXPONENT, lambda ops: call(True)(*ops), lambda ops: call(False)(*ops),
                   operands)
    return out.reshape(bsz * seq, D_MODEL)


def _pack_rows(y):
    bits = lax.bitcast_convert_type(y.astype(BF16).astype(F32), jnp.uint32)
    return (bits[:, :PACK_W] >> 16) | (bits[:, PACK_W:] & jnp.uint32(0xFFFF0000))


def _unpack_rows(w):
    lo = lax.bitcast_convert_type(w << 16, F32)
    hi = lax.bitcast_convert_type(w & jnp.uint32(0xFFFF0000), F32)
    return jnp.concatenate([lo, hi], axis=1)


def _merge_kernel(so_ref, ao_ref, gs_ref, ga_ref, x_ref, wps_ref, wpa_ref, wo_ref, g2_ref,
                  wrh_ref, wrl_ref, br_ref, tri_ref,
                  x1_ref, hlo_ref, hhi_ref, route_ref, gate_ref, cnt_ref, run_ref, *, region):
    @pl.when(pl.program_id(0) == 0)
    def _():
        run_ref[...] = jnp.zeros_like(run_ref)

    tm = x_ref.shape[0]
    n_part = 2 if tm % (2 * MXU_DIM) == 0 else 1
    rows_per = tm // n_part
    expert_f = lax.broadcasted_iota(jnp.int32, (N_EXPERTS, rows_per), 0).astype(F32)
    slot_row = lax.broadcasted_iota(jnp.int32, (SUBLANES, rows_per), 0)
    contract_last = (((1,), (1,)), ((), ()))
    parts = [dict(rows=slice(p * rows_per, (p + 1) * rows_per)) for p in range(n_part)]

    def stage_proj(st):
        def matmul():
            return (jnp.dot(so_ref[st["rows"], :], wps_ref[...], preferred_element_type=F32),
                    jnp.dot(ao_ref[st["rows"], :], wpa_ref[...], preferred_element_type=F32))

        def epilogue(r):
            ps, pa = r
            merged = gs_ref[st["rows"], :].astype(F32) * ps + ga_ref[st["rows"], :].astype(F32) * pa
            st["merged"] = merged.astype(BF16)
        return matmul, epilogue

    def stage_out(st):
        def matmul():
            return jnp.dot(st["merged"], wo_ref[...], preferred_element_type=F32)

        def epilogue(r):
            x1 = x_ref[st["rows"], :] + r
            x1_ref[st["rows"], :] = x1
            h2 = x1 * lax.rsqrt(jnp.mean(x1 * x1, axis=-1, keepdims=True) + NORM_EPS) * g2_ref[...]
            words = _pack_rows(h2)
            hlo_ref[st["rows"], :] = words[:, :PACK_HALF]
            hhi_ref[st["rows"], :] = words[:, PACK_HALF:]
            st["h_hi"] = h2.astype(BF16)
            st["h_lo"] = (h2 - st["h_hi"].astype(F32)).astype(BF16)
        return matmul, epilogue

    def stage_router(st):
        def matmul():
            return (lax.dot_general(wrh_ref[...], st["h_hi"], contract_last, preferred_element_type=F32)
                    + lax.dot_general(wrh_ref[...], st["h_lo"], contract_last, preferred_element_type=F32)
                    + lax.dot_general(wrl_ref[...], st["h_hi"], contract_last, preferred_element_type=F32))

        def epilogue(r):
            work = r + br_ref[...]
            onehots, vals, ids = [], [], []
            for _ in range(TOP_K):
                m = jnp.max(work, axis=0, keepdims=True)
                idx = jnp.min(jnp.where(work == m, expert_f, float(N_EXPERTS)), axis=0, keepdims=True)
                oh = expert_f == idx
                onehots.append(oh)
                vals.append(m)
                ids.append(idx.astype(jnp.int32))
                work = jnp.where(oh, -jnp.inf, work)
            exps = [jnp.exp(v - vals[0]) for v in vals]
            den = exps[0] + exps[1] + exps[2] + exps[3]
            st.update(onehots=onehots, ids=ids, gates=[e / den for e in exps],
                      multi=(onehots[0] | onehots[1] | onehots[2] | onehots[3]).astype(F32))
        return matmul, epilogue

    _run_lookahead([stage(st) for stage in (stage_proj, stage_out, stage_router) for st in parts])

    multi = jnp.concatenate([st["multi"] for st in parts], axis=1)
    before = jnp.dot(multi.astype(BF16), tri_ref[...], preferred_element_type=F32) + run_ref[...]
    gate_row = lax.broadcasted_iota(jnp.int32, (LANES, rows_per), 0)
    for st in parts:
        route = jnp.zeros((SUBLANES, rows_per), jnp.int32)
        gates = jnp.zeros((LANES, rows_per), F32)
        for kk in range(TOP_K):
            rank = jnp.sum(jnp.where(st["onehots"][kk], before[:, st["rows"]], 0.0), axis=0, keepdims=True)
            route = jnp.where(slot_row == kk, st["ids"][kk] * region + rank.astype(jnp.int32), route)
            gates = jnp.where(gate_row == kk, st["gates"][kk], gates)
        route_ref[:, st["rows"]] = route
        gate_ref[st["rows"], :] = gates.T
    run = run_ref[...] + jnp.sum(multi, axis=1, keepdims=True)
    run_ref[...] = run
    cnt_ref[...] = jnp.broadcast_to(run, cnt_ref.shape).astype(jnp.int32)


def _merge_route(so, ao, gs, ga, x2d, wps, wpa, wo, gain2, w_router, b_router, region):
    n_tok = x2d.shape[0]
    tm = min(MERGE_TILE, n_tok)
    wr = w_router.astype(F32).T
    wr_hi = wr.astype(BF16)
    wr_lo = (wr - wr_hi.astype(F32)).astype(BF16)
    br = b_router.astype(F32)[:, None]
    tri = (jnp.arange(tm)[:, None] < jnp.arange(tm)[None, :]).astype(BF16)
    row = lambda w: pl.BlockSpec((tm, w), lambda i: (i, 0))
    slots = pl.BlockSpec((SUBLANES, tm), lambda i: (0, i))
    out_shapes = [jax.ShapeDtypeStruct((n_tok, D_MODEL), F32),
                  jax.ShapeDtypeStruct((n_tok, PACK_HALF), jnp.uint32),
                  jax.ShapeDtypeStruct((n_tok, PACK_HALF), jnp.uint32),
                  jax.ShapeDtypeStruct((SUBLANES, n_tok), jnp.int32),
                  jax.ShapeDtypeStruct((n_tok, LANES), F32),
                  jax.ShapeDtypeStruct((N_EXPERTS, LANES), jnp.int32)]
    return pl.pallas_call(
        functools.partial(_merge_kernel, region=region),
        grid=(n_tok // tm,),
        in_specs=[row(SSM_WIDTH), row(D_MODEL), row(D_MODEL), row(D_MODEL), row(D_MODEL),
                  _full(wps.shape), _full(wpa.shape), _full(wo.shape), _full((1, D_MODEL)),
                  _full(wr_hi.shape), _full(wr_lo.shape), _full((N_EXPERTS, 1)), _full((tm, tm))],
        out_specs=[row(D_MODEL), row(PACK_HALF), row(PACK_HALF), slots, row(LANES), _full((N_EXPERTS, LANES))],
        out_shape=out_shapes,
        scratch_shapes=[pltpu.VMEM((N_EXPERTS, 1), F32)],
        compiler_params=_cparams(("arbitrary",)),
        name="merge_route",
    )(so, ao, gs, ga, x2d, wps, wpa, wo, gain2, wr_hi, wr_lo, br, tri)


def _sc_scatter_rows(rows_lo, rows_hi, dest, n_slots):
    n_tok, width = rows_lo.shape
    mesh = plsc.VectorSubcoreMesh(core_axis_name="core", subcore_axis_name="subcore")
    out_type = jax.ShapeDtypeStruct((n_slots, width), rows_lo.dtype)

    @pl.kernel(out_type=(out_type, out_type), mesh=mesh, scratch_types=[])
    def scatter(lo_hbm, hi_hbm, dest_hbm, out_lo_hbm, out_hi_hbm):
        for rows_hbm, out_hbm in ((lo_hbm, out_lo_hbm), (hi_hbm, out_hi_hbm)):
            def body(rows_vmem, dest_vmem, out_hbm=out_hbm):
                for k in range(TOP_K):
                    pltpu.sync_copy(rows_vmem, out_hbm.at[dest_vmem.at[k]])

            pltpu.emit_pipeline(
                body,
                grid=(n_tok // SC_WINDOW,),
                in_specs=[pl.BlockSpec((SC_WINDOW, width), lambda i: (i, 0)),
                          pl.BlockSpec((TOP_K, SC_WINDOW), lambda i: (0, i))],
                out_specs=[],
                core_axis_name=("core", "subcore"),
                dimension_semantics=(pltpu.PARALLEL,),
            )(rows_hbm, dest_hbm)

    return scatter(rows_lo, rows_hi, dest)


def _sc_gather_rows(table_lo, table_hi, idx):
    n = idx.shape[1]
    width = table_lo.shape[1]
    mesh = plsc.VectorSubcoreMesh(core_axis_name="core", subcore_axis_name="subcore")
    out_type = jax.ShapeDtypeStruct((n, width), table_lo.dtype)

    @pl.kernel(out_type=(out_type, out_type), mesh=mesh, scratch_types=[])
    def gather(lo_hbm, hi_hbm, idx_hbm, out_lo_hbm, out_hi_hbm):
        for table_hbm, out_hbm in ((lo_hbm, out_lo_hbm), (hi_hbm, out_hi_hbm)):
            def body(idx_vmem, out_vmem, table_hbm=table_hbm):
                pltpu.sync_copy(table_hbm.at[idx_vmem.at[0]], out_vmem)

            pltpu.emit_pipeline(
                body,
                grid=(n // SC_WINDOW,),
                in_specs=[pl.BlockSpec((1, SC_WINDOW), lambda i: (0, i))],
                out_specs=[pl.BlockSpec((SC_WINDOW, width), lambda i: (i, 0))],
                core_axis_name=("core", "subcore"),
                dimension_semantics=(pltpu.PARALLEL,),
            )(idx_hbm, out_hbm)

    return gather(table_lo, table_hi, idx)


def _expert_kernel(cnt_ref, xlo_hbm, xhi_hbm, w1_ref, b1_ref, w2_ref, b2_ref, perm_ref, ylo_hbm, yhi_hbm,
                   w1p_ref, w2b_ref, x_ref, act_ref, xin_ref, yout_ref, in_sem, out_sem, *, region):
    e = pl.program_id(0)
    n_blk = (cnt_ref[e] + EXPERT_BLOCK - 1) // EXPERT_BLOCK
    base = e * region
    n_chunks = (2 * D_FF) // MXU_DIM
    x_hbm = (xlo_hbm, xhi_hbm)
    y_hbm = (ylo_hbm, yhi_hbm)

    def rows_of(j):
        return pl.ds(pl.multiple_of(base + j * EXPERT_BLOCK, EXPERT_BLOCK), EXPERT_BLOCK)

    def in_copy(j, slot, half):
        return pltpu.make_async_copy(x_hbm[half].at[rows_of(j), :], xin_ref.at[slot, half], in_sem.at[slot, half])

    def out_copy(j, slot, half):
        return pltpu.make_async_copy(yout_ref.at[slot, half], y_hbm[half].at[rows_of(j), :], out_sem.at[slot, half])

    @pl.when(n_blk > 0)
    def _():
        for half in range(2):
            in_copy(0, 0, half).start(priority=ROW_DMA_PRIORITY)

    perm = perm_ref[...]
    for c in range(n_chunks):
        cols = slice(c * MXU_DIM, (c + 1) * MXU_DIM)
        w1p_ref[c] = jnp.dot(w1_ref[0, :, cols].astype(BF16), perm, preferred_element_type=F32).astype(BF16)
    for c in range(D_MODEL // MXU_DIM):
        w2b_ref[c] = w2_ref[0, :, c * MXU_DIM:(c + 1) * MXU_DIM].astype(BF16)

    def mlp_rows(n_rows, xin, yout):
        words = jnp.concatenate([xin[0], xin[1]], axis=1)
        x_ref[:n_rows, :] = _unpack_rows(words).astype(BF16)

        def up_task(c0):
            chunks = range(c0, c0 + UP_CHUNKS_PER_TASK)

            def matmul():
                return [jnp.dot(x_ref[:n_rows, :], w1p_ref[c], preferred_element_type=F32) for c in chunks]

            def epilogue(results):
                gates, ups = [], []
                for c, r in zip(chunks, results):
                    h = r + b1_ref[0, :, c * MXU_DIM:(c + 1) * MXU_DIM]
                    gates.append(jnp.minimum(h[:, :LANES], SWIGLU_LIMIT))
                    ups.append(jnp.clip(h[:, LANES:], -SWIGLU_LIMIT, SWIGLU_LIMIT))
                gate = jnp.concatenate(gates, axis=1)
                up = jnp.concatenate(ups, axis=1)
                glu = gate * _sigmoid(SWIGLU_ALPHA * gate)
                act_ref[:n_rows, c0 * LANES:(c0 + UP_CHUNKS_PER_TASK) * LANES] = ((up + 1.0) * glu).astype(BF16)
            return matmul, epilogue

        _run_lookahead([up_task(c0) for c0 in range(0, n_chunks, UP_CHUNKS_PER_TASK)])

        def down_task(c):
            cols = slice(c * MXU_DIM, (c + 1) * MXU_DIM)
            half, high = c % 2, c // 2

            def epilogue(r):
                bits = lax.bitcast_convert_type((r + b2_ref[0, :, cols]).astype(BF16).astype(F32), jnp.uint32)
                if high:
                    yout[half] = yout[half] | (bits & jnp.uint32(0xFFFF0000))
                else:
                    yout[half] = bits >> 16
            return (lambda: jnp.dot(act_ref[:n_rows, :], w2b_ref[c], preferred_element_type=F32)), epilogue

        _run_lookahead([down_task(c) for c in range(D_MODEL // MXU_DIM)])

    def block(j, _):
        slot = j % 2
        for half in range(2):
            in_copy(j, slot, half).wait()

        @pl.when(j + 1 < n_blk)
        def _():
            for half in range(2):
                in_copy(j + 1, 1 - slot, half).start(priority=ROW_DMA_PRIORITY)

        @pl.when(j >= 2)
        def _():
            for half in range(2):
                out_copy(j - 2, slot, half).wait()

        mlp_rows(EXPERT_BLOCK, xin_ref.at[slot], yout_ref.at[slot])
        for half in range(2):
            out_copy(j, slot, half).start(priority=ROW_DMA_PRIORITY)
        return 0

    lax.fori_loop(0, n_blk, block, 0)

    for back in (2, 1):
        @pl.when(n_blk >= back)
        def _(back=back):
            j = n_blk - back
            for half in range(2):
                out_copy(j, j % 2, half).wait()


def _gate_up_order():
    j = jnp.arange(MXU_DIM)
    within = jnp.where(j < LANES, 2 * j, 2 * (j - LANES) + 1)
    return within


def _experts(xs_lo, xs_hi, w1, b1, w2, b2, counts, region):
    within = _gate_up_order()
    perm = (jnp.arange(MXU_DIM)[:, None] == within[None, :]).astype(BF16)
    order = (jnp.arange(0, 2 * D_FF, MXU_DIM)[:, None] + within[None, :]).reshape(-1)
    b1p = b1.astype(F32)[:, order][:, None, :]
    b2r = b2.astype(F32)[:, None, :]
    tb = EXPERT_BLOCK
    hbm = pl.BlockSpec(memory_space=pl.ANY)
    wspec = lambda shape: pl.BlockSpec((1,) + shape, lambda e, cnt: (e, 0, 0))
    grid_spec = pltpu.PrefetchScalarGridSpec(
        num_scalar_prefetch=1,
        grid=(N_EXPERTS,),
        in_specs=[hbm, hbm, wspec((D_MODEL, 2 * D_FF)), wspec((1, 2 * D_FF)),
                  wspec((D_FF, D_MODEL)), wspec((1, D_MODEL)),
                  pl.BlockSpec((MXU_DIM, MXU_DIM), lambda e, cnt: (0, 0))],
        out_specs=[hbm, hbm],
        scratch_shapes=[pltpu.VMEM((2 * D_FF // MXU_DIM, D_MODEL, MXU_DIM), BF16),
                        pltpu.VMEM((D_MODEL // MXU_DIM, D_FF, MXU_DIM), BF16),
                        pltpu.VMEM((tb, D_MODEL), BF16), pltpu.VMEM((tb, D_FF), BF16),
                        pltpu.VMEM((2, 2, tb, PACK_HALF), jnp.uint32), pltpu.VMEM((2, 2, tb, PACK_HALF), jnp.uint32),
                        pltpu.SemaphoreType.DMA((2, 2)), pltpu.SemaphoreType.DMA((2, 2))],
    )
    return pl.pallas_call(
        functools.partial(_expert_kernel, region=region),
        grid_spec=grid_spec,
        out_shape=[jax.ShapeDtypeStruct(xs_lo.shape, jnp.uint32)] * 2,
        compiler_params=_cparams(("arbitrary",)),
        name="experts",
    )(counts, xs_lo, xs_hi, w1, b1p, w2, b2r, perm)


def _combine_kernel(x1_ref, gate_ref, *refs):
    lo_refs, hi_refs, o_ref = refs[:TOP_K], refs[TOP_K:2 * TOP_K], refs[2 * TOP_K]
    acc = x1_ref[...]
    gates = gate_ref[...]
    for kk in range(TOP_K):
        words = jnp.concatenate([lo_refs[kk][...], hi_refs[kk][...]], axis=1)
        acc = acc + gates[:, kk:kk + 1] * _unpack_rows(words)
    o_ref[...] = acc


def _combine(x1, gates, yg_lo, yg_hi):
    n_tok = x1.shape[0]
    tm = min(ROW_TILE, n_tok)
    nblk = n_tok // tm
    row = lambda w: pl.BlockSpec((tm, w), lambda i: (i, 0))
    plane = lambda kk: pl.BlockSpec((tm, PACK_HALF), lambda i, kk=kk: (kk * nblk + i, 0))
    planes = [plane(kk) for kk in range(TOP_K)]
    return pl.pallas_call(
        _combine_kernel,
        grid=(nblk,),
        in_specs=[row(D_MODEL), row(LANES)] + planes + planes,
        out_specs=row(D_MODEL),
        out_shape=jax.ShapeDtypeStruct((n_tok, D_MODEL), F32),
        compiler_params=_cparams(("parallel",)),
        name="combine",
    )(x1, gates, *([yg_lo] * TOP_K), *([yg_hi] * TOP_K))


def kernel(x, norm1_gain, w_in, lambda_re, lambda_im, log_dt, ssm_b_re, ssm_b_im, ssm_c_re, ssm_c_im, ssm_d, w_glu, b_glu, q_norm_gain, k_norm_gain, lambda_q1, lambda_k1, lambda_q2, lambda_k2, subln_gain, w_proj_ssm, w_proj_attn, w_out, norm2_gain, w_router, b_router, w_exp1, b_exp1, w_exp2, b_exp2):
    bsz, seq, d = x.shape
    n_tok = bsz * seq
    depth = norm1_gain.shape[0]
    row1 = lambda a: a.astype(F32).reshape(1, -1)
    for l in range(depth):
        lambda_init = 0.8 - 0.6 * math.exp(-0.3 * l)
        x2d = x.reshape(n_tok, d)

        u, q, k, vt, gs, ga = _in_proj(x2d, row1(norm1_gain[l]), w_in[l], q_norm_gain[l], k_norm_gain[l])

        bblk, cblk, a_tile = _ssm_params(lambda_re[l], lambda_im[l], log_dt[l], ssm_b_re[l], ssm_b_im[l],
                                         ssm_c_re[l], ssm_c_im[l])
        so = _ssm(u.reshape(bsz, seq, SSM_WIDTH), bblk, cblk, a_tile, row1(ssm_d[l]),
                  w_glu[l].astype(BF16), row1(b_glu[l])).reshape(n_tok, SSM_WIDTH)

        lam = (jnp.exp(jnp.sum(lambda_q1[l].astype(F32) * lambda_k1[l].astype(F32)))
               - jnp.exp(jnp.sum(lambda_q2[l].astype(F32) * lambda_k2[l].astype(F32)))
               + lambda_init).reshape(1)
        score_bound = (NORM_BOUND_SLACK * HEAD_DIM * Q_SCALE * jnp.max(jnp.abs(q_norm_gain[l].astype(F32)))
                       * jnp.max(jnp.abs(k_norm_gain[l].astype(F32))))
        ao = _attention(q, k, vt, lam, subln_gain[l], score_bound, lambda_init, bsz, seq)

        region = -(-n_tok // EXPERT_BLOCK) * EXPERT_BLOCK
        x1, h_lo, h_hi, route, gates, counts = _merge_route(
            so, ao, gs, ga, x2d, w_proj_ssm[l].astype(BF16), w_proj_attn[l].astype(BF16),
            w_out[l].astype(BF16), row1(norm2_gain[l]), w_router[l], b_router[l], region)

        dest = route[:TOP_K]
        n_slots = N_EXPERTS * region
        xs_lo, xs_hi = _sc_scatter_rows(h_lo, h_hi, dest, n_slots)

        ys_lo, ys_hi = _experts(xs_lo, xs_hi, w_exp1[l], b_exp1[l], w_exp2[l], b_exp2[l], counts[:, 0], region)

        yg_lo, yg_hi = _sc_gather_rows(ys_lo, ys_hi, dest.reshape(1, TOP_K * n_tok))
        x = _combine(x1, gates, yg_lo, yg_hi).reshape(bsz, seq, d)
    return x
```

```python
import functools
import math

import jax
import jax.numpy as jnp
from jax import lax
from jax.experimental import pallas as pl
from jax.experimental.pallas import tpu as pltpu
from jax.experimental.pallas import tpu_sc as plsc

F32 = jnp.float32
BF16 = jnp.bfloat16

D_MODEL = 1024
NORM_EPS = 1e-5
SSM_WIDTH = 512
SSM_GROUP = 16
SSM_GROUPS = 32
SSM_STATE = 64
N_STATE = SSM_GROUPS * SSM_STATE
HEADS = 8
HEAD_DIM = 64
N_EXPERTS = 32
TOP_K = 4
D_FF = 1024
SWIGLU_ALPHA = 1.702
SWIGLU_LIMIT = 7.0

LANES = 128
SUBLANES = 8
MXU_DIM = 256
VMEM_LIMIT = 56 * 1024 * 1024

ROW_TILE = 512
MERGE_TILE = 1024
SSM_CHUNK = 128
SCAN_LANES = 1024
ATTN_BLOCK = 256
ATTN_HEADS_PER_STEP = 8
ROWSUM_ROWS = 16
EXPERT_BLOCK = 512
UP_CHUNKS_PER_TASK = 2
ROW_DMA_PRIORITY = 1
SC_WINDOW = 128
PACK_W = D_MODEL // 2
PACK_HALF = PACK_W // 2

WEIGHT_PIECES = 8
PIECES_PER_BLOCK = 2

_NEG = -1e30
Q_SCALE = math.log2(math.e) / math.sqrt(HEAD_DIM)
ATTN_SAFE_EXPONENT = 40.0
NORM_BOUND_SLACK = 1.05


def _cparams(sem):
    return pltpu.CompilerParams(dimension_semantics=sem, vmem_limit_bytes=VMEM_LIMIT)


def _full(shape):
    nd = len(shape)
    return pl.BlockSpec(shape, lambda *_: (0,) * nd)


def _sigmoid(x):
    return 0.5 * jnp.tanh(0.5 * x) + 0.5


def _run_lookahead(tasks):
    pending = tasks[0][0]()
    for i, (_, epilogue) in enumerate(tasks):
        result = pending
        if i + 1 < len(tasks):
            pending = tasks[i + 1][0]()
        epilogue(result)


_IN_CHUNKS = {"u": (0, SSM_WIDTH // MXU_DIM)}
for _name in ("q", "k", "v", "gs", "ga"):
    _start = max(first + count for first, count in _IN_CHUNKS.values())
    _IN_CHUNKS[_name] = (_start, D_MODEL // MXU_DIM)
N_IN_CHUNKS = max(first + count for first, count in _IN_CHUNKS.values())


def _prep_w_in_kernel(w_ref, wc_ref, wvt_ref):
    c = pl.program_id(0)
    w = w_ref[...]
    wc_ref[0] = w.astype(BF16)
    v_first, v_count = _IN_CHUNKS["v"]

    @pl.when((c >= v_first) & (c < v_first + v_count))
    def _():
        wvt_ref[0] = w.T.astype(BF16)


def _prep_w_in(w_in):
    v_first, v_count = _IN_CHUNKS["v"]
    return pl.pallas_call(
        _prep_w_in_kernel,
        grid=(N_IN_CHUNKS,),
        in_specs=[pl.BlockSpec((D_MODEL, MXU_DIM), lambda c: (0, c))],
        out_specs=[pl.BlockSpec((1, D_MODEL, MXU_DIM), lambda c: (c, 0, 0)),
                   pl.BlockSpec((1, MXU_DIM, D_MODEL), lambda c: (jnp.clip(c - v_first, 0, v_count - 1), 0, 0))],
        out_shape=[jax.ShapeDtypeStruct((N_IN_CHUNKS, D_MODEL, MXU_DIM), BF16),
                   jax.ShapeDtypeStruct((v_count, MXU_DIM, D_MODEL), BF16)],
        compiler_params=_cparams(("arbitrary",)),
        name="prep_w_in",
    )(w_in)


def _inproj_kernel(x_ref, g1_ref, w_ref, wvt_ref, qg_ref, kg_ref, seg_ref,
                   u_ref, q_ref, k_ref, vt_ref, gs_ref, ga_ref):
    x = x_ref[...]
    ms = jnp.mean(x * x, axis=-1, keepdims=True)
    h = (x * lax.rsqrt(ms + NORM_EPS) * g1_ref[...]).astype(BF16)

    def proj(name, c):
        chunk = _IN_CHUNKS[name][0] + c
        return lambda: jnp.dot(h, w_ref[chunk], preferred_element_type=F32)

    seg = seg_ref[...]
    tasks = []

    def plain_task(name, out_ref, c, fn):
        cols = slice(c * MXU_DIM, (c + 1) * MXU_DIM)

        def epilogue(r):
            out_ref[:, cols] = fn(r)
        return proj(name, c), epilogue

    tasks += [plain_task("u", u_ref, c, lambda r: r) for c in range(_IN_CHUNKS["u"][1])]

    def head_norm_tasks(name, gain_ref, out_ref, scale, c):
        cols = slice(c * MXU_DIM, (c + 1) * MXU_DIM)
        kept = {}

        def after_proj(y):
            kept["y"] = y
            kept["sq"] = (y * y).astype(BF16)

        def after_sum(ss):
            yn = kept["y"] * lax.rsqrt(ss * (1.0 / HEAD_DIM) + NORM_EPS) * gain_ref[:, cols]
            out_ref[:, cols] = (yn * scale).astype(BF16)

        return ((proj(name, c), after_proj),
                (lambda: jnp.dot(kept["sq"], seg, preferred_element_type=F32), after_sum))

    pairs = [head_norm_tasks(name, gain_ref, out_ref, scale, c)
             for name, gain_ref, out_ref, scale in (("q", qg_ref, q_ref, Q_SCALE), ("k", kg_ref, k_ref, 1.0))
             for c in range(_IN_CHUNKS[name][1])]
    tasks.append(pairs[0][0])
    for prev, cur in zip(pairs, pairs[1:]):
        tasks += [cur[0], prev[1]]
    tasks.append(pairs[-1][1])

    to_gate = lambda r: _sigmoid(r).astype(BF16)
    for c in range(_IN_CHUNKS["gs"][1]):
        tasks.append(plain_task("gs", gs_ref, c, to_gate))
        tasks.append(plain_task("ga", ga_ref, c, to_gate))

    def vt_task(c):
        def matmul():
            return lax.dot_general(wvt_ref[c], h, (((1,), (1,)), ((), ())), preferred_element_type=F32)

        def epilogue(r):
            vt_ref[c * MXU_DIM:(c + 1) * MXU_DIM, :] = r.astype(BF16)
        return matmul, epilogue

    tasks += [vt_task(c) for c in range(_IN_CHUNKS["v"][1])]
    _run_lookahead(tasks)


def _in_proj(x2d, gain1, w_in, q_gain, k_gain):
    n_tok = x2d.shape[0]
    tm = min(ROW_TILE, n_tok)
    w_chunks, w_vt = _prep_w_in(w_in)
    seg = (jnp.arange(MXU_DIM)[:, None] // HEAD_DIM == jnp.arange(MXU_DIM)[None, :] // HEAD_DIM).astype(BF16)
    reps = D_MODEL // HEAD_DIM
    qg = jnp.tile(q_gain.astype(F32), reps)[None, :]
    kg = jnp.tile(k_gain.astype(F32), reps)[None, :]
    row = lambda w: pl.BlockSpec((tm, w), lambda i: (i, 0))
    tok = jax.ShapeDtypeStruct((n_tok, D_MODEL), BF16)
    out_shapes = [jax.ShapeDtypeStruct((n_tok, SSM_WIDTH), F32), tok, tok,
                  jax.ShapeDtypeStruct((D_MODEL, n_tok), BF16), tok, tok]
    vt_spec = pl.BlockSpec((D_MODEL, tm), lambda i: (0, i))
    return pl.pallas_call(
        _inproj_kernel,
        grid=(n_tok // tm,),
        in_specs=[row(D_MODEL), _full((1, D_MODEL)), _full(w_chunks.shape), _full(w_vt.shape),
                  _full((1, D_MODEL)), _full((1, D_MODEL)), _full((MXU_DIM, MXU_DIM))],
        out_specs=[row(SSM_WIDTH), row(D_MODEL), row(D_MODEL), vt_spec, row(D_MODEL), row(D_MODEL)],
        out_shape=out_shapes,
        compiler_params=_cparams(("parallel",)),
        name="in_proj",
    )(x2d, gain1, w_chunks, w_vt, qg, kg, seg)


def _ssm_kernel(u_ref, bblk_ref, a_ref, cblk_ref, d_ref, wglu_ref, bglu_ref, o_ref, bu_ref, st_ref, tb_ref):
    n_batch, chunk = u_ref.shape[0], u_ref.shape[1]

    @pl.when(pl.program_id(0) == 0)
    def _():
        st_ref[...] = jnp.zeros_like(st_ref)

    n_planes = SSM_WIDTH // LANES
    for b in range(n_batch):
        for j in range(n_planes):
            tb_ref[j, pl.ds(b, chunk, stride=n_batch), :] = u_ref[b, :, j * LANES:(j + 1) * LANES]
    u = jnp.concatenate([tb_ref[j] for j in range(n_planes)], axis=1)
    u_bf = u.astype(BF16)
    tiles_per_part = N_STATE // MXU_DIM
    ch_per_tile = SSM_WIDTH // tiles_per_part
    n_groups = N_STATE // SCAN_LANES
    tiles_per_group = SCAN_LANES // MXU_DIM
    assert SCAN_LANES * SSM_WIDTH == N_STATE * MXU_DIM

    def input_tiles(g):
        for part in range(2):
            for t in range(tiles_per_group):
                tile = g * tiles_per_group + t
                ch0 = (tile * ch_per_tile) // LANES * LANES
                lanes = slice(part * N_STATE + tile * MXU_DIM, part * N_STATE + (tile + 1) * MXU_DIM)
                bu_ref[:, lanes] = jnp.dot(u_bf[:, ch0:ch0 + LANES], bblk_ref[ch0:ch0 + LANES, lanes],
                                           preferred_element_type=F32)

    def scan(g):
        re = slice(g * SCAN_LANES, (g + 1) * SCAN_LANES)
        im = slice(N_STATE + g * SCAN_LANES, N_STATE + (g + 1) * SCAN_LANES)
        ar, ai = a_ref[:, re], a_ref[:, im]
        xr, xi = st_ref[:, re], st_ref[:, im]
        for t in range(chunk):
            rows = slice(t * SUBLANES, (t + 1) * SUBLANES)
            xr, xi = ar * xr - ai * xi + bu_ref[rows, re], ar * xi + ai * xr + bu_ref[rows, im]
            bu_ref[rows, re] = xr
            bu_ref[rows, im] = xi
        st_ref[:, re] = xr
        st_ref[:, im] = xi

    def output_tile(g):
        cols = slice(g * MXU_DIM, (g + 1) * MXU_DIM)
        acc = None
        for part in range(2):
            lanes = slice(part * N_STATE + g * SCAN_LANES, part * N_STATE + (g + 1) * SCAN_LANES)
            term = jnp.dot(bu_ref[:, lanes].astype(BF16), cblk_ref[lanes, cols], preferred_element_type=F32)
            acc = term if acc is None else acc + term
        return acc

    input_tiles(0)
    ys = []
    for g in range(n_groups):
        if g + 1 < n_groups:
            input_tiles(g + 1)
        scan(g)
        ys.append(output_tile(g))
    y = jnp.concatenate(ys, axis=1) + d_ref[...] * u
    z = jax.nn.gelu(y)
    gate = _sigmoid(jnp.dot(z.astype(BF16), wglu_ref[...], preferred_element_type=F32) + bglu_ref[...])
    out = z * gate
    for j in range(n_planes):
        tb_ref[j] = out[:, j * LANES:(j + 1) * LANES]
    for b in range(n_batch):
        for j in range(n_planes):
            o_ref[b, :, j * LANES:(j + 1) * LANES] = tb_ref[j, pl.ds(b, chunk, stride=n_batch), :].astype(BF16)


def _ssm_params(lambda_re, lambda_im, log_dt, b_re, b_im, c_re, c_im):
    dt = jnp.exp(log_dt.astype(F32))[:, None]
    lr = jnp.minimum(lambda_re.astype(F32), -1e-4)
    li = lambda_im.astype(F32)
    mag = jnp.exp(lr * dt)
    abar_re = mag * jnp.cos(li * dt)
    abar_im = mag * jnp.sin(li * dt)
    den = lr * lr + li * li
    nr = abar_re - 1.0
    coef_re = (nr * lr + abar_im * li) / den
    coef_im = (abar_im * lr - nr * li) / den
    br = b_re.astype(F32)
    bi = b_im.astype(F32)
    bbar_re = coef_re[..., None] * br - coef_im[..., None] * bi
    bbar_im = coef_re[..., None] * bi + coef_im[..., None] * br
    ch_group = jnp.arange(SSM_WIDTH) // SSM_GROUP
    lane_group = jnp.arange(N_STATE) // SSM_STATE

    def expand_b(b):
        rows = b.transpose(0, 2, 1).reshape(SSM_WIDTH, SSM_STATE)
        return jnp.where(ch_group[:, None] == lane_group[None, :], jnp.tile(rows, (1, SSM_GROUPS)), 0.0)

    def expand_c(c):
        rows = c.transpose(0, 2, 1).reshape(N_STATE, SSM_GROUP)
        return jnp.where(lane_group[:, None] == ch_group[None, :], jnp.tile(rows, (1, SSM_GROUPS)), 0.0)

    bblk = jnp.concatenate([expand_b(bbar_re), expand_b(bbar_im)], axis=1).astype(BF16)
    cblk = jnp.concatenate([expand_c(c_re.astype(F32)), -expand_c(c_im.astype(F32))], axis=0).astype(BF16)
    a_row = jnp.concatenate([abar_re.reshape(-1), abar_im.reshape(-1)])[None, :]
    return bblk, cblk, jnp.broadcast_to(a_row, (SUBLANES, 2 * N_STATE))


def _ssm(u, bblk, cblk, a_tile, d_skip, w_glu_bf, b_glu):
    bsz, seq, _ = u.shape
    assert bsz == SUBLANES
    chunk = min(SSM_CHUNK, seq)
    rows = chunk * SUBLANES
    tok_spec = pl.BlockSpec((bsz, chunk, SSM_WIDTH), lambda c: (0, c, 0))
    return pl.pallas_call(
        _ssm_kernel,
        grid=(seq // chunk,),
        in_specs=[tok_spec, _full(bblk.shape), _full(a_tile.shape), _full(cblk.shape),
                  _full((1, SSM_WIDTH)), _full(w_glu_bf.shape), _full((1, SSM_WIDTH))],
        out_specs=tok_spec,
        out_shape=jax.ShapeDtypeStruct(u.shape, BF16),
        scratch_shapes=[pltpu.VMEM((rows, 2 * N_STATE), F32), pltpu.VMEM((SUBLANES, 2 * N_STATE), F32),
                        pltpu.VMEM((SSM_WIDTH // LANES, rows, LANES), F32)],
        compiler_params=_cparams(("arbitrary",)),
        name="ssm",
    )(u, bblk, a_tile, cblk, d_skip, w_glu_bf, b_glu)


def _attn_kernel(lam_ref, sg_ref, q_ref, k_ref, vt_ref, o_ref, acc_ref, vta_ref, *, out_scale, blk, nh, bounded):
    seq = q_ref.shape[1]
    hw = 2 * HEAD_DIM
    lane = lax.broadcasted_iota(jnp.int32, (blk, hw), 1)
    key_i = lax.broadcasted_iota(jnp.int32, (blk, blk), 0)
    qry_i = lax.broadcasted_iota(jnp.int32, (blk, blk), 1)
    keep = key_i <= qry_i
    contract_last = (((1,), (1,)), ((), ()))
    lam = lam_ref[0]
    n_chain = 2 * nh

    for hh in range(nh):
        vta_ref[hh, :hw, :] = vt_ref[hh * hw:(hh + 1) * hw, :]
        vta_ref[hh, hw:, :] = jnp.ones((ROWSUM_ROWS, seq), BF16)

    def q_block(qi, _):
        qrows = pl.ds(pl.multiple_of(qi * blk, blk), blk)
        qs = []
        for hh in range(nh):
            q = q_ref[0, qrows, hh * hw:(hh + 1) * hw]
            zero = jnp.zeros_like(q)
            qs += [jnp.where(lane < HEAD_DIM, q, zero), jnp.where(lane >= HEAD_DIM, q, zero)]
        acc_ref[...] = jnp.zeros_like(acc_ref)

        def kv_block(kb, carry, masked, n_key_blocks=1):
            krows = pl.ds(pl.multiple_of(kb * blk, blk), n_key_blocks * blk)
            scores = [lax.dot_general(k_ref[0, krows, (c // 2) * hw:(c // 2 + 1) * hw], qs[c], contract_last,
                                      preferred_element_type=F32) for c in range(n_chain)]
            out = []
            for c in range(n_chain):
                s = jnp.where(keep, scores[c], _NEG) if masked else scores[c]
                vt = vta_ref[c // 2, :, krows]
                if bounded:
                    acc_ref[c] += jnp.dot(vt, jnp.exp2(s).astype(BF16), preferred_element_type=F32)
                    continue
                m = carry[c]
                m_new = jnp.maximum(m, jnp.max(s, axis=0, keepdims=True))
                p = jnp.exp2(s - m_new).astype(BF16)
                alpha = jnp.exp2(m - m_new)
                out.append(m_new)
                acc_ref[c] = alpha * acc_ref[c] + jnp.dot(vt, p, preferred_element_type=F32)
            return tuple(out)

        if bounded:
            lax.fori_loop(0, qi // 2, lambda pair, c: kv_block(2 * pair, c, False, 2), ())

            @pl.when(qi % 2 == 1)
            def _():
                kv_block(qi - 1, (), False)
            carry = ()
        else:
            carry = lax.fori_loop(0, qi, lambda kb, c: kv_block(kb, c, False),
                                  (jnp.full((1, blk), _NEG, F32),) * n_chain)
        kv_block(qi, carry, True)
        for hh in range(nh):
            a1, a2 = acc_ref[2 * hh], acc_ref[2 * hh + 1]
            l1, l2 = a1[hw:hw + 1, :], a2[hw:hw + 1, :]
            ot = a1[:hw, :] * (1.0 / l1) - a2[:hw, :] * (lam / l2)
            ot = ot * lax.rsqrt(jnp.mean(ot * ot, axis=0, keepdims=True) + NORM_EPS)
            o_ref[0, qrows, hh * hw:(hh + 1) * hw] = (ot.T * sg_ref[...] * out_scale).astype(BF16)
        return 0

    lax.fori_loop(0, seq // blk, q_block, 0)


def _attention(q, k, vt, lam, subln_gain, score_bound, lambda_init, bsz, seq):
    blk = min(ATTN_BLOCK, seq)
    nh = ATTN_HEADS_PER_STEP
    hw = 2 * HEAD_DIM
    q3, k3 = (a.reshape(bsz, seq, D_MODEL) for a in (q, k))
    tok_spec = pl.BlockSpec((1, seq, nh * hw), lambda b, h: (b, 0, h))

    def call(bounded):
        return pl.pallas_call(
            functools.partial(_attn_kernel, out_scale=1.0 - lambda_init, blk=blk, nh=nh, bounded=bounded),
            grid=(bsz, HEADS // nh),
            in_specs=[pl.BlockSpec(memory_space=pltpu.SMEM), _full((1, hw)), tok_spec, tok_spec,
                      pl.BlockSpec((nh * hw, seq), lambda b, h: (h, b))],
            out_specs=tok_spec,
            out_shape=jax.ShapeDtypeStruct((bsz, seq, D_MODEL), BF16),
            scratch_shapes=[pltpu.VMEM((2 * nh, hw + ROWSUM_ROWS, blk), F32),
                            pltpu.VMEM((nh, hw + ROWSUM_ROWS, seq), BF16)],
            compiler_params=_cparams(("parallel", "parallel")),
            name="diff_attn_bounded" if bounded else "diff_attn",
        )

    operands = (lam, subln_gain.astype(F32)[None, :], q3, k3, vt)
    out = lax.cond(score_bound <= ATTN_SAFE_EXPONENT, lambda ops: call(True)(*ops), lambda ops: call(False)(*ops),
                   operands)
    return out.reshape(bsz * seq, D_MODEL)


def _pack_rows(y):
    bits = lax.bitcast_convert_type(y.astype(BF16).astype(F32), jnp.uint32)
    return (bits[:, :PACK_W] >> 16) | (bits[:, PACK_W:] & jnp.uint32(0xFFFF0000))


def _unpack_rows(w):
    lo = lax.bitcast_convert_type(w << 16, F32)
    hi = lax.bitcast_convert_type(w & jnp.uint32(0xFFFF0000), F32)
    return jnp.concatenate([lo, hi], axis=1)


def _merge_kernel(so_ref, ao_ref, gs_ref, ga_ref, x_ref, wps_ref, wpa_ref, wo_ref, g2_ref,
                  wrh_ref, wrl_ref, br_ref, tri_ref,
                  x1_ref, hlo_ref, hhi_ref, route_ref, gate_ref, cnt_ref, run_ref, *, region):
    @pl.when(pl.program_id(0) == 0)
    def _():
        run_ref[...] = jnp.zeros_like(run_ref)

    tm = x_ref.shape[0]
    n_part = 2 if tm % (2 * MXU_DIM) == 0 else 1
    rows_per = tm // n_part
    expert_f = lax.broadcasted_iota(jnp.int32, (N_EXPERTS, rows_per), 0).astype(F32)
    slot_row = lax.broadcasted_iota(jnp.int32, (SUBLANES, rows_per), 0)
    contract_last = (((1,), (1,)), ((), ()))
    parts = [dict(rows=slice(p * rows_per, (p + 1) * rows_per)) for p in range(n_part)]

    def stage_proj(st):
        def matmul():
            return (jnp.dot(so_ref[st["rows"], :], wps_ref[...], preferred_element_type=F32),
                    jnp.dot(ao_ref[st["rows"], :], wpa_ref[...], preferred_element_type=F32))

        def epilogue(r):
            ps, pa = r
            merged = gs_ref[st["rows"], :].astype(F32) * ps + ga_ref[st["rows"], :].astype(F32) * pa
            st["merged"] = merged.astype(BF16)
        return matmul, epilogue

    def stage_out(st):
        def matmul():
            return jnp.dot(st["merged"], wo_ref[...], preferred_element_type=F32)

        def epilogue(r):
            x1 = x_ref[st["rows"], :] + r
            x1_ref[st["rows"], :] = x1
            h2 = x1 * lax.rsqrt(jnp.mean(x1 * x1, axis=-1, keepdims=True) + NORM_EPS) * g2_ref[...]
            words = _pack_rows(h2)
            hlo_ref[st["rows"], :] = words[:, :PACK_HALF]
            hhi_ref[st["rows"], :] = words[:, PACK_HALF:]
            st["h_hi"] = h2.astype(BF16)
            st["h_lo"] = (h2 - st["h_hi"].astype(F32)).astype(BF16)
        return matmul, epilogue

    def stage_router(st):
        def matmul():
            return (lax.dot_general(wrh_ref[...], st["h_hi"], contract_last, preferred_element_type=F32)
                    + lax.dot_general(wrh_ref[...], st["h_lo"], contract_last, preferred_element_type=F32)
                    + lax.dot_general(wrl_ref[...], st["h_hi"], contract_last, preferred_element_type=F32))

        def epilogue(r):
            work = r + br_ref[...]
            onehots, vals, ids = [], [], []
            for _ in range(TOP_K):
                m = jnp.max(work, axis=0, keepdims=True)
                idx = jnp.min(jnp.where(work == m, expert_f, float(N_EXPERTS)), axis=0, keepdims=True)
                oh = expert_f == idx
                onehots.append(oh)
                vals.append(m)
                ids.append(idx.astype(jnp.int32))
                work = jnp.where(oh, -jnp.inf, work)
            exps = [jnp.exp(v - vals[0]) for v in vals]
            den = exps[0] + exps[1] + exps[2] + exps[3]
            st.update(onehots=onehots, ids=ids, gates=[e / den for e in exps],
                      multi=(onehots[0] | onehots[1] | onehots[2] | onehots[3]).astype(F32))
        return matmul, epilogue

    _run_lookahead([stage(st) for stage in (stage_proj, stage_out, stage_router) for st in parts])

    multi = jnp.concatenate([st["multi"] for st in parts], axis=1)
    before = jnp.dot(multi.astype(BF16), tri_ref[...], preferred_element_type=F32) + run_ref[...]
    gate_row = lax.broadcasted_iota(jnp.int32, (LANES, rows_per), 0)
    for st in parts:
        route = jnp.zeros((SUBLANES, rows_per), jnp.int32)
        gates = jnp.zeros((LANES, rows_per), F32)
        for kk in range(TOP_K):
            rank = jnp.sum(jnp.where(st["onehots"][kk], before[:, st["rows"]], 0.0), axis=0, keepdims=True)
            route = jnp.where(slot_row == kk, st["ids"][kk] * region + rank.astype(jnp.int32), route)
            gates = jnp.where(gate_row == kk, st["gates"][kk], gates)
        route_ref[:, st["rows"]] = route
        gate_ref[st["rows"], :] = gates.T
    run = run_ref[...] + jnp.sum(multi, axis=1, keepdims=True)
    run_ref[...] = run
    cnt_ref[...] = jnp.broadcast_to(run, cnt_ref.shape).astype(jnp.int32)


def _merge_route(so, ao, gs, ga, x2d, wps, wpa, wo, gain2, w_router, b_router, region):
    n_tok = x2d.shape[0]
    tm = min(MERGE_TILE, n_tok)
    wr = w_router.astype(F32).T
    wr_hi = wr.astype(BF16)
    wr_lo = (wr - wr_hi.astype(F32)).astype(BF16)
    br = b_router.astype(F32)[:, None]
    tri = (jnp.arange(tm)[:, None] < jnp.arange(tm)[None, :]).astype(BF16)
    row = lambda w: pl.BlockSpec((tm, w), lambda i: (i, 0))
    slots = pl.BlockSpec((SUBLANES, tm), lambda i: (0, i))
    out_shapes = [jax.ShapeDtypeStruct((n_tok, D_MODEL), F32),
                  jax.ShapeDtypeStruct((n_tok, PACK_HALF), jnp.uint32),
                  jax.ShapeDtypeStruct((n_tok, PACK_HALF), jnp.uint32),
                  jax.ShapeDtypeStruct((SUBLANES, n_tok), jnp.int32),
                  jax.ShapeDtypeStruct((n_tok, LANES), F32),
                  jax.ShapeDtypeStruct((N_EXPERTS, LANES), jnp.int32)]
    return pl.pallas_call(
        functools.partial(_merge_kernel, region=region),
        grid=(n_tok // tm,),
        in_specs=[row(SSM_WIDTH), row(D_MODEL), row(D_MODEL), row(D_MODEL), row(D_MODEL),
                  _full(wps.shape), _full(wpa.shape), _full(wo.shape), _full((1, D_MODEL)),
                  _full(wr_hi.shape), _full(wr_lo.shape), _full((N_EXPERTS, 1)), _full((tm, tm))],
        out_specs=[row(D_MODEL), row(PACK_HALF), row(PACK_HALF), slots, row(LANES), _full((N_EXPERTS, LANES))],
        out_shape=out_shapes,
        scratch_shapes=[pltpu.VMEM((N_EXPERTS, 1), F32)],
        compiler_params=_cparams(("arbitrary",)),
        name="merge_route",
    )(so, ao, gs, ga, x2d, wps, wpa, wo, gain2, wr_hi, wr_lo, br, tri)


def _sc_scatter_rows(rows_lo, rows_hi, dest, n_slots):
    n_tok, width = rows_lo.shape
    mesh = plsc.VectorSubcoreMesh(core_axis_name="core", subcore_axis_name="subcore")
    out_type = jax.ShapeDtypeStruct((n_slots, width), rows_lo.dtype)

    @pl.kernel(out_type=(out_type, out_type), mesh=mesh, scratch_types=[])
    def scatter(lo_hbm, hi_hbm, dest_hbm, out_lo_hbm, out_hi_hbm):
        for rows_hbm, out_hbm in ((lo_hbm, out_lo_hbm), (hi_hbm, out_hi_hbm)):
            def body(rows_vmem, dest_vmem, out_hbm=out_hbm):
                for k in range(TOP_K):
                    pltpu.sync_copy(rows_vmem, out_hbm.at[dest_vmem.at[k]])

            pltpu.emit_pipeline(
                body,
                grid=(n_tok // SC_WINDOW,),
                in_specs=[pl.BlockSpec((SC_WINDOW, width), lambda i: (i, 0)),
                          pl.BlockSpec((TOP_K, SC_WINDOW), lambda i: (0, i))],
                out_specs=[],
                core_axis_name=("core", "subcore"),
                dimension_semantics=(pltpu.PARALLEL,),
            )(rows_hbm, dest_hbm)

    return scatter(rows_lo, rows_hi, dest)


def _sc_gather_rows(table_lo, table_hi, idx):
    n = idx.shape[1]
    width = table_lo.shape[1]
    mesh = plsc.VectorSubcoreMesh(core_axis_name="core", subcore_axis_name="subcore")
    out_type = jax.ShapeDtypeStruct((n, width), table_lo.dtype)

    @pl.kernel(out_type=(out_type, out_type), mesh=mesh, scratch_types=[])
    def gather(lo_hbm, hi_hbm, idx_hbm, out_lo_hbm, out_hi_hbm):
        for table_hbm, out_hbm in ((lo_hbm, out_lo_hbm), (hi_hbm, out_hi_hbm)):
            def body(idx_vmem, out_vmem, table_hbm=table_hbm):
                pltpu.sync_copy(table_hbm.at[idx_vmem.at[0]], out_vmem)

            pltpu.emit_pipeline(
                body,
                grid=(n // SC_WINDOW,),
                in_specs=[pl.BlockSpec((1, SC_WINDOW), lambda i: (0, i))],
                out_specs=[pl.BlockSpec((SC_WINDOW, width), lambda i: (i, 0))],
                core_axis_name=("core", "subcore"),
                dimension_semantics=(pltpu.PARALLEL,),
            )(idx_hbm, out_hbm)

    return gather(table_lo, table_hi, idx)


def _expert_kernel(cnt_ref, xlo_hbm, xhi_hbm, w1_hbm, b1_ref, w2_hbm, b2_ref, perm_ref, ylo_hbm, yhi_hbm,
                   w1p_ref, w2b_ref, x_ref, act_ref, xin_ref, yout_ref, w1f_ref, w2f_ref,
                   in_sem, out_sem, w_sem, *, region):
    e = pl.program_id(0)
    n_blk = (cnt_ref[e] + EXPERT_BLOCK - 1) // EXPERT_BLOCK
    base = e * region
    n_chunks = (2 * D_FF) // MXU_DIM
    x_hbm = (xlo_hbm, xhi_hbm)
    y_hbm = (ylo_hbm, yhi_hbm)
    w_slot = e % 2
    piece_rows = D_MODEL // WEIGHT_PIECES

    def weight_copies(expert, piece, slot):
        rows = pl.ds(pl.multiple_of(piece * piece_rows, piece_rows), piece_rows)
        return (pltpu.make_async_copy(w1_hbm.at[expert, rows, :], w1f_ref.at[slot, rows, :], w_sem.at[slot, 0]),
                pltpu.make_async_copy(w2_hbm.at[expert, rows, :], w2f_ref.at[slot, rows, :], w_sem.at[slot, 1]))

    def request_next_piece(piece):
        @pl.when(e + 1 < N_EXPERTS)
        def _():
            for copy in weight_copies(e + 1, piece, 1 - w_slot):
                copy.start()

    @pl.when(e == 0)
    def _():
        for piece in range(WEIGHT_PIECES):
            for copy in weight_copies(0, piece, 0):
                copy.start()

    def rows_of(j):
        return pl.ds(pl.multiple_of(base + j * EXPERT_BLOCK, EXPERT_BLOCK), EXPERT_BLOCK)

    def in_copy(j, slot, half):
        return pltpu.make_async_copy(x_hbm[half].at[rows_of(j), :], xin_ref.at[slot, half], in_sem.at[slot, half])

    def out_copy(j, slot, half):
        return pltpu.make_async_copy(yout_ref.at[slot, half], y_hbm[half].at[rows_of(j), :], out_sem.at[slot, half])

    @pl.when(n_blk > 0)
    def _():
        for half in range(2):
            in_copy(0, 0, half).start(priority=ROW_DMA_PRIORITY)

    for piece in range(WEIGHT_PIECES):
        for copy in weight_copies(e, piece, w_slot):
            copy.wait()
    perm = perm_ref[...]
    for c in range(n_chunks):
        cols = slice(c * MXU_DIM, (c + 1) * MXU_DIM)
        w1p_ref[c] = jnp.dot(w1f_ref[w_slot, :, cols].astype(BF16), perm, preferred_element_type=F32).astype(BF16)
    for c in range(D_MODEL // MXU_DIM):
        w2b_ref[c] = w2f_ref[w_slot, :, c * MXU_DIM:(c + 1) * MXU_DIM].astype(BF16)

    def mlp_rows(n_rows, xin, yout):
        words = jnp.concatenate([xin[0], xin[1]], axis=1)
        x_ref[:n_rows, :] = _unpack_rows(words).astype(BF16)

        def up_task(c0):
            chunks = range(c0, c0 + UP_CHUNKS_PER_TASK)

            def matmul():
                return [jnp.dot(x_ref[:n_rows, :], w1p_ref[c], preferred_element_type=F32) for c in chunks]

            def epilogue(results):
                gates, ups = [], []
                for c, r in zip(chunks, results):
                    h = r + b1_ref[0, :, c * MXU_DIM:(c + 1) * MXU_DIM]
                    gates.append(jnp.minimum(h[:, :LANES], SWIGLU_LIMIT))
                    ups.append(jnp.clip(h[:, LANES:], -SWIGLU_LIMIT, SWIGLU_LIMIT))
                gate = jnp.concatenate(gates, axis=1)
                up = jnp.concatenate(ups, axis=1)
                glu = gate * _sigmoid(SWIGLU_ALPHA * gate)
                act_ref[:n_rows, c0 * LANES:(c0 + UP_CHUNKS_PER_TASK) * LANES] = ((up + 1.0) * glu).astype(BF16)
            return matmul, epilogue

        _run_lookahead([up_task(c0) for c0 in range(0, n_chunks, UP_CHUNKS_PER_TASK)])

        def down_task(c):
            cols = slice(c * MXU_DIM, (c + 1) * MXU_DIM)
            half, high = c % 2, c // 2

            def epilogue(r):
                bits = lax.bitcast_convert_type((r + b2_ref[0, :, cols]).astype(BF16).astype(F32), jnp.uint32)
                if high:
                    yout[half] = yout[half] | (bits & jnp.uint32(0xFFFF0000))
                else:
                    yout[half] = bits >> 16
            return (lambda: jnp.dot(act_ref[:n_rows, :], w2b_ref[c], preferred_element_type=F32)), epilogue

        _run_lookahead([down_task(c) for c in range(D_MODEL // MXU_DIM)])

    def block(j, _):
        slot = j % 2
        for half in range(2):
            in_copy(j, slot, half).wait()

        @pl.when(j + 1 < n_blk)
        def _():
            for half in range(2):
                in_copy(j + 1, 1 - slot, half).start(priority=ROW_DMA_PRIORITY)

        @pl.when(j >= 2)
        def _():
            for half in range(2):
                out_copy(j - 2, slot, half).wait()

        for k in range(PIECES_PER_BLOCK):
            @pl.when(j * PIECES_PER_BLOCK + k < WEIGHT_PIECES)
            def _(k=k):
                request_next_piece(j * PIECES_PER_BLOCK + k)

        mlp_rows(EXPERT_BLOCK, xin_ref.at[slot], yout_ref.at[slot])
        for half in range(2):
            out_copy(j, slot, half).start(priority=ROW_DMA_PRIORITY)
        return 0

    lax.fori_loop(0, n_blk, block, 0)

    for piece in range(WEIGHT_PIECES):
        @pl.when(piece >= n_blk * PIECES_PER_BLOCK)
        def _(piece=piece):
            request_next_piece(piece)

    for back in (2, 1):
        @pl.when(n_blk >= back)
        def _(back=back):
            j = n_blk - back
            for half in range(2):
                out_copy(j, j % 2, half).wait()


def _gate_up_order():
    j = jnp.arange(MXU_DIM)
    within = jnp.where(j < LANES, 2 * j, 2 * (j - LANES) + 1)
    return within


def _experts(xs_lo, xs_hi, w1, b1, w2, b2, counts, region):
    within = _gate_up_order()
    perm = (jnp.arange(MXU_DIM)[:, None] == within[None, :]).astype(BF16)
    order = (jnp.arange(0, 2 * D_FF, MXU_DIM)[:, None] + within[None, :]).reshape(-1)
    b1p = b1.astype(F32)[:, order][:, None, :]
    b2r = b2.astype(F32)[:, None, :]
    tb = EXPERT_BLOCK
    hbm = pl.BlockSpec(memory_space=pl.ANY)
    wspec = lambda shape: pl.BlockSpec((1,) + shape, lambda e, cnt: (e, 0, 0))
    grid_spec = pltpu.PrefetchScalarGridSpec(
        num_scalar_prefetch=1,
        grid=(N_EXPERTS,),
        in_specs=[hbm, hbm, hbm, wspec((1, 2 * D_FF)), hbm, wspec((1, D_MODEL)),
                  pl.BlockSpec((MXU_DIM, MXU_DIM), lambda e, cnt: (0, 0))],
        out_specs=[hbm, hbm],
        scratch_shapes=[pltpu.VMEM((2 * D_FF // MXU_DIM, D_MODEL, MXU_DIM), BF16),
                        pltpu.VMEM((D_MODEL // MXU_DIM, D_FF, MXU_DIM), BF16),
                        pltpu.VMEM((tb, D_MODEL), BF16), pltpu.VMEM((tb, D_FF), BF16),
                        pltpu.VMEM((2, 2, tb, PACK_HALF), jnp.uint32), pltpu.VMEM((2, 2, tb, PACK_HALF), jnp.uint32),
                        pltpu.VMEM((2, D_MODEL, 2 * D_FF), F32), pltpu.VMEM((2, D_FF, D_MODEL), F32),
                        pltpu.SemaphoreType.DMA((2, 2)), pltpu.SemaphoreType.DMA((2, 2)),
                        pltpu.SemaphoreType.DMA((2, 2))],
    )
    return pl.pallas_call(
        functools.partial(_expert_kernel, region=region),
        grid_spec=grid_spec,
        out_shape=[jax.ShapeDtypeStruct(xs_lo.shape, jnp.uint32)] * 2,
        compiler_params=_cparams(("arbitrary",)),
        name="experts",
    )(counts, xs_lo, xs_hi, w1, b1p, w2, b2r, perm)


def _combine_kernel(x1_ref, gate_ref, *refs):
    lo_refs, hi_refs, o_ref = refs[:TOP_K], refs[TOP_K:2 * TOP_K], refs[2 * TOP_K]
    acc = x1_ref[...]
    gates = gate_ref[...]
    for kk in range(TOP_K):
        words = jnp.concatenate([lo_refs[kk][...], hi_refs[kk][...]], axis=1)
        acc = acc + gates[:, kk:kk + 1] * _unpack_rows(words)
    o_ref[...] = acc


def _combine(x1, gates, yg_lo, yg_hi):
    n_tok = x1.shape[0]
    tm = min(ROW_TILE, n_tok)
    nblk = n_tok // tm
    row = lambda w: pl.BlockSpec((tm, w), lambda i: (i, 0))
    plane = lambda kk: pl.BlockSpec((tm, PACK_HALF), lambda i, kk=kk: (kk * nblk + i, 0))
    planes = [plane(kk) for kk in range(TOP_K)]
    return pl.pallas_call(
        _combine_kernel,
        grid=(nblk,),
        in_specs=[row(D_MODEL), row(LANES)] + planes + planes,
        out_specs=row(D_MODEL),
        out_shape=jax.ShapeDtypeStruct((n_tok, D_MODEL), F32),
        compiler_params=_cparams(("parallel",)),
        name="combine",
    )(x1, gates, *([yg_lo] * TOP_K), *([yg_hi] * TOP_K))


def kernel(x, norm1_gain, w_in, lambda_re, lambda_im, log_dt, ssm_b_re, ssm_b_im, ssm_c_re, ssm_c_im, ssm_d, w_glu, b_glu, q_norm_gain, k_norm_gain, lambda_q1, lambda_k1, lambda_q2, lambda_k2, subln_gain, w_proj_ssm, w_proj_attn, w_out, norm2_gain, w_router, b_router, w_exp1, b_exp1, w_exp2, b_exp2):
    bsz, seq, d = x.shape
    n_tok = bsz * seq
    depth = norm1_gain.shape[0]
    row1 = lambda a: a.astype(F32).reshape(1, -1)
    for l in range(depth):
        lambda_init = 0.8 - 0.6 * math.exp(-0.3 * l)
        x2d = x.reshape(n_tok, d)

        u, q, k, vt, gs, ga = _in_proj(x2d, row1(norm1_gain[l]), w_in[l], q_norm_gain[l], k_norm_gain[l])

        bblk, cblk, a_tile = _ssm_params(lambda_re[l], lambda_im[l], log_dt[l], ssm_b_re[l], ssm_b_im[l],
                                         ssm_c_re[l], ssm_c_im[l])
        so = _ssm(u.reshape(bsz, seq, SSM_WIDTH), bblk, cblk, a_tile, row1(ssm_d[l]),
                  w_glu[l].astype(BF16), row1(b_glu[l])).reshape(n_tok, SSM_WIDTH)

        lam = (jnp.exp(jnp.sum(lambda_q1[l].astype(F32) * lambda_k1[l].astype(F32)))
               - jnp.exp(jnp.sum(lambda_q2[l].astype(F32) * lambda_k2[l].astype(F32)))
               + lambda_init).reshape(1)
        score_bound = (NORM_BOUND_SLACK * HEAD_DIM * Q_SCALE * jnp.max(jnp.abs(q_norm_gain[l].astype(F32)))
                       * jnp.max(jnp.abs(k_norm_gain[l].astype(F32))))
        ao = _attention(q, k, vt, lam, subln_gain[l], score_bound, lambda_init, bsz, seq)

        region = -(-n_tok // EXPERT_BLOCK) * EXPERT_BLOCK
        x1, h_lo, h_hi, route, gates, counts = _merge_route(
            so, ao, gs, ga, x2d, w_proj_ssm[l].astype(BF16), w_proj_attn[l].astype(BF16),
            w_out[l].astype(BF16), row1(norm2_gain[l]), w_router[l], b_router[l], region)

        dest = route[:TOP_K]
        n_slots = N_EXPERTS * region
        xs_lo, xs_hi = _sc_scatter_rows(h_lo, h_hi, dest, n_slots)

        ys_lo, ys_hi = _experts(xs_lo, xs_hi, w_exp1[l], b_exp1[l], w_exp2[l], b_exp2[l], counts[:, 0], region)

        yg_lo, yg_hi = _sc_gather_rows(ys_lo, ys_hi, dest.reshape(1, TOP_K * n_tok))
        x = _combine(x1, gates, yg_lo, yg_hi).reshape(bsz, seq, d)
    return x
```
